```python
import math
import jax
import jax.numpy as jnp
from jax import lax
import numpy as np

D_MODEL = 2048
BATCH = 4
SEQ = 4096
DEPTH = 2

GRID_W = 64
CTX_LEN = 256
EPS = 1e-6
N_MOD = 6
DN_HEADS = 16
DN_HEAD_DIM = 128
DN_W = DN_HEADS * DN_HEAD_DIM
CONV_K = 5
CHUNK = 64
POOL_WINDOWS = (2, 4, 8, 16)
POOL_GROUPS = 4
POOL_GW = 256
POOL_W = POOL_GROUPS * POOL_GW
QKV_COLS = 3 * DN_W
SCAN_COLS = QKV_COLS + 4 * DN_HEADS
OGATE_OFF = SCAN_COLS
POOL_OFF = OGATE_OFF + DN_W
MERGE_OFF = POOL_OFF + POOL_W
IN_COLS = MERGE_OFF + 2 * D_MODEL
N_EXPERTS = 64
N_GROUPS = 8
GROUP_SIZE = N_EXPERTS // N_GROUPS
TOP_K = 2
D_EXPERT = 512
EXPERT_BLOCK = 128

kernel_name = 'hybrid_deltanet_pool_moe_dit'


def rmsnorm(x, g):
    xf = x.astype(jnp.float32)
    y = xf * lax.rsqrt(jnp.mean(jnp.square(xf), -1, keepdims=True) + EPS)
    return (y * g.astype(jnp.float32)).astype(x.dtype)


def l2norm(x):
    return x * lax.rsqrt(jnp.sum(jnp.square(x), -1, keepdims=True) + EPS)


def dwconv_centred(x, w):
    L = x.shape[1]
    pad = w.shape[0] // 2
    xp = jnp.pad(x, ((0, 0), (pad, pad), (0, 0)))
    return sum(xp[:, j:j + L] * w[j] for j in range(w.shape[0]))


def deltanet_inputs(z, conv_w, a_log, dt_bias):
    B, L, _ = z.shape
    qkv = jax.nn.silu(dwconv_centred(z[..., :QKV_COLS], conv_w)).astype(jnp.float32)
    q, k, v = [t.reshape(B, L, DN_HEADS, DN_HEAD_DIM) for t in jnp.split(qkv, 3, -1)]
    q = l2norm(q) * DN_HEAD_DIM ** -0.5
    k = l2norm(k)
    ba = z[..., QKV_COLS:SCAN_COLS].astype(jnp.float32).reshape(B, L, 4, DN_HEADS)
    beta = jax.nn.sigmoid(ba[:, :, 0:2])
    g = -jnp.exp(a_log.astype(jnp.float32)) * jax.nn.softplus(ba[:, :, 2:4] + dt_bias.astype(jnp.float32))
    return q, k, v, jnp.moveaxis(beta, 2, 0), jnp.moveaxis(g, 2, 0)


def both_directions(fc, fl, bc, bl):
    fwd = jnp.concatenate([fc, fl], 1)
    bwd = jnp.concatenate([jnp.flip(bc, 1), jnp.flip(bl, 1)], 1)
    return jnp.concatenate([fwd, bwd], 0)


def gated_delta_chunked(q, k, v, beta, g):
    n, L, H, dk = q.shape
    dv = v.shape[-1]
    nc = L // CHUNK

    def to_chunks(t):
        t = t.astype(jnp.float32).reshape((n, nc, CHUNK, H) + t.shape[3:])
        return jnp.moveaxis(t, (1, 3), (0, 2))

    qc, kc, vc, bc, gc = [to_chunks(t) for t in (q, k, v, beta, g)]
    gcum = jnp.cumsum(gc, -1)
    incl = jnp.tril(jnp.ones((CHUNK, CHUNK), bool))
    strict = jnp.tril(jnp.ones((CHUNK, CHUNK), bool), -1)
    diff = gcum[..., :, None] - gcum[..., None, :]
    decay = jnp.where(incl, jnp.exp(jnp.where(incl, diff, 0.0)), 0.0)
    kb = kc * bc[..., None]
    a_mat = jnp.where(strict, jnp.einsum('cnhid,cnhjd->cnhij', kb, kc) * decay, 0.0)
    rhs = jnp.concatenate([vc * bc[..., None], kb * jnp.exp(gcum)[..., None]], -1)
    sol = lax.linalg.triangular_solve(a_mat, rhs, left_side=True, lower=True, unit_diagonal=True)
    u, w = sol[..., :dv], sol[..., dv:]
    qk = jnp.einsum('cnhid,cnhjd->cnhij', qc, kc) * decay
    q_dec = qc * jnp.exp(gcum)[..., None]
    k_dec = kc * jnp.exp(gcum[..., -1:] - gcum)[..., None]
    g_last = jnp.exp(gcum[..., -1])[..., None, None]

    def step(S, xs):
        u_i, w_i, qk_i, qd_i, kd_i, gl_i = xs
        v_new = u_i - jnp.einsum('nhck,nhkv->nhcv', w_i, S)
        o_i = jnp.einsum('nhck,nhkv->nhcv', qd_i, S) + jnp.einsum('nhij,nhjv->nhiv', qk_i, v_new)
        S = S * gl_i + jnp.einsum('nhck,nhcv->nhkv', kd_i, v_new)
        return S, o_i

    S0 = jnp.zeros((n, H, dk, dv), jnp.float32)
    _, o = lax.scan(step, S0, (u, w, qk, q_dec, k_dec, g_last))
    return jnp.moveaxis(o, (0, 2), (1, 3)).reshape(n, L, H, dv)


def window_bounds(n, w):
    t = jnp.arange(n)
    return jnp.maximum(t - w // 2, 0), jnp.minimum(t + w - w // 2, n)


def corner(sat, r, cc):
    return sat[:, r][:, :, cc]


def pool_grid(p, rows):
    B, S, _ = p.shape
    pf = p.astype(jnp.float32)
    grid = pf.reshape(B, rows, GRID_W, POOL_W)
    sat = jnp.pad(jnp.cumsum(jnp.cumsum(grid, 1), 2), ((0, 0), (1, 0), (1, 0), (0, 0)))
    means = []
    for gi, w in enumerate(POOL_WINDOWS):
        r_lo, r_hi = window_bounds(rows, w)
        c_lo, c_hi = window_bounds(GRID_W, w)
        sg = sat[..., gi * POOL_GW:(gi + 1) * POOL_GW]
        box = corner(sg, r_hi, c_hi) - corner(sg, r_lo, c_hi) - corner(sg, r_hi, c_lo) + corner(sg, r_lo, c_lo)
        cnt = ((r_hi - r_lo)[:, None] * (c_hi - c_lo)[None, :]).astype(jnp.float32)
        means.append(box / cnt[None, :, :, None])
    return (jnp.concatenate(means, -1).reshape(B, S, POOL_W) - pf).astype(p.dtype)


def pool_seq(p):
    L = p.shape[1]
    pf = p.astype(jnp.float32)
    cs = jnp.pad(jnp.cumsum(pf, 1), ((0, 0), (1, 0), (0, 0)))
    means = []
    for gi, w in enumerate(POOL_WINDOWS):
        lo, hi = window_bounds(L, w)
        cg = cs[..., gi * POOL_GW:(gi + 1) * POOL_GW]
        means.append((cg[:, hi] - cg[:, lo]) / (hi - lo).astype(jnp.float32)[None, :, None])
    return (jnp.concatenate(means, -1) - pf).astype(p.dtype)


def merge_branches(z, o_dn, pooled, onorm_g, w_pool, pool_scale, w_up_dn, w_up_pool, w_out):
    B, L, _ = z.shape
    dtype = z.dtype
    og = z[..., OGATE_OFF:POOL_OFF].astype(jnp.float32).reshape(B, L, DN_HEADS, DN_HEAD_DIM)
    o = (o_dn * lax.rsqrt(jnp.mean(jnp.square(o_dn), -1, keepdims=True) + EPS)
         * onorm_g.astype(jnp.float32) * jax.nn.silu(og))
    y_dn = o.reshape(B, L, DN_W).astype(dtype) @ w_up_dn
    y_pool = jnp.einsum('blgc,gcd->blgd', pooled.reshape(B, L, POOL_GROUPS, POOL_GW), w_pool)
    y_pool = (y_pool.reshape(B, L, POOL_W) * pool_scale) @ w_up_pool
    gate = jax.nn.sigmoid(z[..., MERGE_OFF:].astype(jnp.float32))
    merged = gate[..., :D_MODEL] * y_dn + gate[..., D_MODEL:] * y_pool
    return merged.astype(dtype) @ w_out


def token_mixer(u_ctx, u_lat, w_in, conv_w, a_log, dt_bias, onorm_g, w_pool, pool_scale,
                w_up_dn, w_up_pool, w_out, ctx_out):
    B, S, _ = u_lat.shape
    C = u_ctx.shape[1]
    rows = S // GRID_W
    z_lat = u_lat @ w_in
    z_ctx = u_ctx @ (w_in if ctx_out else w_in[:, :SCAN_COLS])
    qc, kc, vc, bc, gc = deltanet_inputs(z_ctx, conv_w, a_log, dt_bias)
    ql, kl, vl, bl, gl = deltanet_inputs(z_lat, conv_w, a_log, dt_bias)
    o = gated_delta_chunked(both_directions(qc, ql, qc, ql), both_directions(kc, kl, kc, kl),
                            both_directions(vc, vl, vc, vl), both_directions(bc[0], bl[0], bc[1], bl[1]),
                            both_directions(gc[0], gl[0], gc[1], gl[1]))
    o_fwd, o_bwd = o[:B], o[B:]
    o_lat = o_fwd[:, C:] + jnp.flip(o_bwd[:, C:], 1)
    p_lat = z_lat[..., POOL_OFF:MERGE_OFF]
    out_lat = merge_branches(z_lat, o_lat, pool_grid(p_lat, rows), onorm_g, w_pool, pool_scale,
                             w_up_dn, w_up_pool, w_out)
    if not ctx_out:
        return None, out_lat
    o_ctx = o_fwd[:, :C] + jnp.flip(o_bwd[:, :C], 1)
    p_ctx = z_ctx[..., POOL_OFF:MERGE_OFF]
    out_ctx = merge_branches(z_ctx, o_ctx, pool_seq(p_ctx), onorm_g, w_pool, pool_scale,
                             w_up_dn, w_up_pool, w_out)
    return out_ctx, out_lat


def moe_ffn(h, w_router, router_bias, w_g, w_u, w_d):
    T, D = h.shape
    aff = jax.nn.sigmoid(h.astype(jnp.float32) @ w_router.astype(jnp.float32))
    sel = (aff + router_bias.astype(jnp.float32)).reshape(T, N_GROUPS, GROUP_SIZE)
    group_score = lax.top_k(sel, TOP_K)[0].sum(-1)
    g_idx = jnp.argmax(group_score, -1)
    sel_in = jnp.take_along_axis(sel, g_idx[:, None, None], 1)[:, 0]
    _, local = lax.top_k(sel_in, TOP_K)
    expert = g_idx[:, None] * GROUP_SIZE + local
    a = jnp.take_along_axis(aff, expert, 1)
    gate = a / jnp.sum(a, -1, keepdims=True)
    flat_e = expert.reshape(-1)
    order = jnp.argsort(flat_e)
    e_sorted = flat_e[order]
    tok_sorted = order // TOP_K
    counts = jnp.bincount(flat_e, length=N_EXPERTS)
    padded = (counts + EXPERT_BLOCK - 1) // EXPERT_BLOCK * EXPERT_BLOCK
    pad_end = jnp.cumsum(padded)
    pad_start = pad_end - padded
    start = jnp.cumsum(counts) - counts
    dest = pad_start[e_sorted] + jnp.arange(T * TOP_K) - start[e_sorted]
    n_blocks = (T * TOP_K + EXPERT_BLOCK - 1) // EXPERT_BLOCK + N_EXPERTS
    row_tok = jnp.zeros((n_blocks * EXPERT_BLOCK,), jnp.int32).at[dest].set(tok_sorted.astype(jnp.int32))
    block_expert = jnp.minimum(jnp.searchsorted(pad_end, jnp.arange(n_blocks) * EXPERT_BLOCK, side='right'),
                               N_EXPERTS - 1)
    xb = h[row_tok].reshape(n_blocks, EXPERT_BLOCK, D)

    def expert_block(args):
        xblk, e = args
        return (jax.nn.silu(xblk @ w_g[e]) * (xblk @ w_u[e])) @ w_d[e]

    yb = lax.map(expert_block, (xb, block_expert)).reshape(n_blocks * EXPERT_BLOCK, D)
    contrib = yb[dest] * gate.reshape(-1)[order][:, None].astype(h.dtype)
    return jnp.zeros_like(h).at[tok_sorted].add(contrib)


def setup_inputs(seed: int = 0) -> dict:
    key = jax.random.key(seed)
    ks = jax.random.split(key, 24)
    f32 = jnp.float32

    def nrm(k, shape, fan_in, scale=1.0):
        return jax.random.normal(k, shape, f32) * (scale * fan_in ** -0.5)

    def noise(k, shape, s):
        return s * jax.random.normal(k, shape, f32)

    L = DEPTH
    dt = jnp.exp(jax.random.uniform(ks[11], (L, 2, DN_HEADS), f32, math.log(1e-3), math.log(0.1)))
    return {
        'x': jax.random.normal(ks[0], (BATCH, SEQ, D_MODEL), f32),
        'c': jax.random.normal(ks[1], (BATCH, D_MODEL), f32),
        'ctx': jax.random.normal(ks[2], (BATCH, CTX_LEN, D_MODEL), f32),
        'c_ctx': jax.random.normal(ks[3], (D_MODEL,), f32),
        'w_mod': nrm(ks[4], (L, D_MODEL, N_MOD * D_MODEL), D_MODEL, 0.5),
        'b_mod': noise(ks[5], (L, N_MOD * D_MODEL), 0.02),
        'norm1_g': 1.0 + noise(ks[6], (L, D_MODEL), 0.05),
        'norm2_g': 1.0 + noise(ks[7], (L, D_MODEL), 0.05),
        'w_in': nrm(ks[8], (L, D_MODEL, IN_COLS), D_MODEL),
        'conv_w': nrm(ks[9], (L, CONV_K, QKV_COLS), CONV_K),
        'a_log': jnp.log(jax.random.uniform(ks[10], (L, 2, DN_HEADS), f32, 1.0, 16.0)),
        'dt_bias': dt + jnp.log(-jnp.expm1(-dt)),
        'onorm_g': 1.0 + noise(ks[12], (L, DN_HEAD_DIM), 0.05),
        'w_pool': nrm(ks[13], (L, POOL_GROUPS, POOL_GW, POOL_GW), POOL_GW),
        'pool_scale': 1.0 + noise(ks[14], (L, POOL_W), 0.1),
        'w_up_dn': nrm(ks[15], (L, DN_W, D_MODEL), DN_W),
        'w_up_pool': nrm(ks[16], (L, POOL_W, D_MODEL), POOL_W),
        'w_out': nrm(ks[17], (L, D_MODEL, D_MODEL), D_MODEL),
        'w_router': nrm(ks[18], (D_MODEL, N_EXPERTS), D_MODEL),
        'router_bias': noise(ks[19], (N_EXPERTS,), 0.01),
        'w_gate_e': nrm(ks[20], (L, N_EXPERTS, D_MODEL, D_EXPERT), D_MODEL),
        'w_up_e': nrm(ks[21], (L, N_EXPERTS, D_MODEL, D_EXPERT), D_MODEL),
        'w_down_e': nrm(ks[22], (L, N_EXPERTS, D_EXPERT, D_MODEL), D_EXPERT),
        'final_g': 1.0 + noise(ks[23], (D_MODEL,), 0.05),
    }


def reference(x, c, ctx, c_ctx, w_mod, b_mod, norm1_g, norm2_g, w_in, conv_w, a_log, dt_bias, onorm_g,
              w_pool, pool_scale, w_up_dn, w_up_pool, w_out, w_router, router_bias, w_gate_e, w_up_e,
              w_down_e, final_g):
    B, S, D = x.shape
    C = ctx.shape[1]
    h_lat, h_ctx = x, ctx
    for layer in range(DEPTH):
        last = layer == DEPTH - 1
        sh1, sc1, gt1, sh2, sc2, gt2 = jnp.split(
            (jax.nn.silu(c) @ w_mod[layer] + b_mod[layer])[:, None, :], N_MOD, -1)
        csh1, csc1, cgt1, csh2, csc2, cgt2 = jnp.split(
            (jax.nn.silu(c_ctx) @ w_mod[layer] + b_mod[layer])[None, None, :], N_MOD, -1)
        u_lat = rmsnorm(h_lat, norm1_g[layer]) * (1 + sc1) + sh1
        u_ctx = rmsnorm(h_ctx, norm1_g[layer]) * (1 + csc1) + csh1
        m_ctx, m_lat = token_mixer(u_ctx, u_lat, w_in[layer], conv_w[layer], a_log[layer], dt_bias[layer],
                                   onorm_g[layer], w_pool[layer], pool_scale[layer], w_up_dn[layer],
                                   w_up_pool[layer], w_out[layer], not last)
        h_lat = h_lat + gt1 * m_lat
        v_lat = rmsnorm(h_lat, norm2_g[layer]) * (1 + sc2) + sh2
        if last:
            f_lat = moe_ffn(v_lat.reshape(B * S, D), w_router, router_bias, w_gate_e[layer], w_up_e[layer],
                            w_down_e[layer])
            h_lat = h_lat + gt2 * f_lat.reshape(B, S, D)
        else:
            h_ctx = h_ctx + cgt1 * m_ctx
            v_ctx = rmsnorm(h_ctx, norm2_g[layer]) * (1 + csc2) + csh2
            f = moe_ffn(jnp.concatenate([v_ctx.reshape(B * C, D), v_lat.reshape(B * S, D)], 0),
                        w_router, router_bias, w_gate_e[layer], w_up_e[layer], w_down_e[layer])
            h_ctx = h_ctx + cgt2 * f[:B * C].reshape(B, C, D)
            h_lat = h_lat + gt2 * f[B * C:].reshape(B, S, D)
    return rmsnorm(h_lat, final_g)
```

```python
import functools

import jax
import jax.numpy as jnp
from jax import lax
from jax.experimental import pallas as pl
from jax.experimental.pallas import tpu as pltpu

f32 = jnp.float32
bf16 = jnp.bfloat16
HIGHEST = lax.Precision.HIGHEST

EPS = 1e-6
LANES = 128
GRID_W = 64
CHUNK = 64
POOL_WINDOWS = (2, 4, 8, 16)
GROUP_SIZE = 8
EXPERT_BLOCK = 128
VMEM_LIMIT = 56 * 1024 * 1024


def _cparams(sem, vmem=VMEM_LIMIT):
    return pltpu.CompilerParams(dimension_semantics=sem, vmem_limit_bytes=vmem)


def _sigmoid(x):
    return 1.0 / (1.0 + jnp.exp(-x))


def _softplus(x):
    return jnp.maximum(x, 0.0) + jnp.log(1.0 + jnp.exp(-jnp.abs(x)))


def _mod_spec(sec, D, tiles_per_batch, n_lat_tiles, n_batch):
    def index(i, *_):
        return (jnp.where(i < n_lat_tiles, i // tiles_per_batch, n_batch), 0, sec)

    return pl.BlockSpec((1, 1, D), index)


def _mod_body(c_ref, w_ref, b_ref, o_ref):
    c = c_ref[...]
    s = c * _sigmoid(c)
    o_ref[0] = jnp.dot(s, w_ref[0], preferred_element_type=f32, precision=HIGHEST) + b_ref[0]


def _mod_table(c8, w_mod, b_mod):
    L, D, N = w_mod.shape
    tn = next(t for t in (1024, 512, 256, LANES) if N % t == 0)
    return pl.pallas_call(
        _mod_body,
        grid=(L, N // tn),
        in_specs=[pl.BlockSpec((8, D), lambda l, j: (0, 0)),
                  pl.BlockSpec((1, D, tn), lambda l, j: (l, 0, j)),
                  pl.BlockSpec((1, 1, tn), lambda l, j: (l, 0, j))],
        out_specs=pl.BlockSpec((1, 8, tn), lambda l, j: (l, 0, j)),
        out_shape=jax.ShapeDtypeStruct((L, 8, N), f32),
        compiler_params=_cparams(("parallel", "parallel")),
        name="mod_table",
    )(c8, w_mod, b_mod.reshape(L, 1, N))


def _inproj_body(h_ref, sc_ref, sh_ref, g_ref, w_ref, wba_ref, z_ref, ba_ref, u_scr):
    j = pl.program_id(1)

    @pl.when(j == 0)
    def _():
        x = h_ref[...]
        y = x * lax.rsqrt(jnp.mean(x * x, -1, keepdims=True) + EPS) * g_ref[...]
        u = (y * (1.0 + sc_ref[0]) + sh_ref[0]).astype(bf16)
        u_scr[...] = u
        ba_ref[...] = jnp.dot(u, wba_ref[...], preferred_element_type=f32)

    z_ref[...] = jnp.dot(u_scr[...], w_ref[...], preferred_element_type=f32).astype(z_ref.dtype)


def _in_proj(h, mod, sec, gain, w_z, w_ba, *, tm, tn, tpb, nlat, nb):
    T, D = h.shape
    Nz = w_z.shape[1]
    return pl.pallas_call(
        _inproj_body,
        grid=(T // tm, Nz // tn),
        in_specs=[pl.BlockSpec((tm, D), lambda i, j: (i, 0)),
                  _mod_spec(sec + 1, D, tpb, nlat, nb),
                  _mod_spec(sec, D, tpb, nlat, nb),
                  pl.BlockSpec((1, D), lambda i, j: (0, 0)),
                  pl.BlockSpec((D, tn), lambda i, j: (0, j)),
                  pl.BlockSpec((D, LANES), lambda i, j: (0, 0))],
        out_specs=[pl.BlockSpec((tm, tn), lambda i, j: (i, j)),
                   pl.BlockSpec((tm, LANES), lambda i, j: (i, 0))],
        out_shape=[jax.ShapeDtypeStruct((T, Nz), bf16), jax.ShapeDtypeStruct((T, LANES), f32)],
        scratch_shapes=[pltpu.VMEM((tm, D), bf16)],
        compiler_params=_cparams(("parallel", "arbitrary")),
        name="in_proj",
    )(h, mod, mod, gain, w_z, w_ba)


def _conv_body(x_ref, xp_ref, xn_ref, w_ref, o_ref, *, tr, tps_lat, tps_ctx, nlat, qscale, tiles_per_sec, halo):
    i = pl.program_id(0)
    j = pl.program_id(1)
    il = jnp.where(i < nlat, i % tps_lat, (i - nlat) % tps_ctx)
    nt = jnp.where(i < nlat, tps_lat, tps_ctx)
    first = il == 0
    last = il == nt - 1
    x = x_ref[...].astype(f32)
    p = jnp.where(first, 0.0, xp_ref[...].astype(f32))
    n = jnp.where(last, 0.0, xn_ref[...].astype(f32))
    row = lax.broadcasted_iota(jnp.int32, x.shape, 0)
    xm1 = jnp.where(row == 0, p[halo - 1:halo], pltpu.roll(x, 1, 0))
    xm2 = jnp.where(row == 0, p[halo - 2:halo - 1], jnp.where(row == 1, p[halo - 1:halo], pltpu.roll(x, 2, 0)))
    xp1 = jnp.where(row == tr - 1, n[0:1], pltpu.roll(x, tr - 1, 0))
    xp2 = jnp.where(row == tr - 2, n[0:1], jnp.where(row == tr - 1, n[1:2], pltpu.roll(x, tr - 2, 0)))
    w = w_ref[...]
    acc = xm2 * w[0:1] + xm1 * w[1:2] + x * w[2:3] + xp1 * w[3:4] + xp2 * w[4:5]
    y = acc * _sigmoid(acc)
    sec = j // tiles_per_sec
    tc = x.shape[1]
    for s in range(tc // LANES):
        ys = y[:, s * LANES:(s + 1) * LANES]
        rs = lax.rsqrt(jnp.sum(ys * ys, -1, keepdims=True) + EPS)
        scale = jnp.where(sec == 2, 1.0, rs) * jnp.where(sec == 0, qscale, 1.0)
        o_ref[:, s * LANES:(s + 1) * LANES] = (ys * scale).astype(o_ref.dtype)


def _conv_prep(z, conv_w, *, tr, tc, S, C, nlat_rows, dn_w, dk):
    T = z.shape[0]
    halo = 16
    hb = tr // halo
    nh = T // halo
    body = functools.partial(_conv_body, tr=tr, tps_lat=S // tr, tps_ctx=C // tr, nlat=nlat_rows // tr,
                             qscale=float(dk) ** -0.5, tiles_per_sec=dn_w // tc, halo=halo)
    return pl.pallas_call(
        body,
        grid=(T // tr, 3 * dn_w // tc),
        in_specs=[pl.BlockSpec((tr, tc), lambda i, j: (i, j)),
                  pl.BlockSpec((halo, tc), lambda i, j: (jnp.maximum(i * hb - 1, 0), j)),
                  pl.BlockSpec((halo, tc), lambda i, j: (jnp.minimum((i + 1) * hb, nh - 1), j)),
                  pl.BlockSpec((conv_w.shape[0], tc), lambda i, j: (0, j))],
        out_specs=pl.BlockSpec((tr, tc), lambda i, j: (i, j)),
        out_shape=jax.ShapeDtypeStruct((T, 3 * dn_w), bf16),
        compiler_params=_cparams(("parallel", "parallel")),
        name="conv_prep",
    )(z, z, z, conv_w)


def _unit_tri_inverse(a, ri, ci):
    def mm(x, y):
        return jnp.dot(x.astype(bf16), y.astype(bf16), preferred_element_type=f32)

    b16 = (ri >> 4) == (ci >> 4)
    b32 = (ri >> 5) == (ci >> 5)
    eye = jnp.where(ri == ci, 1.0, 0.0)
    a16 = jnp.where(b16, a, 0.0)
    t = eye - a16
    p = mm(a16, a16)
    t = t + mm(t, p)
    p = mm(p, p)
    t = t + mm(t, p)
    p = mm(p, p)
    t = t + mm(t, p)
    a32 = jnp.where(b32, a, 0.0) - a16
    t = t - mm(mm(t, a32), t)
    a64 = jnp.where(b32, 0.0, a)
    t = t - mm(mm(t, a64), t)
    return t


def _dn_body(q_ref, k_ref, v_ref, ba_ref, al_ref, dt_ref, o_ref, s_scr, *, G):
    d = pl.program_id(1)
    c = pl.program_id(3)
    dk = LANES

    @pl.when(c == 0)
    def _():
        s_scr[...] = jnp.zeros_like(s_scr)

    ba = ba_ref[0]
    beta_t = _sigmoid(ba)
    g_t = -jnp.exp(al_ref[0]) * _softplus(ba + dt_ref[0])
    ri = lax.broadcasted_iota(jnp.int32, (CHUNK, CHUNK), 0)
    ci = lax.broadcasted_iota(jnp.int32, (CHUNK, CHUNK), 1)
    fwd = d == 0
    later = jnp.where(fwd, ri, ci)
    earlier = jnp.where(fwd, ci, ri)
    incl = later >= earlier
    strict = later > earlier
    tri = jnp.where(incl, 1.0, 0.0)
    gc = jnp.dot(tri, g_t, preferred_element_type=f32, precision=HIGHEST)
    gct = gc.T
    gtot = jnp.sum(g_t, axis=0, keepdims=True)

    def pick(x, col):
        return jnp.where(fwd, x[:, col:col + 1], x[:, G + col:G + col + 1])

    for hh in range(G):
        sl = slice(hh * dk, (hh + 1) * dk)
        beta_c = pick(beta_t, hh)
        gc_c = pick(gc, 2 * G + hh)
        gc_r = jnp.where(fwd, gct[2 * G + hh:2 * G + hh + 1], gct[3 * G + hh:3 * G + hh + 1])
        gt = pick(gtot, 2 * G + hh)
        q = q_ref[:, sl].astype(f32)
        k = k_ref[:, sl].astype(f32)
        v = v_ref[:, sl].astype(f32)
        kb = k * beta_c
        vb = v * beta_c
        eg = jnp.exp(gc_c)
        dec = jnp.where(incl, jnp.exp(jnp.where(incl, gc_c - gc_r, 0.0)), 0.0)
        xkq = jnp.concatenate([kb, q], axis=0).astype(bf16)
        kq = lax.dot_general(xkq, k.astype(bf16), (((1,), (1,)), ((), ())), preferred_element_type=f32)
        a = jnp.where(strict, kq[:CHUNK] * dec, 0.0)
        qk = kq[CHUNK:] * dec
        t = _unit_tri_inverse(a, ri, ci)
        rhs = jnp.concatenate([vb, kb * eg], axis=1).astype(bf16)
        sol = jnp.dot(t.astype(bf16), rhs, preferred_element_type=f32)
        u = sol[:, :dk]
        w = sol[:, dk:]
        s = s_scr[hh]
        wq = jnp.concatenate([w, q * eg], axis=0).astype(bf16)
        r = jnp.dot(wq, s.astype(bf16), preferred_element_type=f32)
        v_new = u - r[:CHUNK]
        vnb = v_new.astype(bf16)
        o = r[CHUNK:] + jnp.dot(qk.astype(bf16), vnb, preferred_element_type=f32)
        kd = (k * jnp.exp(gt - gc_c)).astype(bf16)
        s_scr[hh] = s * jnp.exp(gt) + lax.dot_general(kd, vnb, (((0,), (0,)), ((), ())),
                                                       preferred_element_type=f32)
        o_ref[0, :, sl] = o.astype(o_ref.dtype)


def _deltanet(qkv, ba, a_log, dt_bias, *, B, S, C, H, G):
    T = qkv.shape[0]
    dn_w = H * LANES
    ncl = S // CHUNK
    ncc = C // CHUNK
    nc = ncl + ncc
    hgs = H // G

    def rowblk(b, d, c):
        cc = jnp.where(d == 0, c, ncc - 1 - c)
        cl = jnp.where(d == 0, c - ncc, nc - 1 - c)
        return jnp.where(c < ncc, B * ncl + b * ncc + cc, b * ncl + cl)

    def grouped(t):
        r = t.shape[0]
        t = t.reshape(r, 4, hgs, G).transpose(2, 0, 1, 3).reshape(hgs, r, 4 * G)
        return jnp.pad(t, ((0, 0), (0, 0), (0, LANES - 4 * G)))

    zeros = jnp.zeros((1, 2 * H), f32)
    ba_g = grouped(ba[:, :4 * H])
    al_g = grouped(jnp.concatenate([zeros, a_log.reshape(1, 2 * H)], 1))
    dt_g = grouped(jnp.concatenate([zeros, dt_bias.reshape(1, 2 * H)], 1))
    body = functools.partial(_dn_body, G=G)
    return pl.pallas_call(
        body,
        grid=(B, 2, hgs, nc),
        in_specs=[pl.BlockSpec((CHUNK, G * LANES), lambda b, d, g, c: (rowblk(b, d, c), g)),
                  pl.BlockSpec((CHUNK, G * LANES), lambda b, d, g, c: (rowblk(b, d, c), hgs + g)),
                  pl.BlockSpec((CHUNK, G * LANES), lambda b, d, g, c: (rowblk(b, d, c), 2 * hgs + g)),
                  pl.BlockSpec((1, CHUNK, LANES), lambda b, d, g, c: (g, rowblk(b, d, c), 0)),
                  pl.BlockSpec((1, 1, LANES), lambda b, d, g, c: (g, 0, 0)),
                  pl.BlockSpec((1, 1, LANES), lambda b, d, g, c: (g, 0, 0))],
        out_specs=pl.BlockSpec((1, CHUNK, G * LANES), lambda b, d, g, c: (d, rowblk(b, d, c), g)),
        out_shape=jax.ShapeDtypeStruct((2, T, dn_w), bf16),
        scratch_shapes=[pltpu.VMEM((G, LANES, LANES), f32)],
        compiler_params=_cparams(("parallel", "parallel", "parallel", "arbitrary")),
        name="deltanet",
    )(qkv, qkv, qkv, ba_g, al_g, dt_g)


def _pool_ctx_body(x_ref, o_ref, *, gw):
    C = x_ref.shape[0]
    ri = lax.broadcasted_iota(jnp.int32, (C, C), 0)
    ci = lax.broadcasted_iota(jnp.int32, (C, C), 1)
    t = lax.broadcasted_iota(jnp.int32, (C, 1), 0)
    for gi, w in enumerate(POOL_WINDOWS):
        sl = slice(gi * gw, (gi + 1) * gw)
        band = jnp.where((ci >= ri - w // 2) & (ci < ri + w - w // 2), 1.0, 0.0).astype(bf16)
        cnt = (jnp.minimum(t + w - w // 2, C) - jnp.maximum(t - w // 2, 0)).astype(f32)
        x = x_ref[:, sl]
        box = jnp.dot(band, x, preferred_element_type=f32)
        o_ref[:, sl] = (box / cnt - x.astype(f32)).astype(o_ref.dtype)


def _pool_lat_body(alias_ref, x_ref, o_ref, y_scr, *, w):
    del alias_ref
    S = x_ref.shape[0]
    rows = S // GRID_W
    sb = 4 * GRID_W
    ri = lax.broadcasted_iota(jnp.int32, (sb, sb), 0)
    ci = lax.broadcasted_iota(jnp.int32, (sb, sb), 1)
    cr = ri & (GRID_W - 1)
    cc = ci & (GRID_W - 1)
    same = (ri >> 6) == (ci >> 6)
    band = jnp.where(same & (cc >= cr - w // 2) & (cc < cr + w - w // 2), 1.0, 0.0).astype(bf16)
    for s in range(S // sb):
        y_scr[s * sb:(s + 1) * sb, :] = jnp.dot(band, x_ref[s * sb:(s + 1) * sb, :], preferred_element_type=f32)
    col = lax.broadcasted_iota(jnp.int32, (GRID_W, 1), 0)
    ncol = (jnp.minimum(col + w - w // 2, GRID_W) - jnp.maximum(col - w // 2, 0)).astype(f32)
    for r in range(rows):
        lo = max(r - w // 2, 0)
        hi = min(r + w - w // 2, rows)
        acc = y_scr[lo * GRID_W:(lo + 1) * GRID_W, :]
        for rr in range(lo + 1, hi):
            acc = acc + y_scr[rr * GRID_W:(rr + 1) * GRID_W, :]
        cnt = ncol * float(hi - lo)
        xs = x_ref[r * GRID_W:(r + 1) * GRID_W, :].astype(f32)
        o_ref[r * GRID_W:(r + 1) * GRID_W, :] = (acc / cnt - xs).astype(o_ref.dtype)


def _pool(z, *, B, S, C, pool_off_blocks, pool_w, gw):
    T = z.shape[0]
    nlat_c = B * S // C
    pooled = pl.pallas_call(
        functools.partial(_pool_ctx_body, gw=gw),
        grid=(B,),
        in_specs=[pl.BlockSpec((C, pool_w), lambda b: (nlat_c + b, pool_off_blocks))],
        out_specs=pl.BlockSpec((C, pool_w), lambda b: (nlat_c + b, 0)),
        out_shape=jax.ShapeDtypeStruct((T, pool_w), bf16),
        compiler_params=_cparams(("parallel",)),
        name="pool_ctx",
    )(z)
    off = pool_off_blocks * (pool_w // gw)
    for gi, w in enumerate(POOL_WINDOWS):
        pooled = pl.pallas_call(
            functools.partial(_pool_lat_body, w=w),
            grid=(B,),
            in_specs=[pl.BlockSpec(memory_space=pl.ANY),
                      pl.BlockSpec((S, gw), lambda b, gi=gi: (b, off + gi))],
            out_specs=pl.BlockSpec((S, gw), lambda b, gi=gi: (b, gi)),
            out_shape=jax.ShapeDtypeStruct((T, pool_w), bf16),
            scratch_shapes=[pltpu.VMEM((S, gw), f32)],
            input_output_aliases={0: 0},
            compiler_params=_cparams(("parallel",)),
            name=f"pool_lat_w{w}",
        )(pooled, z)
    return pooled


def _merge_a_body(o_ref, og_ref, m1_ref, m2_ref, p_ref, ong_ref, wpool_ref, ps_ref, wud_ref, wup_ref,
                  out_ref, a_scr, p_scr, *, H, gw):
    for h in range(H):
        sl = slice(h * LANES, (h + 1) * LANES)
        o = o_ref[0, :, sl].astype(f32) + o_ref[1, :, sl].astype(f32)
        og = og_ref[:, sl].astype(f32)
        y = o * lax.rsqrt(jnp.mean(o * o, -1, keepdims=True) + EPS) * ong_ref[...] * (og * _sigmoid(og))
        a_scr[:, sl] = y.astype(bf16)
    y_dn = jnp.dot(a_scr[...], wud_ref[...], preferred_element_type=f32)
    for g in range(wpool_ref.shape[0]):
        sl = slice(g * gw, (g + 1) * gw)
        yp = jnp.dot(p_ref[:, sl], wpool_ref[g], preferred_element_type=f32) * ps_ref[:, sl]
        p_scr[:, sl] = yp.astype(bf16)
    y_pool = jnp.dot(p_scr[...], wup_ref[...], preferred_element_type=f32)
    g1 = _sigmoid(m1_ref[...].astype(f32))
    g2 = _sigmoid(m2_ref[...].astype(f32))
    out_ref[...] = (g1 * y_dn + g2 * y_pool).astype(out_ref.dtype)


def _merge_a(o_dn, z, pooled, onorm_g, w_pool, pool_scale, w_up_dn, w_up_pool, *, n_rows, tm, H, D):
    dn_w = H * LANES
    pool_w = pooled.shape[1]
    ng, gw, _ = w_pool.shape
    og_blk = 3
    m_blk = 4 * dn_w // D
    const = lambda *_: (0, 0)
    return pl.pallas_call(
        functools.partial(_merge_a_body, H=H, gw=gw),
        grid=(n_rows // tm,),
        in_specs=[pl.BlockSpec((2, tm, dn_w), lambda i: (0, i, 0)),
                  pl.BlockSpec((tm, dn_w), lambda i: (i, og_blk)),
                  pl.BlockSpec((tm, D), lambda i: (i, m_blk)),
                  pl.BlockSpec((tm, D), lambda i: (i, m_blk + 1)),
                  pl.BlockSpec((tm, pool_w), lambda i: (i, 0)),
                  pl.BlockSpec((1, LANES), const),
                  pl.BlockSpec((ng, gw, gw), lambda i: (0, 0, 0)),
                  pl.BlockSpec((1, pool_w), const),
                  pl.BlockSpec((dn_w, D), const),
                  pl.BlockSpec((pool_w, D), const)],
        out_specs=pl.BlockSpec((tm, D), lambda i: (i, 0)),
        out_shape=jax.ShapeDtypeStruct((n_rows, D), bf16),
        scratch_shapes=[pltpu.VMEM((tm, dn_w), bf16), pltpu.VMEM((tm, pool_w), bf16)],
        compiler_params=_cparams(("parallel",)),
        name="merge_a",
    )(o_dn, z, z, z, pooled, onorm_g, w_pool, pool_scale, w_up_dn, w_up_pool)


def _merge_b_body(m_ref, h_ref, gt_ref, sc_ref, sh_ref, g_ref, wout_ref, wr_ref, h1_ref, v_ref, lg_ref,
                  *, ds):
    tm = m_ref.shape[0]
    m = jnp.dot(m_ref[...], wout_ref[...], preferred_element_type=f32)
    h1 = h_ref[...] + gt_ref[0] * m
    h1_ref[...] = h1
    y = h1 * lax.rsqrt(jnp.mean(h1 * h1, -1, keepdims=True) + EPS) * g_ref[...]
    v = y * (1.0 + sc_ref[0]) + sh_ref[0]
    lg_ref[...] = jnp.dot(v, wr_ref[...], preferred_element_type=f32, precision=HIGHEST)
    for j in range(ds):
        v_ref[pl.ds(j, tm, stride=ds), :] = v[:, j * LANES:(j + 1) * LANES]


def _merge_b(merged, h, mod, gain2, w_out, w_router, *, n_rows, tm, tpb, nlat, nb):
    D = h.shape[1]
    E = w_router.shape[1]
    ds = D // LANES
    const = lambda *_: (0, 0)
    return pl.pallas_call(
        functools.partial(_merge_b_body, ds=ds),
        grid=(n_rows // tm,),
        in_specs=[pl.BlockSpec((tm, D), lambda i: (i, 0)),
                  pl.BlockSpec((tm, D), lambda i: (i, 0)),
                  _mod_spec(2, D, tpb, nlat, nb),
                  _mod_spec(4, D, tpb, nlat, nb),
                  _mod_spec(3, D, tpb, nlat, nb),
                  pl.BlockSpec((1, D), const),
                  pl.BlockSpec((D, D), const),
                  pl.BlockSpec((D, E), const)],
        out_specs=[pl.BlockSpec((tm, D), lambda i: (i, 0)),
                   pl.BlockSpec((tm * ds, LANES), lambda i: (i, 0)),
                   pl.BlockSpec((tm, E), lambda i: (i, 0))],
        out_shape=[jax.ShapeDtypeStruct((n_rows, D), f32),
                   jax.ShapeDtypeStruct((n_rows * ds, LANES), f32),
                   jax.ShapeDtypeStruct((n_rows, E), f32)],
        compiler_params=_cparams(("parallel",)),
        name="merge_b",
    )(merged, h, mod, mod, mod, gain2, w_out, w_router)


def _route_body(lg_ref, bias_ref, idx_ref, gate_ref, cnt_ref, carry_scr, *, n_groups):
    i = pl.program_id(0)
    E, tt = lg_ref.shape

    @pl.when(i == 0)
    def _():
        carry_scr[...] = jnp.zeros_like(carry_scr)

    aff = _sigmoid(lg_ref[...])
    sel = aff + bias_ref[...]
    sub = lax.broadcasted_iota(jnp.int32, (GROUP_SIZE, tt), 0)
    neg = -jnp.inf
    best = None
    for g in range(n_groups):
        s = sel[g * GROUP_SIZE:(g + 1) * GROUP_SIZE]
        a = aff[g * GROUP_SIZE:(g + 1) * GROUP_SIZE]
        m1 = jnp.max(s, axis=0, keepdims=True)
        i1 = jnp.min(jnp.where(s == m1, sub, GROUP_SIZE), axis=0, keepdims=True)
        s2 = jnp.where(sub == i1, neg, s)
        m2 = jnp.max(s2, axis=0, keepdims=True)
        i2 = jnp.min(jnp.where(s2 == m2, sub, GROUP_SIZE), axis=0, keepdims=True)
        a1 = jnp.sum(jnp.where(sub == i1, a, 0.0), axis=0, keepdims=True)
        a2 = jnp.sum(jnp.where(sub == i2, a, 0.0), axis=0, keepdims=True)
        cand = (m1 + m2, i1 + g * GROUP_SIZE, i2 + g * GROUP_SIZE, a1, a2)
        if best is None:
            best = cand
        else:
            better = cand[0] > best[0]
            best = tuple(jnp.where(better, cn, bs) for cn, bs in zip(cand, best))
    _, e1, e2, a1, a2 = best
    den = a1 + a2
    gate_ref[0:1, :] = a1 / den
    gate_ref[1:2, :] = a2 / den
    erow = lax.broadcasted_iota(jnp.int32, (E, tt), 0)
    oh1 = erow == e1
    oh2 = erow == e2
    cnt = jnp.where(oh1 | oh2, 1.0, 0.0)
    ti = lax.broadcasted_iota(jnp.int32, (tt, tt), 0)
    tj = lax.broadcasted_iota(jnp.int32, (tt, tt), 1)
    before = jnp.where(ti < tj, 1.0, 0.0).astype(bf16)
    pos = jnp.dot(cnt.astype(bf16), before, preferred_element_type=f32) + carry_scr[...]
    r1 = jnp.sum(jnp.where(oh1, pos, 0.0), axis=0, keepdims=True)
    r2 = jnp.sum(jnp.where(oh2, pos, 0.0), axis=0, keepdims=True)
    idx_ref[0:1, :] = e1
    idx_ref[1:2, :] = e2
    idx_ref[2:3, :] = r1.astype(jnp.int32)
    idx_ref[3:4, :] = r2.astype(jnp.int32)
    carry_scr[...] = carry_scr[...] + jnp.sum(cnt, axis=1, keepdims=True)
    cnt_ref[...] = jnp.broadcast_to(carry_scr[...], cnt_ref.shape)


def _route(logits_t, bias_col, *, tt):
    E, T = logits_t.shape
    return pl.pallas_call(
        functools.partial(_route_body, n_groups=E // GROUP_SIZE),
        grid=(T // tt,),
        in_specs=[pl.BlockSpec((E, tt), lambda i: (0, i)),
                  pl.BlockSpec((E, 1), lambda i: (0, 0))],
        out_specs=[pl.BlockSpec((4, tt), lambda i: (0, i)),
                   pl.BlockSpec((2, tt), lambda i: (0, i)),
                   pl.BlockSpec((E, LANES), lambda i: (0, 0))],
        out_shape=[jax.ShapeDtypeStruct((4, T), jnp.int32),
                   jax.ShapeDtypeStruct((2, T), f32),
                   jax.ShapeDtypeStruct((E, LANES), f32)],
        scratch_shapes=[pltpu.VMEM((E, 1), f32)],
        compiler_params=_cparams(("arbitrary",)),
        name="route",
    )(logits_t, bias_col)


def _row_copy(src, dst, s, t, ds, sem):
    return pltpu.make_async_copy(src.at[pl.ds(s * ds, ds), :], dst.at[pl.ds(t * ds, ds), :], sem)


def _dispatch_body(dest_ref, pad_ref, v_hbm, xs_hbm, zero_scr, sem, zsem, *, tt, ds, n_exp):
    i = pl.program_id(0)

    @pl.when(i == 0)
    def _():
        zero_scr[...] = jnp.zeros_like(zero_scr)

        def per_expert(e, n):
            start = pad_ref[0, e]
            npad = pad_ref[1, e]

            def one(r, m):
                _row_copy(zero_scr, xs_hbm, 0, start + r, ds, zsem).start()
                return m + 1

            return lax.fori_loop(0, npad, one, n)

        total = lax.fori_loop(0, n_exp, per_expert, 0)

        def wait_one(r, m):
            _row_copy(zero_scr, xs_hbm, 0, 0, ds, zsem).wait()
            return m

        lax.fori_loop(0, total, wait_one, 0)

    base = i * tt

    def issue(r, m):
        _row_copy(v_hbm, xs_hbm, base + r, dest_ref[0, 0, r], ds, sem).start()
        _row_copy(v_hbm, xs_hbm, base + r, dest_ref[0, 1, r], ds, sem).start()
        return m

    lax.fori_loop(0, tt, issue, 0)

    def wait(r, m):
        _row_copy(v_hbm, xs_hbm, 0, 0, ds, sem).wait()
        _row_copy(v_hbm, xs_hbm, 0, 0, ds, sem).wait()
        return m

    lax.fori_loop(0, tt, wait, 0)


def _dispatch(dest3, pad_info, v_ts, *, n_tok, n_pad_rows, tt, ds):
    n_exp = pad_info.shape[1]
    return pl.pallas_call(
        functools.partial(_dispatch_body, tt=tt, ds=ds, n_exp=n_exp),
        grid=(n_tok // tt,),
        in_specs=[pl.BlockSpec((1, 2, tt), lambda i: (i, 0, 0), memory_space=pltpu.SMEM),
                  pl.BlockSpec(memory_space=pltpu.SMEM),
                  pl.BlockSpec(memory_space=pl.ANY)],
        out_specs=pl.BlockSpec(memory_space=pl.ANY),
        out_shape=jax.ShapeDtypeStruct((n_pad_rows * ds, LANES), f32),
        scratch_shapes=[pltpu.VMEM((ds, LANES), f32), pltpu.SemaphoreType.DMA(()), pltpu.SemaphoreType.DMA(())],
        compiler_params=_cparams(("arbitrary",)),
        name="dispatch",
    )(dest3, pad_info, v_ts)


def _expert_body(be_ref, nu_ref, x_ref, wg_ref, wu_ref, wd_ref, y_ref, wg_s, wu_s, wd_s, x_s, *, ds, rb):
    b = pl.program_id(0)
    used = b < nu_ref[0]

    @pl.when(used)
    def _():
        prev = be_ref[jnp.maximum(b - 1, 0)]
        changed = jnp.logical_or(b == 0, be_ref[b] != prev)

        @pl.when(changed)
        def _():
            wg_s[...] = wg_ref[0].astype(bf16)
            wu_s[...] = wu_ref[0].astype(bf16)
            wd_s[...] = wd_ref[0].astype(bf16)

        for j in range(ds):
            x_s[:, j * LANES:(j + 1) * LANES] = x_ref[pl.ds(j, rb, stride=ds), :].astype(bf16)
        x = x_s[...]
        hg = jnp.dot(x, wg_s[...], preferred_element_type=f32)
        hu = jnp.dot(x, wu_s[...], preferred_element_type=f32)
        a = (hg * _sigmoid(hg) * hu).astype(bf16)
        y = jnp.dot(a, wd_s[...], preferred_element_type=f32)
        for j in range(ds):
            y_ref[pl.ds(j, rb, stride=ds), :] = y[:, j * LANES:(j + 1) * LANES]

    @pl.when(jnp.logical_not(used))
    def _():
        y_ref[...] = jnp.zeros_like(y_ref)


def _experts(block_expert, n_used, xs_ts, w_g, w_u, w_d, *, ds):
    E, D, De = w_g.shape
    rb = EXPERT_BLOCK
    n_blocks = xs_ts.shape[0] // (rb * ds)
    grid_spec = pltpu.PrefetchScalarGridSpec(
        num_scalar_prefetch=2,
        grid=(n_blocks,),
        in_specs=[pl.BlockSpec((rb * ds, LANES), lambda b, be, nu: (b, 0)),
                  pl.BlockSpec((1, D, De), lambda b, be, nu: (be[b], 0, 0)),
                  pl.BlockSpec((1, D, De), lambda b, be, nu: (be[b], 0, 0)),
                  pl.BlockSpec((1, De, D), lambda b, be, nu: (be[b], 0, 0))],
        out_specs=pl.BlockSpec((rb * ds, LANES), lambda b, be, nu: (b, 0)),
        scratch_shapes=[pltpu.VMEM((D, De), bf16), pltpu.VMEM((D, De), bf16), pltpu.VMEM((De, D), bf16),
                        pltpu.VMEM((rb, D), bf16)],
    )
    return pl.pallas_call(
        functools.partial(_expert_body, ds=ds, rb=rb),
        grid_spec=grid_spec,
        out_shape=jax.ShapeDtypeStruct(xs_ts.shape, f32),
        compiler_params=_cparams(("arbitrary",)),
        name="experts",
    )(block_expert, n_used, xs_ts, w_g, w_u, w_d)


def _combine_body(dest_ref, gate_ref, h_ref, gt_ref, fg_ref, y_hbm, o_ref, y_scr, sem, *, tt, ds, final):

    def issue(r, m):
        _row_copy(y_hbm, y_scr.at[0], dest_ref[0, 0, r], r, ds, sem).start()
        _row_copy(y_hbm, y_scr.at[1], dest_ref[0, 1, r], r, ds, sem).start()
        return m

    lax.fori_loop(0, tt, issue, 0)

    def wait(r, m):
        _row_copy(y_hbm, y_scr.at[0], 0, 0, ds, sem).wait()
        _row_copy(y_hbm, y_scr.at[1], 0, 0, ds, sem).wait()
        return m

    lax.fori_loop(0, tt, wait, 0)
    gt = gt_ref[0]
    g0 = gate_ref[:, 0:1]
    g1 = gate_ref[:, 1:2]
    for j in range(ds):
        sl = slice(j * LANES, (j + 1) * LANES)
        f = g0 * y_scr[0, pl.ds(j, tt, stride=ds), :] + g1 * y_scr[1, pl.ds(j, tt, stride=ds), :]
        o_ref[:, sl] = h_ref[:, sl] + gt[:, sl] * f
    if final:
        x = o_ref[...]
        o_ref[...] = x * lax.rsqrt(jnp.mean(x * x, -1, keepdims=True) + EPS) * fg_ref[...]


def _combine(dest3, gates_t, h1, mod, final_g, y_ts, *, n_tok, tt, ds, tpb, nlat, nb, final):
    D = h1.shape[1]
    return pl.pallas_call(
        functools.partial(_combine_body, tt=tt, ds=ds, final=final),
        grid=(n_tok // tt,),
        in_specs=[pl.BlockSpec((1, 2, tt), lambda i: (i, 0, 0), memory_space=pltpu.SMEM),
                  pl.BlockSpec((tt, 2), lambda i: (i, 0)),
                  pl.BlockSpec((tt, D), lambda i: (i, 0)),
                  _mod_spec(5, D, tpb, nlat, nb),
                  pl.BlockSpec((1, D), lambda i: (0, 0)),
                  pl.BlockSpec(memory_space=pl.ANY)],
        out_specs=pl.BlockSpec((tt, D), lambda i: (i, 0)),
        out_shape=jax.ShapeDtypeStruct((n_tok, D), f32),
        scratch_shapes=[pltpu.VMEM((2, tt * ds, LANES), f32), pltpu.SemaphoreType.DMA(())],
        compiler_params=_cparams(("arbitrary",)),
        name="combine",
    )(dest3, gates_t, h1, mod, final_g, y_ts)


def _moe(v_ts, logits, h1, mod, w_router_bias, w_g, w_u, w_d, final_g, *, n_tok, tt, ds, tpb, nlat, nb, final):
    E = w_g.shape[0]
    route_tt = next(t for t in (512, 256, LANES) if n_tok % t == 0)
    idx, gates, counts = _route(logits[:n_tok].T, w_router_bias.reshape(E, 1), tt=route_tt)
    counts = counts[:, 0].astype(jnp.int32)
    padded = (counts + EXPERT_BLOCK - 1) // EXPERT_BLOCK * EXPERT_BLOCK
    pad_end = jnp.cumsum(padded)
    pad_start = pad_end - padded
    n_blocks = (n_tok * 2 + EXPERT_BLOCK - 1) // EXPERT_BLOCK + E
    n_used = (pad_end[-1] // EXPERT_BLOCK).astype(jnp.int32).reshape(1)
    blk = jnp.arange(n_blocks, dtype=jnp.int32) * EXPERT_BLOCK
    block_expert = jnp.minimum(jnp.searchsorted(pad_end, blk, side="right"), E - 1).astype(jnp.int32)
    last_used = block_expert[jnp.maximum(n_used[0] - 1, 0)]
    block_expert = jnp.where(jnp.arange(n_blocks) < n_used[0], block_expert, last_used)
    dest = pad_start[idx[0:2]] + idx[2:4]
    dest3 = dest.reshape(2, n_tok // tt, tt).transpose(1, 0, 2)
    pad_info = jnp.stack([pad_start + counts, padded - counts]).astype(jnp.int32)
    xs_ts = _dispatch(dest3, pad_info, v_ts, n_tok=n_tok, n_pad_rows=n_blocks * EXPERT_BLOCK, tt=tt, ds=ds)
    y_ts = _experts(block_expert, n_used, xs_ts, w_g, w_u, w_d, ds=ds)
    return _combine(dest3, gates.T, h1, mod, final_g, y_ts, n_tok=n_tok, tt=tt, ds=ds, tpb=tpb, nlat=nlat,
                    nb=nb, final=final)


def kernel(x, c, ctx, c_ctx, w_mod, b_mod, norm1_g, norm2_g, w_in, conv_w, a_log, dt_bias, onorm_g, w_pool, pool_scale, w_up_dn, w_up_pool, w_out, w_router, router_bias, w_gate_e, w_up_e, w_down_e, final_g):
    B, S, D = x.shape
    C = ctx.shape[1]
    L = w_mod.shape[0]
    H = a_log.shape[-1]
    dk = onorm_g.shape[-1]
    assert dk == LANES
    dn_w = H * dk
    _, ng, gw, _ = w_pool.shape
    pool_w = ng * gw
    ds = D // LANES
    n_lat = B * S
    T = n_lat + B * C
    tm = 256
    assert S % tm == 0 and (B * C) % tm == 0 and C % CHUNK == 0 and S % (4 * GRID_W) == 0
    tpb = S // tm
    nlat_tiles = n_lat // tm
    tr = min(256, C)
    G = min(H, 8)

    c8 = jnp.zeros((8, D), f32).at[:B].set(c).at[B].set(c_ctx)
    mod = _mod_table(c8, w_mod, b_mod)

    qkv_c = 3 * dn_w
    scan_c = qkv_c + 4 * H
    og_off, pool_off = scan_c, scan_c + dn_w
    mg_off = pool_off + pool_w
    assert (4 * dn_w) % D == 0 and (4 * dn_w + 2 * D) % pool_w == 0
    pool_off_blocks = (4 * dn_w + 2 * D) // pool_w

    h = jnp.concatenate([x.reshape(n_lat, D), ctx.reshape(B * C, D)], 0)
    out = None
    for l in range(L):
        last = l == L - 1
        mod_l = mod[l].reshape(8, 1, mod.shape[-1])
        wl = w_in[l]
        w_z = jnp.concatenate([wl[:, :qkv_c], wl[:, og_off:pool_off], wl[:, mg_off:], wl[:, pool_off:mg_off]],
                              1).astype(bf16)
        w_ba = jnp.pad(wl[:, qkv_c:scan_c], ((0, 0), (0, LANES - 4 * H))).astype(bf16)
        tn = 1024 if w_z.shape[1] % 1024 == 0 else w_z.shape[1]
        z, ba = _in_proj(h, mod_l, 0, norm1_g[l].reshape(1, D), w_z, w_ba, tm=tm, tn=tn, tpb=tpb,
                         nlat=nlat_tiles, nb=B)
        qkv = _conv_prep(z, conv_w[l], tr=tr, tc=min(512, dn_w), S=S, C=C, nlat_rows=n_lat, dn_w=dn_w, dk=dk)
        o_dn = _deltanet(qkv, ba, a_log[l], dt_bias[l], B=B, S=S, C=C, H=H, G=G)
        pooled = _pool(z, B=B, S=S, C=C, pool_off_blocks=pool_off_blocks, pool_w=pool_w, gw=gw)
        n_rows = n_lat if last else T
        merged = _merge_a(o_dn, z, pooled, onorm_g[l].reshape(1, dk), w_pool[l].astype(bf16),
                          pool_scale[l].reshape(1, pool_w), w_up_dn[l].astype(bf16), w_up_pool[l].astype(bf16),
                          n_rows=n_rows, tm=tm, H=H, D=D)
        h1, v_ts, logits = _merge_b(merged, h, mod_l, norm2_g[l].reshape(1, D), w_out[l].astype(bf16), w_router,
                                    n_rows=n_rows, tm=tm, tpb=tpb, nlat=nlat_tiles, nb=B)
        h = _moe(v_ts, logits, h1, mod_l, router_bias, w_gate_e[l], w_up_e[l], w_down_e[l],
                 final_g.reshape(1, D), n_tok=n_rows, tt=tm, ds=ds, tpb=tpb, nlat=nlat_tiles, nb=B, final=last)
    return h.reshape(B, S, D)
```

```python
import functools

import jax
import jax.numpy as jnp
from jax import lax
from jax.experimental import pallas as pl
from jax.experimental.pallas import tpu as pltpu

f32 = jnp.float32
bf16 = jnp.bfloat16
HIGHEST = lax.Precision.HIGHEST

EPS = 1e-6
LANES = 128
GRID_W = 64
CHUNK = 64
POOL_WINDOWS = (2, 4, 8, 16)
GROUP_SIZE = 8
EXPERT_BLOCK = 128
VMEM_LIMIT = 56 * 1024 * 1024


def _cparams(sem, vmem=VMEM_LIMIT):
    return pltpu.CompilerParams(dimension_semantics=sem, vmem_limit_bytes=vmem)


def _sigmoid(x):
    return 1.0 / (1.0 + jnp.exp(-x))


def _softplus(x):
    return jnp.maximum(x, 0.0) + jnp.log(1.0 + jnp.exp(-jnp.abs(x)))


def _mod_spec(sec, D, tiles_per_batch, n_lat_tiles, n_batch):
    def index(i, *_):
        return (jnp.where(i < n_lat_tiles, i // tiles_per_batch, n_batch), 0, sec)

    return pl.BlockSpec((1, 1, D), index)


def _mod_body(c_ref, w_ref, b_ref, o_ref):
    c = c_ref[...]
    s = c * _sigmoid(c)
    o_ref[0] = jnp.dot(s, w_ref[0], preferred_element_type=f32, precision=HIGHEST) + b_ref[0]


def _mod_table(c8, w_mod, b_mod):
    L, D, N = w_mod.shape
    tn = next(t for t in (1024, 512, 256, LANES) if N % t == 0)
    return pl.pallas_call(
        _mod_body,
        grid=(L, N // tn),
        in_specs=[pl.BlockSpec((8, D), lambda l, j: (0, 0)),
                  pl.BlockSpec((1, D, tn), lambda l, j: (l, 0, j)),
                  pl.BlockSpec((1, 1, tn), lambda l, j: (l, 0, j))],
        out_specs=pl.BlockSpec((1, 8, tn), lambda l, j: (l, 0, j)),
        out_shape=jax.ShapeDtypeStruct((L, 8, N), f32),
        compiler_params=_cparams(("parallel", "parallel")),
        name="mod_table",
    )(c8, w_mod, b_mod.reshape(L, 1, N))


def _inproj_body(h_ref, sc_ref, sh_ref, g_ref, w_ref, wba_ref, z_ref, ba_ref, u_scr):
    j = pl.program_id(1)

    @pl.when(j == 0)
    def _():
        x = h_ref[...]
        y = x * lax.rsqrt(jnp.mean(x * x, -1, keepdims=True) + EPS) * g_ref[...]
        u = (y * (1.0 + sc_ref[0]) + sh_ref[0]).astype(bf16)
        u_scr[...] = u
        ba_ref[...] = jnp.dot(u, wba_ref[...], preferred_element_type=f32)

    z_ref[...] = jnp.dot(u_scr[...], w_ref[...], preferred_element_type=f32).astype(z_ref.dtype)


def _in_proj(h, mod, sec, gain, w_z, w_ba, *, tm, tn, tpb, nlat, nb):
    T, D = h.shape
    Nz = w_z.shape[1]
    return pl.pallas_call(
        _inproj_body,
        grid=(T // tm, Nz // tn),
        in_specs=[pl.BlockSpec((tm, D), lambda i, j: (i, 0)),
                  _mod_spec(sec + 1, D, tpb, nlat, nb),
                  _mod_spec(sec, D, tpb, nlat, nb),
                  pl.BlockSpec((1, D), lambda i, j: (0, 0)),
                  pl.BlockSpec((D, tn), lambda i, j: (0, j)),
                  pl.BlockSpec((D, LANES), lambda i, j: (0, 0))],
        out_specs=[pl.BlockSpec((tm, tn), lambda i, j: (i, j)),
                   pl.BlockSpec((tm, LANES), lambda i, j: (i, 0))],
        out_shape=[jax.ShapeDtypeStruct((T, Nz), bf16), jax.ShapeDtypeStruct((T, LANES), f32)],
        scratch_shapes=[pltpu.VMEM((tm, D), bf16)],
        compiler_params=_cparams(("parallel", "arbitrary")),
        name="in_proj",
    )(h, mod, mod, gain, w_z, w_ba)


def _conv_body(x_ref, xp_ref, xn_ref, w_ref, o_ref, *, tr, tps_lat, tps_ctx, nlat, qscale, tiles_per_sec, halo):
    i = pl.program_id(0)
    j = pl.program_id(1)
    il = jnp.where(i < nlat, i % tps_lat, (i - nlat) % tps_ctx)
    nt = jnp.where(i < nlat, tps_lat, tps_ctx)
    first = il == 0
    last = il == nt - 1
    x = x_ref[...].astype(f32)
    p = jnp.where(first, 0.0, xp_ref[...].astype(f32))
    n = jnp.where(last, 0.0, xn_ref[...].astype(f32))
    row = lax.broadcasted_iota(jnp.int32, x.shape, 0)
    xm1 = jnp.where(row == 0, p[halo - 1:halo], pltpu.roll(x, 1, 0))
    xm2 = jnp.where(row == 0, p[halo - 2:halo - 1], jnp.where(row == 1, p[halo - 1:halo], pltpu.roll(x, 2, 0)))
    xp1 = jnp.where(row == tr - 1, n[0:1], pltpu.roll(x, tr - 1, 0))
    xp2 = jnp.where(row == tr - 2, n[0:1], jnp.where(row == tr - 1, n[1:2], pltpu.roll(x, tr - 2, 0)))
    w = w_ref[...]
    acc = xm2 * w[0:1] + xm1 * w[1:2] + x * w[2:3] + xp1 * w[3:4] + xp2 * w[4:5]
    y = acc * _sigmoid(acc)
    sec = j // tiles_per_sec
    tc = x.shape[1]
    for s in range(tc // LANES):
        ys = y[:, s * LANES:(s + 1) * LANES]
        rs = lax.rsqrt(jnp.sum(ys * ys, -1, keepdims=True) + EPS)
        scale = jnp.where(sec == 2, 1.0, rs) * jnp.where(sec == 0, qscale, 1.0)
        o_ref[:, s * LANES:(s + 1) * LANES] = (ys * scale).astype(o_ref.dtype)


def _conv_prep(z, conv_w, *, tr, tc, S, C, nlat_rows, dn_w, dk):
    T = z.shape[0]
    halo = 16
    hb = tr // halo
    nh = T // halo
    body = functools.partial(_conv_body, tr=tr, tps_lat=S // tr, tps_ctx=C // tr, nlat=nlat_rows // tr,
                             qscale=float(dk) ** -0.5, tiles_per_sec=dn_w // tc, halo=halo)
    return pl.pallas_call(
        body,
        grid=(T // tr, 3 * dn_w // tc),
        in_specs=[pl.BlockSpec((tr, tc), lambda i, j: (i, j)),
                  pl.BlockSpec((halo, tc), lambda i, j: (jnp.maximum(i * hb - 1, 0), j)),
                  pl.BlockSpec((halo, tc), lambda i, j: (jnp.minimum((i + 1) * hb, nh - 1), j)),
                  pl.BlockSpec((conv_w.shape[0], tc), lambda i, j: (0, j))],
        out_specs=pl.BlockSpec((tr, tc), lambda i, j: (i, j)),
        out_shape=jax.ShapeDtypeStruct((T, 3 * dn_w), bf16),
        compiler_params=_cparams(("parallel", "parallel")),
        name="conv_prep",
    )(z, z, z, conv_w)


def _mm(x, y):
    return jnp.dot(x.astype(bf16), y.astype(bf16), preferred_element_type=f32)


def _unit_tri_inverse(a_list, ri, ci):
    b16 = (ri >> 4) == (ci >> 4)
    b32 = (ri >> 5) == (ci >> 5)
    eye = jnp.where(ri == ci, 1.0, 0.0)
    a16 = [jnp.where(b16, a, 0.0) for a in a_list]
    t = [eye - x for x in a16]
    p = [_mm(x, x) for x in a16]
    for step in range(3):
        t = [ti + _mm(ti, pi) for ti, pi in zip(t, p)]
        if step < 2:
            p = [_mm(pi, pi) for pi in p]
    a32 = [jnp.where(b32, a, 0.0) - x for a, x in zip(a_list, a16)]
    ta = [_mm(ti, ai) for ti, ai in zip(t, a32)]
    t = [ti - _mm(tai, ti) for ti, tai in zip(t, ta)]
    a64 = [jnp.where(b32, 0.0, a) for a in a_list]
    ta = [_mm(ti, ai) for ti, ai in zip(t, a64)]
    t = [ti - _mm(tai, ti) for ti, tai in zip(t, ta)]
    return t


def _dn_body(q_ref, k_ref, v_ref, ba_ref, al_ref, dt_ref, o_ref, s_scr, *, G):
    d = pl.program_id(1)
    c = pl.program_id(3)
    dk = LANES
    hs = range(G)

    @pl.when(c == 0)
    def _():
        s_scr[...] = jnp.zeros_like(s_scr)

    ba = ba_ref[0]
    beta_t = _sigmoid(ba)
    g_t = -jnp.exp(al_ref[0]) * _softplus(ba + dt_ref[0])
    ri = lax.broadcasted_iota(jnp.int32, (CHUNK, CHUNK), 0)
    ci = lax.broadcasted_iota(jnp.int32, (CHUNK, CHUNK), 1)
    fwd = d == 0
    later = jnp.where(fwd, ri, ci)
    earlier = jnp.where(fwd, ci, ri)
    incl = later >= earlier
    strict = later > earlier
    tri = jnp.where(incl, 1.0, 0.0)
    gc = jnp.dot(tri, g_t, preferred_element_type=f32, precision=HIGHEST)
    gct = gc.T
    gtot = jnp.sum(g_t, axis=0, keepdims=True)

    def pick(x, col):
        return jnp.where(fwd, x[:, col:col + 1], x[:, G + col:G + col + 1])

    sl = [slice(h * dk, (h + 1) * dk) for h in hs]
    beta_c = [pick(beta_t, h) for h in hs]
    gc_c = [pick(gc, 2 * G + h) for h in hs]
    gc_r = [jnp.where(fwd, gct[2 * G + h:2 * G + h + 1], gct[3 * G + h:3 * G + h + 1]) for h in hs]
    gt = [pick(gtot, 2 * G + h) for h in hs]
    q = [q_ref[:, sl[h]].astype(f32) for h in hs]
    k = [k_ref[:, sl[h]].astype(f32) for h in hs]
    kb = [k[h] * beta_c[h] for h in hs]
    eg = [jnp.exp(gc_c[h]) for h in hs]
    dec = [jnp.where(incl, jnp.exp(jnp.where(incl, gc_c[h] - gc_r[h], 0.0)), 0.0) for h in hs]
    kq = [lax.dot_general(jnp.concatenate([kb[h], q[h]], axis=0).astype(bf16), k[h].astype(bf16),
                          (((1,), (1,)), ((), ())), preferred_element_type=f32) for h in hs]
    a = [jnp.where(strict, kq[h][:CHUNK] * dec[h], 0.0) for h in hs]
    qk = [(kq[h][CHUNK:] * dec[h]).astype(bf16) for h in hs]
    t = _unit_tri_inverse(a, ri, ci)
    rhs = [jnp.concatenate([v_ref[:, sl[h]].astype(f32) * beta_c[h], kb[h] * eg[h]], axis=1) for h in hs]
    sol = [_mm(t[h], rhs[h]) for h in hs]
    s = [s_scr[h] for h in hs]
    r = [_mm(jnp.concatenate([sol[h][:, dk:], q[h] * eg[h]], axis=0), s[h]) for h in hs]
    vnb = [(sol[h][:, :dk] - r[h][:CHUNK]).astype(bf16) for h in hs]
    o = [r[h][CHUNK:] + jnp.dot(qk[h], vnb[h], preferred_element_type=f32) for h in hs]
    kd = [(k[h] * jnp.exp(gt[h] - gc_c[h])).astype(bf16) for h in hs]
    upd = [lax.dot_general(kd[h], vnb[h], (((0,), (0,)), ((), ())), preferred_element_type=f32) for h in hs]
    for h in hs:
        s_scr[h] = s[h] * jnp.exp(gt[h]) + upd[h]
        o_ref[0, :, sl[h]] = o[h].astype(o_ref.dtype)


def _deltanet(qkv, ba, a_log, dt_bias, *, B, S, C, H, G):
    T = qkv.shape[0]
    dn_w = H * LANES
    ncl = S // CHUNK
    ncc = C // CHUNK
    nc = ncl + ncc
    hgs = H // G

    def rowblk(b, d, c):
        cc = jnp.where(d == 0, c, ncc - 1 - c)
        cl = jnp.where(d == 0, c - ncc, nc - 1 - c)
        return jnp.where(c < ncc, B * ncl + b * ncc + cc, b * ncl + cl)

    def grouped(t):
        r = t.shape[0]
        t = t.reshape(r, 4, hgs, G).transpose(2, 0, 1, 3).reshape(hgs, r, 4 * G)
        return jnp.pad(t, ((0, 0), (0, 0), (0, LANES - 4 * G)))

    zeros = jnp.zeros((1, 2 * H), f32)
    ba_g = grouped(ba[:, :4 * H])
    al_g = grouped(jnp.concatenate([zeros, a_log.reshape(1, 2 * H)], 1))
    dt_g = grouped(jnp.concatenate([zeros, dt_bias.reshape(1, 2 * H)], 1))
    body = functools.partial(_dn_body, G=G)
    return pl.pallas_call(
        body,
        grid=(B, 2, hgs, nc),
        in_specs=[pl.BlockSpec((CHUNK, G * LANES), lambda b, d, g, c: (rowblk(b, d, c), g)),
                  pl.BlockSpec((CHUNK, G * LANES), lambda b, d, g, c: (rowblk(b, d, c), hgs + g)),
                  pl.BlockSpec((CHUNK, G * LANES), lambda b, d, g, c: (rowblk(b, d, c), 2 * hgs + g)),
                  pl.BlockSpec((1, CHUNK, LANES), lambda b, d, g, c: (g, rowblk(b, d, c), 0)),
                  pl.BlockSpec((1, 1, LANES), lambda b, d, g, c: (g, 0, 0)),
                  pl.BlockSpec((1, 1, LANES), lambda b, d, g, c: (g, 0, 0))],
        out_specs=pl.BlockSpec((1, CHUNK, G * LANES), lambda b, d, g, c: (d, rowblk(b, d, c), g)),
        out_shape=jax.ShapeDtypeStruct((2, T, dn_w), bf16),
        scratch_shapes=[pltpu.VMEM((G, LANES, LANES), f32)],
        compiler_params=_cparams(("parallel", "parallel", "parallel", "arbitrary")),
        name="deltanet",
    )(qkv, qkv, qkv, ba_g, al_g, dt_g)


def _pool_ctx_body(x_ref, o_ref, *, gw):
    C = x_ref.shape[0]
    ri = lax.broadcasted_iota(jnp.int32, (C, C), 0)
    ci = lax.broadcasted_iota(jnp.int32, (C, C), 1)
    t = lax.broadcasted_iota(jnp.int32, (C, 1), 0)
    for gi, w in enumerate(POOL_WINDOWS):
        sl = slice(gi * gw, (gi + 1) * gw)
        band = jnp.where((ci >= ri - w // 2) & (ci < ri + w - w // 2), 1.0, 0.0).astype(bf16)
        cnt = (jnp.minimum(t + w - w // 2, C) - jnp.maximum(t - w // 2, 0)).astype(f32)
        x = x_ref[:, sl]
        box = jnp.dot(band, x, preferred_element_type=f32)
        o_ref[:, sl] = (box / cnt - x.astype(f32)).astype(o_ref.dtype)


def _pool_lat_body(x_ref, o_ref, y_scr):
    g = pl.program_id(1)
    S = x_ref.shape[0]
    rows = S // GRID_W
    sb = 4 * GRID_W
    ri = lax.broadcasted_iota(jnp.int32, (sb, sb), 0)
    ci = lax.broadcasted_iota(jnp.int32, (sb, sb), 1)
    cr = ri & (GRID_W - 1)
    cc = ci & (GRID_W - 1)
    same = (ri >> 6) == (ci >> 6)
    col = lax.broadcasted_iota(jnp.int32, (GRID_W, 1), 0)
    for gi, w in enumerate(POOL_WINDOWS):
        @pl.when(g == gi)
        def _(w=w):
            band = jnp.where(same & (cc >= cr - w // 2) & (cc < cr + w - w // 2), 1.0, 0.0).astype(bf16)
            for s in range(S // sb):
                y_scr[s * sb:(s + 1) * sb, :] = jnp.dot(band, x_ref[s * sb:(s + 1) * sb, :],
                                                        preferred_element_type=f32)
            ncol = (jnp.minimum(col + w - w // 2, GRID_W) - jnp.maximum(col - w // 2, 0)).astype(f32)
            for r in range(rows):
                lo = max(r - w // 2, 0)
                hi = min(r + w - w // 2, rows)
                acc = y_scr[lo * GRID_W:(lo + 1) * GRID_W, :]
                for rr in range(lo + 1, hi):
                    acc = acc + y_scr[rr * GRID_W:(rr + 1) * GRID_W, :]
                cnt = ncol * float(hi - lo)
                xs = x_ref[r * GRID_W:(r + 1) * GRID_W, :].astype(f32)
                o_ref[r * GRID_W:(r + 1) * GRID_W, :] = (acc / cnt - xs).astype(o_ref.dtype)


def _pool(z, *, B, S, C, pool_off_blocks, pool_w, gw):
    nlat_c = B * S // C
    ng = pool_w // gw
    assert ng == len(POOL_WINDOWS)
    pooled_ctx = pl.pallas_call(
        functools.partial(_pool_ctx_body, gw=gw),
        grid=(B,),
        in_specs=[pl.BlockSpec((C, pool_w), lambda b: (nlat_c + b, pool_off_blocks))],
        out_specs=pl.BlockSpec((C, pool_w), lambda b: (b, 0)),
        out_shape=jax.ShapeDtypeStruct((B * C, pool_w), bf16),
        compiler_params=_cparams(("parallel",)),
        name="pool_ctx",
    )(z)
    off = pool_off_blocks * ng
    pooled_lat = pl.pallas_call(
        _pool_lat_body,
        grid=(B, ng),
        in_specs=[pl.BlockSpec((S, gw), lambda b, g: (b, off + g))],
        out_specs=pl.BlockSpec((S, gw), lambda b, g: (b, g)),
        out_shape=jax.ShapeDtypeStruct((B * S, pool_w), bf16),
        scratch_shapes=[pltpu.VMEM((S, gw), f32)],
        compiler_params=_cparams(("parallel", "parallel")),
        name="pool_lat",
    )(z)
    return pooled_lat, pooled_ctx


def _merge_a_body(o_ref, og_ref, m1_ref, m2_ref, pl_ref, pc_ref, ong_ref, wpool_ref, ps_ref, wud_ref, wup_ref,
                  out_ref, a_scr, p_scr, *, H, gw, nlat):
    is_lat = pl.program_id(0) < nlat
    for h in range(H):
        sl = slice(h * LANES, (h + 1) * LANES)
        o = o_ref[0, :, sl].astype(f32) + o_ref[1, :, sl].astype(f32)
        og = og_ref[:, sl].astype(f32)
        y = o * lax.rsqrt(jnp.mean(o * o, -1, keepdims=True) + EPS) * ong_ref[...] * (og * _sigmoid(og))
        a_scr[:, sl] = y.astype(bf16)
    y_dn = jnp.dot(a_scr[...], wud_ref[...], preferred_element_type=f32)
    for g in range(wpool_ref.shape[0]):
        sl = slice(g * gw, (g + 1) * gw)
        pooled = jnp.where(is_lat, pl_ref[:, sl], pc_ref[:, sl])
        yp = jnp.dot(pooled, wpool_ref[g], preferred_element_type=f32) * ps_ref[:, sl]
        p_scr[:, sl] = yp.astype(bf16)
    y_pool = jnp.dot(p_scr[...], wup_ref[...], preferred_element_type=f32)
    g1 = _sigmoid(m1_ref[...].astype(f32))
    g2 = _sigmoid(m2_ref[...].astype(f32))
    out_ref[...] = (g1 * y_dn + g2 * y_pool).astype(out_ref.dtype)


def _merge_a(o_dn, z, pooled_lat, pooled_ctx, onorm_g, w_pool, pool_scale, w_up_dn, w_up_pool, *, n_rows, tm, H, D):
    dn_w = H * LANES
    pool_w = pooled_lat.shape[1]
    nlat = pooled_lat.shape[0] // tm
    ng, gw, _ = w_pool.shape
    og_blk = 3
    m_blk = 4 * dn_w // D
    const = lambda *_: (0, 0)
    return pl.pallas_call(
        functools.partial(_merge_a_body, H=H, gw=gw, nlat=nlat),
        grid=(n_rows // tm,),
        in_specs=[pl.BlockSpec((2, tm, dn_w), lambda i: (0, i, 0)),
                  pl.BlockSpec((tm, dn_w), lambda i: (i, og_blk)),
                  pl.BlockSpec((tm, D), lambda i: (i, m_blk)),
                  pl.BlockSpec((tm, D), lambda i: (i, m_blk + 1)),
                  pl.BlockSpec((tm, pool_w), lambda i: (jnp.minimum(i, nlat - 1), 0)),
                  pl.BlockSpec((tm, pool_w), lambda i: (jnp.maximum(i - nlat, 0), 0)),
                  pl.BlockSpec((1, LANES), const),
                  pl.BlockSpec((ng, gw, gw), lambda i: (0, 0, 0)),
                  pl.BlockSpec((1, pool_w), const),
                  pl.BlockSpec((dn_w, D), const),
                  pl.BlockSpec((pool_w, D), const)],
        out_specs=pl.BlockSpec((tm, D), lambda i: (i, 0)),
        out_shape=jax.ShapeDtypeStruct((n_rows, D), bf16),
        scratch_shapes=[pltpu.VMEM((tm, dn_w), bf16), pltpu.VMEM((tm, pool_w), bf16)],
        compiler_params=_cparams(("parallel",)),
        name="merge_a",
    )(o_dn, z, z, z, pooled_lat, pooled_ctx, onorm_g, w_pool, pool_scale, w_up_dn, w_up_pool)


def _merge_b_body(m_ref, h_ref, gt_ref, sc_ref, sh_ref, g_ref, wout_ref, wr_ref, h1_ref, v_ref, lg_ref,
                  *, ds):
    tm = m_ref.shape[0]
    m = jnp.dot(m_ref[...], wout_ref[...], preferred_element_type=f32)
    h1 = h_ref[...] + gt_ref[0] * m
    h1_ref[...] = h1
    y = h1 * lax.rsqrt(jnp.mean(h1 * h1, -1, keepdims=True) + EPS) * g_ref[...]
    v = y * (1.0 + sc_ref[0]) + sh_ref[0]
    lg_ref[...] = jnp.dot(v, wr_ref[...], preferred_element_type=f32, precision=HIGHEST)
    for j in range(ds):
        v_ref[pl.ds(j, tm, stride=ds), :] = v[:, j * LANES:(j + 1) * LANES]


def _merge_b(merged, h, mod, gain2, w_out, w_router, *, n_rows, tm, tpb, nlat, nb):
    D = h.shape[1]
    E = w_router.shape[1]
    ds = D // LANES
    const = lambda *_: (0, 0)
    return pl.pallas_call(
        functools.partial(_merge_b_body, ds=ds),
        grid=(n_rows // tm,),
        in_specs=[pl.BlockSpec((tm, D), lambda i: (i, 0)),
                  pl.BlockSpec((tm, D), lambda i: (i, 0)),
                  _mod_spec(2, D, tpb, nlat, nb),
                  _mod_spec(4, D, tpb, nlat, nb),
                  _mod_spec(3, D, tpb, nlat, nb),
                  pl.BlockSpec((1, D), const),
                  pl.BlockSpec((D, D), const),
                  pl.BlockSpec((D, E), const)],
        out_specs=[pl.BlockSpec((tm, D), lambda i: (i, 0)),
                   pl.BlockSpec((tm * ds, LANES), lambda i: (i, 0)),
                   pl.BlockSpec((tm, E), lambda i: (i, 0))],
        out_shape=[jax.ShapeDtypeStruct((n_rows, D), f32),
                   jax.ShapeDtypeStruct((n_rows * ds, LANES), f32),
                   jax.ShapeDtypeStruct((n_rows, E), f32)],
        compiler_params=_cparams(("parallel",)),
        name="merge_b",
    )(merged, h, mod, mod, mod, gain2, w_out, w_router)


def _route_body(lg_ref, bias_ref, idx_ref, gate_ref, cnt_ref, carry_scr, *, n_groups):
    i = pl.program_id(0)
    E, tt = lg_ref.shape

    @pl.when(i == 0)
    def _():
        carry_scr[...] = jnp.zeros_like(carry_scr)

    aff = _sigmoid(lg_ref[...])
    sel = aff + bias_ref[...]
    sub = lax.broadcasted_iota(jnp.int32, (GROUP_SIZE, tt), 0)
    neg = -1e30
    best = None
    for g in range(n_groups):
        s = sel[g * GROUP_SIZE:(g + 1) * GROUP_SIZE]
        a = aff[g * GROUP_SIZE:(g + 1) * GROUP_SIZE]
        m1 = jnp.max(s, axis=0, keepdims=True)
        i1 = jnp.min(jnp.where(s == m1, sub, GROUP_SIZE), axis=0, keepdims=True)
        s2 = jnp.where(sub == i1, neg, s)
        m2 = jnp.max(s2, axis=0, keepdims=True)
        i2 = jnp.min(jnp.where(s2 == m2, sub, GROUP_SIZE), axis=0, keepdims=True)
        a1 = jnp.sum(jnp.where(sub == i1, a, 0.0), axis=0, keepdims=True)
        a2 = jnp.sum(jnp.where(sub == i2, a, 0.0), axis=0, keepdims=True)
        cand = (m1 + m2, i1 + g * GROUP_SIZE, i2 + g * GROUP_SIZE, a1, a2)
        if best is None:
            best = cand
        else:
            better = cand[0] > best[0]
            best = tuple(jnp.where(better, cn, bs) for cn, bs in zip(cand, best))
    _, e1, e2, a1, a2 = best
    den = a1 + a2
    gate_ref[0:1, :] = a1 / den
    gate_ref[1:2, :] = a2 / den
    erow = lax.broadcasted_iota(jnp.int32, (E, tt), 0)
    oh1 = erow == e1
    oh2 = erow == e2
    cnt = jnp.where(oh1 | oh2, 1.0, 0.0)
    ti = lax.broadcasted_iota(jnp.int32, (tt, tt), 0)
    tj = lax.broadcasted_iota(jnp.int32, (tt, tt), 1)
    before = jnp.where(ti < tj, 1.0, 0.0).astype(bf16)
    pos = jnp.dot(cnt.astype(bf16), before, preferred_element_type=f32) + carry_scr[...]
    r1 = jnp.sum(jnp.where(oh1, pos, 0.0), axis=0, keepdims=True)
    r2 = jnp.sum(jnp.where(oh2, pos, 0.0), axis=0, keepdims=True)
    idx_ref[0:1, :] = e1
    idx_ref[1:2, :] = e2
    idx_ref[2:3, :] = r1.astype(jnp.int32)
    idx_ref[3:4, :] = r2.astype(jnp.int32)
    carry_scr[...] = carry_scr[...] + jnp.sum(cnt, axis=1, keepdims=True)
    cnt_ref[...] = jnp.broadcast_to(carry_scr[...], cnt_ref.shape)


def _route(logits_t, bias_col, *, tt):
    E, T = logits_t.shape
    return pl.pallas_call(
        functools.partial(_route_body, n_groups=E // GROUP_SIZE),
        grid=(T // tt,),
        in_specs=[pl.BlockSpec((E, tt), lambda i: (0, i)),
                  pl.BlockSpec((E, 1), lambda i: (0, 0))],
        out_specs=[pl.BlockSpec((4, tt), lambda i: (0, i)),
                   pl.BlockSpec((2, tt), lambda i: (0, i)),
                   pl.BlockSpec((E, LANES), lambda i: (0, 0))],
        out_shape=[jax.ShapeDtypeStruct((4, T), jnp.int32),
                   jax.ShapeDtypeStruct((2, T), f32),
                   jax.ShapeDtypeStruct((E, LANES), f32)],
        scratch_shapes=[pltpu.VMEM((E, 1), f32)],
        compiler_params=_cparams(("arbitrary",)),
        name="route",
    )(logits_t, bias_col)


def _row_copy(src, dst, s, t, n, sem):
    def rows(x):
        return pl.ds(x * n if isinstance(x, int) else pl.multiple_of(x * n, 8), n)

    return pltpu.make_async_copy(src.at[rows(s), :], dst.at[rows(t), :], sem)


def _dispatch_body(dest_ref, pad_ref, v_ref, xs_hbm, zero_scr, sem, zsem, *, tt, ds, n_exp, n_blocks, rb):
    i = pl.program_id(0)

    @pl.when(i == 0)
    def _():
        zero_scr[...] = jnp.zeros_like(zero_scr)

        def per_expert(e, n):
            start = pad_ref[0, e]

            def one(r, m):
                _row_copy(zero_scr, xs_hbm, 0, start + r, ds, zsem).start()
                return m + 1

            return lax.fori_loop(0, pad_ref[1, e], one, n)

        total = lax.fori_loop(0, n_exp, per_expert, 0)

        def wait_one(r, m):
            _row_copy(zero_scr, xs_hbm, 0, 0, ds, zsem).wait()
            return m

        lax.fori_loop(0, total, wait_one, 0)
        n_used = pad_ref[2, 0]

        def blk_start(bk, m):
            _row_copy(zero_scr, xs_hbm, 0, bk, rb * ds, zsem).start()
            return m

        lax.fori_loop(n_used, n_blocks, blk_start, 0)

        def blk_wait(bk, m):
            _row_copy(zero_scr, xs_hbm, 0, 0, rb * ds, zsem).wait()
            return m

        lax.fori_loop(n_used, n_blocks, blk_wait, 0)

    def issue(r, m):
        src = v_ref.at[pl.ds(pl.multiple_of(r * ds, ds), ds), :]
        for kk in range(2):
            dst = xs_hbm.at[pl.ds(pl.multiple_of(dest_ref[0, kk, r] * ds, ds), ds), :]
            pltpu.make_async_copy(src, dst, sem).start()
        return m

    lax.fori_loop(0, tt, issue, 0, unroll=8)
    for kk in range(2):
        pltpu.make_async_copy(v_ref, xs_hbm.at[pl.ds(0, tt * ds), :], sem).wait()


def _dispatch(dest3, pad_info, v_ts, *, n_tok, n_blocks, tt, ds):
    n_exp = pad_info.shape[1]
    rb = EXPERT_BLOCK
    return pl.pallas_call(
        functools.partial(_dispatch_body, tt=tt, ds=ds, n_exp=n_exp, n_blocks=n_blocks, rb=rb),
        grid=(n_tok // tt,),
        in_specs=[pl.BlockSpec((1, 2, tt), lambda i: (i, 0, 0), memory_space=pltpu.SMEM),
                  pl.BlockSpec(memory_space=pltpu.SMEM),
                  pl.BlockSpec((tt * ds, LANES), lambda i: (i, 0))],
        out_specs=pl.BlockSpec(memory_space=pl.ANY),
        out_shape=jax.ShapeDtypeStruct((n_blocks * rb * ds, LANES), f32),
        scratch_shapes=[pltpu.VMEM((rb * ds, LANES), f32), pltpu.SemaphoreType.DMA(()), pltpu.SemaphoreType.DMA(())],
        compiler_params=_cparams(("arbitrary",)),
        name="dispatch",
    )(dest3, pad_info, v_ts)


def _expert_body(be_ref, nu_ref, x_ref, wg_ref, wu_ref, wd_ref, y_ref, wg_s, wu_s, wd_s, x_s, *, ds, rb):
    b = pl.program_id(0)
    used = b < nu_ref[0]

    @pl.when(used)
    def _():
        prev = be_ref[jnp.maximum(b - 1, 0)]
        changed = jnp.logical_or(b == 0, be_ref[b] != prev)

        @pl.when(changed)
        def _():
            wg_s[...] = wg_ref[0].astype(bf16)
            wu_s[...] = wu_ref[0].astype(bf16)
            wd_s[...] = wd_ref[0].astype(bf16)

        for j in range(ds):
            x_s[:, j * LANES:(j + 1) * LANES] = x_ref[pl.ds(j, rb, stride=ds), :].astype(bf16)
        x = x_s[...]
        hg = jnp.dot(x, wg_s[...], preferred_element_type=f32)
        hu = jnp.dot(x, wu_s[...], preferred_element_type=f32)
        a = (hg * _sigmoid(hg) * hu).astype(bf16)
        y = jnp.dot(a, wd_s[...], preferred_element_type=f32)
        for j in range(ds):
            y_ref[pl.ds(j, rb, stride=ds), :] = y[:, j * LANES:(j + 1) * LANES]

    @pl.when(jnp.logical_not(used))
    def _():
        y_ref[...] = jnp.zeros_like(y_ref)


def _experts(block_expert, n_used, xs_ts, w_g, w_u, w_d, *, ds):
    E, D, De = w_g.shape
    rb = EXPERT_BLOCK
    n_blocks = xs_ts.shape[0] // (rb * ds)
    grid_spec = pltpu.PrefetchScalarGridSpec(
        num_scalar_prefetch=2,
        grid=(n_blocks,),
        in_specs=[pl.BlockSpec((rb * ds, LANES), lambda b, be, nu: (b, 0)),
                  pl.BlockSpec((1, D, De), lambda b, be, nu: (be[b], 0, 0)),
                  pl.BlockSpec((1, D, De), lambda b, be, nu: (be[b], 0, 0)),
                  pl.BlockSpec((1, De, D), lambda b, be, nu: (be[b], 0, 0))],
        out_specs=pl.BlockSpec((rb * ds, LANES), lambda b, be, nu: (b, 0)),
        scratch_shapes=[pltpu.VMEM((D, De), bf16), pltpu.VMEM((D, De), bf16), pltpu.VMEM((De, D), bf16),
                        pltpu.VMEM((rb, D), bf16)],
    )
    return pl.pallas_call(
        functools.partial(_expert_body, ds=ds, rb=rb),
        grid_spec=grid_spec,
        out_shape=jax.ShapeDtypeStruct(xs_ts.shape, f32),
        compiler_params=_cparams(("arbitrary",)),
        name="experts",
    )(block_expert, n_used, xs_ts, w_g, w_u, w_d)


def _combine_body(dest_ref, gate_ref, h_ref, gt_ref, fg_ref, y_hbm, o_ref, y_scr, sem, *, tt, ds, final):

    def issue(r, m):
        _row_copy(y_hbm, y_scr.at[0], dest_ref[0, 0, r], r, ds, sem).start()
        _row_copy(y_hbm, y_scr.at[1], dest_ref[0, 1, r], r, ds, sem).start()
        return m

    lax.fori_loop(0, tt, issue, 0, unroll=8)
    for kk in range(2):
        pltpu.make_async_copy(y_hbm.at[pl.ds(0, tt * ds), :], y_scr.at[kk], sem).wait()
    gt = gt_ref[0]
    g0 = gate_ref[:, 0:1]
    g1 = gate_ref[:, 1:2]
    for j in range(ds):
        sl = slice(j * LANES, (j + 1) * LANES)
        f = g0 * y_scr[0, pl.ds(j, tt, stride=ds), :] + g1 * y_scr[1, pl.ds(j, tt, stride=ds), :]
        o_ref[:, sl] = h_ref[:, sl] + gt[:, sl] * f
    if final:
        x = o_ref[...]
        o_ref[...] = x * lax.rsqrt(jnp.mean(x * x, -1, keepdims=True) + EPS) * fg_ref[...]


def _combine(dest3, gates_t, h1, mod, final_g, y_ts, *, n_tok, tt, ds, tpb, nlat, nb, final):
    D = h1.shape[1]
    return pl.pallas_call(
        functools.partial(_combine_body, tt=tt, ds=ds, final=final),
        grid=(n_tok // tt,),
        in_specs=[pl.BlockSpec((1, 2, tt), lambda i: (i, 0, 0), memory_space=pltpu.SMEM),
                  pl.BlockSpec((tt, 2), lambda i: (i, 0)),
                  pl.BlockSpec((tt, D), lambda i: (i, 0)),
                  _mod_spec(5, D, tpb, nlat, nb),
                  pl.BlockSpec((1, D), lambda i: (0, 0)),
                  pl.BlockSpec(memory_space=pl.ANY)],
        out_specs=pl.BlockSpec((tt, D), lambda i: (i, 0)),
        out_shape=jax.ShapeDtypeStruct((n_tok, D), f32),
        scratch_shapes=[pltpu.VMEM((2, tt * ds, LANES), f32), pltpu.SemaphoreType.DMA(())],
        compiler_params=_cparams(("arbitrary",)),
        name="combine",
    )(dest3, gates_t, h1, mod, final_g, y_ts)


def _moe(v_ts, logits, h1, mod, w_router_bias, w_g, w_u, w_d, final_g, *, n_tok, tt, ds, tpb, nlat, nb, final):
    E = w_g.shape[0]
    route_tt = next(t for t in (512, 256, LANES) if n_tok % t == 0)
    idx, gates, counts = _route(logits[:n_tok].T, w_router_bias.reshape(E, 1), tt=route_tt)
    counts = counts[:, 0].astype(jnp.int32)
    padded = (counts + EXPERT_BLOCK - 1) // EXPERT_BLOCK * EXPERT_BLOCK
    pad_end = jnp.cumsum(padded)
    pad_start = pad_end - padded
    n_blocks = (n_tok * 2 + EXPERT_BLOCK - 1) // EXPERT_BLOCK + E
    n_used = (pad_end[-1] // EXPERT_BLOCK).astype(jnp.int32).reshape(1)
    blk = jnp.minimum(jnp.arange(n_blocks, dtype=jnp.int32), jnp.maximum(n_used[0] - 1, 0)) * EXPERT_BLOCK
    block_expert = jnp.minimum(jnp.sum(blk[:, None] >= pad_end[None, :], 1), E - 1).astype(jnp.int32)
    dest = pad_start[idx[0:2]] + idx[2:4]
    dest3 = dest.reshape(2, n_tok // tt, tt).transpose(1, 0, 2)
    pad_info = jnp.stack([pad_start + counts, padded - counts, jnp.broadcast_to(n_used, (E,))]).astype(jnp.int32)
    xs_ts = _dispatch(dest3, pad_info, v_ts, n_tok=n_tok, n_blocks=n_blocks, tt=tt, ds=ds)
    y_ts = _experts(block_expert, n_used, xs_ts, w_g, w_u, w_d, ds=ds)
    return _combine(dest3, gates.T, h1, mod, final_g, y_ts, n_tok=n_tok, tt=tt, ds=ds, tpb=tpb, nlat=nlat,
                    nb=nb, final=final)


def kernel(x, c, ctx, c_ctx, w_mod, b_mod, norm1_g, norm2_g, w_in, conv_w, a_log, dt_bias, onorm_g, w_pool, pool_scale, w_up_dn, w_up_pool, w_out, w_router, router_bias, w_gate_e, w_up_e, w_down_e, final_g):
    B, S, D = x.shape
    C = ctx.shape[1]
    L = w_mod.shape[0]
    H = a_log.shape[-1]
    dk = onorm_g.shape[-1]
    assert dk == LANES
    dn_w = H * dk
    _, ng, gw, _ = w_pool.shape
    pool_w = ng * gw
    ds = D // LANES
    n_lat = B * S
    T = n_lat + B * C
    tm = 256
    assert S % tm == 0 and (B * C) % tm == 0 and C % CHUNK == 0 and S % (4 * GRID_W) == 0
    tpb = S // tm
    nlat_tiles = n_lat // tm
    tm_in = next(t for t in (1024, 512, 256) if S % t == 0 and (B * C) % t == 0)
    tr = min(256, C)
    G = min(H, 8)

    c8 = jnp.zeros((8, D), f32).at[:B].set(c).at[B].set(c_ctx)
    mod = _mod_table(c8, w_mod, b_mod)

    qkv_c = 3 * dn_w
    scan_c = qkv_c + 4 * H
    og_off, pool_off = scan_c, scan_c + dn_w
    mg_off = pool_off + pool_w
    assert (4 * dn_w) % D == 0 and (4 * dn_w + 2 * D) % pool_w == 0
    pool_off_blocks = (4 * dn_w + 2 * D) // pool_w

    h = jnp.concatenate([x.reshape(n_lat, D), ctx.reshape(B * C, D)], 0)
    for l in range(L):
        last = l == L - 1
        mod_l = mod[l].reshape(8, 1, mod.shape[-1])
        wl = w_in[l]
        w_z = jnp.concatenate([wl[:, :qkv_c], wl[:, og_off:pool_off], wl[:, mg_off:], wl[:, pool_off:mg_off]],
                              1).astype(bf16)
        w_ba = jnp.pad(wl[:, qkv_c:scan_c], ((0, 0), (0, LANES - 4 * H))).astype(bf16)
        tn = 1024 if w_z.shape[1] % 1024 == 0 else w_z.shape[1]
        z, ba = _in_proj(h, mod_l, 0, norm1_g[l].reshape(1, D), w_z, w_ba, tm=tm_in, tn=tn, tpb=S // tm_in,
                         nlat=n_lat // tm_in, nb=B)
        qkv = _conv_prep(z, conv_w[l], tr=tr, tc=min(512, dn_w), S=S, C=C, nlat_rows=n_lat, dn_w=dn_w, dk=dk)
        o_dn = _deltanet(qkv, ba, a_log[l], dt_bias[l], B=B, S=S, C=C, H=H, G=G)
        pooled_lat, pooled_ctx = _pool(z, B=B, S=S, C=C, pool_off_blocks=pool_off_blocks, pool_w=pool_w, gw=gw)
        n_rows = n_lat if last else T
        merged = _merge_a(o_dn, z, pooled_lat, pooled_ctx, onorm_g[l].reshape(1, dk), w_pool[l].astype(bf16),
                          pool_scale[l].reshape(1, pool_w), w_up_dn[l].astype(bf16), w_up_pool[l].astype(bf16),
                          n_rows=n_rows, tm=tm, H=H, D=D)
        h1, v_ts, logits = _merge_b(merged, h, mod_l, norm2_g[l].reshape(1, D), w_out[l].astype(bf16), w_router,
                                    n_rows=n_rows, tm=tm, tpb=tpb, nlat=nlat_tiles, nb=B)
        h = _moe(v_ts, logits, h1, mod_l, router_bias, w_gate_e[l], w_up_e[l], w_down_e[l],
                 final_g.reshape(1, D), n_tok=n_rows, tt=tm, ds=ds, tpb=tpb, nlat=nlat_tiles, nb=B, final=last)
    return h.reshape(B, S, D)
```

```python
import functools

import jax
import jax.numpy as jnp
from jax import lax
from jax.experimental import pallas as pl
from jax.experimental.pallas import tpu as pltpu

f32 = jnp.float32
bf16 = jnp.bfloat16
HIGHEST = lax.Precision.HIGHEST

EPS = 1e-6
LANES = 128
GRID_W = 64
CHUNK = 64
POOL_WINDOWS = (2, 4, 8, 16)
GROUP_SIZE = 8
EXPERT_BLOCK = 256
VMEM_LIMIT = 56 * 1024 * 1024


def _cparams(sem, vmem=VMEM_LIMIT):
    return pltpu.CompilerParams(dimension_semantics=sem, vmem_limit_bytes=vmem)


def _sigmoid(x):
    return 1.0 / (1.0 + jnp.exp(-x))


def _softplus(x):
    return jnp.maximum(x, 0.0) + jnp.log(1.0 + jnp.exp(-jnp.abs(x)))


def _mod_spec(sec, D, tiles_per_batch, n_lat_tiles, n_batch):
    def index(i, *_):
        return (jnp.where(i < n_lat_tiles, i // tiles_per_batch, n_batch), 0, sec)

    return pl.BlockSpec((1, 1, D), index)


def _mod_body(c_ref, w_ref, b_ref, o_ref):
    c = c_ref[...]
    s = c * _sigmoid(c)
    o_ref[0] = jnp.dot(s, w_ref[0], preferred_element_type=f32, precision=HIGHEST) + b_ref[0]


def _mod_table(c8, w_mod, b_mod):
    L, D, N = w_mod.shape
    tn = next(t for t in (1024, 512, 256, LANES) if N % t == 0)
    return pl.pallas_call(
        _mod_body,
        grid=(L, N // tn),
        in_specs=[pl.BlockSpec((8, D), lambda l, j: (0, 0)),
                  pl.BlockSpec((1, D, tn), lambda l, j: (l, 0, j)),
                  pl.BlockSpec((1, 1, tn), lambda l, j: (l, 0, j))],
        out_specs=pl.BlockSpec((1, 8, tn), lambda l, j: (l, 0, j)),
        out_shape=jax.ShapeDtypeStruct((L, 8, N), f32),
        compiler_params=_cparams(("parallel", "parallel")),
        name="mod_table",
    )(c8, w_mod, b_mod.reshape(L, 1, N))


def _inproj_body(h_ref, sc_ref, sh_ref, g_ref, w_ref, wba_ref, z_ref, ba_ref, u_scr):
    j = pl.program_id(1)

    @pl.when(j == 0)
    def _():
        x = h_ref[...]
        y = x * lax.rsqrt(jnp.mean(x * x, -1, keepdims=True) + EPS) * g_ref[...]
        u = (y * (1.0 + sc_ref[0]) + sh_ref[0]).astype(bf16)
        u_scr[...] = u
        ba_ref[...] = jnp.dot(u, wba_ref[...], preferred_element_type=f32)

    z_ref[...] = jnp.dot(u_scr[...], w_ref[...], preferred_element_type=f32).astype(z_ref.dtype)


def _in_proj(h, mod, sec, gain, w_z, w_ba, *, tm, tn, tpb, nlat, nb):
    T, D = h.shape
    Nz = w_z.shape[1]
    return pl.pallas_call(
        _inproj_body,
        grid=(T // tm, Nz // tn),
        in_specs=[pl.BlockSpec((tm, D), lambda i, j: (i, 0)),
                  _mod_spec(sec + 1, D, tpb, nlat, nb),
                  _mod_spec(sec, D, tpb, nlat, nb),
                  pl.BlockSpec((1, D), lambda i, j: (0, 0)),
                  pl.BlockSpec((D, tn), lambda i, j: (0, j)),
                  pl.BlockSpec((D, LANES), lambda i, j: (0, 0))],
        out_specs=[pl.BlockSpec((tm, tn), lambda i, j: (i, j)),
                   pl.BlockSpec((tm, LANES), lambda i, j: (i, 0))],
        out_shape=[jax.ShapeDtypeStruct((T, Nz), bf16), jax.ShapeDtypeStruct((T, LANES), f32)],
        scratch_shapes=[pltpu.VMEM((tm, D), bf16)],
        compiler_params=_cparams(("parallel", "arbitrary")),
        name="in_proj",
    )(h, mod, mod, gain, w_z, w_ba)


def _conv_body(x_ref, xp_ref, xn_ref, w_ref, o_ref, *, tr, tps_lat, tps_ctx, nlat, qscale, tiles_per_sec, halo):
    i = pl.program_id(0)
    j = pl.program_id(1)
    il = jnp.where(i < nlat, i % tps_lat, (i - nlat) % tps_ctx)
    nt = jnp.where(i < nlat, tps_lat, tps_ctx)
    first = il == 0
    last = il == nt - 1
    x = x_ref[...].astype(f32)
    p = jnp.where(first, 0.0, xp_ref[...].astype(f32))
    n = jnp.where(last, 0.0, xn_ref[...].astype(f32))
    row = lax.broadcasted_iota(jnp.int32, x.shape, 0)
    xm1 = jnp.where(row == 0, p[halo - 1:halo], pltpu.roll(x, 1, 0))
    xm2 = jnp.where(row == 0, p[halo - 2:halo - 1], jnp.where(row == 1, p[halo - 1:halo], pltpu.roll(x, 2, 0)))
    xp1 = jnp.where(row == tr - 1, n[0:1], pltpu.roll(x, tr - 1, 0))
    xp2 = jnp.where(row == tr - 2, n[0:1], jnp.where(row == tr - 1, n[1:2], pltpu.roll(x, tr - 2, 0)))
    w = w_ref[...]
    acc = xm2 * w[0:1] + xm1 * w[1:2] + x * w[2:3] + xp1 * w[3:4] + xp2 * w[4:5]
    y = acc * _sigmoid(acc)
    sec = j // tiles_per_sec
    tc = x.shape[1]
    for s in range(tc // LANES):
        ys = y[:, s * LANES:(s + 1) * LANES]
        rs = lax.rsqrt(jnp.sum(ys * ys, -1, keepdims=True) + EPS)
        scale = jnp.where(sec == 2, 1.0, rs) * jnp.where(sec == 0, qscale, 1.0)
        o_ref[:, s * LANES:(s + 1) * LANES] = (ys * scale).astype(o_ref.dtype)


def _conv_prep(z, conv_w, *, tr, tc, S, C, nlat_rows, dn_w, dk):
    T = z.shape[0]
    halo = 16
    hb = tr // halo
    nh = T // halo
    body = functools.partial(_conv_body, tr=tr, tps_lat=S // tr, tps_ctx=C // tr, nlat=nlat_rows // tr,
                             qscale=float(dk) ** -0.5, tiles_per_sec=dn_w // tc, halo=halo)
    return pl.pallas_call(
        body,
        grid=(T // tr, 3 * dn_w // tc),
        in_specs=[pl.BlockSpec((tr, tc), lambda i, j: (i, j)),
                  pl.BlockSpec((halo, tc), lambda i, j: (jnp.maximum(i * hb - 1, 0), j)),
                  pl.BlockSpec((halo, tc), lambda i, j: (jnp.minimum((i + 1) * hb, nh - 1), j)),
                  pl.BlockSpec((conv_w.shape[0], tc), lambda i, j: (0, j))],
        out_specs=pl.BlockSpec((tr, tc), lambda i, j: (i, j)),
        out_shape=jax.ShapeDtypeStruct((T, 3 * dn_w), bf16),
        compiler_params=_cparams(("parallel", "parallel")),
        name="conv_prep",
    )(z, z, z, conv_w)


def _mm(x, y):
    return jnp.dot(x.astype(bf16), y.astype(bf16), preferred_element_type=f32)


def _unit_tri_inverse(a_list, ri, ci):
    b16 = (ri >> 4) == (ci >> 4)
    b32 = (ri >> 5) == (ci >> 5)
    eye = jnp.where(ri == ci, 1.0, 0.0)
    a16 = [jnp.where(b16, a, 0.0) for a in a_list]
    t = [eye - x for x in a16]
    p = [_mm(x, x) for x in a16]
    for step in range(3):
        t = [ti + _mm(ti, pi) for ti, pi in zip(t, p)]
        if step < 2:
            p = [_mm(pi, pi) for pi in p]
    a32 = [jnp.where(b32, a, 0.0) - x for a, x in zip(a_list, a16)]
    ta = [_mm(ti, ai) for ti, ai in zip(t, a32)]
    t = [ti - _mm(tai, ti) for ti, tai in zip(t, ta)]
    a64 = [jnp.where(b32, 0.0, a) for a in a_list]
    ta = [_mm(ti, ai) for ti, ai in zip(t, a64)]
    t = [ti - _mm(tai, ti) for ti, tai in zip(t, ta)]
    return t


def _dn_body(q_ref, k_ref, v_ref, ba_ref, al_ref, dt_ref, o_ref, s_scr, *, G):
    d = pl.program_id(1)
    c = pl.program_id(3)
    dk = LANES
    hs = range(G)

    @pl.when(c == 0)
    def _():
        s_scr[...] = jnp.zeros_like(s_scr)

    ba = ba_ref[0]
    beta_t = _sigmoid(ba)
    g_t = -jnp.exp(al_ref[0]) * _softplus(ba + dt_ref[0])
    ri = lax.broadcasted_iota(jnp.int32, (CHUNK, CHUNK), 0)
    ci = lax.broadcasted_iota(jnp.int32, (CHUNK, CHUNK), 1)
    fwd = d == 0
    later = jnp.where(fwd, ri, ci)
    earlier = jnp.where(fwd, ci, ri)
    incl = later >= earlier
    strict = later > earlier
    tri = jnp.where(incl, 1.0, 0.0)
    gc = jnp.dot(tri, g_t, preferred_element_type=f32, precision=HIGHEST)
    gct = gc.T
    gtot = jnp.sum(g_t, axis=0, keepdims=True)

    def pick(x, col):
        return jnp.where(fwd, x[:, col:col + 1], x[:, G + col:G + col + 1])

    sl = [slice(h * dk, (h + 1) * dk) for h in hs]
    beta_c = [pick(beta_t, h) for h in hs]
    gc_c = [pick(gc, 2 * G + h) for h in hs]
    gc_r = [jnp.where(fwd, gct[2 * G + h:2 * G + h + 1], gct[3 * G + h:3 * G + h + 1]) for h in hs]
    gt = [pick(gtot, 2 * G + h) for h in hs]
    q = [q_ref[:, sl[h]].astype(f32) for h in hs]
    k = [k_ref[:, sl[h]].astype(f32) for h in hs]
    kb = [k[h] * beta_c[h] for h in hs]
    eg = [jnp.exp(gc_c[h]) for h in hs]
    dec = [jnp.where(incl, jnp.exp(jnp.where(incl, gc_c[h] - gc_r[h], 0.0)), 0.0) for h in hs]
    kq = [lax.dot_general(jnp.concatenate([kb[h], q[h]], axis=0).astype(bf16), k[h].astype(bf16),
                          (((1,), (1,)), ((), ())), preferred_element_type=f32) for h in hs]
    a = [jnp.where(strict, kq[h][:CHUNK] * dec[h], 0.0) for h in hs]
    qk = [(kq[h][CHUNK:] * dec[h]).astype(bf16) for h in hs]
    t = _unit_tri_inverse(a, ri, ci)
    rhs = [jnp.concatenate([v_ref[:, sl[h]].astype(f32) * beta_c[h], kb[h] * eg[h]], axis=1) for h in hs]
    sol = [_mm(t[h], rhs[h]) for h in hs]
    s = [s_scr[h] for h in hs]
    r = [_mm(jnp.concatenate([sol[h][:, dk:], q[h] * eg[h]], axis=0), s[h]) for h in hs]
    vnb = [(sol[h][:, :dk] - r[h][:CHUNK]).astype(bf16) for h in hs]
    o = [r[h][CHUNK:] + jnp.dot(qk[h], vnb[h], preferred_element_type=f32) for h in hs]
    kd = [(k[h] * jnp.exp(gt[h] - gc_c[h])).astype(bf16) for h in hs]
    upd = [lax.dot_general(kd[h], vnb[h], (((0,), (0,)), ((), ())), preferred_element_type=f32) for h in hs]
    for h in hs:
        s_scr[h] = s[h] * jnp.exp(gt[h]) + upd[h]
        o_ref[0, :, sl[h]] = o[h].astype(o_ref.dtype)


def _deltanet(qkv, ba, a_log, dt_bias, *, B, S, C, H, G):
    T = qkv.shape[0]
    dn_w = H * LANES
    ncl = S // CHUNK
    ncc = C // CHUNK
    nc = ncl + ncc
    hgs = H // G

    def rowblk(b, d, c):
        cc = jnp.where(d == 0, c, ncc - 1 - c)
        cl = jnp.where(d == 0, c - ncc, nc - 1 - c)
        return jnp.where(c < ncc, B * ncl + b * ncc + cc, b * ncl + cl)

    def grouped(t):
        r = t.shape[0]
        t = t.reshape(r, 4, hgs, G).transpose(2, 0, 1, 3).reshape(hgs, r, 4 * G)
        return jnp.pad(t, ((0, 0), (0, 0), (0, LANES - 4 * G)))

    zeros = jnp.zeros((1, 2 * H), f32)
    ba_g = grouped(ba[:, :4 * H])
    al_g = grouped(jnp.concatenate([zeros, a_log.reshape(1, 2 * H)], 1))
    dt_g = grouped(jnp.concatenate([zeros, dt_bias.reshape(1, 2 * H)], 1))
    body = functools.partial(_dn_body, G=G)
    return pl.pallas_call(
        body,
        grid=(B, 2, hgs, nc),
        in_specs=[pl.BlockSpec((CHUNK, G * LANES), lambda b, d, g, c: (rowblk(b, d, c), g)),
                  pl.BlockSpec((CHUNK, G * LANES), lambda b, d, g, c: (rowblk(b, d, c), hgs + g)),
                  pl.BlockSpec((CHUNK, G * LANES), lambda b, d, g, c: (rowblk(b, d, c), 2 * hgs + g)),
                  pl.BlockSpec((1, CHUNK, LANES), lambda b, d, g, c: (g, rowblk(b, d, c), 0)),
                  pl.BlockSpec((1, 1, LANES), lambda b, d, g, c: (g, 0, 0)),
                  pl.BlockSpec((1, 1, LANES), lambda b, d, g, c: (g, 0, 0))],
        out_specs=pl.BlockSpec((1, CHUNK, G * LANES), lambda b, d, g, c: (d, rowblk(b, d, c), g)),
        out_shape=jax.ShapeDtypeStruct((2, T, dn_w), bf16),
        scratch_shapes=[pltpu.VMEM((G, LANES, LANES), f32)],
        compiler_params=_cparams(("parallel", "parallel", "parallel", "arbitrary")),
        name="deltanet",
    )(qkv, qkv, qkv, ba_g, al_g, dt_g)


def _pool_ctx_body(x_ref, o_ref, *, gw):
    C = x_ref.shape[0]
    ri = lax.broadcasted_iota(jnp.int32, (C, C), 0)
    ci = lax.broadcasted_iota(jnp.int32, (C, C), 1)
    t = lax.broadcasted_iota(jnp.int32, (C, 1), 0)
    for gi, w in enumerate(POOL_WINDOWS):
        sl = slice(gi * gw, (gi + 1) * gw)
        band = jnp.where((ci >= ri - w // 2) & (ci < ri + w - w // 2), 1.0, 0.0).astype(bf16)
        cnt = (jnp.minimum(t + w - w // 2, C) - jnp.maximum(t - w // 2, 0)).astype(f32)
        x = x_ref[:, sl]
        box = jnp.dot(band, x, preferred_element_type=f32)
        o_ref[:, sl] = (box / cnt - x.astype(f32)).astype(o_ref.dtype)


def _pool_lat_body(x_ref, o_ref, y_scr):
    g = pl.program_id(1)
    S = x_ref.shape[0]
    rows = S // GRID_W
    sb = 4 * GRID_W
    ri = lax.broadcasted_iota(jnp.int32, (sb, sb), 0)
    ci = lax.broadcasted_iota(jnp.int32, (sb, sb), 1)
    cr = ri & (GRID_W - 1)
    cc = ci & (GRID_W - 1)
    same = (ri >> 6) == (ci >> 6)
    col = lax.broadcasted_iota(jnp.int32, (GRID_W, 1), 0)
    for gi, w in enumerate(POOL_WINDOWS):
        @pl.when(g == gi)
        def _(w=w):
            band = jnp.where(same & (cc >= cr - w // 2) & (cc < cr + w - w // 2), 1.0, 0.0).astype(bf16)
            for s in range(S // sb):
                y_scr[s * sb:(s + 1) * sb, :] = jnp.dot(band, x_ref[s * sb:(s + 1) * sb, :],
                                                        preferred_element_type=f32)
            ncol = (jnp.minimum(col + w - w // 2, GRID_W) - jnp.maximum(col - w // 2, 0)).astype(f32)
            for r in range(rows):
                lo = max(r - w // 2, 0)
                hi = min(r + w - w // 2, rows)
                acc = y_scr[lo * GRID_W:(lo + 1) * GRID_W, :]
                for rr in range(lo + 1, hi):
                    acc = acc + y_scr[rr * GRID_W:(rr + 1) * GRID_W, :]
                cnt = ncol * float(hi - lo)
                xs = x_ref[r * GRID_W:(r + 1) * GRID_W, :].astype(f32)
                o_ref[r * GRID_W:(r + 1) * GRID_W, :] = (acc / cnt - xs).astype(o_ref.dtype)


def _pool(z, *, B, S, C, pool_off_blocks, pool_w, gw):
    nlat_c = B * S // C
    ng = pool_w // gw
    assert ng == len(POOL_WINDOWS)
    pooled_ctx = pl.pallas_call(
        functools.partial(_pool_ctx_body, gw=gw),
        grid=(B,),
        in_specs=[pl.BlockSpec((C, pool_w), lambda b: (nlat_c + b, pool_off_blocks))],
        out_specs=pl.BlockSpec((C, pool_w), lambda b: (b, 0)),
        out_shape=jax.ShapeDtypeStruct((B * C, pool_w), bf16),
        compiler_params=_cparams(("parallel",)),
        name="pool_ctx",
    )(z)
    off = pool_off_blocks * ng
    pooled_lat = pl.pallas_call(
        _pool_lat_body,
        grid=(B, ng),
        in_specs=[pl.BlockSpec((S, gw), lambda b, g: (b, off + g))],
        out_specs=pl.BlockSpec((S, gw), lambda b, g: (b, g)),
        out_shape=jax.ShapeDtypeStruct((B * S, pool_w), bf16),
        scratch_shapes=[pltpu.VMEM((S, gw), f32)],
        compiler_params=_cparams(("parallel", "parallel")),
        name="pool_lat",
    )(z)
    return pooled_lat, pooled_ctx


def _merge_a_body(o_ref, og_ref, m1_ref, m2_ref, pl_ref, pc_ref, ong_ref, wpool_ref, ps_ref, wud_ref, wup_ref,
                  out_ref, a_scr, p_scr, *, H, gw, nlat):
    is_lat = pl.program_id(0) < nlat
    for h in range(H):
        sl = slice(h * LANES, (h + 1) * LANES)
        o = o_ref[0, :, sl].astype(f32) + o_ref[1, :, sl].astype(f32)
        og = og_ref[:, sl].astype(f32)
        y = o * lax.rsqrt(jnp.mean(o * o, -1, keepdims=True) + EPS) * ong_ref[...] * (og * _sigmoid(og))
        a_scr[:, sl] = y.astype(bf16)
    y_dn = jnp.dot(a_scr[...], wud_ref[...], preferred_element_type=f32)
    for g in range(wpool_ref.shape[0]):
        sl = slice(g * gw, (g + 1) * gw)
        pooled = jnp.where(is_lat, pl_ref[:, sl], pc_ref[:, sl])
        yp = jnp.dot(pooled, wpool_ref[g], preferred_element_type=f32) * ps_ref[:, sl]
        p_scr[:, sl] = yp.astype(bf16)
    y_pool = jnp.dot(p_scr[...], wup_ref[...], preferred_element_type=f32)
    g1 = _sigmoid(m1_ref[...].astype(f32))
    g2 = _sigmoid(m2_ref[...].astype(f32))
    out_ref[...] = (g1 * y_dn + g2 * y_pool).astype(out_ref.dtype)


def _merge_a(o_dn, z, pooled_lat, pooled_ctx, onorm_g, w_pool, pool_scale, w_up_dn, w_up_pool, *, n_rows, tm, H, D):
    dn_w = H * LANES
    pool_w = pooled_lat.shape[1]
    nlat = pooled_lat.shape[0] // tm
    ng, gw, _ = w_pool.shape
    og_blk = 3
    m_blk = 4 * dn_w // D
    const = lambda *_: (0, 0)
    return pl.pallas_call(
        functools.partial(_merge_a_body, H=H, gw=gw, nlat=nlat),
        grid=(n_rows // tm,),
        in_specs=[pl.BlockSpec((2, tm, dn_w), lambda i: (0, i, 0)),
                  pl.BlockSpec((tm, dn_w), lambda i: (i, og_blk)),
                  pl.BlockSpec((tm, D), lambda i: (i, m_blk)),
                  pl.BlockSpec((tm, D), lambda i: (i, m_blk + 1)),
                  pl.BlockSpec((tm, pool_w), lambda i: (jnp.minimum(i, nlat - 1), 0)),
                  pl.BlockSpec((tm, pool_w), lambda i: (jnp.maximum(i - nlat, 0), 0)),
                  pl.BlockSpec((1, LANES), const),
                  pl.BlockSpec((ng, gw, gw), lambda i: (0, 0, 0)),
                  pl.BlockSpec((1, pool_w), const),
                  pl.BlockSpec((dn_w, D), const, pipeline_mode=pl.Buffered(1)),
                  pl.BlockSpec((pool_w, D), const, pipeline_mode=pl.Buffered(1))],
        out_specs=pl.BlockSpec((tm, D), lambda i: (i, 0)),
        out_shape=jax.ShapeDtypeStruct((n_rows, D), bf16),
        scratch_shapes=[pltpu.VMEM((tm, dn_w), bf16), pltpu.VMEM((tm, pool_w), bf16)],
        compiler_params=_cparams(("parallel",)),
        name="merge_a",
    )(o_dn, z, z, z, pooled_lat, pooled_ctx, onorm_g, w_pool, pool_scale, w_up_dn, w_up_pool)


def _merge_b_body(m_ref, h_ref, gt_ref, sc_ref, sh_ref, g_ref, wout_ref, wr_ref, h1_ref, v_ref, lg_ref,
                  *, ds):
    tm = m_ref.shape[0]
    m = jnp.dot(m_ref[...], wout_ref[...], preferred_element_type=f32)
    h1 = h_ref[...] + gt_ref[0] * m
    h1_ref[...] = h1
    y = h1 * lax.rsqrt(jnp.mean(h1 * h1, -1, keepdims=True) + EPS) * g_ref[...]
    v = y * (1.0 + sc_ref[0]) + sh_ref[0]
    E = lg_ref.shape[1]
    vh = v.astype(bf16)
    vl = (v - vh.astype(f32)).astype(bf16)
    th = jnp.dot(vh, wr_ref[...], preferred_element_type=f32)
    tl = jnp.dot(vl, wr_ref[...], preferred_element_type=f32)
    lg_ref[...] = th[:, :E] + (th[:, E:] + tl[:, :E])
    for j in range(ds):
        v_ref[pl.ds(j, tm, stride=ds), :] = v[:, j * LANES:(j + 1) * LANES]


def _merge_b(merged, h, mod, gain2, w_out, w_router, *, n_rows, tm, tpb, nlat, nb):
    D = h.shape[1]
    E = w_router.shape[1]
    ds = D // LANES
    const = lambda *_: (0, 0)
    wr_hi = w_router.astype(bf16)
    wr_lo = (w_router - wr_hi.astype(f32)).astype(bf16)
    wr_cat = jnp.concatenate([wr_hi, wr_lo], 1)
    return pl.pallas_call(
        functools.partial(_merge_b_body, ds=ds),
        grid=(n_rows // tm,),
        in_specs=[pl.BlockSpec((tm, D), lambda i: (i, 0)),
                  pl.BlockSpec((tm, D), lambda i: (i, 0)),
                  _mod_spec(2, D, tpb, nlat, nb),
                  _mod_spec(4, D, tpb, nlat, nb),
                  _mod_spec(3, D, tpb, nlat, nb),
                  pl.BlockSpec((1, D), const),
                  pl.BlockSpec((D, D), const, pipeline_mode=pl.Buffered(1)),
                  pl.BlockSpec((D, 2 * E), const)],
        out_specs=[pl.BlockSpec((tm, D), lambda i: (i, 0)),
                   pl.BlockSpec((tm * ds, LANES), lambda i: (i, 0)),
                   pl.BlockSpec((tm, E), lambda i: (i, 0))],
        out_shape=[jax.ShapeDtypeStruct((n_rows, D), f32),
                   jax.ShapeDtypeStruct((n_rows * ds, LANES), f32),
                   jax.ShapeDtypeStruct((n_rows, E), f32)],
        compiler_params=_cparams(("parallel",)),
        name="merge_b",
    )(merged, h, mod, mod, mod, gain2, w_out, wr_cat)


def _route_body(lg_ref, bias_ref, idx_ref, gate_ref, cnt_ref, carry_scr, *, n_groups):
    i = pl.program_id(0)
    E, tt = lg_ref.shape

    @pl.when(i == 0)
    def _():
        carry_scr[...] = jnp.zeros_like(carry_scr)

    aff = _sigmoid(lg_ref[...])
    sel = aff + bias_ref[...]
    sub = lax.broadcasted_iota(jnp.int32, (GROUP_SIZE, tt), 0)
    neg = -1e30
    best = None
    for g in range(n_groups):
        s = sel[g * GROUP_SIZE:(g + 1) * GROUP_SIZE]
        a = aff[g * GROUP_SIZE:(g + 1) * GROUP_SIZE]
        m1 = jnp.max(s, axis=0, keepdims=True)
        i1 = jnp.min(jnp.where(s == m1, sub, GROUP_SIZE), axis=0, keepdims=True)
        s2 = jnp.where(sub == i1, neg, s)
        m2 = jnp.max(s2, axis=0, keepdims=True)
        i2 = jnp.min(jnp.where(s2 == m2, sub, GROUP_SIZE), axis=0, keepdims=True)
        a1 = jnp.sum(jnp.where(sub == i1, a, 0.0), axis=0, keepdims=True)
        a2 = jnp.sum(jnp.where(sub == i2, a, 0.0), axis=0, keepdims=True)
        cand = (m1 + m2, i1 + g * GROUP_SIZE, i2 + g * GROUP_SIZE, a1, a2)
        if best is None:
            best = cand
        else:
            better = cand[0] > best[0]
            best = tuple(jnp.where(better, cn, bs) for cn, bs in zip(cand, best))
    _, e1, e2, a1, a2 = best
    den = a1 + a2
    gate_ref[0:1, :] = a1 / den
    gate_ref[1:2, :] = a2 / den
    erow = lax.broadcasted_iota(jnp.int32, (E, tt), 0)
    oh1 = erow == e1
    oh2 = erow == e2
    cnt = jnp.where(oh1 | oh2, 1.0, 0.0)
    ti = lax.broadcasted_iota(jnp.int32, (tt, tt), 0)
    tj = lax.broadcasted_iota(jnp.int32, (tt, tt), 1)
    before = jnp.where(ti < tj, 1.0, 0.0).astype(bf16)
    pos = jnp.dot(cnt.astype(bf16), before, preferred_element_type=f32) + carry_scr[...]
    r1 = jnp.sum(jnp.where(oh1, pos, 0.0), axis=0, keepdims=True)
    r2 = jnp.sum(jnp.where(oh2, pos, 0.0), axis=0, keepdims=True)
    idx_ref[0:1, :] = e1
    idx_ref[1:2, :] = e2
    idx_ref[2:3, :] = r1.astype(jnp.int32)
    idx_ref[3:4, :] = r2.astype(jnp.int32)
    carry_scr[...] = carry_scr[...] + jnp.sum(cnt, axis=1, keepdims=True)
    cnt_ref[...] = jnp.broadcast_to(carry_scr[...], cnt_ref.shape)


def _route(logits_t, bias_col, *, tt):
    E, T = logits_t.shape
    return pl.pallas_call(
        functools.partial(_route_body, n_groups=E // GROUP_SIZE),
        grid=(T // tt,),
        in_specs=[pl.BlockSpec((E, tt), lambda i: (0, i)),
                  pl.BlockSpec((E, 1), lambda i: (0, 0))],
        out_specs=[pl.BlockSpec((4, tt), lambda i: (0, i)),
                   pl.BlockSpec((2, tt), lambda i: (0, i)),
                   pl.BlockSpec((E, LANES), lambda i: (0, 0))],
        out_shape=[jax.ShapeDtypeStruct((4, T), jnp.int32),
                   jax.ShapeDtypeStruct((2, T), f32),
                   jax.ShapeDtypeStruct((E, LANES), f32)],
        scratch_shapes=[pltpu.VMEM((E, 1), f32)],
        compiler_params=_cparams(("arbitrary",)),
        name="route",
    )(logits_t, bias_col)


def _row_copy(src, dst, s, t, n, sem):
    def rows(x):
        return pl.ds(x * n if isinstance(x, int) else pl.multiple_of(x * n, 8), n)

    return pltpu.make_async_copy(src.at[rows(s), :], dst.at[rows(t), :], sem)


def _dispatch_body(dest_ref, pad_ref, v_ref, xs_hbm, zero_scr, sem, zsem, *, tt, ds, n_exp, n_blocks, rb):
    i = pl.program_id(0)

    @pl.when(i == 0)
    def _():
        zero_scr[...] = jnp.zeros_like(zero_scr)

        def per_expert(e, n):
            start = pad_ref[0, e]

            def one(r, m):
                _row_copy(zero_scr, xs_hbm, 0, start + r, ds, zsem).start()
                return m + 1

            return lax.fori_loop(0, pad_ref[1, e], one, n)

        total = lax.fori_loop(0, n_exp, per_expert, 0)

        def wait_one(r, m):
            _row_copy(zero_scr, xs_hbm, 0, 0, ds, zsem).wait()
            return m

        lax.fori_loop(0, total, wait_one, 0)
        n_used = pad_ref[2, 0]

        def blk_start(bk, m):
            _row_copy(zero_scr, xs_hbm, 0, bk, rb * ds, zsem).start()
            return m

        lax.fori_loop(n_used, n_blocks, blk_start, 0)

        def blk_wait(bk, m):
            _row_copy(zero_scr, xs_hbm, 0, 0, rb * ds, zsem).wait()
            return m

        lax.fori_loop(n_used, n_blocks, blk_wait, 0)

    def issue(r, m):
        src = v_ref.at[pl.ds(pl.multiple_of(r * ds, ds), ds), :]
        for kk in range(2):
            dst = xs_hbm.at[pl.ds(pl.multiple_of(dest_ref[0, kk, r] * ds, ds), ds), :]
            pltpu.make_async_copy(src, dst, sem).start()
        return m

    lax.fori_loop(0, tt, issue, 0, unroll=8)
    for kk in range(2):
        pltpu.make_async_copy(v_ref, xs_hbm.at[pl.ds(0, tt * ds), :], sem).wait()


def _dispatch(dest3, pad_info, v_ts, *, n_tok, n_blocks, tt, ds):
    n_exp = pad_info.shape[1]
    rb = EXPERT_BLOCK
    return pl.pallas_call(
        functools.partial(_dispatch_body, tt=tt, ds=ds, n_exp=n_exp, n_blocks=n_blocks, rb=rb),
        grid=(n_tok // tt,),
        in_specs=[pl.BlockSpec((1, 2, tt), lambda i: (i, 0, 0), memory_space=pltpu.SMEM),
                  pl.BlockSpec(memory_space=pltpu.SMEM),
                  pl.BlockSpec((tt * ds, LANES), lambda i: (i, 0))],
        out_specs=pl.BlockSpec(memory_space=pl.ANY),
        out_shape=jax.ShapeDtypeStruct((n_blocks * rb * ds, LANES), f32),
        scratch_shapes=[pltpu.VMEM((rb * ds, LANES), f32), pltpu.SemaphoreType.DMA(()), pltpu.SemaphoreType.DMA(())],
        compiler_params=_cparams(("arbitrary",)),
        name="dispatch",
    )(dest3, pad_info, v_ts)


def _expert_body(be_ref, nu_ref, x_ref, wg_ref, wu_ref, wd_ref, y_ref, wg_s, wu_s, wd_s, x_s, *, ds, rb):
    b = pl.program_id(0)
    used = b < nu_ref[0]

    @pl.when(used)
    def _():
        prev = be_ref[jnp.maximum(b - 1, 0)]
        changed = jnp.logical_or(b == 0, be_ref[b] != prev)

        @pl.when(changed)
        def _():
            wg_s[...] = wg_ref[0, 0].astype(bf16)
            wu_s[...] = wu_ref[0, 0].astype(bf16)
            wd_s[...] = wd_ref[0, 0].astype(bf16)

        for j in range(ds):
            x_s[:, j * LANES:(j + 1) * LANES] = x_ref[pl.ds(j, rb, stride=ds), :].astype(bf16)
        x = x_s[...]
        hg = jnp.dot(x, wg_s[...], preferred_element_type=f32)
        hu = jnp.dot(x, wu_s[...], preferred_element_type=f32)
        a = (hg * _sigmoid(hg) * hu).astype(bf16)
        y = jnp.dot(a, wd_s[...], preferred_element_type=f32)
        for j in range(ds):
            y_ref[pl.ds(j, rb, stride=ds), :] = y[:, j * LANES:(j + 1) * LANES]

    @pl.when(jnp.logical_not(used))
    def _():
        y_ref[...] = jnp.zeros_like(y_ref)


def _experts(block_expert, n_used, xs_ts, w_g, w_u, w_d, *, layer, ds):
    _, E, D, De = w_g.shape
    rb = EXPERT_BLOCK
    n_blocks = xs_ts.shape[0] // (rb * ds)
    grid_spec = pltpu.PrefetchScalarGridSpec(
        num_scalar_prefetch=2,
        grid=(n_blocks,),
        in_specs=[pl.BlockSpec((rb * ds, LANES), lambda b, be, nu: (b, 0)),
                  pl.BlockSpec((1, 1, D, De), lambda b, be, nu: (layer, be[b], 0, 0)),
                  pl.BlockSpec((1, 1, D, De), lambda b, be, nu: (layer, be[b], 0, 0)),
                  pl.BlockSpec((1, 1, De, D), lambda b, be, nu: (layer, be[b], 0, 0))],
        out_specs=pl.BlockSpec((rb * ds, LANES), lambda b, be, nu: (b, 0)),
        scratch_shapes=[pltpu.VMEM((D, De), bf16), pltpu.VMEM((D, De), bf16), pltpu.VMEM((De, D), bf16),
                        pltpu.VMEM((rb, D), bf16)],
    )
    return pl.pallas_call(
        functools.partial(_expert_body, ds=ds, rb=rb),
        grid_spec=grid_spec,
        out_shape=jax.ShapeDtypeStruct(xs_ts.shape, f32),
        compiler_params=_cparams(("arbitrary",)),
        name="experts",
    )(block_expert, n_used, xs_ts, w_g, w_u, w_d)


def _combine_body(dest_ref, gate_ref, h_ref, gt_ref, fg_ref, y_hbm, o_ref, y_scr, sem, *, tt, ds, final):

    def issue(r, m):
        _row_copy(y_hbm, y_scr.at[0], dest_ref[0, 0, r], r, ds, sem).start()
        _row_copy(y_hbm, y_scr.at[1], dest_ref[0, 1, r], r, ds, sem).start()
        return m

    lax.fori_loop(0, tt, issue, 0, unroll=8)
    for kk in range(2):
        pltpu.make_async_copy(y_hbm.at[pl.ds(0, tt * ds), :], y_scr.at[kk], sem).wait()
    gt = gt_ref[0]
    g0 = gate_ref[:, 0:1]
    g1 = gate_ref[:, 1:2]
    for j in range(ds):
        sl = slice(j * LANES, (j + 1) * LANES)
        f = g0 * y_scr[0, pl.ds(j, tt, stride=ds), :] + g1 * y_scr[1, pl.ds(j, tt, stride=ds), :]
        o_ref[:, sl] = h_ref[:, sl] + gt[:, sl] * f
    if final:
        x = o_ref[...]
        o_ref[...] = x * lax.rsqrt(jnp.mean(x * x, -1, keepdims=True) + EPS) * fg_ref[...]


def _combine(dest3, gates_t, h1, mod, final_g, y_ts, *, n_tok, tt, ds, tpb, nlat, nb, final):
    D = h1.shape[1]
    return pl.pallas_call(
        functools.partial(_combine_body, tt=tt, ds=ds, final=final),
        grid=(n_tok // tt,),
        in_specs=[pl.BlockSpec((1, 2, tt), lambda i: (i, 0, 0), memory_space=pltpu.SMEM),
                  pl.BlockSpec((tt, 2), lambda i: (i, 0)),
                  pl.BlockSpec((tt, D), lambda i: (i, 0)),
                  _mod_spec(5, D, tpb, nlat, nb),
                  pl.BlockSpec((1, D), lambda i: (0, 0)),
                  pl.BlockSpec(memory_space=pl.ANY)],
        out_specs=pl.BlockSpec((tt, D), lambda i: (i, 0)),
        out_shape=jax.ShapeDtypeStruct((n_tok, D), f32),
        scratch_shapes=[pltpu.VMEM((2, tt * ds, LANES), f32), pltpu.SemaphoreType.DMA(())],
        compiler_params=_cparams(("arbitrary",)),
        name="combine",
    )(dest3, gates_t, h1, mod, final_g, y_ts)


def _moe(v_ts, logits, h1, mod, w_router_bias, w_g, w_u, w_d, final_g, *, layer, n_tok, tt, ds, tpb, nlat, nb,
         final):
    E = w_g.shape[1]
    route_tt = next(t for t in (512, 256, LANES) if n_tok % t == 0)
    idx, gates, counts = _route(logits[:n_tok].T, w_router_bias.reshape(E, 1), tt=route_tt)
    counts = counts[:, 0].astype(jnp.int32)
    padded = (counts + EXPERT_BLOCK - 1) // EXPERT_BLOCK * EXPERT_BLOCK
    pad_end = jnp.cumsum(padded)
    pad_start = pad_end - padded
    n_blocks = (n_tok * 2 + EXPERT_BLOCK - 1) // EXPERT_BLOCK + E
    n_used = (pad_end[-1] // EXPERT_BLOCK).astype(jnp.int32).reshape(1)
    blk = jnp.minimum(jnp.arange(n_blocks, dtype=jnp.int32), jnp.maximum(n_used[0] - 1, 0)) * EXPERT_BLOCK
    block_expert = jnp.minimum(jnp.sum(blk[:, None] >= pad_end[None, :], 1), E - 1).astype(jnp.int32)
    e_sel = idx[0:2][..., None] == jnp.arange(E, dtype=jnp.int32)
    dest = jnp.sum(jnp.where(e_sel, pad_start.astype(jnp.int32), 0), -1) + idx[2:4]
    dest3 = dest.reshape(2, n_tok // tt, tt).transpose(1, 0, 2)
    pad_info = jnp.stack([pad_start + counts, padded - counts, jnp.broadcast_to(n_used, (E,))]).astype(jnp.int32)
    xs_ts = _dispatch(dest3, pad_info, v_ts, n_tok=n_tok, n_blocks=n_blocks, tt=tt, ds=ds)
    y_ts = _experts(block_expert, n_used, xs_ts, w_g, w_u, w_d, layer=layer, ds=ds)
    return _combine(dest3, gates.T, h1, mod, final_g, y_ts, n_tok=n_tok, tt=tt, ds=ds, tpb=tpb, nlat=nlat,
                    nb=nb, final=final)


def kernel(x, c, ctx, c_ctx, w_mod, b_mod, norm1_g, norm2_g, w_in, conv_w, a_log, dt_bias, onorm_g, w_pool, pool_scale, w_up_dn, w_up_pool, w_out, w_router, router_bias, w_gate_e, w_up_e, w_down_e, final_g):
    B, S, D = x.shape
    C = ctx.shape[1]
    L = w_mod.shape[0]
    H = a_log.shape[-1]
    dk = onorm_g.shape[-1]
    assert dk == LANES
    dn_w = H * dk
    _, ng, gw, _ = w_pool.shape
    pool_w = ng * gw
    ds = D // LANES
    n_lat = B * S
    T = n_lat + B * C
    tm = 256
    assert S % tm == 0 and (B * C) % tm == 0 and C % CHUNK == 0 and S % (4 * GRID_W) == 0
    tpb = S // tm
    nlat_tiles = n_lat // tm
    tm_in = next(t for t in (1024, 512, 256) if S % t == 0 and (B * C) % t == 0)
    tm_mg = next(t for t in (512, 256) if S % t == 0 and (B * C) % t == 0)
    tr = min(256, C)
    G = min(H, 16)

    c8 = jnp.zeros((8, D), f32).at[:B].set(c).at[B].set(c_ctx)
    mod = _mod_table(c8, w_mod, b_mod)

    qkv_c = 3 * dn_w
    scan_c = qkv_c + 4 * H
    og_off, pool_off = scan_c, scan_c + dn_w
    mg_off = pool_off + pool_w
    assert (4 * dn_w) % D == 0 and (4 * dn_w + 2 * D) % pool_w == 0
    pool_off_blocks = (4 * dn_w + 2 * D) // pool_w

    h = jnp.concatenate([x.reshape(n_lat, D), ctx.reshape(B * C, D)], 0)
    for l in range(L):
        last = l == L - 1
        mod_l = mod[l].reshape(8, 1, mod.shape[-1])
        wl = w_in[l]
        w_z = jnp.concatenate([wl[:, :qkv_c], wl[:, og_off:pool_off], wl[:, mg_off:], wl[:, pool_off:mg_off]],
                              1).astype(bf16)
        w_ba = jnp.pad(wl[:, qkv_c:scan_c], ((0, 0), (0, LANES - 4 * H))).astype(bf16)
        tn = 1024 if w_z.shape[1] % 1024 == 0 else w_z.shape[1]
        z, ba = _in_proj(h, mod_l, 0, norm1_g[l].reshape(1, D), w_z, w_ba, tm=tm_in, tn=tn, tpb=S // tm_in,
                         nlat=n_lat // tm_in, nb=B)
        qkv = _conv_prep(z, conv_w[l], tr=tr, tc=min(1024, dn_w), S=S, C=C, nlat_rows=n_lat, dn_w=dn_w, dk=dk)
        o_dn = _deltanet(qkv, ba, a_log[l], dt_bias[l], B=B, S=S, C=C, H=H, G=G)
        pooled_lat, pooled_ctx = _pool(z, B=B, S=S, C=C, pool_off_blocks=pool_off_blocks, pool_w=pool_w, gw=gw)
        n_rows = n_lat if last else T
        merged = _merge_a(o_dn, z, pooled_lat, pooled_ctx, onorm_g[l].reshape(1, dk), w_pool[l].astype(bf16),
                          pool_scale[l].reshape(1, pool_w), w_up_dn[l].astype(bf16), w_up_pool[l].astype(bf16),
                          n_rows=n_rows, tm=tm_mg, H=H, D=D)
        h1, v_ts, logits = _merge_b(merged, h, mod_l, norm2_g[l].reshape(1, D), w_out[l].astype(bf16), w_router,
                                    n_rows=n_rows, tm=tm_mg, tpb=S // tm_mg, nlat=n_lat // tm_mg, nb=B)
        h = _moe(v_ts, logits, h1, mod_l, router_bias, w_gate_e, w_up_e, w_down_e,
                 final_g.reshape(1, D), layer=l, n_tok=n_rows, tt=tm, ds=ds, tpb=tpb, nlat=nlat_tiles, nb=B, final=last)
    return h.reshape(B, S, D)
```

```python
import functools

import jax
import jax.numpy as jnp
from jax import lax
from jax.experimental import pallas as pl
from jax.experimental.pallas import tpu as pltpu

f32 = jnp.float32
bf16 = jnp.bfloat16
HIGHEST = lax.Precision.HIGHEST

EPS = 1e-6
LANES = 128
GRID_W = 64
CHUNK = 64
POOL_WINDOWS = (2, 4, 8, 16)
GROUP_SIZE = 8
EXPERT_BLOCK = 256
VMEM_LIMIT = 56 * 1024 * 1024


def _cparams(sem, vmem=VMEM_LIMIT):
    return pltpu.CompilerParams(dimension_semantics=sem, vmem_limit_bytes=vmem)


def _sigmoid(x):
    return 1.0 / (1.0 + jnp.exp(-x))


def _softplus(x):
    return jnp.maximum(x, 0.0) + jnp.log(1.0 + jnp.exp(-jnp.abs(x)))


def _mod_spec(sec, D, tiles_per_batch, n_lat_tiles, n_batch):
    def index(i, *_):
        return (jnp.where(i < n_lat_tiles, i // tiles_per_batch, n_batch), 0, sec)

    return pl.BlockSpec((1, 1, D), index)


def _mod_body(c_ref, w_ref, b_ref, o_ref):
    c = c_ref[...]
    s = c * _sigmoid(c)
    o_ref[0] = jnp.dot(s, w_ref[0], preferred_element_type=f32, precision=HIGHEST) + b_ref[0]


def _mod_table(c8, w_mod, b_mod):
    L, D, N = w_mod.shape
    tn = next(t for t in (1024, 512, 256, LANES) if N % t == 0)
    return pl.pallas_call(
        _mod_body,
        grid=(L, N // tn),
        in_specs=[pl.BlockSpec((8, D), lambda l, j: (0, 0)),
                  pl.BlockSpec((1, D, tn), lambda l, j: (l, 0, j)),
                  pl.BlockSpec((1, 1, tn), lambda l, j: (l, 0, j))],
        out_specs=pl.BlockSpec((1, 8, tn), lambda l, j: (l, 0, j)),
        out_shape=jax.ShapeDtypeStruct((L, 8, N), f32),
        compiler_params=_cparams(("parallel", "parallel")),
        name="mod_table",
    )(c8, w_mod, b_mod.reshape(L, 1, N))


def _inproj_body(h_ref, sc_ref, sh_ref, g_ref, w_ref, wba_ref, z_ref, ba_ref, u_scr):
    j = pl.program_id(1)

    @pl.when(j == 0)
    def _():
        x = h_ref[...]
        y = x * lax.rsqrt(jnp.mean(x * x, -1, keepdims=True) + EPS) * g_ref[...]
        u = (y * (1.0 + sc_ref[0]) + sh_ref[0]).astype(bf16)
        u_scr[...] = u
        ba_ref[...] = jnp.dot(u, wba_ref[...], preferred_element_type=f32)

    z_ref[...] = jnp.dot(u_scr[...], w_ref[...], preferred_element_type=f32).astype(z_ref.dtype)


def _in_proj(h, mod, sec, gain, w_z, w_ba, *, tm, tn, tpb, nlat, nb):
    T, D = h.shape
    Nz = w_z.shape[1]
    return pl.pallas_call(
        _inproj_body,
        grid=(T // tm, Nz // tn),
        in_specs=[pl.BlockSpec((tm, D), lambda i, j: (i, 0)),
                  _mod_spec(sec + 1, D, tpb, nlat, nb),
                  _mod_spec(sec, D, tpb, nlat, nb),
                  pl.BlockSpec((1, D), lambda i, j: (0, 0)),
                  pl.BlockSpec((D, tn), lambda i, j: (0, j)),
                  pl.BlockSpec((D, LANES), lambda i, j: (0, 0))],
        out_specs=[pl.BlockSpec((tm, tn), lambda i, j: (i, j)),
                   pl.BlockSpec((tm, LANES), lambda i, j: (i, 0))],
        out_shape=[jax.ShapeDtypeStruct((T, Nz), bf16), jax.ShapeDtypeStruct((T, LANES), f32)],
        scratch_shapes=[pltpu.VMEM((tm, D), bf16)],
        compiler_params=_cparams(("parallel", "arbitrary")),
        name="in_proj",
    )(h, mod, mod, gain, w_z, w_ba)


def _conv_body(x_ref, xp_ref, xn_ref, w_ref, o_ref, *, tr, tps_lat, tps_ctx, nlat, qscale, tiles_per_sec, halo):
    i = pl.program_id(0)
    j = pl.program_id(1)
    il = jnp.where(i < nlat, i % tps_lat, (i - nlat) % tps_ctx)
    nt = jnp.where(i < nlat, tps_lat, tps_ctx)
    first = il == 0
    last = il == nt - 1
    x = x_ref[...].astype(f32)
    p = jnp.where(first, 0.0, xp_ref[...].astype(f32))
    n = jnp.where(last, 0.0, xn_ref[...].astype(f32))
    row = lax.broadcasted_iota(jnp.int32, x.shape, 0)
    xm1 = jnp.where(row == 0, p[halo - 1:halo], pltpu.roll(x, 1, 0))
    xm2 = jnp.where(row == 0, p[halo - 2:halo - 1], jnp.where(row == 1, p[halo - 1:halo], pltpu.roll(x, 2, 0)))
    xp1 = jnp.where(row == tr - 1, n[0:1], pltpu.roll(x, tr - 1, 0))
    xp2 = jnp.where(row == tr - 2, n[0:1], jnp.where(row == tr - 1, n[1:2], pltpu.roll(x, tr - 2, 0)))
    w = w_ref[...]
    acc = xm2 * w[0:1] + xm1 * w[1:2] + x * w[2:3] + xp1 * w[3:4] + xp2 * w[4:5]
    y = acc * _sigmoid(acc)
    sec = j // tiles_per_sec
    tc = x.shape[1]
    for s in range(tc // LANES):
        ys = y[:, s * LANES:(s + 1) * LANES]
        rs = lax.rsqrt(jnp.sum(ys * ys, -1, keepdims=True) + EPS)
        scale = jnp.where(sec == 2, 1.0, rs) * jnp.where(sec == 0, qscale, 1.0)
        o_ref[:, s * LANES:(s + 1) * LANES] = (ys * scale).astype(o_ref.dtype)


def _conv_prep(z, conv_w, *, tr, tc, S, C, nlat_rows, dn_w, dk):
    T = z.shape[0]
    halo = 16
    hb = tr // halo
    nh = T // halo
    body = functools.partial(_conv_body, tr=tr, tps_lat=S // tr, tps_ctx=C // tr, nlat=nlat_rows // tr,
                             qscale=float(dk) ** -0.5, tiles_per_sec=dn_w // tc, halo=halo)
    return pl.pallas_call(
        body,
        grid=(T // tr, 3 * dn_w // tc),
        in_specs=[pl.BlockSpec((tr, tc), lambda i, j: (i, j)),
                  pl.BlockSpec((halo, tc), lambda i, j: (jnp.maximum(i * hb - 1, 0), j)),
                  pl.BlockSpec((halo, tc), lambda i, j: (jnp.minimum((i + 1) * hb, nh - 1), j)),
                  pl.BlockSpec((conv_w.shape[0], tc), lambda i, j: (0, j))],
        out_specs=pl.BlockSpec((tr, tc), lambda i, j: (i, j)),
        out_shape=jax.ShapeDtypeStruct((T, 3 * dn_w), bf16),
        compiler_params=_cparams(("parallel", "parallel")),
        name="conv_prep",
    )(z, z, z, conv_w)


def _mm(x, y):
    return jnp.dot(x.astype(bf16), y.astype(bf16), preferred_element_type=f32)


def _unit_tri_inverse(a_list, ri, ci, interleave=None):
    b16 = (ri >> 4) == (ci >> 4)
    b32 = (ri >> 5) == (ci >> 5)
    eye = jnp.where(ri == ci, 1.0, 0.0)
    a16 = [jnp.where(b16, a, 0.0) for a in a_list]
    t = [eye - x for x in a16]
    p = [_mm(x, x) for x in a16]
    for step in range(3):
        t = [ti + _mm(ti, pi) for ti, pi in zip(t, p)]
        if step == 0 and interleave is not None:
            interleave()
        if step < 2:
            p = [_mm(pi, pi) for pi in p]
    a32 = [jnp.where(b32, a, 0.0) - x for a, x in zip(a_list, a16)]
    ta = [_mm(ti, ai) for ti, ai in zip(t, a32)]
    t = [ti - _mm(tai, ti) for ti, tai in zip(t, ta)]
    a64 = [jnp.where(b32, 0.0, a) for a in a_list]
    ta = [_mm(ti, ai) for ti, ai in zip(t, a64)]
    t = [ti - _mm(tai, ti) for ti, tai in zip(t, ta)]
    return t


def _dn_body(q_ref, k_ref, v_ref, ba_ref, al_ref, dt_ref, o_ref, s_scr, u_scr, wq_scr, qk_scr, kd_scr, eg_scr,
             *, G):
    d = pl.program_id(1)
    c = pl.program_id(3)
    dk = LANES
    hs = range(G)

    @pl.when(c == 0)
    def _():
        for ref in (s_scr, u_scr, wq_scr, qk_scr, kd_scr, eg_scr):
            ref[...] = jnp.zeros_like(ref)

    ba = ba_ref[0]
    beta_t = _sigmoid(ba)
    g_t = -jnp.exp(al_ref[0]) * _softplus(ba + dt_ref[0])
    ri = lax.broadcasted_iota(jnp.int32, (CHUNK, CHUNK), 0)
    ci = lax.broadcasted_iota(jnp.int32, (CHUNK, CHUNK), 1)
    fwd = d == 0
    later = jnp.where(fwd, ri, ci)
    earlier = jnp.where(fwd, ci, ri)
    incl = later >= earlier
    strict = later > earlier
    tri = jnp.where(incl, 1.0, 0.0)
    gc = jnp.dot(tri, g_t, preferred_element_type=f32, precision=HIGHEST)
    gct = gc.T
    gtot = jnp.sum(g_t, axis=0, keepdims=True)

    def pick(x, col):
        return jnp.where(fwd, x[:, col:col + 1], x[:, G + col:G + col + 1])

    sl = [slice(h * dk, (h + 1) * dk) for h in hs]
    beta_c = [pick(beta_t, h) for h in hs]
    gc_c = [pick(gc, 2 * G + h) for h in hs]
    gc_r = [jnp.where(fwd, gct[2 * G + h:2 * G + h + 1], gct[3 * G + h:3 * G + h + 1]) for h in hs]
    gt = [pick(gtot, 2 * G + h) for h in hs]
    q = [q_ref[:, sl[h]].astype(f32) for h in hs]
    k = [k_ref[:, sl[h]].astype(f32) for h in hs]
    kb = [k[h] * beta_c[h] for h in hs]
    eg = [jnp.exp(gc_c[h]) for h in hs]
    dec = [jnp.where(incl, jnp.exp(jnp.where(incl, gc_c[h] - gc_r[h], 0.0)), 0.0) for h in hs]
    kq = [lax.dot_general(jnp.concatenate([kb[h], q[h]], axis=0).astype(bf16), k[h].astype(bf16),
                          (((1,), (1,)), ((), ())), preferred_element_type=f32) for h in hs]
    s = [s_scr[h] for h in hs]
    r = [jnp.dot(wq_scr[h], s[h].astype(bf16), preferred_element_type=f32) for h in hs]
    a = [jnp.where(strict, kq[h][:CHUNK] * dec[h], 0.0) for h in hs]
    qk = [(kq[h][CHUNK:] * dec[h]).astype(bf16) for h in hs]

    def recurrence_tail():
        vnb = [(u_scr[h] - r[h][:CHUNK]).astype(bf16) for h in hs]
        o = [r[h][CHUNK:] + jnp.dot(qk_scr[h], vnb[h], preferred_element_type=f32) for h in hs]
        upd = [lax.dot_general(kd_scr[h], vnb[h], (((0,), (0,)), ((), ())), preferred_element_type=f32)
               for h in hs]
        for h in hs:
            s_scr[h] = s[h] * eg_scr[h] + upd[h]
            o_ref[0, :, sl[h]] = o[h].astype(o_ref.dtype)

    t = _unit_tri_inverse(a, ri, ci, interleave=recurrence_tail)
    rhs = [jnp.concatenate([v_ref[:, sl[h]].astype(f32) * beta_c[h], kb[h] * eg[h]], axis=1) for h in hs]
    sol = [_mm(t[h], rhs[h]) for h in hs]
    for h in hs:
        u_scr[h] = sol[h][:, :dk]
        wq_scr[h] = jnp.concatenate([sol[h][:, dk:], q[h] * eg[h]], axis=0).astype(bf16)
        qk_scr[h] = qk[h]
        kd_scr[h] = (k[h] * jnp.exp(gt[h] - gc_c[h])).astype(bf16)
        eg_scr[h] = jnp.broadcast_to(jnp.exp(gt[h]), (1, dk))


def _deltanet(qkv, ba, a_log, dt_bias, *, B, S, C, H, G):
    T = qkv.shape[0]
    dn_w = H * LANES
    ncl = S // CHUNK
    ncc = C // CHUNK
    nc = ncl + ncc
    hgs = H // G

    def rowblk(b, d, c):
        cc = jnp.where(d == 0, c, ncc - 1 - c)
        cl = jnp.where(d == 0, c - ncc, nc - 1 - c)
        return jnp.where(c < ncc, B * ncl + b * ncc + cc, b * ncl + cl)

    def grouped(t):
        r = t.shape[0]
        t = t.reshape(r, 4, hgs, G).transpose(2, 0, 1, 3).reshape(hgs, r, 4 * G)
        return jnp.pad(t, ((0, 0), (0, 0), (0, LANES - 4 * G)))

    zeros = jnp.zeros((1, 2 * H), f32)
    ba_g = grouped(ba[:, :4 * H])
    al_g = grouped(jnp.concatenate([zeros, a_log.reshape(1, 2 * H)], 1))
    dt_g = grouped(jnp.concatenate([zeros, dt_bias.reshape(1, 2 * H)], 1))
    body = functools.partial(_dn_body, G=G)

    def blk_in(b, d, c):
        return rowblk(b, d, jnp.minimum(c, nc - 1))

    def blk_out(b, d, c):
        return rowblk(b, d, jnp.maximum(c - 1, 0))

    return pl.pallas_call(
        body,
        grid=(B, 2, hgs, nc + 1),
        in_specs=[pl.BlockSpec((CHUNK, G * LANES), lambda b, d, g, c: (blk_in(b, d, c), g)),
                  pl.BlockSpec((CHUNK, G * LANES), lambda b, d, g, c: (blk_in(b, d, c), hgs + g)),
                  pl.BlockSpec((CHUNK, G * LANES), lambda b, d, g, c: (blk_in(b, d, c), 2 * hgs + g)),
                  pl.BlockSpec((1, CHUNK, LANES), lambda b, d, g, c: (g, blk_in(b, d, c), 0)),
                  pl.BlockSpec((1, 1, LANES), lambda b, d, g, c: (g, 0, 0)),
                  pl.BlockSpec((1, 1, LANES), lambda b, d, g, c: (g, 0, 0))],
        out_specs=pl.BlockSpec((1, CHUNK, G * LANES), lambda b, d, g, c: (d, blk_out(b, d, c), g)),
        out_shape=jax.ShapeDtypeStruct((2, T, dn_w), bf16),
        scratch_shapes=[pltpu.VMEM((G, LANES, LANES), f32),
                        pltpu.VMEM((G, CHUNK, LANES), f32),
                        pltpu.VMEM((G, 2 * CHUNK, LANES), bf16),
                        pltpu.VMEM((G, CHUNK, CHUNK), bf16),
                        pltpu.VMEM((G, CHUNK, LANES), bf16),
                        pltpu.VMEM((G, 1, LANES), f32)],
        compiler_params=_cparams(("parallel", "parallel", "parallel", "arbitrary")),
        name="deltanet",
    )(qkv, qkv, qkv, ba_g, al_g, dt_g)


def _pool_ctx_body(x_ref, o_ref, *, gw):
    C = x_ref.shape[0]
    ri = lax.broadcasted_iota(jnp.int32, (C, C), 0)
    ci = lax.broadcasted_iota(jnp.int32, (C, C), 1)
    t = lax.broadcasted_iota(jnp.int32, (C, 1), 0)
    for gi, w in enumerate(POOL_WINDOWS):
        sl = slice(gi * gw, (gi + 1) * gw)
        band = jnp.where((ci >= ri - w // 2) & (ci < ri + w - w // 2), 1.0, 0.0).astype(bf16)
        cnt = (jnp.minimum(t + w - w // 2, C) - jnp.maximum(t - w // 2, 0)).astype(f32)
        x = x_ref[:, sl]
        box = jnp.dot(band, x, preferred_element_type=f32)
        o_ref[:, sl] = (box / cnt - x.astype(f32)).astype(o_ref.dtype)


def _pool_lat_body(x_ref, o_ref, y_scr):
    g = pl.program_id(1)
    S = x_ref.shape[0]
    rows = S // GRID_W
    sb = 4 * GRID_W
    ri = lax.broadcasted_iota(jnp.int32, (sb, sb), 0)
    ci = lax.broadcasted_iota(jnp.int32, (sb, sb), 1)
    cr = ri & (GRID_W - 1)
    cc = ci & (GRID_W - 1)
    same = (ri >> 6) == (ci >> 6)
    col = lax.broadcasted_iota(jnp.int32, (GRID_W, 1), 0)
    for gi, w in enumerate(POOL_WINDOWS):
        @pl.when(g == gi)
        def _(w=w):
            band = jnp.where(same & (cc >= cr - w // 2) & (cc < cr + w - w // 2), 1.0, 0.0).astype(bf16)
            for s in range(S // sb):
                y_scr[s * sb:(s + 1) * sb, :] = jnp.dot(band, x_ref[s * sb:(s + 1) * sb, :],
                                                        preferred_element_type=f32)
            ncol = (jnp.minimum(col + w - w // 2, GRID_W) - jnp.maximum(col - w // 2, 0)).astype(f32)
            for r in range(rows):
                lo = max(r - w // 2, 0)
                hi = min(r + w - w // 2, rows)
                acc = y_scr[lo * GRID_W:(lo + 1) * GRID_W, :]
                for rr in range(lo + 1, hi):
                    acc = acc + y_scr[rr * GRID_W:(rr + 1) * GRID_W, :]
                cnt = ncol * float(hi - lo)
                xs = x_ref[r * GRID_W:(r + 1) * GRID_W, :].astype(f32)
                o_ref[r * GRID_W:(r + 1) * GRID_W, :] = (acc / cnt - xs).astype(o_ref.dtype)


def _pool(z, *, B, S, C, pool_off_blocks, pool_w, gw):
    nlat_c = B * S // C
    ng = pool_w // gw
    assert ng == len(POOL_WINDOWS)
    pooled_ctx = pl.pallas_call(
        functools.partial(_pool_ctx_body, gw=gw),
        grid=(B,),
        in_specs=[pl.BlockSpec((C, pool_w), lambda b: (nlat_c + b, pool_off_blocks))],
        out_specs=pl.BlockSpec((C, pool_w), lambda b: (b, 0)),
        out_shape=jax.ShapeDtypeStruct((B * C, pool_w), bf16),
        compiler_params=_cparams(("parallel",)),
        name="pool_ctx",
    )(z)
    off = pool_off_blocks * ng
    pooled_lat = pl.pallas_call(
        _pool_lat_body,
        grid=(B, ng),
        in_specs=[pl.BlockSpec((S, gw), lambda b, g: (b, off + g))],
        out_specs=pl.BlockSpec((S, gw), lambda b, g: (b, g)),
        out_shape=jax.ShapeDtypeStruct((B * S, pool_w), bf16),
        scratch_shapes=[pltpu.VMEM((S, gw), f32)],
        compiler_params=_cparams(("parallel", "parallel")),
        name="pool_lat",
    )(z)
    return pooled_lat, pooled_ctx


def _merge_a_body(o_ref, og_ref, m1_ref, m2_ref, pl_ref, pc_ref, ong_ref, wpool_ref, ps_ref, wud_ref, wup_ref,
                  out_ref, a_scr, p_scr, *, H, gw, nlat):
    is_lat = pl.program_id(0) < nlat
    for h in range(H):
        sl = slice(h * LANES, (h + 1) * LANES)
        o = o_ref[0, :, sl].astype(f32) + o_ref[1, :, sl].astype(f32)
        og = og_ref[:, sl].astype(f32)
        y = o * lax.rsqrt(jnp.mean(o * o, -1, keepdims=True) + EPS) * ong_ref[...] * (og * _sigmoid(og))
        a_scr[:, sl] = y.astype(bf16)
    y_dn = jnp.dot(a_scr[...], wud_ref[...], preferred_element_type=f32)
    for g in range(wpool_ref.shape[0]):
        sl = slice(g * gw, (g + 1) * gw)
        pooled = jnp.where(is_lat, pl_ref[:, sl], pc_ref[:, sl])
        yp = jnp.dot(pooled, wpool_ref[g], preferred_element_type=f32) * ps_ref[:, sl]
        p_scr[:, sl] = yp.astype(bf16)
    y_pool = jnp.dot(p_scr[...], wup_ref[...], preferred_element_type=f32)
    g1 = _sigmoid(m1_ref[...].astype(f32))
    g2 = _sigmoid(m2_ref[...].astype(f32))
    out_ref[...] = (g1 * y_dn + g2 * y_pool).astype(out_ref.dtype)


def _merge_a(o_dn, z, pooled_lat, pooled_ctx, onorm_g, w_pool, pool_scale, w_up_dn, w_up_pool, *, n_rows, tm, H, D):
    dn_w = H * LANES
    pool_w = pooled_lat.shape[1]
    nlat = pooled_lat.shape[0] // tm
    ng, gw, _ = w_pool.shape
    og_blk = 3
    m_blk = 4 * dn_w // D
    const = lambda *_: (0, 0)
    return pl.pallas_call(
        functools.partial(_merge_a_body, H=H, gw=gw, nlat=nlat),
        grid=(n_rows // tm,),
        in_specs=[pl.BlockSpec((2, tm, dn_w), lambda i: (0, i, 0)),
                  pl.BlockSpec((tm, dn_w), lambda i: (i, og_blk)),
                  pl.BlockSpec((tm, D), lambda i: (i, m_blk)),
                  pl.BlockSpec((tm, D), lambda i: (i, m_blk + 1)),
                  pl.BlockSpec((tm, pool_w), lambda i: (jnp.minimum(i, nlat - 1), 0)),
                  pl.BlockSpec((tm, pool_w), lambda i: (jnp.maximum(i - nlat, 0), 0)),
                  pl.BlockSpec((1, LANES), const),
                  pl.BlockSpec((ng, gw, gw), lambda i: (0, 0, 0)),
                  pl.BlockSpec((1, pool_w), const),
                  pl.BlockSpec((dn_w, D), const, pipeline_mode=pl.Buffered(1)),
                  pl.BlockSpec((pool_w, D), const, pipeline_mode=pl.Buffered(1))],
        out_specs=pl.BlockSpec((tm, D), lambda i: (i, 0)),
        out_shape=jax.ShapeDtypeStruct((n_rows, D), bf16),
        scratch_shapes=[pltpu.VMEM((tm, dn_w), bf16), pltpu.VMEM((tm, pool_w), bf16)],
        compiler_params=_cparams(("parallel",)),
        name="merge_a",
    )(o_dn, z, z, z, pooled_lat, pooled_ctx, onorm_g, w_pool, pool_scale, w_up_dn, w_up_pool)


def _merge_b_body(m_ref, h_ref, gt_ref, sc_ref, sh_ref, g_ref, wout_ref, wr_ref, h1_ref, v_ref, lg_ref,
                  *, ds):
    tm = m_ref.shape[0]
    m = jnp.dot(m_ref[...], wout_ref[...], preferred_element_type=f32)
    h1 = h_ref[...] + gt_ref[0] * m
    h1_ref[...] = h1
    y = h1 * lax.rsqrt(jnp.mean(h1 * h1, -1, keepdims=True) + EPS) * g_ref[...]
    v = y * (1.0 + sc_ref[0]) + sh_ref[0]
    E = lg_ref.shape[1]
    vh = v.astype(bf16)
    vl = (v - vh.astype(f32)).astype(bf16)
    th = jnp.dot(vh, wr_ref[...], preferred_element_type=f32)
    tl = jnp.dot(vl, wr_ref[...], preferred_element_type=f32)
    lg_ref[...] = th[:, :E] + (th[:, E:] + tl[:, :E])
    for j in range(ds):
        v_ref[pl.ds(j, tm, stride=ds), :] = v[:, j * LANES:(j + 1) * LANES]


def _merge_b(merged, h, mod, gain2, w_out, w_router, *, n_rows, tm, tpb, nlat, nb):
    D = h.shape[1]
    E = w_router.shape[1]
    ds = D // LANES
    const = lambda *_: (0, 0)
    wr_hi = w_router.astype(bf16)
    wr_lo = (w_router - wr_hi.astype(f32)).astype(bf16)
    wr_cat = jnp.concatenate([wr_hi, wr_lo], 1)
    return pl.pallas_call(
        functools.partial(_merge_b_body, ds=ds),
        grid=(n_rows // tm,),
        in_specs=[pl.BlockSpec((tm, D), lambda i: (i, 0)),
                  pl.BlockSpec((tm, D), lambda i: (i, 0)),
                  _mod_spec(2, D, tpb, nlat, nb),
                  _mod_spec(4, D, tpb, nlat, nb),
                  _mod_spec(3, D, tpb, nlat, nb),
                  pl.BlockSpec((1, D), const),
                  pl.BlockSpec((D, D), const, pipeline_mode=pl.Buffered(1)),
                  pl.BlockSpec((D, 2 * E), const)],
        out_specs=[pl.BlockSpec((tm, D), lambda i: (i, 0)),
                   pl.BlockSpec((tm * ds, LANES), lambda i: (i, 0)),
                   pl.BlockSpec((tm, E), lambda i: (i, 0))],
        out_shape=[jax.ShapeDtypeStruct((n_rows, D), f32),
                   jax.ShapeDtypeStruct((n_rows * ds, LANES), f32),
                   jax.ShapeDtypeStruct((n_rows, E), f32)],
        compiler_params=_cparams(("parallel",)),
        name="merge_b",
    )(merged, h, mod, mod, mod, gain2, w_out, wr_cat)


def _route_body(lg_ref, bias_ref, idx_ref, gate_ref, cnt_ref, carry_scr, *, n_groups):
    i = pl.program_id(0)
    E, tt = lg_ref.shape

    @pl.when(i == 0)
    def _():
        carry_scr[...] = jnp.zeros_like(carry_scr)

    aff = _sigmoid(lg_ref[...])
    sel = aff + bias_ref[...]
    sub = lax.broadcasted_iota(jnp.int32, (GROUP_SIZE, tt), 0)
    neg = -1e30
    best = None
    for g in range(n_groups):
        s = sel[g * GROUP_SIZE:(g + 1) * GROUP_SIZE]
        a = aff[g * GROUP_SIZE:(g + 1) * GROUP_SIZE]
        m1 = jnp.max(s, axis=0, keepdims=True)
        i1 = jnp.min(jnp.where(s == m1, sub, GROUP_SIZE), axis=0, keepdims=True)
        s2 = jnp.where(sub == i1, neg, s)
        m2 = jnp.max(s2, axis=0, keepdims=True)
        i2 = jnp.min(jnp.where(s2 == m2, sub, GROUP_SIZE), axis=0, keepdims=True)
        a1 = jnp.sum(jnp.where(sub == i1, a, 0.0), axis=0, keepdims=True)
        a2 = jnp.sum(jnp.where(sub == i2, a, 0.0), axis=0, keepdims=True)
        cand = (m1 + m2, i1 + g * GROUP_SIZE, i2 + g * GROUP_SIZE, a1, a2)
        if best is None:
            best = cand
        else:
            better = cand[0] > best[0]
            best = tuple(jnp.where(better, cn, bs) for cn, bs in zip(cand, best))
    _, e1, e2, a1, a2 = best
    den = a1 + a2
    gate_ref[0:1, :] = a1 / den
    gate_ref[1:2, :] = a2 / den
    erow = lax.broadcasted_iota(jnp.int32, (E, tt), 0)
    oh1 = erow == e1
    oh2 = erow == e2
    cnt = jnp.where(oh1 | oh2, 1.0, 0.0)
    ti = lax.broadcasted_iota(jnp.int32, (tt, tt), 0)
    tj = lax.broadcasted_iota(jnp.int32, (tt, tt), 1)
    before = jnp.where(ti < tj, 1.0, 0.0).astype(bf16)
    pos = jnp.dot(cnt.astype(bf16), before, preferred_element_type=f32) + carry_scr[...]
    r1 = jnp.sum(jnp.where(oh1, pos, 0.0), axis=0, keepdims=True)
    r2 = jnp.sum(jnp.where(oh2, pos, 0.0), axis=0, keepdims=True)
    idx_ref[0:1, :] = e1
    idx_ref[1:2, :] = e2
    idx_ref[2:3, :] = r1.astype(jnp.int32)
    idx_ref[3:4, :] = r2.astype(jnp.int32)
    carry_scr[...] = carry_scr[...] + jnp.sum(cnt, axis=1, keepdims=True)
    cnt_ref[...] = jnp.broadcast_to(carry_scr[...], cnt_ref.shape)


def _route(logits_t, bias_col, *, tt):
    E, T = logits_t.shape
    return pl.pallas_call(
        functools.partial(_route_body, n_groups=E // GROUP_SIZE),
        grid=(T // tt,),
        in_specs=[pl.BlockSpec((E, tt), lambda i: (0, i)),
                  pl.BlockSpec((E, 1), lambda i: (0, 0))],
        out_specs=[pl.BlockSpec((4, tt), lambda i: (0, i)),
                   pl.BlockSpec((2, tt), lambda i: (0, i)),
                   pl.BlockSpec((E, LANES), lambda i: (0, 0))],
        out_shape=[jax.ShapeDtypeStruct((4, T), jnp.int32),
                   jax.ShapeDtypeStruct((2, T), f32),
                   jax.ShapeDtypeStruct((E, LANES), f32)],
        scratch_shapes=[pltpu.VMEM((E, 1), f32)],
        compiler_params=_cparams(("arbitrary",)),
        name="route",
    )(logits_t, bias_col)


def _row_copy(src, dst, s, t, n, sem):
    def rows(x):
        return pl.ds(x * n if isinstance(x, int) else pl.multiple_of(x * n, 8), n)

    return pltpu.make_async_copy(src.at[rows(s), :], dst.at[rows(t), :], sem)


def _dispatch_body(dest_ref, pad_ref, v_ref, xs_hbm, zero_scr, sem, zsem, *, tt, ds, n_exp, n_blocks, rb):
    i = pl.program_id(0)

    @pl.when(i == 0)
    def _():
        zero_scr[...] = jnp.zeros_like(zero_scr)

        def per_expert(e, n):
            start = pad_ref[0, e]

            def one(r, m):
                _row_copy(zero_scr, xs_hbm, 0, start + r, ds, zsem).start()
                return m + 1

            return lax.fori_loop(0, pad_ref[1, e], one, n)

        total = lax.fori_loop(0, n_exp, per_expert, 0)

        def wait_one(r, m):
            _row_copy(zero_scr, xs_hbm, 0, 0, ds, zsem).wait()
            return m

        lax.fori_loop(0, total, wait_one, 0)
        n_used = pad_ref[2, 0]

        def blk_start(bk, m):
            _row_copy(zero_scr, xs_hbm, 0, bk, rb * ds, zsem).start()
            return m

        lax.fori_loop(n_used, n_blocks, blk_start, 0)

        def blk_wait(bk, m):
            _row_copy(zero_scr, xs_hbm, 0, 0, rb * ds, zsem).wait()
            return m

        lax.fori_loop(n_used, n_blocks, blk_wait, 0)

    def issue(r, m):
        src = v_ref.at[pl.ds(pl.multiple_of(r * ds, ds), ds), :]
        for kk in range(2):
            dst = xs_hbm.at[pl.ds(pl.multiple_of(dest_ref[0, kk, r] * ds, ds), ds), :]
            pltpu.make_async_copy(src, dst, sem).start(priority=kk)
        return m

    lax.fori_loop(0, tt, issue, 0, unroll=8)
    for kk in range(2):
        pltpu.make_async_copy(v_ref, xs_hbm.at[pl.ds(0, tt * ds), :], sem).wait()


def _dispatch(dest3, pad_info, v_ts, *, n_tok, n_blocks, tt, ds):
    n_exp = pad_info.shape[1]
    rb = EXPERT_BLOCK
    return pl.pallas_call(
        functools.partial(_dispatch_body, tt=tt, ds=ds, n_exp=n_exp, n_blocks=n_blocks, rb=rb),
        grid=(n_tok // tt,),
        in_specs=[pl.BlockSpec((1, 2, tt), lambda i: (i, 0, 0), memory_space=pltpu.SMEM),
                  pl.BlockSpec(memory_space=pltpu.SMEM),
                  pl.BlockSpec((tt * ds, LANES), lambda i: (i, 0))],
        out_specs=pl.BlockSpec(memory_space=pl.ANY),
        out_shape=jax.ShapeDtypeStruct((n_blocks * rb * ds, LANES), f32),
        scratch_shapes=[pltpu.VMEM((rb * ds, LANES), f32), pltpu.SemaphoreType.DMA(()), pltpu.SemaphoreType.DMA(())],
        compiler_params=_cparams(("arbitrary",)),
        name="dispatch",
    )(dest3, pad_info, v_ts)


def _expert_body(be_ref, nu_ref, x_ref, wg_ref, wu_ref, wd_ref, y_ref, x_s, *, ds, rb):
    b = pl.program_id(0)
    used = b < nu_ref[0]

    @pl.when(used)
    def _():
        for j in range(ds):
            x_s[:, j * LANES:(j + 1) * LANES] = x_ref[pl.ds(j, rb, stride=ds), :].astype(bf16)
        x = x_s[...]
        hg = jnp.dot(x, wg_ref[0, 0].astype(bf16), preferred_element_type=f32)
        hu = jnp.dot(x, wu_ref[0, 0].astype(bf16), preferred_element_type=f32)
        a = (hg * _sigmoid(hg) * hu).astype(bf16)
        y = jnp.dot(a, wd_ref[0, 0].astype(bf16), preferred_element_type=f32)
        for j in range(ds):
            y_ref[pl.ds(j, rb, stride=ds), :] = y[:, j * LANES:(j + 1) * LANES]

    @pl.when(jnp.logical_not(used))
    def _():
        y_ref[...] = jnp.zeros_like(y_ref)


def _experts(block_expert, n_used, xs_ts, w_g, w_u, w_d, *, layer, ds):
    _, E, D, De = w_g.shape
    rb = EXPERT_BLOCK
    n_blocks = xs_ts.shape[0] // (rb * ds)
    grid_spec = pltpu.PrefetchScalarGridSpec(
        num_scalar_prefetch=2,
        grid=(n_blocks,),
        in_specs=[pl.BlockSpec((rb * ds, LANES), lambda b, be, nu: (b, 0)),
                  pl.BlockSpec((1, 1, D, De), lambda b, be, nu: (layer, be[b], 0, 0)),
                  pl.BlockSpec((1, 1, D, De), lambda b, be, nu: (layer, be[b], 0, 0)),
                  pl.BlockSpec((1, 1, De, D), lambda b, be, nu: (layer, be[b], 0, 0))],
        out_specs=pl.BlockSpec((rb * ds, LANES), lambda b, be, nu: (b, 0)),
        scratch_shapes=[pltpu.VMEM((rb, D), bf16)],
    )
    return pl.pallas_call(
        functools.partial(_expert_body, ds=ds, rb=rb),
        grid_spec=grid_spec,
        out_shape=jax.ShapeDtypeStruct(xs_ts.shape, f32),
        compiler_params=_cparams(("arbitrary",)),
        name="experts",
    )(block_expert, n_used, xs_ts, w_g, w_u, w_d)


def _combine_body(dest_ref, dnext_ref, gate_ref, h_ref, gt_ref, fg_ref, y_hbm, o_ref, y_scr, sem, *, tt, ds, final):
    i = pl.program_id(0)
    slot = i % 2

    def gather(d_ref, to_slot):
        def body(r, m):
            for kk in range(2):
                _row_copy(y_hbm, y_scr.at[to_slot, kk], d_ref[0, kk, r], r, ds, sem.at[to_slot]).start(priority=kk)
            return m

        lax.fori_loop(0, tt, body, 0, unroll=8)

    @pl.when(i == 0)
    def _():
        gather(dest_ref, 0)

    @pl.when(i + 1 < pl.num_programs(0))
    def _():
        gather(dnext_ref, 1 - slot)

    for kk in range(2):
        pltpu.make_async_copy(y_hbm.at[pl.ds(0, tt * ds), :], y_scr.at[slot, kk], sem.at[slot]).wait()
    gt = gt_ref[0]
    g0 = gate_ref[:, 0:1]
    g1 = gate_ref[:, 1:2]
    for j in range(ds):
        sl = slice(j * LANES, (j + 1) * LANES)
        f = (g0 * y_scr[slot, 0, pl.ds(j, tt, stride=ds), :] + g1 * y_scr[slot, 1, pl.ds(j, tt, stride=ds), :])
        o_ref[:, sl] = h_ref[:, sl] + gt[:, sl] * f
    if final:
        x = o_ref[...]
        o_ref[...] = x * lax.rsqrt(jnp.mean(x * x, -1, keepdims=True) + EPS) * fg_ref[...]


def _combine(dest3, gates_t, h1, mod, final_g, y_ts, *, n_tok, tt, ds, tpb, nlat, nb, final):
    D = h1.shape[1]
    nt = n_tok // tt
    return pl.pallas_call(
        functools.partial(_combine_body, tt=tt, ds=ds, final=final),
        grid=(nt,),
        in_specs=[pl.BlockSpec((1, 2, tt), lambda i: (i, 0, 0), memory_space=pltpu.SMEM),
                  pl.BlockSpec((1, 2, tt), lambda i: (jnp.minimum(i + 1, nt - 1), 0, 0), memory_space=pltpu.SMEM),
                  pl.BlockSpec((tt, 2), lambda i: (i, 0)),
                  pl.BlockSpec((tt, D), lambda i: (i, 0)),
                  _mod_spec(5, D, tpb, nlat, nb),
                  pl.BlockSpec((1, D), lambda i: (0, 0)),
                  pl.BlockSpec(memory_space=pl.ANY)],
        out_specs=pl.BlockSpec((tt, D), lambda i: (i, 0)),
        out_shape=jax.ShapeDtypeStruct((n_tok, D), f32),
        scratch_shapes=[pltpu.VMEM((2, 2, tt * ds, LANES), f32), pltpu.SemaphoreType.DMA((2,))],
        compiler_params=_cparams(("arbitrary",)),
        name="combine",
    )(dest3, dest3, gates_t, h1, mod, final_g, y_ts)


def _moe(v_ts, logits, h1, mod, w_router_bias, w_g, w_u, w_d, final_g, *, layer, n_tok, tt, ds, tpb, nlat, nb,
         final):
    E = w_g.shape[1]
    route_tt = next(t for t in (512, 256, LANES) if n_tok % t == 0)
    idx, gates, counts = _route(logits[:n_tok].T, w_router_bias.reshape(E, 1), tt=route_tt)
    counts = counts[:, 0].astype(jnp.int32)
    padded = (counts + EXPERT_BLOCK - 1) // EXPERT_BLOCK * EXPERT_BLOCK
    pad_end = jnp.cumsum(padded)
    pad_start = pad_end - padded
    n_blocks = (n_tok * 2 + EXPERT_BLOCK - 1) // EXPERT_BLOCK + E
    n_used = (pad_end[-1] // EXPERT_BLOCK).astype(jnp.int32).reshape(1)
    blk = jnp.minimum(jnp.arange(n_blocks, dtype=jnp.int32), jnp.maximum(n_used[0] - 1, 0)) * EXPERT_BLOCK
    block_expert = jnp.minimum(jnp.sum(blk[:, None] >= pad_end[None, :], 1), E - 1).astype(jnp.int32)
    e_sel = idx[0:2][..., None] == jnp.arange(E, dtype=jnp.int32)
    dest = jnp.sum(jnp.where(e_sel, pad_start.astype(jnp.int32), 0), -1) + idx[2:4]
    dest3 = dest.reshape(2, n_tok // tt, tt).transpose(1, 0, 2)
    pad_info = jnp.stack([pad_start + counts, padded - counts, jnp.broadcast_to(n_used, (E,))]).astype(jnp.int32)
    xs_ts = _dispatch(dest3, pad_info, v_ts, n_tok=n_tok, n_blocks=n_blocks, tt=tt, ds=ds)
    y_ts = _experts(block_expert, n_used, xs_ts, w_g, w_u, w_d, layer=layer, ds=ds)
    return _combine(dest3, gates.T, h1, mod, final_g, y_ts, n_tok=n_tok, tt=tt, ds=ds, tpb=tpb, nlat=nlat,
                    nb=nb, final=final)


def kernel(x, c, ctx, c_ctx, w_mod, b_mod, norm1_g, norm2_g, w_in, conv_w, a_log, dt_bias, onorm_g, w_pool, pool_scale, w_up_dn, w_up_pool, w_out, w_router, router_bias, w_gate_e, w_up_e, w_down_e, final_g):
    B, S, D = x.shape
    C = ctx.shape[1]
    L = w_mod.shape[0]
    H = a_log.shape[-1]
    dk = onorm_g.shape[-1]
    assert dk == LANES
    dn_w = H * dk
    _, ng, gw, _ = w_pool.shape
    pool_w = ng * gw
    ds = D // LANES
    n_lat = B * S
    T = n_lat + B * C
    tm = 256
    assert S % tm == 0 and (B * C) % tm == 0 and C % CHUNK == 0 and S % (4 * GRID_W) == 0
    tpb = S // tm
    nlat_tiles = n_lat // tm
    tm_in = next(t for t in (1024, 512, 256) if S % t == 0 and (B * C) % t == 0)
    tm_mg = next(t for t in (512, 256) if S % t == 0 and (B * C) % t == 0)
    tr = min(256, C)
    G = min(H, 16)

    c8 = jnp.zeros((8, D), f32).at[:B].set(c).at[B].set(c_ctx)
    mod = _mod_table(c8, w_mod, b_mod)

    qkv_c = 3 * dn_w
    scan_c = qkv_c + 4 * H
    og_off, pool_off = scan_c, scan_c + dn_w
    mg_off = pool_off + pool_w
    assert (4 * dn_w) % D == 0 and (4 * dn_w + 2 * D) % pool_w == 0
    pool_off_blocks = (4 * dn_w + 2 * D) // pool_w

    h = jnp.concatenate([x.reshape(n_lat, D), ctx.reshape(B * C, D)], 0)
    for l in range(L):
        last = l == L - 1
        mod_l = mod[l].reshape(8, 1, mod.shape[-1])
        wl = w_in[l]
        w_z = jnp.concatenate([wl[:, :qkv_c], wl[:, og_off:pool_off], wl[:, mg_off:], wl[:, pool_off:mg_off]],
                              1).astype(bf16)
        w_ba = jnp.pad(wl[:, qkv_c:scan_c], ((0, 0), (0, LANES - 4 * H))).astype(bf16)
        tn = 1024 if w_z.shape[1] % 1024 == 0 else w_z.shape[1]
        z, ba = _in_proj(h, mod_l, 0, norm1_g[l].reshape(1, D), w_z, w_ba, tm=tm_in, tn=tn, tpb=S // tm_in,
                         nlat=n_lat // tm_in, nb=B)
        qkv = _conv_prep(z, conv_w[l], tr=tr, tc=min(1024, dn_w), S=S, C=C, nlat_rows=n_lat, dn_w=dn_w, dk=dk)
        o_dn = _deltanet(qkv, ba, a_log[l], dt_bias[l], B=B, S=S, C=C, H=H, G=G)
        pooled_lat, pooled_ctx = _pool(z, B=B, S=S, C=C, pool_off_blocks=pool_off_blocks, pool_w=pool_w, gw=gw)
        n_rows = n_lat if last else T
        merged = _merge_a(o_dn, z, pooled_lat, pooled_ctx, onorm_g[l].reshape(1, dk), w_pool[l].astype(bf16),
                          pool_scale[l].reshape(1, pool_w), w_up_dn[l].astype(bf16), w_up_pool[l].astype(bf16),
                          n_rows=n_rows, tm=tm_mg, H=H, D=D)
        h1, v_ts, logits = _merge_b(merged, h, mod_l, norm2_g[l].reshape(1, D), w_out[l].astype(bf16), w_router,
                                    n_rows=n_rows, tm=tm_mg, tpb=S // tm_mg, nlat=n_lat // tm_mg, nb=B)
        h = _moe(v_ts, logits, h1, mod_l, router_bias, w_gate_e, w_up_e, w_down_e,
                 final_g.reshape(1, D), layer=l, n_tok=n_rows, tt=tm, ds=ds, tpb=tpb, nlat=nlat_tiles, nb=B, final=last)
    return h.reshape(B, S, D)
```

```python
import functools

import jax
import jax.numpy as jnp
from jax import lax
from jax.experimental import pallas as pl
from jax.experimental.pallas import tpu as pltpu

f32 = jnp.float32
bf16 = jnp.bfloat16
HIGHEST = lax.Precision.HIGHEST

EPS = 1e-6
LANES = 128
GRID_W = 64
CHUNK = 64
POOL_WINDOWS = (2, 4, 8, 16)
GROUP_SIZE = 8
EXPERT_BLOCK = 256
VMEM_LIMIT = 56 * 1024 * 1024


def _cparams(sem, vmem=VMEM_LIMIT):
    return pltpu.CompilerParams(dimension_semantics=sem, vmem_limit_bytes=vmem)


def _sigmoid(x):
    return 1.0 / (1.0 + jnp.exp(-x))


def _softplus(x):
    return jnp.maximum(x, 0.0) + jnp.log(1.0 + jnp.exp(-jnp.abs(x)))


def _mod_spec(sec, D, tiles_per_batch, n_lat_tiles, n_batch):
    def index(i, *_):
        return (jnp.where(i < n_lat_tiles, i // tiles_per_batch, n_batch), 0, sec)

    return pl.BlockSpec((1, 1, D), index)


def _mod_body(c_ref, w_ref, b_ref, o_ref):
    c = c_ref[...]
    s = c * _sigmoid(c)
    o_ref[0] = jnp.dot(s, w_ref[0], preferred_element_type=f32, precision=HIGHEST) + b_ref[0]


def _mod_table(c8, w_mod, b_mod):
    L, D, N = w_mod.shape
    tn = next(t for t in (1024, 512, 256, LANES) if N % t == 0)
    return pl.pallas_call(
        _mod_body,
        grid=(L, N // tn),
        in_specs=[pl.BlockSpec((8, D), lambda l, j: (0, 0)),
                  pl.BlockSpec((1, D, tn), lambda l, j: (l, 0, j)),
                  pl.BlockSpec((1, 1, tn), lambda l, j: (l, 0, j))],
        out_specs=pl.BlockSpec((1, 8, tn), lambda l, j: (l, 0, j)),
        out_shape=jax.ShapeDtypeStruct((L, 8, N), f32),
        compiler_params=_cparams(("parallel", "parallel")),
        name="mod_table",
    )(c8, w_mod, b_mod.reshape(L, 1, N))


def _inproj_body(h_ref, sc_ref, sh_ref, g_ref, w_ref, wba_ref, z_ref, ba_ref, u_scr):
    j = pl.program_id(1)

    @pl.when(j == 0)
    def _():
        x = h_ref[...]
        y = x * lax.rsqrt(jnp.mean(x * x, -1, keepdims=True) + EPS) * g_ref[...]
        u = (y * (1.0 + sc_ref[0]) + sh_ref[0]).astype(bf16)
        u_scr[...] = u
        ba_ref[...] = jnp.dot(u, wba_ref[...], preferred_element_type=f32)

    z_ref[...] = jnp.dot(u_scr[...], w_ref[...], preferred_element_type=f32).astype(z_ref.dtype)


def _in_proj(h, mod, sec, gain, w_z, w_ba, *, tm, tn, tpb, nlat, nb):
    T, D = h.shape
    Nz = w_z.shape[1]
    return pl.pallas_call(
        _inproj_body,
        grid=(T // tm, Nz // tn),
        in_specs=[pl.BlockSpec((tm, D), lambda i, j: (i, 0)),
                  _mod_spec(sec + 1, D, tpb, nlat, nb),
                  _mod_spec(sec, D, tpb, nlat, nb),
                  pl.BlockSpec((1, D), lambda i, j: (0, 0)),
                  pl.BlockSpec((D, tn), lambda i, j: (0, j)),
                  pl.BlockSpec((D, LANES), lambda i, j: (0, 0))],
        out_specs=[pl.BlockSpec((tm, tn), lambda i, j: (i, j)),
                   pl.BlockSpec((tm, LANES), lambda i, j: (i, 0))],
        out_shape=[jax.ShapeDtypeStruct((T, Nz), bf16), jax.ShapeDtypeStruct((T, LANES), f32)],
        scratch_shapes=[pltpu.VMEM((tm, D), bf16)],
        compiler_params=_cparams(("parallel", "arbitrary")),
        name="in_proj",
    )(h, mod, mod, gain, w_z, w_ba)


def _conv_body(x_ref, xp_ref, xn_ref, w_ref, sh_ref, o_ref, *, tr, tps_lat, tps_ctx, nlat, qscale, tiles_per_sec,
               halo):
    i = pl.program_id(0)
    j = pl.program_id(1)
    il = jnp.where(i < nlat, i % tps_lat, (i - nlat) % tps_ctx)
    nt = jnp.where(i < nlat, tps_lat, tps_ctx)
    first = il == 0
    last = il == nt - 1
    xb = x_ref[...]
    w = w_ref[...]
    ys = jnp.dot(sh_ref[...], xb, preferred_element_type=f32)
    acc = (ys[0:tr] * w[0:1] + ys[tr:2 * tr] * w[1:2] + xb.astype(f32) * w[2:3]
           + ys[2 * tr:3 * tr] * w[3:4] + ys[3 * tr:] * w[4:5])
    p = jnp.where(first, 0.0, xp_ref[...].astype(f32))[halo - 8:]
    n = jnp.where(last, 0.0, xn_ref[...].astype(f32))[:8]
    r8 = lax.broadcasted_iota(jnp.int32, (8, 1), 0)
    top = jnp.where(r8 == 0, p[6:7] * w[0:1] + p[7:8] * w[1:2], jnp.where(r8 == 1, p[7:8] * w[0:1], 0.0))
    bot = jnp.where(r8 == 7, n[0:1] * w[3:4] + n[1:2] * w[4:5], jnp.where(r8 == 6, n[0:1] * w[4:5], 0.0))
    acc = jnp.concatenate([acc[:8] + top, acc[8:tr - 8], acc[tr - 8:] + bot], axis=0)
    y = acc * _sigmoid(acc)
    sec = j // tiles_per_sec
    tc = acc.shape[1]
    for s in range(tc // LANES):
        ys = y[:, s * LANES:(s + 1) * LANES]
        rs = lax.rsqrt(jnp.sum(ys * ys, -1, keepdims=True) + EPS)
        scale = jnp.where(sec == 2, 1.0, rs) * jnp.where(sec == 0, qscale, 1.0)
        o_ref[:, s * LANES:(s + 1) * LANES] = (ys * scale).astype(o_ref.dtype)


def _conv_prep(z, conv_w, *, tr, tc, S, C, nlat_rows, dn_w, dk):
    T = z.shape[0]
    halo = 16
    hb = tr // halo
    nh = T // halo
    body = functools.partial(_conv_body, tr=tr, tps_lat=S // tr, tps_ctx=C // tr, nlat=nlat_rows // tr,
                             qscale=float(dk) ** -0.5, tiles_per_sec=dn_w // tc, halo=halo)
    assert conv_w.shape[0] == 5
    ri = jnp.arange(tr)[:, None]
    ci = jnp.arange(tr)[None, :]
    shifts = jnp.concatenate([(ci == ri + off).astype(bf16) for off in (-2, -1, 1, 2)], 0)
    return pl.pallas_call(
        body,
        grid=(T // tr, 3 * dn_w // tc),
        in_specs=[pl.BlockSpec((tr, tc), lambda i, j: (i, j)),
                  pl.BlockSpec((halo, tc), lambda i, j: (jnp.maximum(i * hb - 1, 0), j)),
                  pl.BlockSpec((halo, tc), lambda i, j: (jnp.minimum((i + 1) * hb, nh - 1), j)),
                  pl.BlockSpec((conv_w.shape[0], tc), lambda i, j: (0, j)),
                  pl.BlockSpec((4 * tr, tr), lambda i, j: (0, 0))],
        out_specs=pl.BlockSpec((tr, tc), lambda i, j: (i, j)),
        out_shape=jax.ShapeDtypeStruct((T, 3 * dn_w), bf16),
        compiler_params=_cparams(("parallel", "parallel")),
        name="conv_prep",
    )(z, z, z, conv_w, shifts)


def _mm(x, y):
    return jnp.dot(x.astype(bf16), y.astype(bf16), preferred_element_type=f32)


def _unit_tri_inverse(a_list, ri, ci, interleave=None):
    b16 = (ri >> 4) == (ci >> 4)
    b32 = (ri >> 5) == (ci >> 5)
    eye = jnp.where(ri == ci, 1.0, 0.0)
    n = a_list[0].shape[0]
    a16 = [jnp.where(b16, a, 0.0) for a in a_list]
    t = [eye - x for x in a16]
    p = [_mm(x, x) for x in a16]
    for step in range(2):
        tp = [_mm(jnp.concatenate([ti, pi], axis=0), pi) for ti, pi in zip(t, p)]
        t = [ti + x[:n] for ti, x in zip(t, tp)]
        p = [x[n:] for x in tp]
        if step == 0 and interleave is not None:
            interleave()
    t = [ti + _mm(ti, pi) for ti, pi in zip(t, p)]
    a32 = [jnp.where(b32, a, 0.0) - x for a, x in zip(a_list, a16)]
    ta = [_mm(ti, ai) for ti, ai in zip(t, a32)]
    t = [ti - _mm(tai, ti) for ti, tai in zip(t, ta)]
    a64 = [jnp.where(b32, 0.0, a) for a in a_list]
    ta = [_mm(ti, ai) for ti, ai in zip(t, a64)]
    t = [ti - _mm(tai, ti) for ti, tai in zip(t, ta)]
    return t


def _dn_body(q_ref, k_ref, v_ref, ba_ref, al_ref, dt_ref, o_ref, s_scr, u_scr, wq_scr, qk_scr, eg_scr, *, G):
    d = pl.program_id(1)
    c = pl.program_id(3)
    dk = LANES
    hs = range(G)

    @pl.when(c == 0)
    def _():
        for ref in (s_scr, u_scr, wq_scr, qk_scr, eg_scr):
            ref[...] = jnp.zeros_like(ref)

    ba = ba_ref[0]
    beta_t = _sigmoid(ba)
    g_t = -jnp.exp(al_ref[0]) * _softplus(ba + dt_ref[0])
    ri = lax.broadcasted_iota(jnp.int32, (CHUNK, CHUNK), 0)
    ci = lax.broadcasted_iota(jnp.int32, (CHUNK, CHUNK), 1)
    fwd = d == 0
    later = jnp.where(fwd, ri, ci)
    earlier = jnp.where(fwd, ci, ri)
    incl = later >= earlier
    strict = later > earlier
    tri = jnp.where(incl, 1.0, 0.0)
    gc = jnp.dot(tri, g_t, preferred_element_type=f32, precision=HIGHEST)
    gct = gc.T
    gtot = jnp.sum(g_t, axis=0, keepdims=True)

    def pick(x, col):
        return jnp.where(fwd, x[:, col:col + 1], x[:, G + col:G + col + 1])

    sl = [slice(h * dk, (h + 1) * dk) for h in hs]
    beta_c = [pick(beta_t, h) for h in hs]
    gc_c = [pick(gc, 2 * G + h) for h in hs]
    gc_r = [jnp.where(fwd, gct[2 * G + h:2 * G + h + 1], gct[3 * G + h:3 * G + h + 1]) for h in hs]
    gt = [pick(gtot, 2 * G + h) for h in hs]
    q = [q_ref[:, sl[h]].astype(f32) for h in hs]
    k = [k_ref[:, sl[h]].astype(f32) for h in hs]
    kb = [k[h] * beta_c[h] for h in hs]
    eg = [jnp.exp(gc_c[h]) for h in hs]
    dec = [jnp.where(incl, jnp.exp(jnp.where(incl, gc_c[h] - gc_r[h], 0.0)), 0.0) for h in hs]
    kq = [lax.dot_general(jnp.concatenate([kb[h], q[h]], axis=0).astype(bf16), k[h].astype(bf16),
                          (((1,), (1,)), ((), ())), preferred_element_type=f32) for h in hs]
    s = [s_scr[h] for h in hs]
    r = [jnp.dot(wq_scr[h], s[h].astype(bf16), preferred_element_type=f32) for h in hs]
    a = [jnp.where(strict, kq[h][:CHUNK] * dec[h], 0.0) for h in hs]
    qk = [(kq[h][CHUNK:] * dec[h]).astype(bf16) for h in hs]

    def recurrence_tail():
        vnb = [(u_scr[h] - r[h][:CHUNK]).astype(bf16) for h in hs]
        ou = [jnp.dot(qk_scr[h], vnb[h], preferred_element_type=f32) for h in hs]
        for h in hs:
            s_scr[h] = s[h] * eg_scr[h] + ou[h][CHUNK:]
            o_ref[0, :, sl[h]] = (r[h][CHUNK:] + ou[h][:CHUNK]).astype(o_ref.dtype)

    t = _unit_tri_inverse(a, ri, ci, interleave=recurrence_tail)
    rhs = [jnp.concatenate([v_ref[:, sl[h]].astype(f32) * beta_c[h], kb[h] * eg[h]], axis=1) for h in hs]
    sol = [_mm(t[h], rhs[h]) for h in hs]
    for h in hs:
        u_scr[h] = sol[h][:, :dk]
        wq_scr[h] = jnp.concatenate([sol[h][:, dk:], q[h] * eg[h]], axis=0).astype(bf16)
        kd_t = (k[h] * jnp.exp(gt[h] - gc_c[h])).T.astype(bf16)
        qk_scr[h] = jnp.concatenate([qk[h], kd_t], axis=0)
        eg_scr[h] = jnp.broadcast_to(jnp.exp(gt[h]), (1, dk))


def _deltanet(qkv, ba, a_log, dt_bias, *, B, S, C, H, G):
    T = qkv.shape[0]
    dn_w = H * LANES
    ncl = S // CHUNK
    ncc = C // CHUNK
    nc = ncl + ncc
    hgs = H // G

    def rowblk(b, d, c):
        cc = jnp.where(d == 0, c, ncc - 1 - c)
        cl = jnp.where(d == 0, c - ncc, nc - 1 - c)
        return jnp.where(c < ncc, B * ncl + b * ncc + cc, b * ncl + cl)

    def grouped(t):
        r = t.shape[0]
        t = t.reshape(r, 4, hgs, G).transpose(2, 0, 1, 3).reshape(hgs, r, 4 * G)
        return jnp.pad(t, ((0, 0), (0, 0), (0, LANES - 4 * G)))

    zeros = jnp.zeros((1, 2 * H), f32)
    ba_g = grouped(ba[:, :4 * H])
    al_g = grouped(jnp.concatenate([zeros, a_log.reshape(1, 2 * H)], 1))
    dt_g = grouped(jnp.concatenate([zeros, dt_bias.reshape(1, 2 * H)], 1))
    body = functools.partial(_dn_body, G=G)

    def blk_in(b, d, c):
        return rowblk(b, d, jnp.minimum(c, nc - 1))

    def blk_out(b, d, c):
        return rowblk(b, d, jnp.maximum(c - 1, 0))

    return pl.pallas_call(
        body,
        grid=(B, 2, hgs, nc + 1),
        in_specs=[pl.BlockSpec((CHUNK, G * LANES), lambda b, d, g, c: (blk_in(b, d, c), g)),
                  pl.BlockSpec((CHUNK, G * LANES), lambda b, d, g, c: (blk_in(b, d, c), hgs + g)),
                  pl.BlockSpec((CHUNK, G * LANES), lambda b, d, g, c: (blk_in(b, d, c), 2 * hgs + g)),
                  pl.BlockSpec((1, CHUNK, LANES), lambda b, d, g, c: (g, blk_in(b, d, c), 0)),
                  pl.BlockSpec((1, 1, LANES), lambda b, d, g, c: (g, 0, 0)),
                  pl.BlockSpec((1, 1, LANES), lambda b, d, g, c: (g, 0, 0))],
        out_specs=pl.BlockSpec((1, CHUNK, G * LANES), lambda b, d, g, c: (d, blk_out(b, d, c), g)),
        out_shape=jax.ShapeDtypeStruct((2, T, dn_w), bf16),
        scratch_shapes=[pltpu.VMEM((G, LANES, LANES), f32),
                        pltpu.VMEM((G, CHUNK, LANES), f32),
                        pltpu.VMEM((G, 2 * CHUNK, LANES), bf16),
                        pltpu.VMEM((G, CHUNK + LANES, CHUNK), bf16),
                        pltpu.VMEM((G, 1, LANES), f32)],
        compiler_params=_cparams(("parallel", "parallel", "parallel", "arbitrary")),
        name="deltanet",
    )(qkv, qkv, qkv, ba_g, al_g, dt_g)


def _pool_ctx_body(x_ref, o_ref, *, gw):
    C = x_ref.shape[0]
    ri = lax.broadcasted_iota(jnp.int32, (C, C), 0)
    ci = lax.broadcasted_iota(jnp.int32, (C, C), 1)
    t = lax.broadcasted_iota(jnp.int32, (C, 1), 0)
    for gi, w in enumerate(POOL_WINDOWS):
        sl = slice(gi * gw, (gi + 1) * gw)
        band = jnp.where((ci >= ri - w // 2) & (ci < ri + w - w // 2), 1.0, 0.0).astype(bf16)
        cnt = (jnp.minimum(t + w - w // 2, C) - jnp.maximum(t - w // 2, 0)).astype(f32)
        x = x_ref[:, sl]
        box = jnp.dot(band, x, preferred_element_type=f32)
        o_ref[:, sl] = (box / cnt - x.astype(f32)).astype(o_ref.dtype)


def _pool_lat_body(x_ref, o_ref, y_scr):
    g = pl.program_id(1)
    S = x_ref.shape[0]
    rows = S // GRID_W
    sb = 4 * GRID_W
    ri = lax.broadcasted_iota(jnp.int32, (sb, sb), 0)
    ci = lax.broadcasted_iota(jnp.int32, (sb, sb), 1)
    cr = ri & (GRID_W - 1)
    cc = ci & (GRID_W - 1)
    same = (ri >> 6) == (ci >> 6)
    col = lax.broadcasted_iota(jnp.int32, (GRID_W, 1), 0)
    for gi, w in enumerate(POOL_WINDOWS):
        @pl.when(g == gi)
        def _(w=w):
            band = jnp.where(same & (cc >= cr - w // 2) & (cc < cr + w - w // 2), 1.0, 0.0).astype(bf16)
            for s in range(S // sb):
                y_scr[s * sb:(s + 1) * sb, :] = jnp.dot(band, x_ref[s * sb:(s + 1) * sb, :],
                                                        preferred_element_type=f32)
            ncol = (jnp.minimum(col + w - w // 2, GRID_W) - jnp.maximum(col - w // 2, 0)).astype(f32)
            for r in range(rows):
                lo = max(r - w // 2, 0)
                hi = min(r + w - w // 2, rows)
                acc = y_scr[lo * GRID_W:(lo + 1) * GRID_W, :]
                for rr in range(lo + 1, hi):
                    acc = acc + y_scr[rr * GRID_W:(rr + 1) * GRID_W, :]
                cnt = ncol * float(hi - lo)
                xs = x_ref[r * GRID_W:(r + 1) * GRID_W, :].astype(f32)
                o_ref[r * GRID_W:(r + 1) * GRID_W, :] = (acc / cnt - xs).astype(o_ref.dtype)


def _pool(z, *, B, S, C, pool_off_blocks, pool_w, gw):
    nlat_c = B * S // C
    ng = pool_w // gw
    assert ng == len(POOL_WINDOWS)
    pooled_ctx = pl.pallas_call(
        functools.partial(_pool_ctx_body, gw=gw),
        grid=(B,),
        in_specs=[pl.BlockSpec((C, pool_w), lambda b: (nlat_c + b, pool_off_blocks))],
        out_specs=pl.BlockSpec((C, pool_w), lambda b: (b, 0)),
        out_shape=jax.ShapeDtypeStruct((B * C, pool_w), bf16),
        compiler_params=_cparams(("parallel",)),
        name="pool_ctx",
    )(z)
    off = pool_off_blocks * ng
    pooled_lat = pl.pallas_call(
        _pool_lat_body,
        grid=(B, ng),
        in_specs=[pl.BlockSpec((S, gw), lambda b, g: (b, off + g))],
        out_specs=pl.BlockSpec((S, gw), lambda b, g: (b, g)),
        out_shape=jax.ShapeDtypeStruct((B * S, pool_w), bf16),
        scratch_shapes=[pltpu.VMEM((S, gw), f32)],
        compiler_params=_cparams(("parallel", "parallel")),
        name="pool_lat",
    )(z)
    return pooled_lat, pooled_ctx


def _merge_a_body(o_ref, og_ref, m1_ref, m2_ref, pl_ref, pc_ref, ong_ref, wpool_ref, ps_ref, wud_ref, wup_ref,
                  out_ref, a_scr, p_scr, *, H, gw, nlat):
    is_lat = pl.program_id(0) < nlat
    for h in range(H):
        sl = slice(h * LANES, (h + 1) * LANES)
        o = o_ref[0, :, sl].astype(f32) + o_ref[1, :, sl].astype(f32)
        og = og_ref[:, sl].astype(f32)
        y = o * lax.rsqrt(jnp.mean(o * o, -1, keepdims=True) + EPS) * ong_ref[...] * (og * _sigmoid(og))
        a_scr[:, sl] = y.astype(bf16)
    y_dn = jnp.dot(a_scr[...], wud_ref[...], preferred_element_type=f32)
    for g in range(wpool_ref.shape[0]):
        sl = slice(g * gw, (g + 1) * gw)
        pooled = jnp.where(is_lat, pl_ref[:, sl], pc_ref[:, sl])
        yp = jnp.dot(pooled, wpool_ref[g], preferred_element_type=f32) * ps_ref[:, sl]
        p_scr[:, sl] = yp.astype(bf16)
    y_pool = jnp.dot(p_scr[...], wup_ref[...], preferred_element_type=f32)
    g1 = _sigmoid(m1_ref[...].astype(f32))
    g2 = _sigmoid(m2_ref[...].astype(f32))
    out_ref[...] = (g1 * y_dn + g2 * y_pool).astype(out_ref.dtype)


def _merge_a(o_dn, z, pooled_lat, pooled_ctx, onorm_g, w_pool, pool_scale, w_up_dn, w_up_pool, *, n_rows, tm, H, D):
    dn_w = H * LANES
    pool_w = pooled_lat.shape[1]
    nlat = pooled_lat.shape[0] // tm
    ng, gw, _ = w_pool.shape
    og_blk = 3
    m_blk = 4 * dn_w // D
    const = lambda *_: (0, 0)
    return pl.pallas_call(
        functools.partial(_merge_a_body, H=H, gw=gw, nlat=nlat),
        grid=(n_rows // tm,),
        in_specs=[pl.BlockSpec((2, tm, dn_w), lambda i: (0, i, 0)),
                  pl.BlockSpec((tm, dn_w), lambda i: (i, og_blk)),
                  pl.BlockSpec((tm, D), lambda i: (i, m_blk)),
                  pl.BlockSpec((tm, D), lambda i: (i, m_blk + 1)),
                  pl.BlockSpec((tm, pool_w), lambda i: (jnp.minimum(i, nlat - 1), 0)),
                  pl.BlockSpec((tm, pool_w), lambda i: (jnp.maximum(i - nlat, 0), 0)),
                  pl.BlockSpec((1, LANES), const),
                  pl.BlockSpec((ng, gw, gw), lambda i: (0, 0, 0)),
                  pl.BlockSpec((1, pool_w), const),
                  pl.BlockSpec((dn_w, D), const, pipeline_mode=pl.Buffered(1)),
                  pl.BlockSpec((pool_w, D), const, pipeline_mode=pl.Buffered(1))],
        out_specs=pl.BlockSpec((tm, D), lambda i: (i, 0)),
        out_shape=jax.ShapeDtypeStruct((n_rows, D), bf16),
        scratch_shapes=[pltpu.VMEM((tm, dn_w), bf16), pltpu.VMEM((tm, pool_w), bf16)],
        compiler_params=_cparams(("parallel",)),
        name="merge_a",
    )(o_dn, z, z, z, pooled_lat, pooled_ctx, onorm_g, w_pool, pool_scale, w_up_dn, w_up_pool)


def _merge_b_body(m_ref, h_ref, gt_ref, sc_ref, sh_ref, g_ref, wout_ref, wr_ref, h1_ref, v_ref, lg_ref,
                  *, ds):
    tm = m_ref.shape[0]
    m = jnp.dot(m_ref[...], wout_ref[...], preferred_element_type=f32)
    h1 = h_ref[...] + gt_ref[0] * m
    h1_ref[...] = h1
    y = h1 * lax.rsqrt(jnp.mean(h1 * h1, -1, keepdims=True) + EPS) * g_ref[...]
    v = y * (1.0 + sc_ref[0]) + sh_ref[0]
    E = lg_ref.shape[1]
    vh = v.astype(bf16)
    vl = (v - vh.astype(f32)).astype(bf16)
    th = jnp.dot(vh, wr_ref[...], preferred_element_type=f32)
    tl = jnp.dot(vl, wr_ref[...], preferred_element_type=f32)
    lg_ref[...] = th[:, :E] + (th[:, E:] + tl[:, :E])
    for j in range(ds):
        v_ref[pl.ds(j, tm, stride=ds), :] = v[:, j * LANES:(j + 1) * LANES]


def _merge_b(merged, h, mod, gain2, w_out, w_router, *, n_rows, tm, tpb, nlat, nb):
    D = h.shape[1]
    E = w_router.shape[1]
    ds = D // LANES
    const = lambda *_: (0, 0)
    wr_hi = w_router.astype(bf16)
    wr_lo = (w_router - wr_hi.astype(f32)).astype(bf16)
    wr_cat = jnp.concatenate([wr_hi, wr_lo], 1)
    return pl.pallas_call(
        functools.partial(_merge_b_body, ds=ds),
        grid=(n_rows // tm,),
        in_specs=[pl.BlockSpec((tm, D), lambda i: (i, 0)),
                  pl.BlockSpec((tm, D), lambda i: (i, 0)),
                  _mod_spec(2, D, tpb, nlat, nb),
                  _mod_spec(4, D, tpb, nlat, nb),
                  _mod_spec(3, D, tpb, nlat, nb),
                  pl.BlockSpec((1, D), const),
                  pl.BlockSpec((D, D), const, pipeline_mode=pl.Buffered(1)),
                  pl.BlockSpec((D, 2 * E), const)],
        out_specs=[pl.BlockSpec((tm, D), lambda i: (i, 0)),
                   pl.BlockSpec((tm * ds, LANES), lambda i: (i, 0)),
                   pl.BlockSpec((tm, E), lambda i: (i, 0))],
        out_shape=[jax.ShapeDtypeStruct((n_rows, D), f32),
                   jax.ShapeDtypeStruct((n_rows * ds, LANES), f32),
                   jax.ShapeDtypeStruct((n_rows, E), f32)],
        compiler_params=_cparams(("parallel",)),
        name="merge_b",
    )(merged, h, mod, mod, mod, gain2, w_out, wr_cat)


def _route_body(lg_ref, bias_ref, idx_ref, gate_ref, cnt_ref, carry_scr, *, n_groups):
    i = pl.program_id(0)
    E, tt = lg_ref.shape

    @pl.when(i == 0)
    def _():
        carry_scr[...] = jnp.zeros_like(carry_scr)

    aff = _sigmoid(lg_ref[...])
    sel = aff + bias_ref[...]
    sub = lax.broadcasted_iota(jnp.int32, (GROUP_SIZE, tt), 0)
    neg = -1e30
    best = None
    for g in range(n_groups):
        s = sel[g * GROUP_SIZE:(g + 1) * GROUP_SIZE]
        a = aff[g * GROUP_SIZE:(g + 1) * GROUP_SIZE]
        m1 = jnp.max(s, axis=0, keepdims=True)
        i1 = jnp.min(jnp.where(s == m1, sub, GROUP_SIZE), axis=0, keepdims=True)
        s2 = jnp.where(sub == i1, neg, s)
        m2 = jnp.max(s2, axis=0, keepdims=True)
        i2 = jnp.min(jnp.where(s2 == m2, sub, GROUP_SIZE), axis=0, keepdims=True)
        a1 = jnp.sum(jnp.where(sub == i1, a, 0.0), axis=0, keepdims=True)
        a2 = jnp.sum(jnp.where(sub == i2, a, 0.0), axis=0, keepdims=True)
        cand = (m1 + m2, i1 + g * GROUP_SIZE, i2 + g * GROUP_SIZE, a1, a2)
        if best is None:
            best = cand
        else:
            better = cand[0] > best[0]
            best = tuple(jnp.where(better, cn, bs) for cn, bs in zip(cand, best))
    _, e1, e2, a1, a2 = best
    den = a1 + a2
    gate_ref[0:1, :] = a1 / den
    gate_ref[1:2, :] = a2 / den
    erow = lax.broadcasted_iota(jnp.int32, (E, tt), 0)
    oh1 = erow == e1
    oh2 = erow == e2
    cnt = jnp.where(oh1 | oh2, 1.0, 0.0)
    ti = lax.broadcasted_iota(jnp.int32, (tt, tt), 0)
    tj = lax.broadcasted_iota(jnp.int32, (tt, tt), 1)
    before = jnp.where(ti < tj, 1.0, 0.0).astype(bf16)
    pos = jnp.dot(cnt.astype(bf16), before, preferred_element_type=f32) + carry_scr[...]
    r1 = jnp.sum(jnp.where(oh1, pos, 0.0), axis=0, keepdims=True)
    r2 = jnp.sum(jnp.where(oh2, pos, 0.0), axis=0, keepdims=True)
    idx_ref[0:1, :] = e1
    idx_ref[1:2, :] = e2
    idx_ref[2:3, :] = r1.astype(jnp.int32)
    idx_ref[3:4, :] = r2.astype(jnp.int32)
    carry_scr[...] = carry_scr[...] + jnp.sum(cnt, axis=1, keepdims=True)
    cnt_ref[...] = jnp.broadcast_to(carry_scr[...], cnt_ref.shape)


def _route(logits_t, bias_col, *, tt):
    E, T = logits_t.shape
    return pl.pallas_call(
        functools.partial(_route_body, n_groups=E // GROUP_SIZE),
        grid=(T // tt,),
        in_specs=[pl.BlockSpec((E, tt), lambda i: (0, i)),
                  pl.BlockSpec((E, 1), lambda i: (0, 0))],
        out_specs=[pl.BlockSpec((4, tt), lambda i: (0, i)),
                   pl.BlockSpec((2, tt), lambda i: (0, i)),
                   pl.BlockSpec((E, LANES), lambda i: (0, 0))],
        out_shape=[jax.ShapeDtypeStruct((4, T), jnp.int32),
                   jax.ShapeDtypeStruct((2, T), f32),
                   jax.ShapeDtypeStruct((E, LANES), f32)],
        scratch_shapes=[pltpu.VMEM((E, 1), f32)],
        compiler_params=_cparams(("arbitrary",)),
        name="route",
    )(logits_t, bias_col)


def _row_copy(src, dst, s, t, n, sem):
    def rows(x):
        return pl.ds(x * n if isinstance(x, int) else pl.multiple_of(x * n, 8), n)

    return pltpu.make_async_copy(src.at[rows(s), :], dst.at[rows(t), :], sem)


def _dispatch_body(dest_ref, pad_ref, v_ref, xs_hbm, zero_scr, sem, zsem, *, tt, ds, n_exp, n_blocks, rb):
    i = pl.program_id(0)

    @pl.when(i == 0)
    def _():
        zero_scr[...] = jnp.zeros_like(zero_scr)

        def per_expert(e, n):
            start = pad_ref[0, e]

            def one(r, m):
                _row_copy(zero_scr, xs_hbm, 0, start + r, ds, zsem).start()
                return m + 1

            return lax.fori_loop(0, pad_ref[1, e], one, n)

        total = lax.fori_loop(0, n_exp, per_expert, 0)

        def wait_one(r, m):
            _row_copy(zero_scr, xs_hbm, 0, 0, ds, zsem).wait()
            return m

        lax.fori_loop(0, total, wait_one, 0)
        n_used = pad_ref[2, 0]

        def blk_start(bk, m):
            _row_copy(zero_scr, xs_hbm, 0, bk, rb * ds, zsem).start()
            return m

        lax.fori_loop(n_used, n_blocks, blk_start, 0)

        def blk_wait(bk, m):
            _row_copy(zero_scr, xs_hbm, 0, 0, rb * ds, zsem).wait()
            return m

        lax.fori_loop(n_used, n_blocks, blk_wait, 0)

    def issue(r, m):
        src = v_ref.at[pl.ds(pl.multiple_of(r * ds, ds), ds), :]
        for kk in range(2):
            dst = xs_hbm.at[pl.ds(pl.multiple_of(dest_ref[0, kk, r] * ds, ds), ds), :]
            pltpu.make_async_copy(src, dst, sem).start(priority=kk)
        return m

    lax.fori_loop(0, tt, issue, 0, unroll=8)
    for kk in range(2):
        pltpu.make_async_copy(v_ref, xs_hbm.at[pl.ds(0, tt * ds), :], sem).wait()


def _dispatch(dest3, pad_info, v_ts, *, n_tok, n_blocks, tt, ds):
    n_exp = pad_info.shape[1]
    rb = EXPERT_BLOCK
    return pl.pallas_call(
        functools.partial(_dispatch_body, tt=tt, ds=ds, n_exp=n_exp, n_blocks=n_blocks, rb=rb),
        grid=(n_tok // tt,),
        in_specs=[pl.BlockSpec((1, 2, tt), lambda i: (i, 0, 0), memory_space=pltpu.SMEM),
                  pl.BlockSpec(memory_space=pltpu.SMEM),
                  pl.BlockSpec((tt * ds, LANES), lambda i: (i, 0))],
        out_specs=pl.BlockSpec(memory_space=pl.ANY),
        out_shape=jax.ShapeDtypeStruct((n_blocks * rb * ds, LANES), f32),
        scratch_shapes=[pltpu.VMEM((rb * ds, LANES), f32), pltpu.SemaphoreType.DMA(()), pltpu.SemaphoreType.DMA(())],
        compiler_params=_cparams(("arbitrary",)),
        name="dispatch",
    )(dest3, pad_info, v_ts)


def _expert_body(be_ref, nu_ref, x_ref, wg_ref, wu_ref, wd_ref, y_ref, x_s, *, ds, rb):
    b = pl.program_id(0)
    used = b < nu_ref[0]

    @pl.when(used)
    def _():
        for j in range(ds):
            x_s[:, j * LANES:(j + 1) * LANES] = x_ref[pl.ds(j, rb, stride=ds), :].astype(bf16)
        x = x_s[...]
        hg = jnp.dot(x, wg_ref[0, 0].astype(bf16), preferred_element_type=f32)
        hu = jnp.dot(x, wu_ref[0, 0].astype(bf16), preferred_element_type=f32)
        a = (hg * _sigmoid(hg) * hu).astype(bf16)
        y = jnp.dot(a, wd_ref[0, 0].astype(bf16), preferred_element_type=f32)
        for j in range(ds):
            y_ref[pl.ds(j, rb, stride=ds), :] = y[:, j * LANES:(j + 1) * LANES]

    @pl.when(jnp.logical_not(used))
    def _():
        y_ref[...] = jnp.zeros_like(y_ref)


def _experts(block_expert, n_used, xs_ts, w_g, w_u, w_d, *, layer, ds):
    _, E, D, De = w_g.shape
    rb = EXPERT_BLOCK
    n_blocks = xs_ts.shape[0] // (rb * ds)
    grid_spec = pltpu.PrefetchScalarGridSpec(
        num_scalar_prefetch=2,
        grid=(n_blocks,),
        in_specs=[pl.BlockSpec((rb * ds, LANES), lambda b, be, nu: (b, 0)),
                  pl.BlockSpec((1, 1, D, De), lambda b, be, nu: (layer, be[b], 0, 0)),
                  pl.BlockSpec((1, 1, D, De), lambda b, be, nu: (layer, be[b], 0, 0)),
                  pl.BlockSpec((1, 1, De, D), lambda b, be, nu: (layer, be[b], 0, 0))],
        out_specs=pl.BlockSpec((rb * ds, LANES), lambda b, be, nu: (b, 0)),
        scratch_shapes=[pltpu.VMEM((rb, D), bf16)],
    )
    return pl.pallas_call(
        functools.partial(_expert_body, ds=ds, rb=rb),
        grid_spec=grid_spec,
        out_shape=jax.ShapeDtypeStruct(xs_ts.shape, f32),
        compiler_params=_cparams(("arbitrary",)),
        name="experts",
    )(block_expert, n_used, xs_ts, w_g, w_u, w_d)


def _combine_body(dest_ref, dnext_ref, gate_ref, h_ref, gt_ref, fg_ref, y_hbm, o_ref, y_scr, sem, *, tt, ds, final):
    i = pl.program_id(0)
    slot = i % 2

    def gather(d_ref, to_slot):
        def body(r, m):
            for kk in range(2):
                _row_copy(y_hbm, y_scr.at[to_slot, kk], d_ref[0, kk, r], r, ds, sem.at[to_slot]).start(priority=kk)
            return m

        lax.fori_loop(0, tt, body, 0, unroll=8)

    @pl.when(i == 0)
    def _():
        gather(dest_ref, 0)

    @pl.when(i + 1 < pl.num_programs(0))
    def _():
        gather(dnext_ref, 1 - slot)

    for kk in range(2):
        pltpu.make_async_copy(y_hbm.at[pl.ds(0, tt * ds), :], y_scr.at[slot, kk], sem.at[slot]).wait()
    gt = gt_ref[0]
    g0 = gate_ref[:, 0:1]
    g1 = gate_ref[:, 1:2]
    for j in range(ds):
        sl = slice(j * LANES, (j + 1) * LANES)
        f = (g0 * y_scr[slot, 0, pl.ds(j, tt, stride=ds), :] + g1 * y_scr[slot, 1, pl.ds(j, tt, stride=ds), :])
        o_ref[:, sl] = h_ref[:, sl] + gt[:, sl] * f
    if final:
        x = o_ref[...]
        o_ref[...] = x * lax.rsqrt(jnp.mean(x * x, -1, keepdims=True) + EPS) * fg_ref[...]


def _combine(dest3, gates_t, h1, mod, final_g, y_ts, *, n_tok, tt, ds, tpb, nlat, nb, final):
    D = h1.shape[1]
    nt = n_tok // tt
    return pl.pallas_call(
        functools.partial(_combine_body, tt=tt, ds=ds, final=final),
        grid=(nt,),
        in_specs=[pl.BlockSpec((1, 2, tt), lambda i: (i, 0, 0), memory_space=pltpu.SMEM),
                  pl.BlockSpec((1, 2, tt), lambda i: (jnp.minimum(i + 1, nt - 1), 0, 0), memory_space=pltpu.SMEM),
                  pl.BlockSpec((tt, 2), lambda i: (i, 0)),
                  pl.BlockSpec((tt, D), lambda i: (i, 0)),
                  _mod_spec(5, D, tpb, nlat, nb),
                  pl.BlockSpec((1, D), lambda i: (0, 0)),
                  pl.BlockSpec(memory_space=pl.ANY)],
        out_specs=pl.BlockSpec((tt, D), lambda i: (i, 0)),
        out_shape=jax.ShapeDtypeStruct((n_tok, D), f32),
        scratch_shapes=[pltpu.VMEM((2, 2, tt * ds, LANES), f32), pltpu.SemaphoreType.DMA((2,))],
        compiler_params=_cparams(("arbitrary",)),
        name="combine",
    )(dest3, dest3, gates_t, h1, mod, final_g, y_ts)


def _moe(v_ts, logits, h1, mod, w_router_bias, w_g, w_u, w_d, final_g, *, layer, n_tok, tt, ds, tpb, nlat, nb,
         final):
    E = w_g.shape[1]
    route_tt = next(t for t in (512, 256, LANES) if n_tok % t == 0)
    idx, gates, counts = _route(logits[:n_tok].T, w_router_bias.reshape(E, 1), tt=route_tt)
    counts = counts[:, 0].astype(jnp.int32)
    padded = (counts + EXPERT_BLOCK - 1) // EXPERT_BLOCK * EXPERT_BLOCK
    pad_end = jnp.cumsum(padded)
    pad_start = pad_end - padded
    n_blocks = (n_tok * 2 + EXPERT_BLOCK - 1) // EXPERT_BLOCK + E
    n_used = (pad_end[-1] // EXPERT_BLOCK).astype(jnp.int32).reshape(1)
    blk = jnp.minimum(jnp.arange(n_blocks, dtype=jnp.int32), jnp.maximum(n_used[0] - 1, 0)) * EXPERT_BLOCK
    block_expert = jnp.minimum(jnp.sum(blk[:, None] >= pad_end[None, :], 1), E - 1).astype(jnp.int32)
    e_sel = idx[0:2][..., None] == jnp.arange(E, dtype=jnp.int32)
    dest = jnp.sum(jnp.where(e_sel, pad_start.astype(jnp.int32), 0), -1) + idx[2:4]
    dest3 = dest.reshape(2, n_tok // tt, tt).transpose(1, 0, 2)
    pad_info = jnp.stack([pad_start + counts, padded - counts, jnp.broadcast_to(n_used, (E,))]).astype(jnp.int32)
    xs_ts = _dispatch(dest3, pad_info, v_ts, n_tok=n_tok, n_blocks=n_blocks, tt=tt, ds=ds)
    y_ts = _experts(block_expert, n_used, xs_ts, w_g, w_u, w_d, layer=layer, ds=ds)
    return _combine(dest3, gates.T, h1, mod, final_g, y_ts, n_tok=n_tok, tt=tt, ds=ds, tpb=tpb, nlat=nlat,
                    nb=nb, final=final)


def kernel(x, c, ctx, c_ctx, w_mod, b_mod, norm1_g, norm2_g, w_in, conv_w, a_log, dt_bias, onorm_g, w_pool, pool_scale, w_up_dn, w_up_pool, w_out, w_router, router_bias, w_gate_e, w_up_e, w_down_e, final_g):
    B, S, D = x.shape
    C = ctx.shape[1]
    L = w_mod.shape[0]
    H = a_log.shape[-1]
    dk = onorm_g.shape[-1]
    assert dk == LANES
    dn_w = H * dk
    _, ng, gw, _ = w_pool.shape
    pool_w = ng * gw
    ds = D // LANES
    n_lat = B * S
    T = n_lat + B * C
    tm = 256
    assert S % tm == 0 and (B * C) % tm == 0 and C % CHUNK == 0 and S % (4 * GRID_W) == 0
    tpb = S // tm
    nlat_tiles = n_lat // tm
    tm_in = next(t for t in (1024, 512, 256) if S % t == 0 and (B * C) % t == 0)
    tm_mg = next(t for t in (512, 256) if S % t == 0 and (B * C) % t == 0)
    tr = min(256, C)
    G = min(H, 16)

    c8 = jnp.zeros((8, D), f32).at[:B].set(c).at[B].set(c_ctx)
    mod = _mod_table(c8, w_mod, b_mod)

    qkv_c = 3 * dn_w
    scan_c = qkv_c + 4 * H
    og_off, pool_off = scan_c, scan_c + dn_w
    mg_off = pool_off + pool_w
    assert (4 * dn_w) % D == 0 and (4 * dn_w + 2 * D) % pool_w == 0
    pool_off_blocks = (4 * dn_w + 2 * D) // pool_w

    h = jnp.concatenate([x.reshape(n_lat, D), ctx.reshape(B * C, D)], 0)
    for l in range(L):
        last = l == L - 1
        mod_l = mod[l].reshape(8, 1, mod.shape[-1])
        wl = w_in[l]
        w_z = jnp.concatenate([wl[:, :qkv_c], wl[:, og_off:pool_off], wl[:, mg_off:], wl[:, pool_off:mg_off]],
                              1).astype(bf16)
        w_ba = jnp.pad(wl[:, qkv_c:scan_c], ((0, 0), (0, LANES - 4 * H))).astype(bf16)
        tn = 1024 if w_z.shape[1] % 1024 == 0 else w_z.shape[1]
        z, ba = _in_proj(h, mod_l, 0, norm1_g[l].reshape(1, D), w_z, w_ba, tm=tm_in, tn=tn, tpb=S // tm_in,
                         nlat=n_lat // tm_in, nb=B)
        qkv = _conv_prep(z, conv_w[l], tr=tr, tc=min(1024, dn_w), S=S, C=C, nlat_rows=n_lat, dn_w=dn_w, dk=dk)
        o_dn = _deltanet(qkv, ba, a_log[l], dt_bias[l], B=B, S=S, C=C, H=H, G=G)
        pooled_lat, pooled_ctx = _pool(z, B=B, S=S, C=C, pool_off_blocks=pool_off_blocks, pool_w=pool_w, gw=gw)
        n_rows = n_lat if last else T
        merged = _merge_a(o_dn, z, pooled_lat, pooled_ctx, onorm_g[l].reshape(1, dk), w_pool[l].astype(bf16),
                          pool_scale[l].reshape(1, pool_w), w_up_dn[l].astype(bf16), w_up_pool[l].astype(bf16),
                          n_rows=n_rows, tm=tm_mg, H=H, D=D)
        h1, v_ts, logits = _merge_b(merged, h, mod_l, norm2_g[l].reshape(1, D), w_out[l].astype(bf16), w_router,
                                    n_rows=n_rows, tm=tm_mg, tpb=S // tm_mg, nlat=n_lat // tm_mg, nb=B)
        h = _moe(v_ts, logits, h1, mod_l, router_bias, w_gate_e, w_up_e, w_down_e,
                 final_g.reshape(1, D), layer=l, n_tok=n_rows, tt=tm, ds=ds, tpb=tpb, nlat=nlat_tiles, nb=B, final=last)
    return h.reshape(B, S, D)
```

```python
import functools

import jax
import jax.numpy as jnp
from jax import lax
from jax.experimental import pallas as pl
from jax.experimental.pallas import tpu as pltpu

f32 = jnp.float32
bf16 = jnp.bfloat16
HIGHEST = lax.Precision.HIGHEST

EPS = 1e-6
LANES = 128
GRID_W = 64
CHUNK = 64
POOL_WINDOWS = (2, 4, 8, 16)
GROUP_SIZE = 8
EXPERT_BLOCK = 256
VMEM_LIMIT = 56 * 1024 * 1024


def _cparams(sem, vmem=VMEM_LIMIT):
    return pltpu.CompilerParams(dimension_semantics=sem, vmem_limit_bytes=vmem)


def _sigmoid(x):
    return 1.0 / (1.0 + jnp.exp(-x))


def _softplus(x):
    return jnp.maximum(x, 0.0) + jnp.log(1.0 + jnp.exp(-jnp.abs(x)))


def _pack_pair(lo, hi):
    def rounded(x):
        u = lax.bitcast_convert_type(x, jnp.uint32)
        return u + jnp.uint32(0x7FFF) + ((u >> 16) & jnp.uint32(1))

    return (rounded(lo) >> 16) | (rounded(hi) & jnp.uint32(0xFFFF0000))


def _unpack_pair(w):
    lo = lax.bitcast_convert_type(w << 16, f32)
    hi = lax.bitcast_convert_type(w & jnp.uint32(0xFFFF0000), f32)
    return lo, hi


def _mod_spec(sec, D, tiles_per_batch, n_lat_tiles, n_batch):
    def index(i, *_):
        return (jnp.where(i < n_lat_tiles, i // tiles_per_batch, n_batch), 0, sec)

    return pl.BlockSpec((1, 1, D), index)


def _mod_body(c_ref, w_ref, b_ref, o_ref):
    c = c_ref[...]
    s = c * _sigmoid(c)
    o_ref[0] = jnp.dot(s, w_ref[0], preferred_element_type=f32, precision=HIGHEST) + b_ref[0]


def _mod_table(c8, w_mod, b_mod):
    L, D, N = w_mod.shape
    tn = next(t for t in (1024, 512, 256, LANES) if N % t == 0)
    return pl.pallas_call(
        _mod_body,
        grid=(L, N // tn),
        in_specs=[pl.BlockSpec((8, D), lambda l, j: (0, 0)),
                  pl.BlockSpec((1, D, tn), lambda l, j: (l, 0, j)),
                  pl.BlockSpec((1, 1, tn), lambda l, j: (l, 0, j))],
        out_specs=pl.BlockSpec((1, 8, tn), lambda l, j: (l, 0, j)),
        out_shape=jax.ShapeDtypeStruct((L, 8, N), f32),
        compiler_params=_cparams(("parallel", "parallel")),
        name="mod_table",
    )(c8, w_mod, b_mod.reshape(L, 1, N))


def _inproj_body(h_ref, sc_ref, sh_ref, g_ref, w_ref, wba_ref, z_ref, ba_ref, u_scr):
    j = pl.program_id(1)

    @pl.when(j == 0)
    def _():
        x = h_ref[...]
        y = x * lax.rsqrt(jnp.mean(x * x, -1, keepdims=True) + EPS) * g_ref[...]
        u = (y * (1.0 + sc_ref[0]) + sh_ref[0]).astype(bf16)
        u_scr[...] = u
        ba_ref[...] = jnp.dot(u, wba_ref[...], preferred_element_type=f32)

    z_ref[...] = jnp.dot(u_scr[...], w_ref[...], preferred_element_type=f32).astype(z_ref.dtype)


def _in_proj(h, mod, sec, gain, w_z, w_ba, *, tm, tn, tpb, nlat, nb):
    T, D = h.shape
    Nz = w_z.shape[1]
    return pl.pallas_call(
        _inproj_body,
        grid=(T // tm, Nz // tn),
        in_specs=[pl.BlockSpec((tm, D), lambda i, j: (i, 0)),
                  _mod_spec(sec + 1, D, tpb, nlat, nb),
                  _mod_spec(sec, D, tpb, nlat, nb),
                  pl.BlockSpec((1, D), lambda i, j: (0, 0)),
                  pl.BlockSpec((D, tn), lambda i, j: (0, j)),
                  pl.BlockSpec((D, LANES), lambda i, j: (0, 0))],
        out_specs=[pl.BlockSpec((tm, tn), lambda i, j: (i, j)),
                   pl.BlockSpec((tm, LANES), lambda i, j: (i, 0))],
        out_shape=[jax.ShapeDtypeStruct((T, Nz), bf16), jax.ShapeDtypeStruct((T, LANES), f32)],
        scratch_shapes=[pltpu.VMEM((tm, D), bf16)],
        compiler_params=_cparams(("parallel", "arbitrary")),
        name="in_proj",
    )(h, mod, mod, gain, w_z, w_ba)


def _conv_body(x_ref, xp_ref, xn_ref, w_ref, sh_ref, o_ref, *, tr, tps_lat, tps_ctx, nlat, qscale, tiles_per_sec,
               halo):
    i = pl.program_id(0)
    j = pl.program_id(1)
    il = jnp.where(i < nlat, i % tps_lat, (i - nlat) % tps_ctx)
    nt = jnp.where(i < nlat, tps_lat, tps_ctx)
    first = il == 0
    last = il == nt - 1
    xb = x_ref[...]
    w = w_ref[...]
    tc_ = xb.shape[1]
    ys = jnp.dot(sh_ref[...], xb, preferred_element_type=f32).reshape(tr // 8, 32, tc_)
    acc = (ys[:, 0:8] * w[0:1] + ys[:, 8:16] * w[1:2] + ys[:, 16:24] * w[3:4] + ys[:, 24:32] * w[4:5])
    acc = acc.reshape(tr, tc_) + xb.astype(f32) * w[2:3]
    p = jnp.where(first, 0.0, xp_ref[...].astype(f32))[halo - 8:]
    n = jnp.where(last, 0.0, xn_ref[...].astype(f32))[:8]
    r8 = lax.broadcasted_iota(jnp.int32, (8, 1), 0)
    top = jnp.where(r8 == 0, p[6:7] * w[0:1] + p[7:8] * w[1:2], jnp.where(r8 == 1, p[7:8] * w[0:1], 0.0))
    bot = jnp.where(r8 == 7, n[0:1] * w[3:4] + n[1:2] * w[4:5], jnp.where(r8 == 6, n[0:1] * w[4:5], 0.0))
    acc = jnp.concatenate([acc[:8] + top, acc[8:tr - 8], acc[tr - 8:] + bot], axis=0)
    y = acc * _sigmoid(acc)
    sec = j // tiles_per_sec
    tc = acc.shape[1]
    for s in range(tc // LANES):
        ys = y[:, s * LANES:(s + 1) * LANES]
        rs = lax.rsqrt(jnp.sum(ys * ys, -1, keepdims=True) + EPS)
        scale = jnp.where(sec == 2, 1.0, rs) * jnp.where(sec == 0, qscale, 1.0)
        o_ref[:, s * LANES:(s + 1) * LANES] = (ys * scale).astype(o_ref.dtype)


def _conv_prep(z, conv_w, *, tr, tc, S, C, nlat_rows, dn_w, dk):
    T = z.shape[0]
    halo = 16
    hb = tr // halo
    nh = T // halo
    body = functools.partial(_conv_body, tr=tr, tps_lat=S // tr, tps_ctx=C // tr, nlat=nlat_rows // tr,
                             qscale=float(dk) ** -0.5, tiles_per_sec=dn_w // tc, halo=halo)
    assert conv_w.shape[0] == 5
    ri = jnp.arange(tr)[:, None]
    ci = jnp.arange(tr)[None, :]
    shifts = jnp.stack([(ci == ri + off).astype(bf16) for off in (-2, -1, 1, 2)], 0)
    shifts = shifts.reshape(4, tr // 8, 8, tr).transpose(1, 0, 2, 3).reshape(4 * tr, tr)
    return pl.pallas_call(
        body,
        grid=(T // tr, 3 * dn_w // tc),
        in_specs=[pl.BlockSpec((tr, tc), lambda i, j: (i, j)),
                  pl.BlockSpec((halo, tc), lambda i, j: (jnp.maximum(i * hb - 1, 0), j)),
                  pl.BlockSpec((halo, tc), lambda i, j: (jnp.minimum((i + 1) * hb, nh - 1), j)),
                  pl.BlockSpec((conv_w.shape[0], tc), lambda i, j: (0, j)),
                  pl.BlockSpec((4 * tr, tr), lambda i, j: (0, 0))],
        out_specs=pl.BlockSpec((tr, tc), lambda i, j: (i, j)),
        out_shape=jax.ShapeDtypeStruct((T, 3 * dn_w), bf16),
        compiler_params=_cparams(("parallel", "parallel")),
        name="conv_prep",
    )(z, z, z, conv_w, shifts)


def _mm(x, y):
    return jnp.dot(x.astype(bf16), y.astype(bf16), preferred_element_type=f32)


def _unit_tri_inverse(a_list, ri, ci, interleave=None):
    b16 = (ri >> 4) == (ci >> 4)
    b32 = (ri >> 5) == (ci >> 5)
    eye = jnp.where(ri == ci, 1.0, 0.0)
    n = a_list[0].shape[0]
    a16 = [jnp.where(b16, a, 0.0) for a in a_list]
    t = [eye - x for x in a16]
    p = [_mm(x, x) for x in a16]
    for step in range(2):
        tp = [_mm(jnp.concatenate([ti, pi], axis=0), pi) for ti, pi in zip(t, p)]
        t = [ti + x[:n] for ti, x in zip(t, tp)]
        p = [x[n:] for x in tp]
        if step == 0 and interleave is not None:
            interleave()
    t = [ti + _mm(ti, pi) for ti, pi in zip(t, p)]
    a32 = [jnp.where(b32, a, 0.0) - x for a, x in zip(a_list, a16)]
    ta = [_mm(ti, ai) for ti, ai in zip(t, a32)]
    t = [ti - _mm(tai, ti) for ti, tai in zip(t, ta)]
    a64 = [jnp.where(b32, 0.0, a) for a in a_list]
    ta = [_mm(ti, ai) for ti, ai in zip(t, a64)]
    t = [ti - _mm(tai, ti) for ti, tai in zip(t, ta)]
    return t


def _dn_body(q_ref, k_ref, v_ref, ba_ref, al_ref, dt_ref, o_ref, s_scr, u_scr, wq_scr, qk_scr, eg_scr, *, G):
    d = pl.program_id(1)
    c = pl.program_id(3)
    dk = LANES
    hs = range(G)

    @pl.when(c == 0)
    def _():
        for ref in (s_scr, u_scr, wq_scr, qk_scr, eg_scr):
            ref[...] = jnp.zeros_like(ref)

    ba = ba_ref[0]
    beta_t = _sigmoid(ba)
    g_t = -jnp.exp(al_ref[0]) * _softplus(ba + dt_ref[0])
    ri = lax.broadcasted_iota(jnp.int32, (CHUNK, CHUNK), 0)
    ci = lax.broadcasted_iota(jnp.int32, (CHUNK, CHUNK), 1)
    fwd = d == 0
    later = jnp.where(fwd, ri, ci)
    earlier = jnp.where(fwd, ci, ri)
    incl = later >= earlier
    strict = later > earlier
    tri = jnp.where(incl, 1.0, 0.0)
    gc = jnp.dot(tri, g_t, preferred_element_type=f32, precision=HIGHEST)
    gct = gc.T
    gtot = jnp.sum(g_t, axis=0, keepdims=True)

    def pick(x, col):
        return jnp.where(fwd, x[:, col:col + 1], x[:, G + col:G + col + 1])

    sl = [slice(h * dk, (h + 1) * dk) for h in hs]
    beta_c = [pick(beta_t, h) for h in hs]
    gc_c = [pick(gc, 2 * G + h) for h in hs]
    gc_r = [jnp.where(fwd, gct[2 * G + h:2 * G + h + 1], gct[3 * G + h:3 * G + h + 1]) for h in hs]
    gt = [pick(gtot, 2 * G + h) for h in hs]
    q = [q_ref[:, sl[h]].astype(f32) for h in hs]
    k = [k_ref[:, sl[h]].astype(f32) for h in hs]
    kb = [k[h] * beta_c[h] for h in hs]
    eg = [jnp.exp(gc_c[h]) for h in hs]
    dec = [jnp.where(incl, jnp.exp(jnp.where(incl, gc_c[h] - gc_r[h], 0.0)), 0.0) for h in hs]
    kq = [lax.dot_general(jnp.concatenate([kb[h], q[h]], axis=0).astype(bf16), k[h].astype(bf16),
                          (((1,), (1,)), ((), ())), preferred_element_type=f32) for h in hs]
    s = [s_scr[h] for h in hs]
    r = [jnp.dot(wq_scr[h], s[h].astype(bf16), preferred_element_type=f32) for h in hs]
    a = [jnp.where(strict, kq[h][:CHUNK] * dec[h], 0.0) for h in hs]
    qk = [(kq[h][CHUNK:] * dec[h]).astype(bf16) for h in hs]

    def recurrence_tail():
        vnb = [(u_scr[h] - r[h][:CHUNK]).astype(bf16) for h in hs]
        ou = [jnp.dot(qk_scr[h], vnb[h], preferred_element_type=f32) for h in hs]
        for h in hs:
            s_scr[h] = s[h] * eg_scr[h] + ou[h][CHUNK:]
            o_ref[0, :, sl[h]] = (r[h][CHUNK:] + ou[h][:CHUNK]).astype(o_ref.dtype)

    t = _unit_tri_inverse(a, ri, ci, interleave=recurrence_tail)
    rhs = [jnp.concatenate([v_ref[:, sl[h]].astype(f32) * beta_c[h], kb[h] * eg[h]], axis=1) for h in hs]
    sol = [_mm(t[h], rhs[h]) for h in hs]
    for h in hs:
        u_scr[h] = sol[h][:, :dk]
        wq_scr[h] = jnp.concatenate([sol[h][:, dk:], q[h] * eg[h]], axis=0).astype(bf16)
        kd_t = (k[h] * jnp.exp(gt[h] - gc_c[h])).T.astype(bf16)
        qk_scr[h] = jnp.concatenate([qk[h], kd_t], axis=0)
        eg_scr[h] = jnp.broadcast_to(jnp.exp(gt[h]), (1, dk))


def _deltanet(qkv, ba, a_log, dt_bias, *, B, S, C, H, G):
    T = qkv.shape[0]
    dn_w = H * LANES
    ncl = S // CHUNK
    ncc = C // CHUNK
    nc = ncl + ncc
    hgs = H // G

    def rowblk(b, d, c):
        cc = jnp.where(d == 0, c, ncc - 1 - c)
        cl = jnp.where(d == 0, c - ncc, nc - 1 - c)
        return jnp.where(c < ncc, B * ncl + b * ncc + cc, b * ncl + cl)

    def grouped(t):
        r = t.shape[0]
        t = t.reshape(r, 4, hgs, G).transpose(2, 0, 1, 3).reshape(hgs, r, 4 * G)
        return jnp.pad(t, ((0, 0), (0, 0), (0, LANES - 4 * G)))

    zeros = jnp.zeros((1, 2 * H), f32)
    ba_g = grouped(ba[:, :4 * H])
    al_g = grouped(jnp.concatenate([zeros, a_log.reshape(1, 2 * H)], 1))
    dt_g = grouped(jnp.concatenate([zeros, dt_bias.reshape(1, 2 * H)], 1))
    body = functools.partial(_dn_body, G=G)

    def blk_in(b, d, c):
        return rowblk(b, d, jnp.minimum(c, nc - 1))

    def blk_out(b, d, c):
        return rowblk(b, d, jnp.maximum(c - 1, 0))

    return pl.pallas_call(
        body,
        grid=(B, 2, hgs, nc + 1),
        in_specs=[pl.BlockSpec((CHUNK, G * LANES), lambda b, d, g, c: (blk_in(b, d, c), g)),
                  pl.BlockSpec((CHUNK, G * LANES), lambda b, d, g, c: (blk_in(b, d, c), hgs + g)),
                  pl.BlockSpec((CHUNK, G * LANES), lambda b, d, g, c: (blk_in(b, d, c), 2 * hgs + g)),
                  pl.BlockSpec((1, CHUNK, LANES), lambda b, d, g, c: (g, blk_in(b, d, c), 0)),
                  pl.BlockSpec((1, 1, LANES), lambda b, d, g, c: (g, 0, 0)),
                  pl.BlockSpec((1, 1, LANES), lambda b, d, g, c: (g, 0, 0))],
        out_specs=pl.BlockSpec((1, CHUNK, G * LANES), lambda b, d, g, c: (d, blk_out(b, d, c), g)),
        out_shape=jax.ShapeDtypeStruct((2, T, dn_w), bf16),
        scratch_shapes=[pltpu.VMEM((G, LANES, LANES), f32),
                        pltpu.VMEM((G, CHUNK, LANES), f32),
                        pltpu.VMEM((G, 2 * CHUNK, LANES), bf16),
                        pltpu.VMEM((G, CHUNK + LANES, CHUNK), bf16),
                        pltpu.VMEM((G, 1, LANES), f32)],
        compiler_params=_cparams(("parallel", "parallel", "parallel", "arbitrary")),
        name="deltanet",
    )(qkv, qkv, qkv, ba_g, al_g, dt_g)


def _pool_ctx_body(x_ref, o_ref, *, gw):
    C = x_ref.shape[0]
    ri = lax.broadcasted_iota(jnp.int32, (C, C), 0)
    ci = lax.broadcasted_iota(jnp.int32, (C, C), 1)
    t = lax.broadcasted_iota(jnp.int32, (C, 1), 0)
    for gi, w in enumerate(POOL_WINDOWS):
        sl = slice(gi * gw, (gi + 1) * gw)
        band = jnp.where((ci >= ri - w // 2) & (ci < ri + w - w // 2), 1.0, 0.0).astype(bf16)
        cnt = (jnp.minimum(t + w - w // 2, C) - jnp.maximum(t - w // 2, 0)).astype(f32)
        x = x_ref[:, sl]
        box = jnp.dot(band, x, preferred_element_type=f32)
        o_ref[:, sl] = (box / cnt - x.astype(f32)).astype(o_ref.dtype)


def _pool_lat_body(x_ref, o_ref, y_scr):
    g = pl.program_id(1)
    S = x_ref.shape[0]
    rows = S // GRID_W
    sb = 4 * GRID_W
    ri = lax.broadcasted_iota(jnp.int32, (sb, sb), 0)
    ci = lax.broadcasted_iota(jnp.int32, (sb, sb), 1)
    cr = ri & (GRID_W - 1)
    cc = ci & (GRID_W - 1)
    same = (ri >> 6) == (ci >> 6)
    col = lax.broadcasted_iota(jnp.int32, (GRID_W, 1), 0)
    for gi, w in enumerate(POOL_WINDOWS):
        @pl.when(g == gi)
        def _(w=w):
            band = jnp.where(same & (cc >= cr - w // 2) & (cc < cr + w - w // 2), 1.0, 0.0).astype(bf16)
            for s in range(S // sb):
                y_scr[s * sb:(s + 1) * sb, :] = jnp.dot(band, x_ref[s * sb:(s + 1) * sb, :],
                                                        preferred_element_type=f32)
            ncol = (jnp.minimum(col + w - w // 2, GRID_W) - jnp.maximum(col - w // 2, 0)).astype(f32)
            for r in range(rows):
                lo = max(r - w // 2, 0)
                hi = min(r + w - w // 2, rows)
                acc = y_scr[lo * GRID_W:(lo + 1) * GRID_W, :]
                for rr in range(lo + 1, hi):
                    acc = acc + y_scr[rr * GRID_W:(rr + 1) * GRID_W, :]
                cnt = ncol * float(hi - lo)
                xs = x_ref[r * GRID_W:(r + 1) * GRID_W, :].astype(f32)
                o_ref[r * GRID_W:(r + 1) * GRID_W, :] = (acc / cnt - xs).astype(o_ref.dtype)


def _pool(z, *, B, S, C, pool_off_blocks, pool_w, gw):
    nlat_c = B * S // C
    ng = pool_w // gw
    assert ng == len(POOL_WINDOWS)
    pooled_ctx = pl.pallas_call(
        functools.partial(_pool_ctx_body, gw=gw),
        grid=(B,),
        in_specs=[pl.BlockSpec((C, pool_w), lambda b: (nlat_c + b, pool_off_blocks))],
        out_specs=pl.BlockSpec((C, pool_w), lambda b: (b, 0)),
        out_shape=jax.ShapeDtypeStruct((B * C, pool_w), bf16),
        compiler_params=_cparams(("parallel",)),
        name="pool_ctx",
    )(z)
    off = pool_off_blocks * ng
    pooled_lat = pl.pallas_call(
        _pool_lat_body,
        grid=(B, ng),
        in_specs=[pl.BlockSpec((S, gw), lambda b, g: (b, off + g))],
        out_specs=pl.BlockSpec((S, gw), lambda b, g: (b, g)),
        out_shape=jax.ShapeDtypeStruct((B * S, pool_w), bf16),
        scratch_shapes=[pltpu.VMEM((S, gw), f32)],
        compiler_params=_cparams(("parallel", "parallel")),
        name="pool_lat",
    )(z)
    return pooled_lat, pooled_ctx


def _merge_a_body(o_ref, og_ref, m1_ref, m2_ref, pl_ref, pc_ref, ong_ref, wpool_ref, ps_ref, wud_ref, wup_ref,
                  out_ref, a_scr, p_scr, *, H, gw, nlat):
    is_lat = pl.program_id(0) < nlat
    for h in range(H):
        sl = slice(h * LANES, (h + 1) * LANES)
        o = o_ref[0, :, sl].astype(f32) + o_ref[1, :, sl].astype(f32)
        og = og_ref[:, sl].astype(f32)
        y = o * lax.rsqrt(jnp.mean(o * o, -1, keepdims=True) + EPS) * ong_ref[...] * (og * _sigmoid(og))
        a_scr[:, sl] = y.astype(bf16)
    y_dn = jnp.dot(a_scr[...], wud_ref[...], preferred_element_type=f32)
    for g in range(wpool_ref.shape[0]):
        sl = slice(g * gw, (g + 1) * gw)
        pooled = jnp.where(is_lat, pl_ref[:, sl], pc_ref[:, sl])
        yp = jnp.dot(pooled, wpool_ref[g], preferred_element_type=f32) * ps_ref[:, sl]
        p_scr[:, sl] = yp.astype(bf16)
    y_pool = jnp.dot(p_scr[...], wup_ref[...], preferred_element_type=f32)
    g1 = _sigmoid(m1_ref[...].astype(f32))
    g2 = _sigmoid(m2_ref[...].astype(f32))
    out_ref[...] = (g1 * y_dn + g2 * y_pool).astype(out_ref.dtype)


def _merge_a(o_dn, z, pooled_lat, pooled_ctx, onorm_g, w_pool, pool_scale, w_up_dn, w_up_pool, *, n_rows, tm, H, D):
    dn_w = H * LANES
    pool_w = pooled_lat.shape[1]
    nlat = pooled_lat.shape[0] // tm
    ng, gw, _ = w_pool.shape
    og_blk = 3
    m_blk = 4 * dn_w // D
    const = lambda *_: (0, 0)
    return pl.pallas_call(
        functools.partial(_merge_a_body, H=H, gw=gw, nlat=nlat),
        grid=(n_rows // tm,),
        in_specs=[pl.BlockSpec((2, tm, dn_w), lambda i: (0, i, 0)),
                  pl.BlockSpec((tm, dn_w), lambda i: (i, og_blk)),
                  pl.BlockSpec((tm, D), lambda i: (i, m_blk)),
                  pl.BlockSpec((tm, D), lambda i: (i, m_blk + 1)),
                  pl.BlockSpec((tm, pool_w), lambda i: (jnp.minimum(i, nlat - 1), 0)),
                  pl.BlockSpec((tm, pool_w), lambda i: (jnp.maximum(i - nlat, 0), 0)),
                  pl.BlockSpec((1, LANES), const),
                  pl.BlockSpec((ng, gw, gw), lambda i: (0, 0, 0)),
                  pl.BlockSpec((1, pool_w), const),
                  pl.BlockSpec((dn_w, D), const, pipeline_mode=pl.Buffered(1)),
                  pl.BlockSpec((pool_w, D), const, pipeline_mode=pl.Buffered(1))],
        out_specs=pl.BlockSpec((tm, D), lambda i: (i, 0)),
        out_shape=jax.ShapeDtypeStruct((n_rows, D), bf16),
        scratch_shapes=[pltpu.VMEM((tm, dn_w), bf16), pltpu.VMEM((tm, pool_w), bf16)],
        compiler_params=_cparams(("parallel",)),
        name="merge_a",
    )(o_dn, z, z, z, pooled_lat, pooled_ctx, onorm_g, w_pool, pool_scale, w_up_dn, w_up_pool)


def _merge_b_body(m_ref, h_ref, gt_ref, sc_ref, sh_ref, g_ref, wout_ref, wr_ref, h1_ref, v_ref, lg_ref,
                  *, ds):
    tm = m_ref.shape[0]
    m = jnp.dot(m_ref[...], wout_ref[...], preferred_element_type=f32)
    h1 = h_ref[...] + gt_ref[0] * m
    h1_ref[...] = h1
    y = h1 * lax.rsqrt(jnp.mean(h1 * h1, -1, keepdims=True) + EPS) * g_ref[...]
    v = y * (1.0 + sc_ref[0]) + sh_ref[0]
    E = lg_ref.shape[1]
    vh = v.astype(bf16)
    vl = (v - vh.astype(f32)).astype(bf16)
    th = jnp.dot(vh, wr_ref[...], preferred_element_type=f32)
    tl = jnp.dot(vl, wr_ref[...], preferred_element_type=f32)
    lg_ref[...] = th[:, :E] + (th[:, E:] + tl[:, :E])
    half = ds * LANES
    for j in range(ds):
        v_ref[pl.ds(j, tm, stride=ds), :] = _pack_pair(v[:, j * LANES:(j + 1) * LANES],
                                                       v[:, half + j * LANES:half + (j + 1) * LANES])


def _merge_b(merged, h, mod, gain2, w_out, w_router, *, n_rows, tm, tpb, nlat, nb):
    D = h.shape[1]
    E = w_router.shape[1]
    ds = D // (2 * LANES)
    const = lambda *_: (0, 0)
    wr_hi = w_router.astype(bf16)
    wr_lo = (w_router - wr_hi.astype(f32)).astype(bf16)
    wr_cat = jnp.concatenate([wr_hi, wr_lo], 1)
    return pl.pallas_call(
        functools.partial(_merge_b_body, ds=ds),
        grid=(n_rows // tm,),
        in_specs=[pl.BlockSpec((tm, D), lambda i: (i, 0)),
                  pl.BlockSpec((tm, D), lambda i: (i, 0)),
                  _mod_spec(2, D, tpb, nlat, nb),
                  _mod_spec(4, D, tpb, nlat, nb),
                  _mod_spec(3, D, tpb, nlat, nb),
                  pl.BlockSpec((1, D), const),
                  pl.BlockSpec((D, D), const, pipeline_mode=pl.Buffered(1)),
                  pl.BlockSpec((D, 2 * E), const)],
        out_specs=[pl.BlockSpec((tm, D), lambda i: (i, 0)),
                   pl.BlockSpec((tm * ds, LANES), lambda i: (i, 0)),
                   pl.BlockSpec((tm, E), lambda i: (i, 0))],
        out_shape=[jax.ShapeDtypeStruct((n_rows, D), f32),
                   jax.ShapeDtypeStruct((n_rows * ds, LANES), jnp.uint32),
                   jax.ShapeDtypeStruct((n_rows, E), f32)],
        compiler_params=_cparams(("parallel",)),
        name="merge_b",
    )(merged, h, mod, mod, mod, gain2, w_out, wr_cat)


def _route_body(lg_ref, bias_ref, idx_ref, gate_ref, cnt_ref, carry_scr, *, n_groups):
    i = pl.program_id(0)
    E, tt = lg_ref.shape

    @pl.when(i == 0)
    def _():
        carry_scr[...] = jnp.zeros_like(carry_scr)

    aff = _sigmoid(lg_ref[...])
    sel = aff + bias_ref[...]
    sub = lax.broadcasted_iota(jnp.int32, (GROUP_SIZE, tt), 0)
    neg = -1e30
    best = None
    for g in range(n_groups):
        s = sel[g * GROUP_SIZE:(g + 1) * GROUP_SIZE]
        a = aff[g * GROUP_SIZE:(g + 1) * GROUP_SIZE]
        m1 = jnp.max(s, axis=0, keepdims=True)
        i1 = jnp.min(jnp.where(s == m1, sub, GROUP_SIZE), axis=0, keepdims=True)
        s2 = jnp.where(sub == i1, neg, s)
        m2 = jnp.max(s2, axis=0, keepdims=True)
        i2 = jnp.min(jnp.where(s2 == m2, sub, GROUP_SIZE), axis=0, keepdims=True)
        a1 = jnp.sum(jnp.where(sub == i1, a, 0.0), axis=0, keepdims=True)
        a2 = jnp.sum(jnp.where(sub == i2, a, 0.0), axis=0, keepdims=True)
        cand = (m1 + m2, i1 + g * GROUP_SIZE, i2 + g * GROUP_SIZE, a1, a2)
        if best is None:
            best = cand
        else:
            better = cand[0] > best[0]
            best = tuple(jnp.where(better, cn, bs) for cn, bs in zip(cand, best))
    _, e1, e2, a1, a2 = best
    den = a1 + a2
    gate_ref[0:1, :] = a1 / den
    gate_ref[1:2, :] = a2 / den
    erow = lax.broadcasted_iota(jnp.int32, (E, tt), 0)
    oh1 = erow == e1
    oh2 = erow == e2
    cnt = jnp.where(oh1 | oh2, 1.0, 0.0)
    ti = lax.broadcasted_iota(jnp.int32, (tt, tt), 0)
    tj = lax.broadcasted_iota(jnp.int32, (tt, tt), 1)
    before = jnp.where(ti < tj, 1.0, 0.0).astype(bf16)
    pos = jnp.dot(cnt.astype(bf16), before, preferred_element_type=f32) + carry_scr[...]
    r1 = jnp.sum(jnp.where(oh1, pos, 0.0), axis=0, keepdims=True)
    r2 = jnp.sum(jnp.where(oh2, pos, 0.0), axis=0, keepdims=True)
    idx_ref[0:1, :] = e1
    idx_ref[1:2, :] = e2
    idx_ref[2:3, :] = r1.astype(jnp.int32)
    idx_ref[3:4, :] = r2.astype(jnp.int32)
    carry_scr[...] = carry_scr[...] + jnp.sum(cnt, axis=1, keepdims=True)
    cnt_ref[...] = jnp.broadcast_to(carry_scr[...], cnt_ref.shape)


def _route(logits_t, bias_col, *, tt):
    E, T = logits_t.shape
    return pl.pallas_call(
        functools.partial(_route_body, n_groups=E // GROUP_SIZE),
        grid=(T // tt,),
        in_specs=[pl.BlockSpec((E, tt), lambda i: (0, i)),
                  pl.BlockSpec((E, 1), lambda i: (0, 0))],
        out_specs=[pl.BlockSpec((4, tt), lambda i: (0, i)),
                   pl.BlockSpec((2, tt), lambda i: (0, i)),
                   pl.BlockSpec((E, LANES), lambda i: (0, 0))],
        out_shape=[jax.ShapeDtypeStruct((4, T), jnp.int32),
                   jax.ShapeDtypeStruct((2, T), f32),
                   jax.ShapeDtypeStruct((E, LANES), f32)],
        scratch_shapes=[pltpu.VMEM((E, 1), f32)],
        compiler_params=_cparams(("arbitrary",)),
        name="route",
    )(logits_t, bias_col)


def _row_copy(src, dst, s, t, n, sem):
    def rows(x):
        return pl.ds(x * n if isinstance(x, int) else pl.multiple_of(x * n, 8), n)

    return pltpu.make_async_copy(src.at[rows(s), :], dst.at[rows(t), :], sem)


def _dispatch_body(dest_ref, pad_ref, v_ref, xs_hbm, zero_scr, sem, zsem, *, tt, ds, n_exp, n_blocks, rb):
    i = pl.program_id(0)

    @pl.when(i == 0)
    def _():
        zero_scr[...] = jnp.zeros_like(zero_scr)

        def per_expert(e, n):
            start = pad_ref[0, e]

            def one(r, m):
                _row_copy(zero_scr, xs_hbm, 0, start + r, ds, zsem).start()
                return m + 1

            return lax.fori_loop(0, pad_ref[1, e], one, n)

        total = lax.fori_loop(0, n_exp, per_expert, 0)

        def wait_one(r, m):
            _row_copy(zero_scr, xs_hbm, 0, 0, ds, zsem).wait()
            return m

        lax.fori_loop(0, total, wait_one, 0)
        n_used = pad_ref[2, 0]

        def blk_start(bk, m):
            _row_copy(zero_scr, xs_hbm, 0, bk, rb * ds, zsem).start()
            return m

        lax.fori_loop(n_used, n_blocks, blk_start, 0)

        def blk_wait(bk, m):
            _row_copy(zero_scr, xs_hbm, 0, 0, rb * ds, zsem).wait()
            return m

        lax.fori_loop(n_used, n_blocks, blk_wait, 0)

    def issue(r, m):
        src = v_ref.at[pl.ds(pl.multiple_of(r * ds, ds), ds), :]
        for kk in range(2):
            dst = xs_hbm.at[pl.ds(pl.multiple_of(dest_ref[0, kk, r] * ds, ds), ds), :]
            pltpu.make_async_copy(src, dst, sem).start(priority=kk)
        return m

    lax.fori_loop(0, tt, issue, 0, unroll=8)
    for kk in range(2):
        pltpu.make_async_copy(v_ref, xs_hbm.at[pl.ds(0, tt * ds), :], sem).wait()


def _dispatch(dest3, pad_info, v_ts, *, n_tok, n_blocks, tt, ds):
    n_exp = pad_info.shape[1]
    rb = EXPERT_BLOCK
    return pl.pallas_call(
        functools.partial(_dispatch_body, tt=tt, ds=ds, n_exp=n_exp, n_blocks=n_blocks, rb=rb),
        grid=(n_tok // tt,),
        in_specs=[pl.BlockSpec((1, 2, tt), lambda i: (i, 0, 0), memory_space=pltpu.SMEM),
                  pl.BlockSpec(memory_space=pltpu.SMEM),
                  pl.BlockSpec((tt * ds, LANES), lambda i: (i, 0))],
        out_specs=pl.BlockSpec(memory_space=pl.ANY),
        out_shape=jax.ShapeDtypeStruct((n_blocks * rb * ds, LANES), v_ts.dtype),
        scratch_shapes=[pltpu.VMEM((rb * ds, LANES), v_ts.dtype), pltpu.SemaphoreType.DMA(()),
                        pltpu.SemaphoreType.DMA(())],
        compiler_params=_cparams(("arbitrary",)),
        name="dispatch",
    )(dest3, pad_info, v_ts)


def _expert_body(be_ref, nu_ref, x_ref, wg_ref, wu_ref, wd_ref, y_ref, x_s, *, ds, rb):
    b = pl.program_id(0)
    used = b < nu_ref[0]

    @pl.when(used)
    def _():
        half = ds * LANES
        for j in range(ds):
            lo, hi = _unpack_pair(x_ref[pl.ds(j, rb, stride=ds), :])
            x_s[:, j * LANES:(j + 1) * LANES] = lo.astype(bf16)
            x_s[:, half + j * LANES:half + (j + 1) * LANES] = hi.astype(bf16)
        x = x_s[...]
        hg = jnp.dot(x, wg_ref[0, 0].astype(bf16), preferred_element_type=f32)
        hu = jnp.dot(x, wu_ref[0, 0].astype(bf16), preferred_element_type=f32)
        a = (hg * _sigmoid(hg) * hu).astype(bf16)
        y = jnp.dot(a, wd_ref[0, 0].astype(bf16), preferred_element_type=f32)
        for j in range(ds):
            y_ref[pl.ds(j, rb, stride=ds), :] = _pack_pair(y[:, j * LANES:(j + 1) * LANES],
                                                           y[:, half + j * LANES:half + (j + 1) * LANES])

    @pl.when(jnp.logical_not(used))
    def _():
        y_ref[...] = jnp.zeros_like(y_ref)


def _experts(block_expert, n_used, xs_ts, w_g, w_u, w_d, *, layer, ds):
    _, E, D, De = w_g.shape
    rb = EXPERT_BLOCK
    n_blocks = xs_ts.shape[0] // (rb * ds)
    grid_spec = pltpu.PrefetchScalarGridSpec(
        num_scalar_prefetch=2,
        grid=(n_blocks,),
        in_specs=[pl.BlockSpec((rb * ds, LANES), lambda b, be, nu: (b, 0)),
                  pl.BlockSpec((1, 1, D, De), lambda b, be, nu: (layer, be[b], 0, 0)),
                  pl.BlockSpec((1, 1, D, De), lambda b, be, nu: (layer, be[b], 0, 0)),
                  pl.BlockSpec((1, 1, De, D), lambda b, be, nu: (layer, be[b], 0, 0))],
        out_specs=pl.BlockSpec((rb * ds, LANES), lambda b, be, nu: (b, 0)),
        scratch_shapes=[pltpu.VMEM((rb, D), bf16)],
    )
    return pl.pallas_call(
        functools.partial(_expert_body, ds=ds, rb=rb),
        grid_spec=grid_spec,
        out_shape=jax.ShapeDtypeStruct(xs_ts.shape, xs_ts.dtype),
        compiler_params=_cparams(("arbitrary",)),
        name="experts",
    )(block_expert, n_used, xs_ts, w_g, w_u, w_d)


def _combine_body(dest_ref, dnext_ref, gate_ref, h_ref, gt_ref, fg_ref, y_hbm, o_ref, y_scr, sem, *, tt, ds, final):
    i = pl.program_id(0)
    slot = i % 2

    def gather(d_ref, to_slot):
        def body(r, m):
            for kk in range(2):
                _row_copy(y_hbm, y_scr.at[to_slot, kk], d_ref[0, kk, r], r, ds, sem.at[to_slot]).start(priority=kk)
            return m

        lax.fori_loop(0, tt, body, 0, unroll=8)

    @pl.when(i == 0)
    def _():
        gather(dest_ref, 0)

    @pl.when(i + 1 < pl.num_programs(0))
    def _():
        gather(dnext_ref, 1 - slot)

    for kk in range(2):
        pltpu.make_async_copy(y_hbm.at[pl.ds(0, tt * ds), :], y_scr.at[slot, kk], sem.at[slot]).wait()
    gt = gt_ref[0]
    g0 = gate_ref[:, 0:1]
    g1 = gate_ref[:, 1:2]
    half = ds * LANES
    for j in range(ds):
        lo0, hi0 = _unpack_pair(y_scr[slot, 0, pl.ds(j, tt, stride=ds), :])
        lo1, hi1 = _unpack_pair(y_scr[slot, 1, pl.ds(j, tt, stride=ds), :])
        for base, y0, y1 in ((0, lo0, lo1), (half, hi0, hi1)):
            sl = slice(base + j * LANES, base + (j + 1) * LANES)
            o_ref[:, sl] = h_ref[:, sl] + gt[:, sl] * (g0 * y0 + g1 * y1)
    if final:
        x = o_ref[...]
        o_ref[...] = x * lax.rsqrt(jnp.mean(x * x, -1, keepdims=True) + EPS) * fg_ref[...]


def _combine(dest3, gates_t, h1, mod, final_g, y_ts, *, n_tok, tt, ds, tpb, nlat, nb, final):
    D = h1.shape[1]
    nt = n_tok // tt
    return pl.pallas_call(
        functools.partial(_combine_body, tt=tt, ds=ds, final=final),
        grid=(nt,),
        in_specs=[pl.BlockSpec((1, 2, tt), lambda i: (i, 0, 0), memory_space=pltpu.SMEM),
                  pl.BlockSpec((1, 2, tt), lambda i: (jnp.minimum(i + 1, nt - 1), 0, 0), memory_space=pltpu.SMEM),
                  pl.BlockSpec((tt, 2), lambda i: (i, 0)),
                  pl.BlockSpec((tt, D), lambda i: (i, 0)),
                  _mod_spec(5, D, tpb, nlat, nb),
                  pl.BlockSpec((1, D), lambda i: (0, 0)),
                  pl.BlockSpec(memory_space=pl.ANY)],
        out_specs=pl.BlockSpec((tt, D), lambda i: (i, 0)),
        out_shape=jax.ShapeDtypeStruct((n_tok, D), f32),
        scratch_shapes=[pltpu.VMEM((2, 2, tt * ds, LANES), y_ts.dtype), pltpu.SemaphoreType.DMA((2,))],
        compiler_params=_cparams(("arbitrary",)),
        name="combine",
    )(dest3, dest3, gates_t, h1, mod, final_g, y_ts)


def _moe(v_ts, logits, h1, mod, w_router_bias, w_g, w_u, w_d, final_g, *, layer, n_tok, tt, ds, tpb, nlat, nb,
         final):
    E = w_g.shape[1]
    route_tt = next(t for t in (512, 256, LANES) if n_tok % t == 0)
    idx, gates, counts = _route(logits[:n_tok].T, w_router_bias.reshape(E, 1), tt=route_tt)
    counts = counts[:, 0].astype(jnp.int32)
    padded = (counts + EXPERT_BLOCK - 1) // EXPERT_BLOCK * EXPERT_BLOCK
    pad_end = jnp.cumsum(padded)
    pad_start = pad_end - padded
    n_blocks = (n_tok * 2 + EXPERT_BLOCK - 1) // EXPERT_BLOCK + E
    n_used = (pad_end[-1] // EXPERT_BLOCK).astype(jnp.int32).reshape(1)
    blk = jnp.minimum(jnp.arange(n_blocks, dtype=jnp.int32), jnp.maximum(n_used[0] - 1, 0)) * EXPERT_BLOCK
    block_expert = jnp.minimum(jnp.sum(blk[:, None] >= pad_end[None, :], 1), E - 1).astype(jnp.int32)
    e_sel = idx[0:2][..., None] == jnp.arange(E, dtype=jnp.int32)
    dest = jnp.sum(jnp.where(e_sel, pad_start.astype(jnp.int32), 0), -1) + idx[2:4]
    dest3 = dest.reshape(2, n_tok // tt, tt).transpose(1, 0, 2)
    pad_info = jnp.stack([pad_start + counts, padded - counts, jnp.broadcast_to(n_used, (E,))]).astype(jnp.int32)
    xs_ts = _dispatch(dest3, pad_info, v_ts, n_tok=n_tok, n_blocks=n_blocks, tt=tt, ds=ds)
    y_ts = _experts(block_expert, n_used, xs_ts, w_g, w_u, w_d, layer=layer, ds=ds)
    return _combine(dest3, gates.T, h1, mod, final_g, y_ts, n_tok=n_tok, tt=tt, ds=ds, tpb=tpb, nlat=nlat,
                    nb=nb, final=final)


def kernel(x, c, ctx, c_ctx, w_mod, b_mod, norm1_g, norm2_g, w_in, conv_w, a_log, dt_bias, onorm_g, w_pool, pool_scale, w_up_dn, w_up_pool, w_out, w_router, router_bias, w_gate_e, w_up_e, w_down_e, final_g):
    B, S, D = x.shape
    C = ctx.shape[1]
    L = w_mod.shape[0]
    H = a_log.shape[-1]
    dk = onorm_g.shape[-1]
    assert dk == LANES
    dn_w = H * dk
    _, ng, gw, _ = w_pool.shape
    pool_w = ng * gw
    ds = D // (2 * LANES)
    n_lat = B * S
    T = n_lat + B * C
    tm = 256
    assert S % tm == 0 and (B * C) % tm == 0 and C % CHUNK == 0 and S % (4 * GRID_W) == 0
    tpb = S // tm
    nlat_tiles = n_lat // tm
    tm_in = next(t for t in (1024, 512, 256) if S % t == 0 and (B * C) % t == 0)
    tm_mg = next(t for t in (512, 256) if S % t == 0 and (B * C) % t == 0)
    tr = min(256, C)
    G = min(H, 16)

    c8 = jnp.zeros((8, D), f32).at[:B].set(c).at[B].set(c_ctx)
    mod = _mod_table(c8, w_mod, b_mod)

    qkv_c = 3 * dn_w
    scan_c = qkv_c + 4 * H
    og_off, pool_off = scan_c, scan_c + dn_w
    mg_off = pool_off + pool_w
    assert (4 * dn_w) % D == 0 and (4 * dn_w + 2 * D) % pool_w == 0
    pool_off_blocks = (4 * dn_w + 2 * D) // pool_w

    h = jnp.concatenate([x.reshape(n_lat, D), ctx.reshape(B * C, D)], 0)
    for l in range(L):
        last = l == L - 1
        mod_l = mod[l].reshape(8, 1, mod.shape[-1])
        wl = w_in[l]
        w_z = jnp.concatenate([wl[:, :qkv_c], wl[:, og_off:pool_off], wl[:, mg_off:], wl[:, pool_off:mg_off]],
                              1).astype(bf16)
        w_ba = jnp.pad(wl[:, qkv_c:scan_c], ((0, 0), (0, LANES - 4 * H))).astype(bf16)
        tn = 1024 if w_z.shape[1] % 1024 == 0 else w_z.shape[1]
        z, ba = _in_proj(h, mod_l, 0, norm1_g[l].reshape(1, D), w_z, w_ba, tm=tm_in, tn=tn, tpb=S // tm_in,
                         nlat=n_lat // tm_in, nb=B)
        qkv = _conv_prep(z, conv_w[l], tr=tr, tc=min(1024, dn_w), S=S, C=C, nlat_rows=n_lat, dn_w=dn_w, dk=dk)
        o_dn = _deltanet(qkv, ba, a_log[l], dt_bias[l], B=B, S=S, C=C, H=H, G=G)
        pooled_lat, pooled_ctx = _pool(z, B=B, S=S, C=C, pool_off_blocks=pool_off_blocks, pool_w=pool_w, gw=gw)
        n_rows = n_lat if last else T
        merged = _merge_a(o_dn, z, pooled_lat, pooled_ctx, onorm_g[l].reshape(1, dk), w_pool[l].astype(bf16),
                          pool_scale[l].reshape(1, pool_w), w_up_dn[l].astype(bf16), w_up_pool[l].astype(bf16),
                          n_rows=n_rows, tm=tm_mg, H=H, D=D)
        h1, v_ts, logits = _merge_b(merged, h, mod_l, norm2_g[l].reshape(1, D), w_out[l].astype(bf16), w_router,
                                    n_rows=n_rows, tm=tm_mg, tpb=S // tm_mg, nlat=n_lat // tm_mg, nb=B)
        h = _moe(v_ts, logits, h1, mod_l, router_bias, w_gate_e, w_up_e, w_down_e,
                 final_g.reshape(1, D), layer=l, n_tok=n_rows, tt=tm, ds=ds, tpb=tpb, nlat=nlat_tiles, nb=B, final=last)
    return h.reshape(B, S, D)
```

```python
import functools

import jax
import jax.numpy as jnp
from jax import lax
from jax.experimental import pallas as pl
from jax.experimental.pallas import tpu as pltpu

f32 = jnp.float32
bf16 = jnp.bfloat16
HIGHEST = lax.Precision.HIGHEST

EPS = 1e-6
LANES = 128
GRID_W = 64
CHUNK = 64
POOL_WINDOWS = (2, 4, 8, 16)
GROUP_SIZE = 8
EXPERT_BLOCK = 256
VMEM_LIMIT = 56 * 1024 * 1024


def _cparams(sem, vmem=VMEM_LIMIT):
    return pltpu.CompilerParams(dimension_semantics=sem, vmem_limit_bytes=vmem)


def _sigmoid(x):
    return 1.0 / (1.0 + jnp.exp(-x))


def _softplus(x):
    return jnp.maximum(x, 0.0) + jnp.log(1.0 + jnp.exp(-jnp.abs(x)))


def _pack_pair(lo, hi):
    def rounded(x):
        u = lax.bitcast_convert_type(x, jnp.uint32)
        return u + jnp.uint32(0x7FFF) + ((u >> 16) & jnp.uint32(1))

    return (rounded(lo) >> 16) | (rounded(hi) & jnp.uint32(0xFFFF0000))


def _unpack_pair(w):
    lo = lax.bitcast_convert_type(w << 16, f32)
    hi = lax.bitcast_convert_type(w & jnp.uint32(0xFFFF0000), f32)
    return lo, hi


def _mod_spec(sec, D, tiles_per_batch, n_lat_tiles, n_batch):
    def index(i, *_):
        return (jnp.where(i < n_lat_tiles, i // tiles_per_batch, n_batch), 0, sec)

    return pl.BlockSpec((1, 1, D), index)


def _mod_body(c_ref, w_ref, b_ref, o_ref):
    c = c_ref[...]
    s = c * _sigmoid(c)
    o_ref[0] = jnp.dot(s, w_ref[0], preferred_element_type=f32, precision=HIGHEST) + b_ref[0]


def _mod_table(c8, w_mod, b_mod):
    L, D, N = w_mod.shape
    tn = next(t for t in (1024, 512, 256, LANES) if N % t == 0)
    return pl.pallas_call(
        _mod_body,
        grid=(L, N // tn),
        in_specs=[pl.BlockSpec((8, D), lambda l, j: (0, 0)),
                  pl.BlockSpec((1, D, tn), lambda l, j: (l, 0, j)),
                  pl.BlockSpec((1, 1, tn), lambda l, j: (l, 0, j))],
        out_specs=pl.BlockSpec((1, 8, tn), lambda l, j: (l, 0, j)),
        out_shape=jax.ShapeDtypeStruct((L, 8, N), f32),
        compiler_params=_cparams(("parallel", "parallel")),
        name="mod_table",
    )(c8, w_mod, b_mod.reshape(L, 1, N))


def _inproj_body(h_ref, sc_ref, sh_ref, g_ref, w_ref, wba_ref, z_ref, ba_ref, u_scr):
    j = pl.program_id(1)

    @pl.when(j == 0)
    def _():
        x = h_ref[...]
        y = x * lax.rsqrt(jnp.mean(x * x, -1, keepdims=True) + EPS) * g_ref[...]
        u = (y * (1.0 + sc_ref[0]) + sh_ref[0]).astype(bf16)
        u_scr[...] = u
        ba_ref[...] = jnp.dot(u, wba_ref[...], preferred_element_type=f32)

    z_ref[...] = jnp.dot(u_scr[...], w_ref[...], preferred_element_type=f32).astype(z_ref.dtype)


def _in_proj(h, mod, sec, gain, w_z, w_ba, *, tm, tn, tpb, nlat, nb):
    T, D = h.shape
    Nz = w_z.shape[1]
    return pl.pallas_call(
        _inproj_body,
        grid=(T // tm, Nz // tn),
        in_specs=[pl.BlockSpec((tm, D), lambda i, j: (i, 0)),
                  _mod_spec(sec + 1, D, tpb, nlat, nb),
                  _mod_spec(sec, D, tpb, nlat, nb),
                  pl.BlockSpec((1, D), lambda i, j: (0, 0)),
                  pl.BlockSpec((D, tn), lambda i, j: (0, j)),
                  pl.BlockSpec((D, LANES), lambda i, j: (0, 0))],
        out_specs=[pl.BlockSpec((tm, tn), lambda i, j: (i, j)),
                   pl.BlockSpec((tm, LANES), lambda i, j: (i, 0))],
        out_shape=[jax.ShapeDtypeStruct((T, Nz), bf16), jax.ShapeDtypeStruct((T, LANES), f32)],
        scratch_shapes=[pltpu.VMEM((tm, D), bf16)],
        compiler_params=_cparams(("parallel", "arbitrary")),
        name="in_proj",
    )(h, mod, mod, gain, w_z, w_ba)


def _conv_body(x_ref, xp_ref, xn_ref, w_ref, sh_ref, o_ref, *, tr, tps_lat, tps_ctx, nlat, qscale, tiles_per_sec,
               halo):
    i = pl.program_id(0)
    il =jnp.where(i < nlat, i % tps_lat, (i - nlat) % tps_ctx)
    nt = jnp.where(i < nlat, tps_lat, tps_ctx)
    first = il == 0
    last = il == nt - 1
    xb = x_ref[...]
    w = w_ref[...]
    tc_ = xb.shape[1]
    ys = jnp.dot(sh_ref[...], xb, preferred_element_type=f32).reshape(tr // 8, 32, tc_)
    acc = (ys[:, 0:8] * w[0:1] + ys[:, 8:16] * w[1:2] + ys[:, 16:24] * w[3:4] + ys[:, 24:32] * w[4:5])
    acc = acc.reshape(tr, tc_) + xb.astype(f32) * w[2:3]
    p = jnp.where(first, 0.0, xp_ref[...].astype(f32))[halo - 8:]
    n = jnp.where(last, 0.0, xn_ref[...].astype(f32))[:8]
    r8 = lax.broadcasted_iota(jnp.int32, (8, 1), 0)
    top = jnp.where(r8 == 0, p[6:7] * w[0:1] + p[7:8] * w[1:2], jnp.where(r8 == 1, p[7:8] * w[0:1], 0.0))
    bot = jnp.where(r8 == 7, n[0:1] * w[3:4] + n[1:2] * w[4:5], jnp.where(r8 == 6, n[0:1] * w[4:5], 0.0))
    acc = jnp.concatenate([acc[:8] + top, acc[8:tr - 8], acc[tr - 8:] + bot], axis=0)
    y = acc * _sigmoid(acc)
    sec = pl.program_id(1) // tiles_per_sec
    for s in range(acc.shape[1] // LANES):
        ys = y[:, s * LANES:(s + 1) * LANES]
        rs = lax.rsqrt(jnp.sum(ys * ys, -1, keepdims=True) + EPS)
        scale = jnp.where(sec == 2, 1.0, rs) * jnp.where(sec == 0, qscale, 1.0)
        o_ref[:, s * LANES:(s + 1) * LANES] = (ys * scale).astype(o_ref.dtype)


def _conv_prep(z, conv_w, *, tr, tc, S, C, nlat_rows, dn_w, dk):
    T = z.shape[0]
    halo = 16
    hb = tr // halo
    nh = T // halo
    body = functools.partial(_conv_body, tr=tr, tps_lat=S // tr, tps_ctx=C // tr, nlat=nlat_rows // tr,
                             qscale=float(dk) ** -0.5, tiles_per_sec=dn_w // tc, halo=halo)
    assert conv_w.shape[0] == 5
    ri = jnp.arange(tr)[:, None]
    ci = jnp.arange(tr)[None, :]
    shifts = jnp.stack([(ci == ri + off).astype(bf16) for off in (-2, -1, 1, 2)], 0)
    shifts = shifts.reshape(4, tr // 8, 8, tr).transpose(1, 0, 2, 3).reshape(4 * tr, tr)
    return pl.pallas_call(
        body,
        grid=(T // tr, 3 * dn_w // tc),
        in_specs=[pl.BlockSpec((tr, tc), lambda i, j: (i, j)),
                  pl.BlockSpec((halo, tc), lambda i, j: (jnp.maximum(i * hb - 1, 0), j)),
                  pl.BlockSpec((halo, tc), lambda i, j: (jnp.minimum((i + 1) * hb, nh - 1), j)),
                  pl.BlockSpec((conv_w.shape[0], tc), lambda i, j: (0, j)),
                  pl.BlockSpec((4 * tr, tr), lambda i, j: (0, 0))],
        out_specs=pl.BlockSpec((tr, tc), lambda i, j: (i, j)),
        out_shape=jax.ShapeDtypeStruct((T, 3 * dn_w), bf16),
        compiler_params=_cparams(("parallel", "parallel")),
        name="conv_prep",
    )(z, z, z, conv_w, shifts)


def _mm(x, y):
    return jnp.dot(x.astype(bf16), y.astype(bf16), preferred_element_type=f32)


def _unit_tri_inverse(a_list, ri, ci, interleave=None):
    b16 = (ri >> 4) == (ci >> 4)
    b32 = (ri >> 5) == (ci >> 5)
    eye = jnp.where(ri == ci, 1.0, 0.0)
    n = a_list[0].shape[0]
    a16 = [jnp.where(b16, a, 0.0) for a in a_list]
    t = [eye - x for x in a16]
    p = [_mm(x, x) for x in a16]
    for step in range(2):
        tp = [_mm(jnp.concatenate([ti, pi], axis=0), pi) for ti, pi in zip(t, p)]
        t = [ti + x[:n] for ti, x in zip(t, tp)]
        p = [x[n:] for x in tp]
        if step == 0 and interleave is not None:
            interleave()
    t = [ti + _mm(ti, pi) for ti, pi in zip(t, p)]
    a32 = [jnp.where(b32, a, 0.0) - x for a, x in zip(a_list, a16)]
    ta = [_mm(ti, ai) for ti, ai in zip(t, a32)]
    t = [ti - _mm(tai, ti) for ti, tai in zip(t, ta)]
    a64 = [jnp.where(b32, 0.0, a) for a in a_list]
    ta = [_mm(ti, ai) for ti, ai in zip(t, a64)]
    t = [ti - _mm(tai, ti) for ti, tai in zip(t, ta)]
    return t


def _dn_body(q_ref, k_ref, v_ref, ba_ref, al_ref, dt_ref, o_ref, s_scr, u_scr, wq_scr, qk_scr, eg_scr, *, G):
    d = pl.program_id(1)
    c = pl.program_id(3)
    dk = LANES
    hs = range(G)

    @pl.when(c == 0)
    def _():
        for ref in (s_scr, u_scr, wq_scr, qk_scr, eg_scr):
            ref[...] = jnp.zeros_like(ref)

    ba = ba_ref[0]
    beta_t = _sigmoid(ba)
    g_t = -jnp.exp(al_ref[0]) * _softplus(ba + dt_ref[0])
    ri = lax.broadcasted_iota(jnp.int32, (CHUNK, CHUNK), 0)
    ci = lax.broadcasted_iota(jnp.int32, (CHUNK, CHUNK), 1)
    fwd = d == 0
    later = jnp.where(fwd, ri, ci)
    earlier = jnp.where(fwd, ci, ri)
    incl = later >= earlier
    strict = later > earlier
    tri = jnp.where(incl, 1.0, 0.0)
    gc = jnp.dot(tri, g_t, preferred_element_type=f32, precision=HIGHEST)
    gct = gc.T
    gtot = jnp.sum(g_t, axis=0, keepdims=True)

    def pick(x, col):
        return jnp.where(fwd, x[:, col:col + 1], x[:, G + col:G + col + 1])

    sl = [slice(h * dk, (h + 1) * dk) for h in hs]
    beta_c = [pick(beta_t, h) for h in hs]
    gc_c = [pick(gc, 2 * G + h) for h in hs]
    gc_r = [jnp.where(fwd, gct[2 * G + h:2 * G + h + 1], gct[3 * G + h:3 * G + h + 1]) for h in hs]
    gt = [pick(gtot, 2 * G + h) for h in hs]
    q = [q_ref[:, sl[h]].astype(f32) for h in hs]
    k = [k_ref[:, sl[h]].astype(f32) for h in hs]
    kb = [k[h] * beta_c[h] for h in hs]
    eg = [jnp.exp(gc_c[h]) for h in hs]
    dec = [jnp.where(incl, jnp.exp(jnp.where(incl, gc_c[h] - gc_r[h], 0.0)), 0.0) for h in hs]
    kq = [lax.dot_general(jnp.concatenate([kb[h], q[h]], axis=0).astype(bf16), k[h].astype(bf16),
                          (((1,), (1,)), ((), ())), preferred_element_type=f32) for h in hs]
    s = [s_scr[h] for h in hs]
    r = [jnp.dot(wq_scr[h], s[h].astype(bf16), preferred_element_type=f32) for h in hs]
    a = [jnp.where(strict, kq[h][:CHUNK] * dec[h], 0.0) for h in hs]
    qk = [(kq[h][CHUNK:] * dec[h]).astype(bf16) for h in hs]

    def recurrence_tail():
        vnb = [(u_scr[h] - r[h][:CHUNK]).astype(bf16) for h in hs]
        ou = [jnp.dot(qk_scr[h], vnb[h], preferred_element_type=f32) for h in hs]
        for h in hs:
            s_scr[h] = s[h] * eg_scr[h] + ou[h][CHUNK:]
            o_ref[0, :, sl[h]] = (r[h][CHUNK:] + ou[h][:CHUNK]).astype(o_ref.dtype)

    t = _unit_tri_inverse(a, ri, ci, interleave=recurrence_tail)
    rhs = [jnp.concatenate([v_ref[:, sl[h]].astype(f32) * beta_c[h], kb[h] * eg[h]], axis=1) for h in hs]
    sol = [_mm(t[h], rhs[h]) for h in hs]
    for h in hs:
        u_scr[h] = sol[h][:, :dk]
        wq_scr[h] = jnp.concatenate([sol[h][:, dk:], q[h] * eg[h]], axis=0).astype(bf16)
        kd_t = (k[h] * jnp.exp(gt[h] - gc_c[h])).T.astype(bf16)
        qk_scr[h] = jnp.concatenate([qk[h], kd_t], axis=0)
        eg_scr[h] = jnp.broadcast_to(jnp.exp(gt[h]), (1, dk))


def _deltanet(qkv, ba, a_log, dt_bias, *, B, S, C, H, G):
    T = qkv.shape[0]
    dn_w = H * LANES
    ncl = S // CHUNK
    ncc = C // CHUNK
    nc = ncl + ncc
    hgs = H // G

    def rowblk(b, d, c):
        cc = jnp.where(d == 0, c, ncc - 1 - c)
        cl = jnp.where(d == 0, c - ncc, nc - 1 - c)
        return jnp.where(c < ncc, B * ncl + b * ncc + cc, b * ncl + cl)

    def grouped(t):
        r = t.shape[0]
        t = t.reshape(r, 4, hgs, G).transpose(2, 0, 1, 3).reshape(hgs, r, 4 * G)
        return jnp.pad(t, ((0, 0), (0, 0), (0, LANES - 4 * G)))

    zeros = jnp.zeros((1, 2 * H), f32)
    ba_g = grouped(ba[:, :4 * H])
    al_g = grouped(jnp.concatenate([zeros, a_log.reshape(1, 2 * H)], 1))
    dt_g = grouped(jnp.concatenate([zeros, dt_bias.reshape(1, 2 * H)], 1))
    body = functools.partial(_dn_body, G=G)

    def blk_in(b, d, c):
        return rowblk(b, d, jnp.minimum(c, nc - 1))

    def blk_out(b, d, c):
        return rowblk(b, d, jnp.maximum(c - 1, 0))

    return pl.pallas_call(
        body,
        grid=(B, 2, hgs, nc + 1),
        in_specs=[pl.BlockSpec((CHUNK, G * LANES), lambda b, d, g, c: (blk_in(b, d, c), g)),
                  pl.BlockSpec((CHUNK, G * LANES), lambda b, d, g, c: (blk_in(b, d, c), hgs + g)),
                  pl.BlockSpec((CHUNK, G * LANES), lambda b, d, g, c: (blk_in(b, d, c), 2 * hgs + g)),
                  pl.BlockSpec((1, CHUNK, LANES), lambda b, d, g, c: (g, blk_in(b, d, c), 0)),
                  pl.BlockSpec((1, 1, LANES), lambda b, d, g, c: (g, 0, 0)),
                  pl.BlockSpec((1, 1, LANES), lambda b, d, g, c: (g, 0, 0))],
        out_specs=pl.BlockSpec((1, CHUNK, G * LANES), lambda b, d, g, c: (d, blk_out(b, d, c), g)),
        out_shape=jax.ShapeDtypeStruct((2, T, dn_w), bf16),
        scratch_shapes=[pltpu.VMEM((G, LANES, LANES), f32),
                        pltpu.VMEM((G, CHUNK, LANES), f32),
                        pltpu.VMEM((G, 2 * CHUNK, LANES), bf16),
                        pltpu.VMEM((G, CHUNK + LANES, CHUNK), bf16),
                        pltpu.VMEM((G, 1, LANES), f32)],
        compiler_params=_cparams(("parallel", "parallel", "parallel", "arbitrary")),
        name="deltanet",
    )(qkv, qkv, qkv, ba_g, al_g, dt_g)


def _pool_ctx_body(x_ref, o_ref, *, gw):
    C = x_ref.shape[0]
    ri = lax.broadcasted_iota(jnp.int32, (C, C), 0)
    ci = lax.broadcasted_iota(jnp.int32, (C, C), 1)
    t = lax.broadcasted_iota(jnp.int32, (C, 1), 0)
    for gi, w in enumerate(POOL_WINDOWS):
        sl = slice(gi * gw, (gi + 1) * gw)
        band = jnp.where((ci >= ri - w // 2) & (ci < ri + w - w // 2), 1.0, 0.0).astype(bf16)
        cnt = (jnp.minimum(t + w - w // 2, C) - jnp.maximum(t - w // 2, 0)).astype(f32)
        x = x_ref[:, sl]
        box = jnp.dot(band, x, preferred_element_type=f32)
        o_ref[:, sl] = (box / cnt - x.astype(f32)).astype(o_ref.dtype)


def _pool_lat_body(x_ref, o_ref, y_scr):
    g = pl.program_id(1)
    S = x_ref.shape[0]
    rows = S // GRID_W
    sb = 4 * GRID_W
    ri = lax.broadcasted_iota(jnp.int32, (sb, sb), 0)
    ci = lax.broadcasted_iota(jnp.int32, (sb, sb), 1)
    cr = ri & (GRID_W - 1)
    cc = ci & (GRID_W - 1)
    same = (ri >> 6) == (ci >> 6)
    col = lax.broadcasted_iota(jnp.int32, (GRID_W, 1), 0)
    for gi, w in enumerate(POOL_WINDOWS):
        @pl.when(g == gi)
        def _(w=w):
            band = jnp.where(same & (cc >= cr - w // 2) & (cc < cr + w - w // 2), 1.0, 0.0).astype(bf16)
            for s in range(S // sb):
                y_scr[s * sb:(s + 1) * sb, :] = jnp.dot(band, x_ref[s * sb:(s + 1) * sb, :],
                                                        preferred_element_type=f32)
            ncol = (jnp.minimum(col + w - w // 2, GRID_W) - jnp.maximum(col - w // 2, 0)).astype(f32)
            for r in range(rows):
                lo = max(r - w // 2, 0)
                hi = min(r + w - w // 2, rows)
                acc = y_scr[lo * GRID_W:(lo + 1) * GRID_W, :]
                for rr in range(lo + 1, hi):
                    acc = acc + y_scr[rr * GRID_W:(rr + 1) * GRID_W, :]
                cnt = ncol * float(hi - lo)
                xs = x_ref[r * GRID_W:(r + 1) * GRID_W, :].astype(f32)
                o_ref[r * GRID_W:(r + 1) * GRID_W, :] = (acc / cnt - xs).astype(o_ref.dtype)


def _pool(z, *, B, S, C, pool_off_blocks, pool_w, gw):
    nlat_c = B * S // C
    ng = pool_w // gw
    assert ng == len(POOL_WINDOWS)
    pooled_ctx = pl.pallas_call(
        functools.partial(_pool_ctx_body, gw=gw),
        grid=(B,),
        in_specs=[pl.BlockSpec((C, pool_w), lambda b: (nlat_c + b, pool_off_blocks))],
        out_specs=pl.BlockSpec((C, pool_w), lambda b: (b, 0)),
        out_shape=jax.ShapeDtypeStruct((B * C, pool_w), bf16),
        compiler_params=_cparams(("parallel",)),
        name="pool_ctx",
    )(z)
    off = pool_off_blocks * ng
    pooled_lat = pl.pallas_call(
        _pool_lat_body,
        grid=(B, ng),
        in_specs=[pl.BlockSpec((S, gw), lambda b, g: (b, off + g))],
        out_specs=pl.BlockSpec((S, gw), lambda b, g: (b, g)),
        out_shape=jax.ShapeDtypeStruct((B * S, pool_w), bf16),
        scratch_shapes=[pltpu.VMEM((S, gw), f32)],
        compiler_params=_cparams(("parallel", "parallel")),
        name="pool_lat",
    )(z)
    return pooled_lat, pooled_ctx


def _merge_a_body(o_ref, og_ref, m1_ref, m2_ref, pl_ref, pc_ref, ong_ref, wpool_ref, ps_ref, wud_ref, wup_ref,
                  out_ref, a_scr, p_scr, *, H, gw, nlat):
    is_lat = pl.program_id(0) < nlat
    for h in range(H):
        sl = slice(h * LANES, (h + 1) * LANES)
        o = o_ref[0, :, sl].astype(f32) + o_ref[1, :, sl].astype(f32)
        og = og_ref[:, sl].astype(f32)
        y = o * lax.rsqrt(jnp.mean(o * o, -1, keepdims=True) + EPS) * ong_ref[...] * (og * _sigmoid(og))
        a_scr[:, sl] = y.astype(bf16)
    y_dn = jnp.dot(a_scr[...], wud_ref[0].astype(bf16), preferred_element_type=f32)
    for g in range(wpool_ref.shape[1]):
        sl = slice(g * gw, (g + 1) * gw)
        pooled = jnp.where(is_lat, pl_ref[:, sl], pc_ref[:, sl])
        yp = jnp.dot(pooled, wpool_ref[0, g].astype(bf16), preferred_element_type=f32) * ps_ref[:, sl]
        p_scr[:, sl] = yp.astype(bf16)
    y_pool = jnp.dot(p_scr[...], wup_ref[0].astype(bf16), preferred_element_type=f32)
    g1 = _sigmoid(m1_ref[...].astype(f32))
    g2 = _sigmoid(m2_ref[...].astype(f32))
    out_ref[...] = (g1 * y_dn + g2 * y_pool).astype(out_ref.dtype)


def _merge_a(o_dn, z, pooled_lat, pooled_ctx, onorm_g, w_pool, pool_scale, w_up_dn, w_up_pool, *, layer, n_rows, tm,
             H, D):
    dn_w = H * LANES
    pool_w = pooled_lat.shape[1]
    nlat = pooled_lat.shape[0] // tm
    _, ng, gw, _ = w_pool.shape
    once = pl.Buffered(1)
    og_blk = 3
    m_blk = 4 * dn_w // D
    const = lambda *_: (0, 0)
    return pl.pallas_call(
        functools.partial(_merge_a_body, H=H, gw=gw, nlat=nlat),
        grid=(n_rows // tm,),
        in_specs=[pl.BlockSpec((2, tm, dn_w), lambda i: (0, i, 0)),
                  pl.BlockSpec((tm, dn_w), lambda i: (i, og_blk)),
                  pl.BlockSpec((tm, D), lambda i: (i, m_blk)),
                  pl.BlockSpec((tm, D), lambda i: (i, m_blk + 1)),
                  pl.BlockSpec((tm, pool_w), lambda i: (jnp.minimum(i, nlat - 1), 0)),
                  pl.BlockSpec((tm, pool_w), lambda i: (jnp.maximum(i - nlat, 0), 0)),
                  pl.BlockSpec((1, LANES), const),
                  pl.BlockSpec((1, ng, gw, gw), lambda i: (layer, 0, 0, 0), pipeline_mode=once),
                  pl.BlockSpec((1, pool_w), const),
                  pl.BlockSpec((1, dn_w, D), lambda i: (layer, 0, 0), pipeline_mode=once),
                  pl.BlockSpec((1, pool_w, D), lambda i: (layer, 0, 0), pipeline_mode=once)],
        out_specs=pl.BlockSpec((tm, D), lambda i: (i, 0)),
        out_shape=jax.ShapeDtypeStruct((n_rows, D), bf16),
        scratch_shapes=[pltpu.VMEM((tm, dn_w), bf16), pltpu.VMEM((tm, pool_w), bf16)],
        compiler_params=_cparams(("parallel",)),
        name="merge_a",
    )(o_dn, z, z, z, pooled_lat, pooled_ctx, onorm_g, w_pool, pool_scale, w_up_dn, w_up_pool)


def _merge_b_body(m_ref, h_ref, gt_ref, sc_ref, sh_ref, g_ref, wout_ref, wr_ref, h1_ref, v_ref, lg_ref,
                  *, ds):
    tm = m_ref.shape[0]
    m = jnp.dot(m_ref[...], wout_ref[0].astype(bf16), preferred_element_type=f32)
    h1 = h_ref[...] + gt_ref[0] * m
    h1_ref[...] = h1
    y = h1 * lax.rsqrt(jnp.mean(h1 * h1, -1, keepdims=True) + EPS) * g_ref[...]
    v = y * (1.0 + sc_ref[0]) + sh_ref[0]
    E = lg_ref.shape[1]
    vh = v.astype(bf16)
    vl = (v - vh.astype(f32)).astype(bf16)
    th = jnp.dot(vh, wr_ref[...], preferred_element_type=f32)
    tl = jnp.dot(vl, wr_ref[...], preferred_element_type=f32)
    lg_ref[...] = th[:, :E] + (th[:, E:] + tl[:, :E])
    half = ds * LANES
    for j in range(ds):
        v_ref[pl.ds(j, tm, stride=ds), :] = _pack_pair(v[:, j * LANES:(j + 1) * LANES],
                                                       v[:, half + j * LANES:half + (j + 1) * LANES])


def _merge_b(merged, h, mod, gain2, w_out, w_router, *, layer, n_rows, tm, tpb, nlat, nb):
    D = h.shape[1]
    E = w_router.shape[1]
    ds = D // (2 * LANES)
    const = lambda *_: (0, 0)
    wr_hi = w_router.astype(bf16)
    wr_lo = (w_router - wr_hi.astype(f32)).astype(bf16)
    wr_cat = jnp.concatenate([wr_hi, wr_lo], 1)
    return pl.pallas_call(
        functools.partial(_merge_b_body, ds=ds),
        grid=(n_rows // tm,),
        in_specs=[pl.BlockSpec((tm, D), lambda i: (i, 0)),
                  pl.BlockSpec((tm, D), lambda i: (i, 0)),
                  _mod_spec(2, D, tpb, nlat, nb),
                  _mod_spec(4, D, tpb, nlat, nb),
                  _mod_spec(3, D, tpb, nlat, nb),
                  pl.BlockSpec((1, D), const),
                  pl.BlockSpec((1, D, D), lambda i: (layer, 0, 0), pipeline_mode=pl.Buffered(1)),
                  pl.BlockSpec((D, 2 * E), const)],
        out_specs=[pl.BlockSpec((tm, D), lambda i: (i, 0)),
                   pl.BlockSpec((tm * ds, LANES), lambda i: (i, 0)),
                   pl.BlockSpec((tm, E), lambda i: (i, 0))],
        out_shape=[jax.ShapeDtypeStruct((n_rows, D), f32),
                   jax.ShapeDtypeStruct((n_rows * ds, LANES), jnp.uint32),
                   jax.ShapeDtypeStruct((n_rows, E), f32)],
        compiler_params=_cparams(("parallel",)),
        name="merge_b",
    )(merged, h, mod, mod, mod, gain2, w_out, wr_cat)


def _route_body(lg_ref, bias_ref, idx_ref, gate_ref, cnt_ref, carry_scr, *, n_groups):
    i = pl.program_id(0)
    E, tt = lg_ref.shape

    @pl.when(i == 0)
    def _():
        carry_scr[...] = jnp.zeros_like(carry_scr)

    aff = _sigmoid(lg_ref[...])
    sel = aff + bias_ref[...]
    sub = lax.broadcasted_iota(jnp.int32, (GROUP_SIZE, tt), 0)
    neg = -1e30
    best = None
    for g in range(n_groups):
        s = sel[g * GROUP_SIZE:(g + 1) * GROUP_SIZE]
        a = aff[g * GROUP_SIZE:(g + 1) * GROUP_SIZE]
        m1 = jnp.max(s, axis=0, keepdims=True)
        i1 = jnp.min(jnp.where(s == m1, sub, GROUP_SIZE), axis=0, keepdims=True)
        s2 = jnp.where(sub == i1, neg, s)
        m2 = jnp.max(s2, axis=0, keepdims=True)
        i2 = jnp.min(jnp.where(s2 == m2, sub, GROUP_SIZE), axis=0, keepdims=True)
        a1 = jnp.sum(jnp.where(sub == i1, a, 0.0), axis=0, keepdims=True)
        a2 = jnp.sum(jnp.where(sub == i2, a, 0.0), axis=0, keepdims=True)
        cand = (m1 + m2, i1 + g * GROUP_SIZE, i2 + g * GROUP_SIZE, a1, a2)
        if best is None:
            best = cand
        else:
            better = cand[0] > best[0]
            best = tuple(jnp.where(better, cn, bs) for cn, bs in zip(cand, best))
    _, e1, e2, a1, a2 = best
    den = a1 + a2
    gate_ref[0:1, :] = a1 / den
    gate_ref[1:2, :] = a2 / den
    erow = lax.broadcasted_iota(jnp.int32, (E, tt), 0)
    oh1 = erow == e1
    oh2 = erow == e2
    cnt = jnp.where(oh1 | oh2, 1.0, 0.0)
    ti = lax.broadcasted_iota(jnp.int32, (tt, tt), 0)
    tj = lax.broadcasted_iota(jnp.int32, (tt, tt), 1)
    before = jnp.where(ti < tj, 1.0, 0.0).astype(bf16)
    pos = jnp.dot(cnt.astype(bf16), before, preferred_element_type=f32) + carry_scr[...]
    r1 = jnp.sum(jnp.where(oh1, pos, 0.0), axis=0, keepdims=True)
    r2 = jnp.sum(jnp.where(oh2, pos, 0.0), axis=0, keepdims=True)
    idx_ref[0:1, :] = e1
    idx_ref[1:2, :] = e2
    idx_ref[2:3, :] = r1.astype(jnp.int32)
    idx_ref[3:4, :] = r2.astype(jnp.int32)
    carry_scr[...] = carry_scr[...] + jnp.sum(cnt, axis=1, keepdims=True)
    cnt_ref[...] = jnp.broadcast_to(carry_scr[...], cnt_ref.shape)


def _route(logits_t, bias_col, *, tt):
    E, T = logits_t.shape
    return pl.pallas_call(
        functools.partial(_route_body, n_groups=E // GROUP_SIZE),
        grid=(T // tt,),
        in_specs=[pl.BlockSpec((E, tt), lambda i: (0, i)),
                  pl.BlockSpec((E, 1), lambda i: (0, 0))],
        out_specs=[pl.BlockSpec((4, tt), lambda i: (0, i)),
                   pl.BlockSpec((2, tt), lambda i: (0, i)),
                   pl.BlockSpec((E, LANES), lambda i: (0, 0))],
        out_shape=[jax.ShapeDtypeStruct((4, T), jnp.int32),
                   jax.ShapeDtypeStruct((2, T), f32),
                   jax.ShapeDtypeStruct((E, LANES), f32)],
        scratch_shapes=[pltpu.VMEM((E, 1), f32)],
        compiler_params=_cparams(("arbitrary",)),
        name="route",
    )(logits_t, bias_col)


def _row_copy(src, dst, s, t, n, sem):
    def rows(x):
        return pl.ds(x * n if isinstance(x, int) else pl.multiple_of(x * n, 8), n)

    return pltpu.make_async_copy(src.at[rows(s), :], dst.at[rows(t), :], sem)


def _dispatch_body(dest_ref, pad_ref, v_ref, xs_hbm, zero_scr, sem, zsem, *, tt, ds, n_exp, n_blocks, rb):
    i = pl.program_id(0)

    @pl.when(i == 0)
    def _():
        zero_scr[...] = jnp.zeros_like(zero_scr)

        def per_expert(e, n):
            start = pad_ref[0, e]

            def one(r, m):
                _row_copy(zero_scr, xs_hbm, 0, start + r, ds, zsem).start()
                return m + 1

            return lax.fori_loop(0, pad_ref[1, e], one, n)

        total = lax.fori_loop(0, n_exp, per_expert, 0)

        def wait_one(r, m):
            _row_copy(zero_scr, xs_hbm, 0, 0, ds, zsem).wait()
            return m

        lax.fori_loop(0, total, wait_one, 0)
        n_used = pad_ref[2, 0]

        def blk_start(bk, m):
            _row_copy(zero_scr, xs_hbm, 0, bk, rb * ds, zsem).start()
            return m

        lax.fori_loop(n_used, n_blocks, blk_start, 0)

        def blk_wait(bk, m):
            _row_copy(zero_scr, xs_hbm, 0, 0, rb * ds, zsem).wait()
            return m

        lax.fori_loop(n_used, n_blocks, blk_wait, 0)

    def issue(r, m):
        src = v_ref.at[pl.ds(pl.multiple_of(r * ds, ds), ds), :]
        for kk in range(2):
            dst = xs_hbm.at[pl.ds(pl.multiple_of(dest_ref[0, kk, r] * ds, ds), ds), :]
            pltpu.make_async_copy(src, dst, sem).start(priority=kk)
        return m

    lax.fori_loop(0, tt, issue, 0, unroll=8)
    for kk in range(2):
        pltpu.make_async_copy(v_ref, xs_hbm.at[pl.ds(0, tt * ds), :], sem).wait()


def _dispatch(dest3, pad_info, v_ts, *, n_tok, n_blocks, tt, ds):
    n_exp = pad_info.shape[1]
    rb = EXPERT_BLOCK
    return pl.pallas_call(
        functools.partial(_dispatch_body, tt=tt, ds=ds, n_exp=n_exp, n_blocks=n_blocks, rb=rb),
        grid=(n_tok // tt,),
        in_specs=[pl.BlockSpec((1, 2, tt), lambda i: (i, 0, 0), memory_space=pltpu.SMEM),
                  pl.BlockSpec(memory_space=pltpu.SMEM),
                  pl.BlockSpec((tt * ds, LANES), lambda i: (i, 0))],
        out_specs=pl.BlockSpec(memory_space=pl.ANY),
        out_shape=jax.ShapeDtypeStruct((n_blocks * rb * ds, LANES), v_ts.dtype),
        scratch_shapes=[pltpu.VMEM((rb * ds, LANES), v_ts.dtype), pltpu.SemaphoreType.DMA(()),
                        pltpu.SemaphoreType.DMA(())],
        compiler_params=_cparams(("arbitrary",)),
        name="dispatch",
    )(dest3, pad_info, v_ts)


def _expert_body(blk0_ref, nblk_ref, nu_ref, x_hbm, wg_ref, wu_ref, wd_ref, y_hbm, x_buf, y_buf, x_s, sem_x, sem_y,
                 *, ds, rb, n_blocks):
    e = pl.program_id(0)
    b0 = blk0_ref[e]
    nb = nblk_ref[e]
    rows = rb * ds

    def x_copy(i, slot):
        return pltpu.make_async_copy(x_hbm.at[pl.ds(pl.multiple_of((b0 + i) * rows, rows), rows), :],
                                     x_buf.at[slot], sem_x.at[slot])

    def y_copy(i, slot):
        return pltpu.make_async_copy(y_buf.at[slot],
                                     y_hbm.at[pl.ds(pl.multiple_of((b0 + i) * rows, rows), rows), :], sem_y.at[slot])

    @pl.when(nb > 0)
    def _():
        x_copy(0, 0).start()

    def block(i, carry):
        slot = i % 2
        x_copy(i, slot).wait()

        @pl.when(i + 1 < nb)
        def _():
            x_copy(i + 1, 1 - slot).start()

        @pl.when(i >= 2)
        def _():
            y_copy(i - 2, slot).wait()

        half = ds * LANES
        for j in range(ds):
            lo, hi = _unpack_pair(x_buf[slot, pl.ds(j, rb, stride=ds), :])
            x_s[:, j * LANES:(j + 1) * LANES] = lo.astype(bf16)
            x_s[:, half + j * LANES:half + (j + 1) * LANES] = hi.astype(bf16)
        x = x_s[...]
        hg = jnp.dot(x, wg_ref[0, 0].astype(bf16), preferred_element_type=f32)
        hu = jnp.dot(x, wu_ref[0, 0].astype(bf16), preferred_element_type=f32)
        a = (hg * _sigmoid(hg) * hu).astype(bf16)
        y = jnp.dot(a, wd_ref[0, 0].astype(bf16), preferred_element_type=f32)
        for j in range(ds):
            y_buf[slot, pl.ds(j, rb, stride=ds), :] = _pack_pair(y[:, j * LANES:(j + 1) * LANES],
                                                                 y[:, half + j * LANES:half + (j + 1) * LANES])
        y_copy(i, slot).start()
        return carry

    lax.fori_loop(0, nb, block, 0)

    @pl.when(nb >= 2)
    def _():
        y_copy(nb - 2, nb % 2).wait()

    @pl.when(nb >= 1)
    def _():
        y_copy(nb - 1, (nb - 1) % 2).wait()

    @pl.when(e == pl.num_programs(0) - 1)
    def _():
        y_buf[0] = jnp.zeros(y_buf.shape[1:], y_buf.dtype)
        n_used = nu_ref[0]

        def start(bk, m):
            _row_copy(y_buf.at[0], y_hbm, 0, bk, rows, sem_y.at[0]).start()
            return m

        lax.fori_loop(n_used, n_blocks, start, 0)

        def wait(bk, m):
            _row_copy(y_buf.at[0], y_hbm, 0, 0, rows, sem_y.at[0]).wait()
            return m

        lax.fori_loop(n_used, n_blocks, wait, 0)


def _experts(blk0, nblk, n_used, xs_ts, w_g, w_u, w_d, *, layer, ds):
    _, E, D, De = w_g.shape
    rb = EXPERT_BLOCK
    n_blocks = xs_ts.shape[0] // (rb * ds)
    grid_spec = pltpu.PrefetchScalarGridSpec(
        num_scalar_prefetch=3,
        grid=(E,),
        in_specs=[pl.BlockSpec(memory_space=pl.ANY),
                  pl.BlockSpec((1, 1, D, De), lambda e, *_: (layer, e, 0, 0)),
                  pl.BlockSpec((1, 1, D, De), lambda e, *_: (layer, e, 0, 0)),
                  pl.BlockSpec((1, 1, De, D), lambda e, *_: (layer, e, 0, 0))],
        out_specs=pl.BlockSpec(memory_space=pl.ANY),
        scratch_shapes=[pltpu.VMEM((2, rb * ds, LANES), xs_ts.dtype), pltpu.VMEM((2, rb * ds, LANES), xs_ts.dtype),
                        pltpu.VMEM((rb, D), bf16), pltpu.SemaphoreType.DMA((2,)), pltpu.SemaphoreType.DMA((2,))],
    )
    return pl.pallas_call(
        functools.partial(_expert_body, ds=ds, rb=rb, n_blocks=n_blocks),
        grid_spec=grid_spec,
        out_shape=jax.ShapeDtypeStruct(xs_ts.shape, xs_ts.dtype),
        compiler_params=_cparams(("arbitrary",)),
        name="experts",
    )(blk0, nblk, n_used, xs_ts, w_g, w_u, w_d)


def _combine_body(dest_ref, dnext_ref, gate_ref, h_ref, gt_ref, fg_ref, y_hbm, o_ref, y_scr, sem, *, tt, ds, final):
    i = pl.program_id(0)
    slot = i % 2

    def gather(d_ref, to_slot):
        def body(r, m):
            for kk in range(2):
                _row_copy(y_hbm, y_scr.at[to_slot, kk], d_ref[0, kk, r], r, ds, sem.at[to_slot]).start(priority=kk)
            return m

        lax.fori_loop(0, tt, body, 0, unroll=8)

    @pl.when(i == 0)
    def _():
        gather(dest_ref, 0)

    @pl.when(i + 1 < pl.num_programs(0))
    def _():
        gather(dnext_ref, 1 - slot)

    for kk in range(2):
        pltpu.make_async_copy(y_hbm.at[pl.ds(0, tt * ds), :], y_scr.at[slot, kk], sem.at[slot]).wait()
    gt = gt_ref[0]
    g0 = gate_ref[:, 0:1]
    g1 = gate_ref[:, 1:2]
    half = ds * LANES
    for j in range(ds):
        lo0, hi0 = _unpack_pair(y_scr[slot, 0, pl.ds(j, tt, stride=ds), :])
        lo1, hi1 = _unpack_pair(y_scr[slot, 1, pl.ds(j, tt, stride=ds), :])
        for base, y0, y1 in ((0, lo0, lo1), (half, hi0, hi1)):
            sl = slice(base + j * LANES, base + (j + 1) * LANES)
            o_ref[:, sl] = h_ref[:, sl] + gt[:, sl] * (g0 * y0 + g1 * y1)
    if final:
        x = o_ref[...]
        o_ref[...] = x * lax.rsqrt(jnp.mean(x * x, -1, keepdims=True) + EPS) * fg_ref[...]


def _combine(dest3, gates_t, h1, mod, final_g, y_ts, *, n_tok, tt, ds, tpb, nlat, nb, final):
    D = h1.shape[1]
    nt = n_tok // tt
    return pl.pallas_call(
        functools.partial(_combine_body, tt=tt, ds=ds, final=final),
        grid=(nt,),
        in_specs=[pl.BlockSpec((1, 2, tt), lambda i: (i, 0, 0), memory_space=pltpu.SMEM),
                  pl.BlockSpec((1, 2, tt), lambda i: (jnp.minimum(i + 1, nt - 1), 0, 0), memory_space=pltpu.SMEM),
                  pl.BlockSpec((tt, 2), lambda i: (i, 0)),
                  pl.BlockSpec((tt, D), lambda i: (i, 0)),
                  _mod_spec(5, D, tpb, nlat, nb),
                  pl.BlockSpec((1, D), lambda i: (0, 0)),
                  pl.BlockSpec(memory_space=pl.ANY)],
        out_specs=pl.BlockSpec((tt, D), lambda i: (i, 0)),
        out_shape=jax.ShapeDtypeStruct((n_tok, D), f32),
        scratch_shapes=[pltpu.VMEM((2, 2, tt * ds, LANES), y_ts.dtype), pltpu.SemaphoreType.DMA((2,))],
        compiler_params=_cparams(("arbitrary",)),
        name="combine",
    )(dest3, dest3, gates_t, h1, mod, final_g, y_ts)


def _moe(v_ts, logits, h1, mod, w_router_bias, w_g, w_u, w_d, final_g, *, layer, n_tok, tt, ds, tpb, nlat, nb,
         final):
    E = w_g.shape[1]
    route_tt = next(t for t in (512, 256, LANES) if n_tok % t == 0)
    idx, gates, counts = _route(logits[:n_tok].T, w_router_bias.reshape(E, 1), tt=route_tt)
    counts = counts[:, 0].astype(jnp.int32)
    padded = (counts + EXPERT_BLOCK - 1) // EXPERT_BLOCK * EXPERT_BLOCK
    pad_end = jnp.cumsum(padded)
    pad_start = pad_end - padded
    n_blocks = (n_tok * 2 + EXPERT_BLOCK - 1) // EXPERT_BLOCK + E
    n_used = (pad_end[-1] // EXPERT_BLOCK).astype(jnp.int32).reshape(1)
    e_sel = idx[0:2][..., None] == jnp.arange(E, dtype=jnp.int32)
    dest = jnp.sum(jnp.where(e_sel, pad_start.astype(jnp.int32), 0), -1) + idx[2:4]
    dest3 = dest.reshape(2, n_tok // tt, tt).transpose(1, 0, 2)
    pad_info = jnp.stack([pad_start + counts, padded - counts, jnp.broadcast_to(n_used, (E,))]).astype(jnp.int32)
    xs_ts = _dispatch(dest3, pad_info, v_ts, n_tok=n_tok, n_blocks=n_blocks, tt=tt, ds=ds)
    y_ts = _experts((pad_start // EXPERT_BLOCK).astype(jnp.int32), (padded // EXPERT_BLOCK).astype(jnp.int32), n_used,
                    xs_ts, w_g, w_u, w_d, layer=layer, ds=ds)
    return _combine(dest3, gates.T, h1, mod, final_g, y_ts, n_tok=n_tok, tt=tt, ds=ds, tpb=tpb, nlat=nlat,
                    nb=nb, final=final)


def kernel(x, c, ctx, c_ctx, w_mod, b_mod, norm1_g, norm2_g, w_in, conv_w, a_log, dt_bias, onorm_g, w_pool, pool_scale, w_up_dn, w_up_pool, w_out, w_router, router_bias, w_gate_e, w_up_e, w_down_e, final_g):
    B, S, D = x.shape
    C = ctx.shape[1]
    L = w_mod.shape[0]
    H = a_log.shape[-1]
    dk = onorm_g.shape[-1]
    assert dk == LANES
    dn_w = H * dk
    _, ng, gw, _ = w_pool.shape
    pool_w = ng * gw
    ds = D // (2 * LANES)
    n_lat = B * S
    T = n_lat + B * C
    tm = 256
    assert S % tm == 0 and (B * C) % tm == 0 and C % CHUNK == 0 and S % (4 * GRID_W) == 0
    tpb = S // tm
    nlat_tiles = n_lat // tm
    tm_in = next(t for t in (1024, 512, 256) if S % t == 0 and (B * C) % t == 0)
    tm_mg = next(t for t in (512, 256) if S % t == 0 and (B * C) % t == 0)
    tr = min(256, C)
    G = min(H, 16)

    c8 = jnp.zeros((8, D), f32).at[:B].set(c).at[B].set(c_ctx)
    mod = _mod_table(c8, w_mod, b_mod)

    qkv_c = 3 * dn_w
    scan_c = qkv_c + 4 * H
    og_off, pool_off = scan_c, scan_c + dn_w
    mg_off = pool_off + pool_w
    assert (4 * dn_w) % D == 0 and (4 * dn_w + 2 * D) % pool_w == 0
    pool_off_blocks = (4 * dn_w + 2 * D) // pool_w

    h = jnp.concatenate([x.reshape(n_lat, D), ctx.reshape(B * C, D)], 0)
    for l in range(L):
        last = l == L - 1
        mod_l = mod[l].reshape(8, 1, mod.shape[-1])
        wl = w_in[l]
        w_z = jnp.concatenate([wl[:, :qkv_c], wl[:, og_off:pool_off], wl[:, mg_off:], wl[:, pool_off:mg_off]],
                              1).astype(bf16)
        w_ba = jnp.pad(wl[:, qkv_c:scan_c], ((0, 0), (0, LANES - 4 * H))).astype(bf16)
        tn = 1024 if w_z.shape[1] % 1024 == 0 else w_z.shape[1]
        z, ba = _in_proj(h, mod_l, 0, norm1_g[l].reshape(1, D), w_z, w_ba, tm=tm_in, tn=tn, tpb=S // tm_in,
                         nlat=n_lat // tm_in, nb=B)
        qkv = _conv_prep(z, conv_w[l], tr=tr, tc=min(1024, dn_w), S=S, C=C, nlat_rows=n_lat, dn_w=dn_w, dk=dk)
        o_dn = _deltanet(qkv, ba, a_log[l], dt_bias[l], B=B, S=S, C=C, H=H, G=G)
        pooled_lat, pooled_ctx = _pool(z, B=B, S=S, C=C, pool_off_blocks=pool_off_blocks, pool_w=pool_w, gw=gw)
        n_rows = n_lat if last else T
        merged = _merge_a(o_dn, z, pooled_lat, pooled_ctx, onorm_g[l].reshape(1, dk), w_pool,
                          pool_scale[l].reshape(1, pool_w), w_up_dn, w_up_pool,
                          layer=l, n_rows=n_rows, tm=tm, H=H, D=D)
        h1, v_ts, logits = _merge_b(merged, h, mod_l, norm2_g[l].reshape(1, D), w_out, w_router,
                                    layer=l, n_rows=n_rows, tm=tm_mg, tpb=S // tm_mg, nlat=n_lat // tm_mg, nb=B)
        h = _moe(v_ts, logits, h1, mod_l, router_bias, w_gate_e, w_up_e, w_down_e,
                 final_g.reshape(1, D), layer=l, n_tok=n_rows, tt=tm, ds=ds, tpb=tpb, nlat=nlat_tiles, nb=B, final=last)
    return h.reshape(B, S, D)
```

```python
import functools

import jax
import jax.numpy as jnp
from jax import lax
from jax.experimental import pallas as pl
from jax.experimental.pallas import tpu as pltpu

f32 = jnp.float32
bf16 = jnp.bfloat16
HIGHEST = lax.Precision.HIGHEST

EPS = 1e-6
LANES = 128
GRID_W = 64
CHUNK = 64
POOL_WINDOWS = (2, 4, 8, 16)
GROUP_SIZE = 8
EXPERT_BLOCK = 256
VMEM_LIMIT = 56 * 1024 * 1024


def _cparams(sem, vmem=VMEM_LIMIT):
    return pltpu.CompilerParams(dimension_semantics=sem, vmem_limit_bytes=vmem)


def _sigmoid(x):
    return 1.0 / (1.0 + jnp.exp(-x))


def _softplus(x):
    return jnp.maximum(x, 0.0) + jnp.log(1.0 + jnp.exp(-jnp.abs(x)))


def _pack_pair(lo, hi):
    def rounded(x):
        u = lax.bitcast_convert_type(x, jnp.uint32)
        return u + jnp.uint32(0x7FFF) + ((u >> 16) & jnp.uint32(1))

    return (rounded(lo) >> 16) | (rounded(hi) & jnp.uint32(0xFFFF0000))


def _unpack_pair(w):
    lo = lax.bitcast_convert_type(w << 16, f32)
    hi = lax.bitcast_convert_type(w & jnp.uint32(0xFFFF0000), f32)
    return lo, hi


def _mod_spec(sec, D, tiles_per_batch, n_lat_tiles, n_batch):
    def index(i, *_):
        return (jnp.where(i < n_lat_tiles, i // tiles_per_batch, n_batch), 0, sec)

    return pl.BlockSpec((1, 1, D), index)


def _mod_body(c_ref, w_ref, b_ref, o_ref):
    c = c_ref[...]
    s = c * _sigmoid(c)
    o_ref[0] = jnp.dot(s, w_ref[0], preferred_element_type=f32, precision=HIGHEST) + b_ref[0]


def _mod_table(c8, w_mod, b_mod):
    L, D, N = w_mod.shape
    tn = next(t for t in (1024, 512, 256, LANES) if N % t == 0)
    return pl.pallas_call(
        _mod_body,
        grid=(L, N // tn),
        in_specs=[pl.BlockSpec((8, D), lambda l, j: (0, 0)),
                  pl.BlockSpec((1, D, tn), lambda l, j: (l, 0, j)),
                  pl.BlockSpec((1, 1, tn), lambda l, j: (l, 0, j))],
        out_specs=pl.BlockSpec((1, 8, tn), lambda l, j: (l, 0, j)),
        out_shape=jax.ShapeDtypeStruct((L, 8, N), f32),
        compiler_params=_cparams(("parallel", "parallel")),
        name="mod_table",
    )(c8, w_mod, b_mod.reshape(L, 1, N))


def _inproj_body(h_ref, sc_ref, sh_ref, g_ref, w_ref, wba_ref, z_ref, ba_ref, u_scr):
    j = pl.program_id(1)

    @pl.when(j == 0)
    def _():
        x = h_ref[...]
        y = x * lax.rsqrt(jnp.mean(x * x, -1, keepdims=True) + EPS) * g_ref[...]
        u = (y * (1.0 + sc_ref[0]) + sh_ref[0]).astype(bf16)
        u_scr[...] = u
        ba_ref[...] = jnp.dot(u, wba_ref[...], preferred_element_type=f32)

    z_ref[...] = jnp.dot(u_scr[...], w_ref[...], preferred_element_type=f32).astype(z_ref.dtype)


def _in_proj(h, mod, sec, gain, w_z, w_ba, *, tm, tn, tpb, nlat, nb):
    T, D = h.shape
    Nz = w_z.shape[1]
    return pl.pallas_call(
        _inproj_body,
        grid=(T // tm, Nz // tn),
        in_specs=[pl.BlockSpec((tm, D), lambda i, j: (i, 0)),
                  _mod_spec(sec + 1, D, tpb, nlat, nb),
                  _mod_spec(sec, D, tpb, nlat, nb),
                  pl.BlockSpec((1, D), lambda i, j: (0, 0)),
                  pl.BlockSpec((D, tn), lambda i, j: (0, j)),
                  pl.BlockSpec((D, LANES), lambda i, j: (0, 0))],
        out_specs=[pl.BlockSpec((tm, tn), lambda i, j: (i, j)),
                   pl.BlockSpec((tm, LANES), lambda i, j: (i, 0))],
        out_shape=[jax.ShapeDtypeStruct((T, Nz), bf16), jax.ShapeDtypeStruct((T, LANES), f32)],
        scratch_shapes=[pltpu.VMEM((tm, D), bf16)],
        compiler_params=_cparams(("parallel", "arbitrary")),
        name="in_proj",
    )(h, mod, mod, gain, w_z, w_ba)


def _conv_body(x_ref, xp_ref, xn_ref, w_ref, sh_ref, o_ref, *, tr, tps_lat, tps_ctx, nlat, qscale, tiles_per_sec,
               halo):
    i = pl.program_id(0)
    il =jnp.where(i < nlat, i % tps_lat, (i - nlat) % tps_ctx)
    nt = jnp.where(i < nlat, tps_lat, tps_ctx)
    first = il == 0
    last = il == nt - 1
    xb = x_ref[...]
    w = w_ref[...]
    tc_ = xb.shape[1]
    ys = jnp.dot(sh_ref[...], xb, preferred_element_type=f32).reshape(tr // 8, 32, tc_)
    acc = (ys[:, 0:8] * w[0:1] + ys[:, 8:16] * w[1:2] + ys[:, 16:24] * w[3:4] + ys[:, 24:32] * w[4:5])
    acc = acc.reshape(tr, tc_) + xb.astype(f32) * w[2:3]
    p = jnp.where(first, 0.0, xp_ref[...].astype(f32))[halo - 8:]
    n = jnp.where(last, 0.0, xn_ref[...].astype(f32))[:8]
    r8 = lax.broadcasted_iota(jnp.int32, (8, 1), 0)
    top = jnp.where(r8 == 0, p[6:7] * w[0:1] + p[7:8] * w[1:2], jnp.where(r8 == 1, p[7:8] * w[0:1], 0.0))
    bot = jnp.where(r8 == 7, n[0:1] * w[3:4] + n[1:2] * w[4:5], jnp.where(r8 == 6, n[0:1] * w[4:5], 0.0))
    acc = jnp.concatenate([acc[:8] + top, acc[8:tr - 8], acc[tr - 8:] + bot], axis=0)
    y = acc * _sigmoid(acc)
    sec = pl.program_id(1) // tiles_per_sec
    for s in range(acc.shape[1] // LANES):
        ys = y[:, s * LANES:(s + 1) * LANES]
        rs = lax.rsqrt(jnp.sum(ys * ys, -1, keepdims=True) + EPS)
        scale = jnp.where(sec == 2, 1.0, rs) * jnp.where(sec == 0, qscale, 1.0)
        o_ref[:, s * LANES:(s + 1) * LANES] = (ys * scale).astype(o_ref.dtype)


def _conv_prep(z, conv_w, *, tr, tc, S, C, nlat_rows, dn_w, dk):
    T = z.shape[0]
    halo = 16
    hb = tr // halo
    nh = T // halo
    body = functools.partial(_conv_body, tr=tr, tps_lat=S // tr, tps_ctx=C // tr, nlat=nlat_rows // tr,
                             qscale=float(dk) ** -0.5, tiles_per_sec=dn_w // tc, halo=halo)
    assert conv_w.shape[0] == 5
    ri = jnp.arange(tr)[:, None]
    ci = jnp.arange(tr)[None, :]
    shifts = jnp.stack([(ci == ri + off).astype(bf16) for off in (-2, -1, 1, 2)], 0)
    shifts = shifts.reshape(4, tr // 8, 8, tr).transpose(1, 0, 2, 3).reshape(4 * tr, tr)
    return pl.pallas_call(
        body,
        grid=(T // tr, 3 * dn_w // tc),
        in_specs=[pl.BlockSpec((tr, tc), lambda i, j: (i, j)),
                  pl.BlockSpec((halo, tc), lambda i, j: (jnp.maximum(i * hb - 1, 0), j)),
                  pl.BlockSpec((halo, tc), lambda i, j: (jnp.minimum((i + 1) * hb, nh - 1), j)),
                  pl.BlockSpec((conv_w.shape[0], tc), lambda i, j: (0, j)),
                  pl.BlockSpec((4 * tr, tr), lambda i, j: (0, 0))],
        out_specs=pl.BlockSpec((tr, tc), lambda i, j: (i, j)),
        out_shape=jax.ShapeDtypeStruct((T, 3 * dn_w), bf16),
        compiler_params=_cparams(("parallel", "parallel")),
        name="conv_prep",
    )(z, z, z, conv_w, shifts)


def _mm(x, y):
    return jnp.dot(x.astype(bf16), y.astype(bf16), preferred_element_type=f32)


def _unit_tri_inverse(a_list, ri, ci, interleave=None):
    b16 = (ri >> 4) == (ci >> 4)
    b32 = (ri >> 5) == (ci >> 5)
    eye = jnp.where(ri == ci, 1.0, 0.0)
    n = a_list[0].shape[0]
    a16 = [jnp.where(b16, a, 0.0) for a in a_list]
    t = [eye - x for x in a16]
    p = [_mm(x, x) for x in a16]
    for step in range(2):
        tp = [_mm(jnp.concatenate([ti, pi], axis=0), pi) for ti, pi in zip(t, p)]
        t = [ti + x[:n] for ti, x in zip(t, tp)]
        p = [x[n:] for x in tp]
        if step == 0 and interleave is not None:
            interleave()
    t = [ti + _mm(ti, pi) for ti, pi in zip(t, p)]
    a32 = [jnp.where(b32, a, 0.0) - x for a, x in zip(a_list, a16)]
    ta = [_mm(ti, ai) for ti, ai in zip(t, a32)]
    t = [ti - _mm(tai, ti) for ti, tai in zip(t, ta)]
    a64 = [jnp.where(b32, 0.0, a) for a in a_list]
    ta = [_mm(ti, ai) for ti, ai in zip(t, a64)]
    t = [ti - _mm(tai, ti) for ti, tai in zip(t, ta)]
    return t


def _dn_body(q_ref, k_ref, v_ref, ba_ref, al_ref, dt_ref, o_ref, s_scr, u_scr, wq_scr, qk_scr, eg_scr, *, G):
    d = pl.program_id(1)
    c = pl.program_id(3)
    dk = LANES
    hs = range(G)

    @pl.when(c == 0)
    def _():
        for ref in (s_scr, u_scr, wq_scr, qk_scr, eg_scr):
            ref[...] = jnp.zeros_like(ref)

    ba = ba_ref[0]
    beta_t = _sigmoid(ba)
    g_t = -jnp.exp(al_ref[0]) * _softplus(ba + dt_ref[0])
    ri = lax.broadcasted_iota(jnp.int32, (CHUNK, CHUNK), 0)
    ci = lax.broadcasted_iota(jnp.int32, (CHUNK, CHUNK), 1)
    fwd = d == 0
    later = jnp.where(fwd, ri, ci)
    earlier = jnp.where(fwd, ci, ri)
    incl = later >= earlier
    strict = later > earlier
    tri = jnp.where(incl, 1.0, 0.0)
    gc = jnp.dot(tri, g_t, preferred_element_type=f32, precision=HIGHEST)
    gct = gc.T
    gtot = jnp.sum(g_t, axis=0, keepdims=True)

    def pick(x, col):
        return jnp.where(fwd, x[:, col:col + 1], x[:, G + col:G + col + 1])

    sl = [slice(h * dk, (h + 1) * dk) for h in hs]
    beta_c = [pick(beta_t, h) for h in hs]
    gc_c = [pick(gc, 2 * G + h) for h in hs]
    gc_r = [jnp.where(fwd, gct[2 * G + h:2 * G + h + 1], gct[3 * G + h:3 * G + h + 1]) for h in hs]
    gt = [pick(gtot, 2 * G + h) for h in hs]
    q = [q_ref[:, sl[h]].astype(f32) for h in hs]
    k = [k_ref[:, sl[h]].astype(f32) for h in hs]
    kb = [k[h] * beta_c[h] for h in hs]
    eg = [jnp.exp(gc_c[h]) for h in hs]
    dec = [jnp.where(incl, jnp.exp(jnp.where(incl, gc_c[h] - gc_r[h], 0.0)), 0.0) for h in hs]
    kq = [lax.dot_general(jnp.concatenate([kb[h], q[h]], axis=0).astype(bf16), k[h].astype(bf16),
                          (((1,), (1,)), ((), ())), preferred_element_type=f32) for h in hs]
    s = [s_scr[h] for h in hs]
    r = [jnp.dot(wq_scr[h], s[h].astype(bf16), preferred_element_type=f32) for h in hs]
    a = [jnp.where(strict, kq[h][:CHUNK] * dec[h], 0.0) for h in hs]
    qk = [(kq[h][CHUNK:] * dec[h]).astype(bf16) for h in hs]

    def recurrence_tail():
        vnb = [(u_scr[h] - r[h][:CHUNK]).astype(bf16) for h in hs]
        ou = [jnp.dot(qk_scr[h], vnb[h], preferred_element_type=f32) for h in hs]
        for h in hs:
            s_scr[h] = s[h] * eg_scr[h] + ou[h][CHUNK:]
            o_ref[0, :, sl[h]] = (r[h][CHUNK:] + ou[h][:CHUNK]).astype(o_ref.dtype)

    t = _unit_tri_inverse(a, ri, ci, interleave=recurrence_tail)
    rhs = [jnp.concatenate([v_ref[:, sl[h]].astype(f32) * beta_c[h], kb[h] * eg[h]], axis=1) for h in hs]
    sol = [_mm(t[h], rhs[h]) for h in hs]
    for h in hs:
        u_scr[h] = sol[h][:, :dk]
        wq_scr[h] = jnp.concatenate([sol[h][:, dk:], q[h] * eg[h]], axis=0).astype(bf16)
        kd_t = (k[h] * jnp.exp(gt[h] - gc_c[h])).T.astype(bf16)
        qk_scr[h] = jnp.concatenate([qk[h], kd_t], axis=0)
        eg_scr[h] = jnp.broadcast_to(jnp.exp(gt[h]), (1, dk))


def _deltanet(qkv, ba, a_log, dt_bias, *, B, S, C, H, G):
    T = qkv.shape[0]
    dn_w = H * LANES
    ncl = S // CHUNK
    ncc = C // CHUNK
    nc = ncl + ncc
    hgs = H // G

    def rowblk(b, d, c):
        cc = jnp.where(d == 0, c, ncc - 1 - c)
        cl = jnp.where(d == 0, c - ncc, nc - 1 - c)
        return jnp.where(c < ncc, B * ncl + b * ncc + cc, b * ncl + cl)

    def grouped(t):
        r = t.shape[0]
        t = t.reshape(r, 4, hgs, G).transpose(2, 0, 1, 3).reshape(hgs, r, 4 * G)
        return jnp.pad(t, ((0, 0), (0, 0), (0, LANES - 4 * G)))

    zeros = jnp.zeros((1, 2 * H), f32)
    ba_g = grouped(ba[:, :4 * H])
    al_g = grouped(jnp.concatenate([zeros, a_log.reshape(1, 2 * H)], 1))
    dt_g = grouped(jnp.concatenate([zeros, dt_bias.reshape(1, 2 * H)], 1))
    body = functools.partial(_dn_body, G=G)

    def blk_in(b, d, c):
        return rowblk(b, d, jnp.minimum(c, nc - 1))

    def blk_out(b, d, c):
        return rowblk(b, d, jnp.maximum(c - 1, 0))

    return pl.pallas_call(
        body,
        grid=(B, 2, hgs, nc + 1),
        in_specs=[pl.BlockSpec((CHUNK, G * LANES), lambda b, d, g, c: (blk_in(b, d, c), g)),
                  pl.BlockSpec((CHUNK, G * LANES), lambda b, d, g, c: (blk_in(b, d, c), hgs + g)),
                  pl.BlockSpec((CHUNK, G * LANES), lambda b, d, g, c: (blk_in(b, d, c), 2 * hgs + g)),
                  pl.BlockSpec((1, CHUNK, LANES), lambda b, d, g, c: (g, blk_in(b, d, c), 0)),
                  pl.BlockSpec((1, 1, LANES), lambda b, d, g, c: (g, 0, 0)),
                  pl.BlockSpec((1, 1, LANES), lambda b, d, g, c: (g, 0, 0))],
        out_specs=pl.BlockSpec((1, CHUNK, G * LANES), lambda b, d, g, c: (d, blk_out(b, d, c), g)),
        out_shape=jax.ShapeDtypeStruct((2, T, dn_w), bf16),
        scratch_shapes=[pltpu.VMEM((G, LANES, LANES), f32),
                        pltpu.VMEM((G, CHUNK, LANES), f32),
                        pltpu.VMEM((G, 2 * CHUNK, LANES), bf16),
                        pltpu.VMEM((G, CHUNK + LANES, CHUNK), bf16),
                        pltpu.VMEM((G, 1, LANES), f32)],
        compiler_params=_cparams(("parallel", "parallel", "parallel", "arbitrary")),
        name="deltanet",
    )(qkv, qkv, qkv, ba_g, al_g, dt_g)


def _pool_ctx_body(x_ref, o_ref, *, gw):
    C = x_ref.shape[0]
    ri = lax.broadcasted_iota(jnp.int32, (C, C), 0)
    ci = lax.broadcasted_iota(jnp.int32, (C, C), 1)
    t = lax.broadcasted_iota(jnp.int32, (C, 1), 0)
    for gi, w in enumerate(POOL_WINDOWS):
        sl = slice(gi * gw, (gi + 1) * gw)
        band = jnp.where((ci >= ri - w // 2) & (ci < ri + w - w // 2), 1.0, 0.0).astype(bf16)
        cnt = (jnp.minimum(t + w - w // 2, C) - jnp.maximum(t - w // 2, 0)).astype(f32)
        x = x_ref[:, sl]
        box = jnp.dot(band, x, preferred_element_type=f32)
        o_ref[:, sl] = (box / cnt - x.astype(f32)).astype(o_ref.dtype)


def _pool_lat_body(x_ref, o_ref, y_scr):
    g = pl.program_id(1)
    S = x_ref.shape[0]
    rows = S // GRID_W
    sb = 4 * GRID_W
    ri = lax.broadcasted_iota(jnp.int32, (sb, sb), 0)
    ci = lax.broadcasted_iota(jnp.int32, (sb, sb), 1)
    cr = ri & (GRID_W - 1)
    cc = ci & (GRID_W - 1)
    same = (ri >> 6) == (ci >> 6)
    col = lax.broadcasted_iota(jnp.int32, (GRID_W, 1), 0)
    for gi, w in enumerate(POOL_WINDOWS):
        @pl.when(g == gi)
        def _(w=w):
            band = jnp.where(same & (cc >= cr - w // 2) & (cc < cr + w - w // 2), 1.0, 0.0).astype(bf16)
            for s in range(S // sb):
                y_scr[s * sb:(s + 1) * sb, :] = jnp.dot(band, x_ref[s * sb:(s + 1) * sb, :],
                                                        preferred_element_type=f32)
            ncol = (jnp.minimum(col + w - w // 2, GRID_W) - jnp.maximum(col - w // 2, 0)).astype(f32)
            for r in range(rows):
                lo = max(r - w // 2, 0)
                hi = min(r + w - w // 2, rows)
                acc = y_scr[lo * GRID_W:(lo + 1) * GRID_W, :]
                for rr in range(lo + 1, hi):
                    acc = acc + y_scr[rr * GRID_W:(rr + 1) * GRID_W, :]
                cnt = ncol * float(hi - lo)
                xs = x_ref[r * GRID_W:(r + 1) * GRID_W, :].astype(f32)
                o_ref[r * GRID_W:(r + 1) * GRID_W, :] = (acc / cnt - xs).astype(o_ref.dtype)


def _pool(z, *, B, S, C, pool_off_blocks, pool_w, gw):
    nlat_c = B * S // C
    ng = pool_w // gw
    assert ng == len(POOL_WINDOWS)
    pooled_ctx = pl.pallas_call(
        functools.partial(_pool_ctx_body, gw=gw),
        grid=(B,),
        in_specs=[pl.BlockSpec((C, pool_w), lambda b: (nlat_c + b, pool_off_blocks))],
        out_specs=pl.BlockSpec((C, pool_w), lambda b: (b, 0)),
        out_shape=jax.ShapeDtypeStruct((B * C, pool_w), bf16),
        compiler_params=_cparams(("parallel",)),
        name="pool_ctx",
    )(z)
    off = pool_off_blocks * ng
    pooled_lat = pl.pallas_call(
        _pool_lat_body,
        grid=(B, ng),
        in_specs=[pl.BlockSpec((S, gw), lambda b, g: (b, off + g))],
        out_specs=pl.BlockSpec((S, gw), lambda b, g: (b, g)),
        out_shape=jax.ShapeDtypeStruct((B * S, pool_w), bf16),
        scratch_shapes=[pltpu.VMEM((S, gw), f32)],
        compiler_params=_cparams(("parallel", "parallel")),
        name="pool_lat",
    )(z)
    return pooled_lat, pooled_ctx


def _merge_a_body(o_ref, og_ref, m1_ref, m2_ref, pl_ref, pc_ref, ong_ref, wpool_ref, ps_ref, wud_ref, wup_ref,
                  out_ref, a_scr, p_scr, *, H, gw, nlat):
    is_lat = pl.program_id(0) < nlat
    for h in range(H):
        sl = slice(h * LANES, (h + 1) * LANES)
        o = o_ref[0, :, sl].astype(f32) + o_ref[1, :, sl].astype(f32)
        og = og_ref[:, sl].astype(f32)
        y = o * lax.rsqrt(jnp.mean(o * o, -1, keepdims=True) + EPS) * ong_ref[...] * (og * _sigmoid(og))
        a_scr[:, sl] = y.astype(bf16)
    y_dn = jnp.dot(a_scr[...], wud_ref[0].astype(bf16), preferred_element_type=f32)
    for g in range(wpool_ref.shape[1]):
        sl = slice(g * gw, (g + 1) * gw)
        pooled = jnp.where(is_lat, pl_ref[:, sl], pc_ref[:, sl])
        yp = jnp.dot(pooled, wpool_ref[0, g].astype(bf16), preferred_element_type=f32) * ps_ref[:, sl]
        p_scr[:, sl] = yp.astype(bf16)
    y_pool = jnp.dot(p_scr[...], wup_ref[0].astype(bf16), preferred_element_type=f32)
    g1 = _sigmoid(m1_ref[...].astype(f32))
    g2 = _sigmoid(m2_ref[...].astype(f32))
    out_ref[...] = (g1 * y_dn + g2 * y_pool).astype(out_ref.dtype)


def _merge_a(o_dn, z, pooled_lat, pooled_ctx, onorm_g, w_pool, pool_scale, w_up_dn, w_up_pool, *, layer, n_rows, tm,
             H, D):
    dn_w = H * LANES
    pool_w = pooled_lat.shape[1]
    nlat = pooled_lat.shape[0] // tm
    _, ng, gw, _ = w_pool.shape
    once = pl.Buffered(1)
    og_blk = 3
    m_blk = 4 * dn_w // D
    const = lambda *_: (0, 0)
    return pl.pallas_call(
        functools.partial(_merge_a_body, H=H, gw=gw, nlat=nlat),
        grid=(n_rows // tm,),
        in_specs=[pl.BlockSpec((2, tm, dn_w), lambda i: (0, i, 0)),
                  pl.BlockSpec((tm, dn_w), lambda i: (i, og_blk)),
                  pl.BlockSpec((tm, D), lambda i: (i, m_blk)),
                  pl.BlockSpec((tm, D), lambda i: (i, m_blk + 1)),
                  pl.BlockSpec((tm, pool_w), lambda i: (jnp.minimum(i, nlat - 1), 0)),
                  pl.BlockSpec((tm, pool_w), lambda i: (jnp.maximum(i - nlat, 0), 0)),
                  pl.BlockSpec((1, LANES), const),
                  pl.BlockSpec((1, ng, gw, gw), lambda i: (layer, 0, 0, 0), pipeline_mode=once),
                  pl.BlockSpec((1, pool_w), const),
                  pl.BlockSpec((1, dn_w, D), lambda i: (layer, 0, 0), pipeline_mode=once),
                  pl.BlockSpec((1, pool_w, D), lambda i: (layer, 0, 0), pipeline_mode=once)],
        out_specs=pl.BlockSpec((tm, D), lambda i: (i, 0)),
        out_shape=jax.ShapeDtypeStruct((n_rows, D), bf16),
        scratch_shapes=[pltpu.VMEM((tm, dn_w), bf16), pltpu.VMEM((tm, pool_w), bf16)],
        compiler_params=_cparams(("parallel",)),
        name="merge_a",
    )(o_dn, z, z, z, pooled_lat, pooled_ctx, onorm_g, w_pool, pool_scale, w_up_dn, w_up_pool)


def _merge_b_body(m_ref, h_ref, gt_ref, sc_ref, sh_ref, g_ref, wout_ref, wr_ref, h1_ref, v_ref, lg_ref,
                  *, ds):
    tm = m_ref.shape[0]
    m = jnp.dot(m_ref[...], wout_ref[0].astype(bf16), preferred_element_type=f32)
    h1 = h_ref[...] + gt_ref[0] * m
    h1_ref[...] = h1
    y = h1 * lax.rsqrt(jnp.mean(h1 * h1, -1, keepdims=True) + EPS) * g_ref[...]
    v = y * (1.0 + sc_ref[0]) + sh_ref[0]
    E = lg_ref.shape[1]
    vh = v.astype(bf16)
    vl = (v - vh.astype(f32)).astype(bf16)
    th = jnp.dot(vh, wr_ref[...], preferred_element_type=f32)
    tl = jnp.dot(vl, wr_ref[...], preferred_element_type=f32)
    lg_ref[...] = th[:, :E] + (th[:, E:] + tl[:, :E])
    half = ds * LANES
    for j in range(ds):
        v_ref[pl.ds(j, tm, stride=ds), :] = _pack_pair(v[:, j * LANES:(j + 1) * LANES],
                                                       v[:, half + j * LANES:half + (j + 1) * LANES])


def _merge_b(merged, h, mod, gain2, w_out, w_router, *, layer, n_rows, tm, tpb, nlat, nb):
    D = h.shape[1]
    E = w_router.shape[1]
    ds = D // (2 * LANES)
    const = lambda *_: (0, 0)
    wr_hi = w_router.astype(bf16)
    wr_lo = (w_router - wr_hi.astype(f32)).astype(bf16)
    wr_cat = jnp.concatenate([wr_hi, wr_lo], 1)
    return pl.pallas_call(
        functools.partial(_merge_b_body, ds=ds),
        grid=(n_rows // tm,),
        in_specs=[pl.BlockSpec((tm, D), lambda i: (i, 0)),
                  pl.BlockSpec((tm, D), lambda i: (i, 0)),
                  _mod_spec(2, D, tpb, nlat, nb),
                  _mod_spec(4, D, tpb, nlat, nb),
                  _mod_spec(3, D, tpb, nlat, nb),
                  pl.BlockSpec((1, D), const),
                  pl.BlockSpec((1, D, D), lambda i: (layer, 0, 0), pipeline_mode=pl.Buffered(1)),
                  pl.BlockSpec((D, 2 * E), const)],
        out_specs=[pl.BlockSpec((tm, D), lambda i: (i, 0)),
                   pl.BlockSpec((tm * ds, LANES), lambda i: (i, 0)),
                   pl.BlockSpec((tm, E), lambda i: (i, 0))],
        out_shape=[jax.ShapeDtypeStruct((n_rows, D), f32),
                   jax.ShapeDtypeStruct((n_rows * ds, LANES), jnp.uint32),
                   jax.ShapeDtypeStruct((n_rows, E), f32)],
        compiler_params=_cparams(("parallel",)),
        name="merge_b",
    )(merged, h, mod, mod, mod, gain2, w_out, wr_cat)


def _route_body(lg_ref, bias_ref, idx_ref, gate_ref, cnt_ref, carry_scr, *, n_groups):
    i = pl.program_id(0)
    E, tt = lg_ref.shape

    @pl.when(i == 0)
    def _():
        carry_scr[...] = jnp.zeros_like(carry_scr)

    aff = _sigmoid(lg_ref[...])
    sel = aff + bias_ref[...]
    sub = lax.broadcasted_iota(jnp.int32, (GROUP_SIZE, tt), 0)
    neg = -1e30
    best = None
    for g in range(n_groups):
        s = sel[g * GROUP_SIZE:(g + 1) * GROUP_SIZE]
        a = aff[g * GROUP_SIZE:(g + 1) * GROUP_SIZE]
        m1 = jnp.max(s, axis=0, keepdims=True)
        i1 = jnp.min(jnp.where(s == m1, sub, GROUP_SIZE), axis=0, keepdims=True)
        s2 = jnp.where(sub == i1, neg, s)
        m2 = jnp.max(s2, axis=0, keepdims=True)
        i2 = jnp.min(jnp.where(s2 == m2, sub, GROUP_SIZE), axis=0, keepdims=True)
        a1 = jnp.sum(jnp.where(sub == i1, a, 0.0), axis=0, keepdims=True)
        a2 = jnp.sum(jnp.where(sub == i2, a, 0.0), axis=0, keepdims=True)
        cand = (m1 + m2, i1 + g * GROUP_SIZE, i2 + g * GROUP_SIZE, a1, a2)
        if best is None:
            best = cand
        else:
            better = cand[0] > best[0]
            best = tuple(jnp.where(better, cn, bs) for cn, bs in zip(cand, best))
    _, e1, e2, a1, a2 = best
    den = a1 + a2
    gate_ref[0:1, :] = a1 / den
    gate_ref[1:2, :] = a2 / den
    erow = lax.broadcasted_iota(jnp.int32, (E, tt), 0)
    oh1 = erow == e1
    oh2 = erow == e2
    cnt = jnp.where(oh1 | oh2, 1.0, 0.0)
    ti = lax.broadcasted_iota(jnp.int32, (tt, tt), 0)
    tj = lax.broadcasted_iota(jnp.int32, (tt, tt), 1)
    before = jnp.where(ti < tj, 1.0, 0.0).astype(bf16)
    pos = jnp.dot(cnt.astype(bf16), before, preferred_element_type=f32) + carry_scr[...]
    r1 = jnp.sum(jnp.where(oh1, pos, 0.0), axis=0, keepdims=True)
    r2 = jnp.sum(jnp.where(oh2, pos, 0.0), axis=0, keepdims=True)
    idx_ref[0:1, :] = e1
    idx_ref[1:2, :] = e2
    idx_ref[2:3, :] = r1.astype(jnp.int32)
    idx_ref[3:4, :] = r2.astype(jnp.int32)
    carry_scr[...] = carry_scr[...] + jnp.sum(cnt, axis=1, keepdims=True)
    cnt_ref[...] = jnp.broadcast_to(carry_scr[...], cnt_ref.shape)


def _route(logits_t, bias_col, *, tt):
    E, T = logits_t.shape
    return pl.pallas_call(
        functools.partial(_route_body, n_groups=E // GROUP_SIZE),
        grid=(T // tt,),
        in_specs=[pl.BlockSpec((E, tt), lambda i: (0, i)),
                  pl.BlockSpec((E, 1), lambda i: (0, 0))],
        out_specs=[pl.BlockSpec((4, tt), lambda i: (0, i)),
                   pl.BlockSpec((2, tt), lambda i: (0, i)),
                   pl.BlockSpec((E, LANES), lambda i: (0, 0))],
        out_shape=[jax.ShapeDtypeStruct((4, T), jnp.int32),
                   jax.ShapeDtypeStruct((2, T), f32),
                   jax.ShapeDtypeStruct((E, LANES), f32)],
        scratch_shapes=[pltpu.VMEM((E, 1), f32)],
        compiler_params=_cparams(("arbitrary",)),
        name="route",
    )(logits_t, bias_col)


def _row_copy(src, dst, s, t, n, sem):
    def rows(x):
        return pl.ds(x * n if isinstance(x, int) else pl.multiple_of(x * n, 8), n)

    return pltpu.make_async_copy(src.at[rows(s), :], dst.at[rows(t), :], sem)


def _dispatch_body(dest_ref, pad_ref, v_ref, xs_hbm, zero_scr, sem, zsem, *, tt, ds, n_exp, n_blocks, rb):
    i = pl.program_id(0)

    @pl.when(i == 0)
    def _():
        zero_scr[...] = jnp.zeros_like(zero_scr)

        def per_expert(e, n):
            start = pad_ref[0, e]

            def one(r, m):
                _row_copy(zero_scr, xs_hbm, 0, start + r, ds, zsem).start()
                return m + 1

            return lax.fori_loop(0, pad_ref[1, e], one, n)

        total = lax.fori_loop(0, n_exp, per_expert, 0)

        def wait_one(r, m):
            _row_copy(zero_scr, xs_hbm, 0, 0, ds, zsem).wait()
            return m

        lax.fori_loop(0, total, wait_one, 0)
        n_used = pad_ref[2, 0]

        def blk_start(bk, m):
            _row_copy(zero_scr, xs_hbm, 0, bk, rb * ds, zsem).start()
            return m

        lax.fori_loop(n_used, n_blocks, blk_start, 0)

        def blk_wait(bk, m):
            _row_copy(zero_scr, xs_hbm, 0, 0, rb * ds, zsem).wait()
            return m

        lax.fori_loop(n_used, n_blocks, blk_wait, 0)

    def issue(r, m):
        src = v_ref.at[pl.ds(pl.multiple_of(r * ds, ds), ds), :]
        for kk in range(2):
            dst = xs_hbm.at[pl.ds(pl.multiple_of(dest_ref[0, kk, r] * ds, ds), ds), :]
            pltpu.make_async_copy(src, dst, sem).start(priority=kk)
        return m

    lax.fori_loop(0, tt, issue, 0, unroll=8)
    for kk in range(2):
        pltpu.make_async_copy(v_ref, xs_hbm.at[pl.ds(0, tt * ds), :], sem).wait()


def _dispatch(dest3, pad_info, v_ts, *, n_tok, n_blocks, tt, ds):
    n_exp = pad_info.shape[1]
    rb = EXPERT_BLOCK
    return pl.pallas_call(
        functools.partial(_dispatch_body, tt=tt, ds=ds, n_exp=n_exp, n_blocks=n_blocks, rb=rb),
        grid=(n_tok // tt,),
        in_specs=[pl.BlockSpec((1, 2, tt), lambda i: (i, 0, 0), memory_space=pltpu.SMEM),
                  pl.BlockSpec(memory_space=pltpu.SMEM),
                  pl.BlockSpec((tt * ds, LANES), lambda i: (i, 0))],
        out_specs=pl.BlockSpec(memory_space=pl.ANY),
        out_shape=jax.ShapeDtypeStruct((n_blocks * rb * ds, LANES), v_ts.dtype),
        scratch_shapes=[pltpu.VMEM((rb * ds, LANES), v_ts.dtype), pltpu.SemaphoreType.DMA(()),
                        pltpu.SemaphoreType.DMA(())],
        compiler_params=_cparams(("arbitrary",)),
        name="dispatch",
    )(dest3, pad_info, v_ts)


def _expert_body(blk0_ref, nblk_ref, nu_ref, x_hbm, wg_ref, wu_ref, wd_ref, y_hbm, x_buf, y_buf, x_s, sem_x, sem_y,
                 *, ds, rb, n_blocks):
    e = pl.program_id(0)
    b0 = blk0_ref[e]
    nb = nblk_ref[e]
    n_used = nu_ref[0]
    rows = rb * ds

    def x_copy(g, slot):
        return pltpu.make_async_copy(x_hbm.at[pl.ds(pl.multiple_of(g * rows, rows), rows), :],
                                     x_buf.at[slot], sem_x.at[slot])

    def y_copy(g, slot):
        return pltpu.make_async_copy(y_buf.at[slot],
                                     y_hbm.at[pl.ds(pl.multiple_of(g * rows, rows), rows), :], sem_y.at[slot])

    @pl.when(jnp.logical_and(e == 0, n_used > 0))
    def _():
        x_copy(0, 0).start()

    def block(i, carry):
        g = b0 + i
        slot = g % 2
        x_copy(g, slot).wait()

        @pl.when(g + 1 < n_used)
        def _():
            x_copy(g + 1, 1 - slot).start()

        @pl.when(g >= 2)
        def _():
            y_copy(g - 2, slot).wait()

        half = ds * LANES
        for j in range(ds):
            lo, hi = _unpack_pair(x_buf[slot, pl.ds(j, rb, stride=ds), :])
            x_s[:, j * LANES:(j + 1) * LANES] = lo.astype(bf16)
            x_s[:, half + j * LANES:half + (j + 1) * LANES] = hi.astype(bf16)
        x = x_s[...]
        hg = jnp.dot(x, wg_ref[0, 0].astype(bf16), preferred_element_type=f32)
        hu = jnp.dot(x, wu_ref[0, 0].astype(bf16), preferred_element_type=f32)
        a = (hg * _sigmoid(hg) * hu).astype(bf16)
        y = jnp.dot(a, wd_ref[0, 0].astype(bf16), preferred_element_type=f32)
        for j in range(ds):
            y_buf[slot, pl.ds(j, rb, stride=ds), :] = _pack_pair(y[:, j * LANES:(j + 1) * LANES],
                                                                 y[:, half + j * LANES:half + (j + 1) * LANES])
        y_copy(g, slot).start()
        return carry

    lax.fori_loop(0, nb, block, 0)

    @pl.when(e == pl.num_programs(0) - 1)
    def _():
        @pl.when(n_used >= 2)
        def _():
            y_copy(n_used - 2, n_used % 2).wait()

        @pl.when(n_used >= 1)
        def _():
            y_copy(n_used - 1, (n_used - 1) % 2).wait()

        y_buf[0] = jnp.zeros(y_buf.shape[1:], y_buf.dtype)

        def start(bk, m):
            _row_copy(y_buf.at[0], y_hbm, 0, bk, rows, sem_y.at[0]).start()
            return m

        lax.fori_loop(n_used, n_blocks, start, 0)

        def wait(bk, m):
            _row_copy(y_buf.at[0], y_hbm, 0, 0, rows, sem_y.at[0]).wait()
            return m

        lax.fori_loop(n_used, n_blocks, wait, 0)


def _experts(blk0, nblk, n_used, xs_ts, w_g, w_u, w_d, *, layer, ds):
    _, E, D, De = w_g.shape
    rb = EXPERT_BLOCK
    n_blocks = xs_ts.shape[0] // (rb * ds)
    grid_spec = pltpu.PrefetchScalarGridSpec(
        num_scalar_prefetch=3,
        grid=(E,),
        in_specs=[pl.BlockSpec(memory_space=pl.ANY),
                  pl.BlockSpec((1, 1, D, De), lambda e, *_: (layer, e, 0, 0)),
                  pl.BlockSpec((1, 1, D, De), lambda e, *_: (layer, e, 0, 0)),
                  pl.BlockSpec((1, 1, De, D), lambda e, *_: (layer, e, 0, 0))],
        out_specs=pl.BlockSpec(memory_space=pl.ANY),
        scratch_shapes=[pltpu.VMEM((2, rb * ds, LANES), xs_ts.dtype), pltpu.VMEM((2, rb * ds, LANES), xs_ts.dtype),
                        pltpu.VMEM((rb, D), bf16), pltpu.SemaphoreType.DMA((2,)), pltpu.SemaphoreType.DMA((2,))],
    )
    return pl.pallas_call(
        functools.partial(_expert_body, ds=ds, rb=rb, n_blocks=n_blocks),
        grid_spec=grid_spec,
        out_shape=jax.ShapeDtypeStruct(xs_ts.shape, xs_ts.dtype),
        compiler_params=_cparams(("arbitrary",)),
        name="experts",
    )(blk0, nblk, n_used, xs_ts, w_g, w_u, w_d)


def _combine_body(dest_ref, dnext_ref, gate_ref, h_ref, gt_ref, fg_ref, y_hbm, o_ref, y_scr, sem, *, tt, ds, final):
    i = pl.program_id(0)
    slot = i % 2

    def gather(d_ref, to_slot):
        def body(r, m):
            for kk in range(2):
                _row_copy(y_hbm, y_scr.at[to_slot, kk], d_ref[0, kk, r], r, ds, sem.at[to_slot]).start(priority=kk)
            return m

        lax.fori_loop(0, tt, body, 0, unroll=8)

    @pl.when(i == 0)
    def _():
        gather(dest_ref, 0)

    @pl.when(i + 1 < pl.num_programs(0))
    def _():
        gather(dnext_ref, 1 - slot)

    for kk in range(2):
        pltpu.make_async_copy(y_hbm.at[pl.ds(0, tt * ds), :], y_scr.at[slot, kk], sem.at[slot]).wait()
    gt = gt_ref[0]
    g0 = gate_ref[:, 0:1]
    g1 = gate_ref[:, 1:2]
    half = ds * LANES
    for j in range(ds):
        lo0, hi0 = _unpack_pair(y_scr[slot, 0, pl.ds(j, tt, stride=ds), :])
        lo1, hi1 = _unpack_pair(y_scr[slot, 1, pl.ds(j, tt, stride=ds), :])
        for base, y0, y1 in ((0, lo0, lo1), (half, hi0, hi1)):
            sl = slice(base + j * LANES, base + (j + 1) * LANES)
            o_ref[:, sl] = h_ref[:, sl] + gt[:, sl] * (g0 * y0 + g1 * y1)
    if final:
        x = o_ref[...]
        o_ref[...] = x * lax.rsqrt(jnp.mean(x * x, -1, keepdims=True) + EPS) * fg_ref[...]


def _combine(dest3, gates_t, h1, mod, final_g, y_ts, *, n_tok, tt, ds, tpb, nlat, nb, final):
    D = h1.shape[1]
    nt = n_tok // tt
    return pl.pallas_call(
        functools.partial(_combine_body, tt=tt, ds=ds, final=final),
        grid=(nt,),
        in_specs=[pl.BlockSpec((1, 2, tt), lambda i: (i, 0, 0), memory_space=pltpu.SMEM),
                  pl.BlockSpec((1, 2, tt), lambda i: (jnp.minimum(i + 1, nt - 1), 0, 0), memory_space=pltpu.SMEM),
                  pl.BlockSpec((tt, 2), lambda i: (i, 0)),
                  pl.BlockSpec((tt, D), lambda i: (i, 0)),
                  _mod_spec(5, D, tpb, nlat, nb),
                  pl.BlockSpec((1, D), lambda i: (0, 0)),
                  pl.BlockSpec(memory_space=pl.ANY)],
        out_specs=pl.BlockSpec((tt, D), lambda i: (i, 0)),
        out_shape=jax.ShapeDtypeStruct((n_tok, D), f32),
        scratch_shapes=[pltpu.VMEM((2, 2, tt * ds, LANES), y_ts.dtype), pltpu.SemaphoreType.DMA((2,))],
        compiler_params=_cparams(("arbitrary",)),
        name="combine",
    )(dest3, dest3, gates_t, h1, mod, final_g, y_ts)


def _moe(v_ts, logits, h1, mod, w_router_bias, w_g, w_u, w_d, final_g, *, layer, n_tok, tt, ds, tpb, nlat, nb,
         final):
    E = w_g.shape[1]
    route_tt = next(t for t in (512, 256, LANES) if n_tok % t == 0)
    idx, gates, counts = _route(logits[:n_tok].T, w_router_bias.reshape(E, 1), tt=route_tt)
    counts = counts[:, 0].astype(jnp.int32)
    padded = (counts + EXPERT_BLOCK - 1) // EXPERT_BLOCK * EXPERT_BLOCK
    pad_end = jnp.cumsum(padded)
    pad_start = pad_end - padded
    n_blocks = (n_tok * 2 + EXPERT_BLOCK - 1) // EXPERT_BLOCK + E
    n_used = (pad_end[-1] // EXPERT_BLOCK).astype(jnp.int32).reshape(1)
    e_sel = idx[0:2][..., None] == jnp.arange(E, dtype=jnp.int32)
    dest = jnp.sum(jnp.where(e_sel, pad_start.astype(jnp.int32), 0), -1) + idx[2:4]
    dest3 = dest.reshape(2, n_tok // tt, tt).transpose(1, 0, 2)
    pad_info = jnp.stack([pad_start + counts, padded - counts, jnp.broadcast_to(n_used, (E,))]).astype(jnp.int32)
    xs_ts = _dispatch(dest3, pad_info, v_ts, n_tok=n_tok, n_blocks=n_blocks, tt=tt, ds=ds)
    y_ts = _experts((pad_start // EXPERT_BLOCK).astype(jnp.int32), (padded // EXPERT_BLOCK).astype(jnp.int32), n_used,
                    xs_ts, w_g, w_u, w_d, layer=layer, ds=ds)
    return _combine(dest3, gates.T, h1, mod, final_g, y_ts, n_tok=n_tok, tt=tt, ds=ds, tpb=tpb, nlat=nlat,
                    nb=nb, final=final)


def kernel(x, c, ctx, c_ctx, w_mod, b_mod, norm1_g, norm2_g, w_in, conv_w, a_log, dt_bias, onorm_g, w_pool, pool_scale, w_up_dn, w_up_pool, w_out, w_router, router_bias, w_gate_e, w_up_e, w_down_e, final_g):
    B, S, D = x.shape
    C = ctx.shape[1]
    L = w_mod.shape[0]
    H = a_log.shape[-1]
    dk = onorm_g.shape[-1]
    assert dk == LANES
    dn_w = H * dk
    _, ng, gw, _ = w_pool.shape
    pool_w = ng * gw
    ds = D // (2 * LANES)
    n_lat = B * S
    T = n_lat + B * C
    tm = 256
    assert S % tm == 0 and (B * C) % tm == 0 and C % CHUNK == 0 and S % (4 * GRID_W) == 0
    tpb = S // tm
    nlat_tiles = n_lat // tm
    tm_in = next(t for t in (1024, 512, 256) if S % t == 0 and (B * C) % t == 0)
    tm_mg = next(t for t in (512, 256) if S % t == 0 and (B * C) % t == 0)
    tr = min(256, C)
    G = min(H, 16)

    c8 = jnp.zeros((8, D), f32).at[:B].set(c).at[B].set(c_ctx)
    mod = _mod_table(c8, w_mod, b_mod)

    qkv_c = 3 * dn_w
    scan_c = qkv_c + 4 * H
    og_off, pool_off = scan_c, scan_c + dn_w
    mg_off = pool_off + pool_w
    assert (4 * dn_w) % D == 0 and (4 * dn_w + 2 * D) % pool_w == 0
    pool_off_blocks = (4 * dn_w + 2 * D) // pool_w

    h = jnp.concatenate([x.reshape(n_lat, D), ctx.reshape(B * C, D)], 0)
    for l in range(L):
        last = l == L - 1
        mod_l = mod[l].reshape(8, 1, mod.shape[-1])
        wl = w_in[l]
        w_z = jnp.concatenate([wl[:, :qkv_c], wl[:, og_off:pool_off], wl[:, mg_off:], wl[:, pool_off:mg_off]],
                              1).astype(bf16)
        w_ba = jnp.pad(wl[:, qkv_c:scan_c], ((0, 0), (0, LANES - 4 * H))).astype(bf16)
        tn = 1024 if w_z.shape[1] % 1024 == 0 else w_z.shape[1]
        z, ba = _in_proj(h, mod_l, 0, norm1_g[l].reshape(1, D), w_z, w_ba, tm=tm_in, tn=tn, tpb=S // tm_in,
                         nlat=n_lat // tm_in, nb=B)
        qkv = _conv_prep(z, conv_w[l], tr=tr, tc=min(1024, dn_w), S=S, C=C, nlat_rows=n_lat, dn_w=dn_w, dk=dk)
        o_dn = _deltanet(qkv, ba, a_log[l], dt_bias[l], B=B, S=S, C=C, H=H, G=G)
        pooled_lat, pooled_ctx = _pool(z, B=B, S=S, C=C, pool_off_blocks=pool_off_blocks, pool_w=pool_w, gw=gw)
        n_rows = n_lat if last else T
        merged = _merge_a(o_dn, z, pooled_lat, pooled_ctx, onorm_g[l].reshape(1, dk), w_pool,
                          pool_scale[l].reshape(1, pool_w), w_up_dn, w_up_pool,
                          layer=l, n_rows=n_rows, tm=tm, H=H, D=D)
        h1, v_ts, logits = _merge_b(merged, h, mod_l, norm2_g[l].reshape(1, D), w_out, w_router,
                                    layer=l, n_rows=n_rows, tm=tm_mg, tpb=S // tm_mg, nlat=n_lat // tm_mg, nb=B)
        h = _moe(v_ts, logits, h1, mod_l, router_bias, w_gate_e, w_up_e, w_down_e,
                 final_g.reshape(1, D), layer=l, n_tok=n_rows, tt=tm, ds=ds, tpb=tpb, nlat=nlat_tiles, nb=B, final=last)
    return h.reshape(B, S, D)
```

```python
import functools

import jax
import jax.numpy as jnp
from jax import lax
from jax.experimental import pallas as pl
from jax.experimental.pallas import tpu as pltpu

f32 = jnp.float32
bf16 = jnp.bfloat16
HIGHEST = lax.Precision.HIGHEST

EPS = 1e-6
LANES = 128
GRID_W = 64
CHUNK = 64
POOL_WINDOWS = (2, 4, 8, 16)
GROUP_SIZE = 8
EXPERT_BLOCK = 256
VMEM_LIMIT = 56 * 1024 * 1024


def _cparams(sem, vmem=VMEM_LIMIT):
    return pltpu.CompilerParams(dimension_semantics=sem, vmem_limit_bytes=vmem)


def _sigmoid(x):
    return 1.0 / (1.0 + jnp.exp(-x))


def _softplus(x):
    return jnp.maximum(x, 0.0) + jnp.log(1.0 + jnp.exp(-jnp.abs(x)))


def _pack_pair(lo, hi):
    def rounded(x):
        u = lax.bitcast_convert_type(x, jnp.uint32)
        return u + jnp.uint32(0x7FFF) + ((u >> 16) & jnp.uint32(1))

    return (rounded(lo) >> 16) | (rounded(hi) & jnp.uint32(0xFFFF0000))


def _unpack_pair(w):
    lo = lax.bitcast_convert_type(w << 16, f32)
    hi = lax.bitcast_convert_type(w & jnp.uint32(0xFFFF0000), f32)
    return lo, hi


def _mod_spec(sec, D, tiles_per_batch, n_lat_tiles, n_batch):
    def index(i, *_):
        return (jnp.where(i < n_lat_tiles, i // tiles_per_batch, n_batch), 0, sec)

    return pl.BlockSpec((1, 1, D), index)


def _mod_body(c_ref, w_ref, b_ref, o_ref):
    c = c_ref[...]
    s = c * _sigmoid(c)
    o_ref[0] = jnp.dot(s, w_ref[0], preferred_element_type=f32, precision=HIGHEST) + b_ref[0]


def _mod_table(c8, w_mod, b_mod):
    L, D, N = w_mod.shape
    tn = next(t for t in (1024, 512, 256, LANES) if N % t == 0)
    return pl.pallas_call(
        _mod_body,
        grid=(L, N // tn),
        in_specs=[pl.BlockSpec((8, D), lambda l, j: (0, 0)),
                  pl.BlockSpec((1, D, tn), lambda l, j: (l, 0, j)),
                  pl.BlockSpec((1, 1, tn), lambda l, j: (l, 0, j))],
        out_specs=pl.BlockSpec((1, 8, tn), lambda l, j: (l, 0, j)),
        out_shape=jax.ShapeDtypeStruct((L, 8, N), f32),
        compiler_params=_cparams(("parallel", "parallel")),
        name="mod_table",
    )(c8, w_mod, b_mod.reshape(L, 1, N))


def _inproj_body(h_ref, sc_ref, sh_ref, g_ref, w_ref, wba_ref, z_ref, ba_ref, u_scr):
    j = pl.program_id(1)

    @pl.when(j == 0)
    def _():
        x = h_ref[...]
        y = x * lax.rsqrt(jnp.mean(x * x, -1, keepdims=True) + EPS) * g_ref[...]
        u = (y * (1.0 + sc_ref[0]) + sh_ref[0]).astype(bf16)
        u_scr[...] = u
        ba_ref[...] = jnp.dot(u, wba_ref[...], preferred_element_type=f32)

    z_ref[...] = jnp.dot(u_scr[...], w_ref[...], preferred_element_type=f32).astype(z_ref.dtype)


def _in_proj(h, mod, sec, gain, w_z, w_ba, *, tm, tn, tpb, nlat, nb):
    T, D = h.shape
    Nz = w_z.shape[1]
    return pl.pallas_call(
        _inproj_body,
        grid=(T // tm, Nz // tn),
        in_specs=[pl.BlockSpec((tm, D), lambda i, j: (i, 0)),
                  _mod_spec(sec + 1, D, tpb, nlat, nb),
                  _mod_spec(sec, D, tpb, nlat, nb),
                  pl.BlockSpec((1, D), lambda i, j: (0, 0)),
                  pl.BlockSpec((D, tn), lambda i, j: (0, j)),
                  pl.BlockSpec((D, LANES), lambda i, j: (0, 0))],
        out_specs=[pl.BlockSpec((tm, tn), lambda i, j: (i, j)),
                   pl.BlockSpec((tm, LANES), lambda i, j: (i, 0))],
        out_shape=[jax.ShapeDtypeStruct((T, Nz), bf16), jax.ShapeDtypeStruct((T, LANES), f32)],
        scratch_shapes=[pltpu.VMEM((tm, D), bf16)],
        compiler_params=_cparams(("parallel", "arbitrary")),
        name="in_proj",
    )(h, mod, mod, gain, w_z, w_ba)


def _conv_body(x_ref, xp_ref, xn_ref, w_ref, sh_ref, o_ref, *, tr, tps_lat, tps_ctx, nlat, qscale, tiles_per_sec,
               halo):
    i = pl.program_id(0)
    il =jnp.where(i < nlat, i % tps_lat, (i - nlat) % tps_ctx)
    nt = jnp.where(i < nlat, tps_lat, tps_ctx)
    first = il == 0
    last = il == nt - 1
    xb = x_ref[...]
    w = w_ref[...]
    tc_ = xb.shape[1]
    ys = jnp.dot(sh_ref[...], xb, preferred_element_type=f32).reshape(tr // 8, 32, tc_)
    acc = (ys[:, 0:8] * w[0:1] + ys[:, 8:16] * w[1:2] + ys[:, 16:24] * w[3:4] + ys[:, 24:32] * w[4:5])
    acc = acc.reshape(tr, tc_) + xb.astype(f32) * w[2:3]
    p = jnp.where(first, 0.0, xp_ref[...].astype(f32))[halo - 8:]
    n = jnp.where(last, 0.0, xn_ref[...].astype(f32))[:8]
    r8 = lax.broadcasted_iota(jnp.int32, (8, 1), 0)
    top = jnp.where(r8 == 0, p[6:7] * w[0:1] + p[7:8] * w[1:2], jnp.where(r8 == 1, p[7:8] * w[0:1], 0.0))
    bot = jnp.where(r8 == 7, n[0:1] * w[3:4] + n[1:2] * w[4:5], jnp.where(r8 == 6, n[0:1] * w[4:5], 0.0))
    acc = jnp.concatenate([acc[:8] + top, acc[8:tr - 8], acc[tr - 8:] + bot], axis=0)
    y = acc * _sigmoid(acc)
    sec = pl.program_id(1) // tiles_per_sec

    @pl.when(sec == 2)
    def _():
        o_ref[...] = y.astype(o_ref.dtype)

    @pl.when(sec != 2)
    def _():
        qk_scale = jnp.where(sec == 0, qscale, 1.0)
        for s in range(acc.shape[1] // LANES):
            ys = y[:, s * LANES:(s + 1) * LANES]
            rs = lax.rsqrt(jnp.sum(ys * ys, -1, keepdims=True) + EPS) * qk_scale
            o_ref[:, s * LANES:(s + 1) * LANES] = (ys * rs).astype(o_ref.dtype)


def _conv_prep(z, conv_w, *, tr, tc, S, C, nlat_rows, dn_w, dk):
    T = z.shape[0]
    halo = 16
    hb = tr // halo
    nh = T // halo
    body = functools.partial(_conv_body, tr=tr, tps_lat=S // tr, tps_ctx=C // tr, nlat=nlat_rows // tr,
                             qscale=float(dk) ** -0.5, tiles_per_sec=dn_w // tc, halo=halo)
    assert conv_w.shape[0] == 5
    ri = jnp.arange(tr)[:, None]
    ci = jnp.arange(tr)[None, :]
    shifts = jnp.stack([(ci == ri + off).astype(bf16) for off in (-2, -1, 1, 2)], 0)
    shifts = shifts.reshape(4, tr // 8, 8, tr).transpose(1, 0, 2, 3).reshape(4 * tr, tr)
    return pl.pallas_call(
        body,
        grid=(T // tr, 3 * dn_w // tc),
        in_specs=[pl.BlockSpec((tr, tc), lambda i, j: (i, j)),
                  pl.BlockSpec((halo, tc), lambda i, j: (jnp.maximum(i * hb - 1, 0), j)),
                  pl.BlockSpec((halo, tc), lambda i, j: (jnp.minimum((i + 1) * hb, nh - 1), j)),
                  pl.BlockSpec((conv_w.shape[0], tc), lambda i, j: (0, j)),
                  pl.BlockSpec((4 * tr, tr), lambda i, j: (0, 0))],
        out_specs=pl.BlockSpec((tr, tc), lambda i, j: (i, j)),
        out_shape=jax.ShapeDtypeStruct((T, 3 * dn_w), bf16),
        compiler_params=_cparams(("parallel", "parallel")),
        name="conv_prep",
    )(z, z, z, conv_w, shifts)


def _mm(x, y):
    return jnp.dot(x.astype(bf16), y.astype(bf16), preferred_element_type=f32)


def _unit_tri_inverse(a_list, ri, ci, interleave=None):
    b16 = (ri >> 4) == (ci >> 4)
    b32 = (ri >> 5) == (ci >> 5)
    eye = jnp.where(ri == ci, 1.0, 0.0)
    n = a_list[0].shape[0]

    def mmb(x, y):
        return jnp.dot(x, y, preferred_element_type=f32)

    a16 = [jnp.where(b16, a, 0.0) for a in a_list]
    t = [eye - x for x in a16]
    pb = [x.astype(bf16) for x in a16]
    pb = [mmb(x, x).astype(bf16) for x in pb]
    for step in range(2):
        tp = [mmb(jnp.concatenate([ti.astype(bf16), pi], axis=0), pi) for ti, pi in zip(t, pb)]
        t = [ti + x[:n] for ti, x in zip(t, tp)]
        pb = [x[n:].astype(bf16) for x in tp]
        if step == 0 and interleave is not None:
            interleave()
    t = [ti + mmb(ti.astype(bf16), pi) for ti, pi in zip(t, pb)]
    for off in ([jnp.where(b32, a, 0.0) - x for a, x in zip(a_list, a16)],
                [jnp.where(b32, 0.0, a) for a in a_list]):
        tb = [ti.astype(bf16) for ti in t]
        ta = [mmb(ti, ai.astype(bf16)).astype(bf16) for ti, ai in zip(tb, off)]
        t = [ti - mmb(tai, tbi) for ti, tai, tbi in zip(t, ta, tb)]
    return t


def _dn_body(q_ref, k_ref, v_ref, ba_ref, al_ref, dt_ref, o_ref, s_scr, u_scr, wq_scr, qk_scr, eg_scr, *, G):
    d = pl.program_id(1)
    c = pl.program_id(3)
    dk = LANES
    hs = range(G)

    @pl.when(c == 0)
    def _():
        for ref in (s_scr, u_scr, wq_scr, qk_scr, eg_scr):
            ref[...] = jnp.zeros_like(ref)

    ba = ba_ref[0]
    beta_t = _sigmoid(ba)
    g_t = -jnp.exp(al_ref[0]) * _softplus(ba + dt_ref[0])
    ri = lax.broadcasted_iota(jnp.int32, (CHUNK, CHUNK), 0)
    ci = lax.broadcasted_iota(jnp.int32, (CHUNK, CHUNK), 1)
    fwd = d == 0
    later = jnp.where(fwd, ri, ci)
    earlier = jnp.where(fwd, ci, ri)
    incl = later >= earlier
    strict = later > earlier
    tri = jnp.where(incl, 1.0, 0.0)
    gc = jnp.dot(tri, g_t, preferred_element_type=f32, precision=HIGHEST)
    gct = gc.T
    gtot = jnp.sum(g_t, axis=0, keepdims=True)

    def pick(x, col):
        return jnp.where(fwd, x[:, col:col + 1], x[:, G + col:G + col + 1])

    sl = [slice(h * dk, (h + 1) * dk) for h in hs]
    beta_c = [pick(beta_t, h) for h in hs]
    gc_c = [pick(gc, 2 * G + h) for h in hs]
    gc_r = [jnp.where(fwd, gct[2 * G + h:2 * G + h + 1], gct[3 * G + h:3 * G + h + 1]) for h in hs]
    gt = [pick(gtot, 2 * G + h) for h in hs]
    q = [q_ref[:, sl[h]].astype(f32) for h in hs]
    k = [k_ref[:, sl[h]].astype(f32) for h in hs]
    kb = [k[h] * beta_c[h] for h in hs]
    eg = [jnp.exp(gc_c[h]) for h in hs]
    dec = [jnp.where(incl, jnp.exp(jnp.where(incl, gc_c[h] - gc_r[h], 0.0)), 0.0) for h in hs]
    kq = [lax.dot_general(jnp.concatenate([kb[h], q[h]], axis=0).astype(bf16), k_ref[:, sl[h]],
                          (((1,), (1,)), ((), ())), preferred_element_type=f32) for h in hs]
    s = [s_scr[h] for h in hs]
    r = [jnp.dot(wq_scr[h], s[h].astype(bf16), preferred_element_type=f32) for h in hs]
    a = [jnp.where(strict, kq[h][:CHUNK] * dec[h], 0.0) for h in hs]
    qk = [(kq[h][CHUNK:] * dec[h]).astype(bf16) for h in hs]

    def recurrence_tail():
        vnb = [(u_scr[h] - r[h][:CHUNK]).astype(bf16) for h in hs]
        ou = [jnp.dot(qk_scr[h], vnb[h], preferred_element_type=f32) for h in hs]
        for h in hs:
            s_scr[h] = s[h] * eg_scr[h] + ou[h][CHUNK:]
            o_ref[0, :, sl[h]] = (r[h][CHUNK:] + ou[h][:CHUNK]).astype(o_ref.dtype)

    t = _unit_tri_inverse(a, ri, ci, interleave=recurrence_tail)
    rhs = [jnp.concatenate([v_ref[:, sl[h]].astype(f32) * beta_c[h], kb[h] * eg[h]], axis=1) for h in hs]
    sol = [_mm(t[h], rhs[h]) for h in hs]
    for h in hs:
        u_scr[h] = sol[h][:, :dk]
        wq_scr[h] = jnp.concatenate([sol[h][:, dk:], q[h] * eg[h]], axis=0).astype(bf16)
        kd_t = (k[h] * jnp.exp(gt[h] - gc_c[h])).T.astype(bf16)
        qk_scr[h] = jnp.concatenate([qk[h], kd_t], axis=0)
        eg_scr[h] = jnp.broadcast_to(jnp.exp(gt[h]), (1, dk))


def _deltanet(qkv, ba, a_log, dt_bias, *, B, S, C, H, G):
    T = qkv.shape[0]
    dn_w = H * LANES
    ncl = S // CHUNK
    ncc = C // CHUNK
    nc = ncl + ncc
    hgs = H // G

    def rowblk(b, d, c):
        cc = jnp.where(d == 0, c, ncc - 1 - c)
        cl = jnp.where(d == 0, c - ncc, nc - 1 - c)
        return jnp.where(c < ncc, B * ncl + b * ncc + cc, b * ncl + cl)

    def grouped(t):
        r = t.shape[0]
        t = t.reshape(r, 4, hgs, G).transpose(2, 0, 1, 3).reshape(hgs, r, 4 * G)
        return jnp.pad(t, ((0, 0), (0, 0), (0, LANES - 4 * G)))

    zeros = jnp.zeros((1, 2 * H), f32)
    ba_g = grouped(ba[:, :4 * H])
    al_g = grouped(jnp.concatenate([zeros, a_log.reshape(1, 2 * H)], 1))
    dt_g = grouped(jnp.concatenate([zeros, dt_bias.reshape(1, 2 * H)], 1))
    body = functools.partial(_dn_body, G=G)

    def blk_in(b, d, c):
        return rowblk(b, d, jnp.minimum(c, nc - 1))

    def blk_out(b, d, c):
        return rowblk(b, d, jnp.maximum(c - 1, 0))

    return pl.pallas_call(
        body,
        grid=(B, 2, hgs, nc + 1),
        in_specs=[pl.BlockSpec((CHUNK, G * LANES), lambda b, d, g, c: (blk_in(b, d, c), g)),
                  pl.BlockSpec((CHUNK, G * LANES), lambda b, d, g, c: (blk_in(b, d, c), hgs + g)),
                  pl.BlockSpec((CHUNK, G * LANES), lambda b, d, g, c: (blk_in(b, d, c), 2 * hgs + g)),
                  pl.BlockSpec((1, CHUNK, LANES), lambda b, d, g, c: (g, blk_in(b, d, c), 0)),
                  pl.BlockSpec((1, 1, LANES), lambda b, d, g, c: (g, 0, 0)),
                  pl.BlockSpec((1, 1, LANES), lambda b, d, g, c: (g, 0, 0))],
        out_specs=pl.BlockSpec((1, CHUNK, G * LANES), lambda b, d, g, c: (d, blk_out(b, d, c), g)),
        out_shape=jax.ShapeDtypeStruct((2, T, dn_w), bf16),
        scratch_shapes=[pltpu.VMEM((G, LANES, LANES), f32),
                        pltpu.VMEM((G, CHUNK, LANES), f32),
                        pltpu.VMEM((G, 2 * CHUNK, LANES), bf16),
                        pltpu.VMEM((G, CHUNK + LANES, CHUNK), bf16),
                        pltpu.VMEM((G, 1, LANES), f32)],
        compiler_params=_cparams(("parallel", "parallel", "parallel", "arbitrary")),
        name="deltanet",
    )(qkv, qkv, qkv, ba_g, al_g, dt_g)


def _pool_ctx_body(x_ref, o_ref, *, gw):
    C = x_ref.shape[0]
    ri = lax.broadcasted_iota(jnp.int32, (C, C), 0)
    ci = lax.broadcasted_iota(jnp.int32, (C, C), 1)
    t = lax.broadcasted_iota(jnp.int32, (C, 1), 0)
    for gi, w in enumerate(POOL_WINDOWS):
        sl = slice(gi * gw, (gi + 1) * gw)
        band = jnp.where((ci >= ri - w // 2) & (ci < ri + w - w // 2), 1.0, 0.0).astype(bf16)
        cnt = (jnp.minimum(t + w - w // 2, C) - jnp.maximum(t - w // 2, 0)).astype(f32)
        x = x_ref[:, sl]
        box = jnp.dot(band, x, preferred_element_type=f32)
        o_ref[:, sl] = (box / cnt - x.astype(f32)).astype(o_ref.dtype)


def _pool_lat_body(x_ref, o_ref, y_scr):
    g = pl.program_id(1)
    S = x_ref.shape[0]
    rows = S // GRID_W
    sb = 4 * GRID_W
    ri = lax.broadcasted_iota(jnp.int32, (sb, sb), 0)
    ci = lax.broadcasted_iota(jnp.int32, (sb, sb), 1)
    cr = ri & (GRID_W - 1)
    cc = ci & (GRID_W - 1)
    same = (ri >> 6) == (ci >> 6)
    col = lax.broadcasted_iota(jnp.int32, (GRID_W, 1), 0)
    for gi, w in enumerate(POOL_WINDOWS):
        @pl.when(g == gi)
        def _(w=w):
            band = jnp.where(same & (cc >= cr - w // 2) & (cc < cr + w - w // 2), 1.0, 0.0).astype(bf16)
            for s in range(S // sb):
                y_scr[s * sb:(s + 1) * sb, :] = jnp.dot(band, x_ref[s * sb:(s + 1) * sb, :],
                                                        preferred_element_type=f32)
            ncol = (jnp.minimum(col + w - w // 2, GRID_W) - jnp.maximum(col - w // 2, 0)).astype(f32)
            for r in range(rows):
                lo = max(r - w // 2, 0)
                hi = min(r + w - w // 2, rows)
                acc = y_scr[lo * GRID_W:(lo + 1) * GRID_W, :]
                for rr in range(lo + 1, hi):
                    acc = acc + y_scr[rr * GRID_W:(rr + 1) * GRID_W, :]
                cnt = ncol * float(hi - lo)
                xs = x_ref[r * GRID_W:(r + 1) * GRID_W, :].astype(f32)
                o_ref[r * GRID_W:(r + 1) * GRID_W, :] = (acc / cnt - xs).astype(o_ref.dtype)


def _pool(z, *, B, S, C, pool_off_blocks, pool_w, gw):
    nlat_c = B * S // C
    ng = pool_w // gw
    assert ng == len(POOL_WINDOWS)
    pooled_ctx = pl.pallas_call(
        functools.partial(_pool_ctx_body, gw=gw),
        grid=(B,),
        in_specs=[pl.BlockSpec((C, pool_w), lambda b: (nlat_c + b, pool_off_blocks))],
        out_specs=pl.BlockSpec((C, pool_w), lambda b: (b, 0)),
        out_shape=jax.ShapeDtypeStruct((B * C, pool_w), bf16),
        compiler_params=_cparams(("parallel",)),
        name="pool_ctx",
    )(z)
    off = pool_off_blocks * ng
    pooled_lat = pl.pallas_call(
        _pool_lat_body,
        grid=(B, ng),
        in_specs=[pl.BlockSpec((S, gw), lambda b, g: (b, off + g))],
        out_specs=pl.BlockSpec((S, gw), lambda b, g: (b, g)),
        out_shape=jax.ShapeDtypeStruct((B * S, pool_w), bf16),
        scratch_shapes=[pltpu.VMEM((S, gw), f32)],
        compiler_params=_cparams(("parallel", "parallel")),
        name="pool_lat",
    )(z)
    return pooled_lat, pooled_ctx


def _merge_a_body(o_ref, og_ref, m1_ref, m2_ref, pl_ref, pc_ref, ong_ref, wpool_ref, ps_ref, wud_ref, wup_ref,
                  out_ref, a_scr, p_scr, *, H, gw, nlat):
    is_lat = pl.program_id(0) < nlat
    for h in range(H):
        sl = slice(h * LANES, (h + 1) * LANES)
        o = o_ref[0, :, sl].astype(f32) + o_ref[1, :, sl].astype(f32)
        og = og_ref[:, sl].astype(f32)
        y = o * lax.rsqrt(jnp.mean(o * o, -1, keepdims=True) + EPS) * ong_ref[...] * (og * _sigmoid(og))
        a_scr[:, sl] = y.astype(bf16)
    y_dn = jnp.dot(a_scr[...], wud_ref[0].astype(bf16), preferred_element_type=f32)
    for g in range(wpool_ref.shape[1]):
        sl = slice(g * gw, (g + 1) * gw)
        pooled = jnp.where(is_lat, pl_ref[:, sl], pc_ref[:, sl])
        yp = jnp.dot(pooled, wpool_ref[0, g].astype(bf16), preferred_element_type=f32) * ps_ref[:, sl]
        p_scr[:, sl] = yp.astype(bf16)
    y_pool = jnp.dot(p_scr[...], wup_ref[0].astype(bf16), preferred_element_type=f32)
    g1 = _sigmoid(m1_ref[...].astype(f32))
    g2 = _sigmoid(m2_ref[...].astype(f32))
    out_ref[...] = (g1 * y_dn + g2 * y_pool).astype(out_ref.dtype)


def _merge_a(o_dn, z, pooled_lat, pooled_ctx, onorm_g, w_pool, pool_scale, w_up_dn, w_up_pool, *, layer, n_rows, tm,
             H, D):
    dn_w = H * LANES
    pool_w = pooled_lat.shape[1]
    nlat = pooled_lat.shape[0] // tm
    _, ng, gw, _ = w_pool.shape
    once = pl.Buffered(1)
    og_blk = 3
    m_blk = 4 * dn_w // D
    const = lambda *_: (0, 0)
    return pl.pallas_call(
        functools.partial(_merge_a_body, H=H, gw=gw, nlat=nlat),
        grid=(n_rows // tm,),
        in_specs=[pl.BlockSpec((2, tm, dn_w), lambda i: (0, i, 0)),
                  pl.BlockSpec((tm, dn_w), lambda i: (i, og_blk)),
                  pl.BlockSpec((tm, D), lambda i: (i, m_blk)),
                  pl.BlockSpec((tm, D), lambda i: (i, m_blk + 1)),
                  pl.BlockSpec((tm, pool_w), lambda i: (jnp.minimum(i, nlat - 1), 0)),
                  pl.BlockSpec((tm, pool_w), lambda i: (jnp.maximum(i - nlat, 0), 0)),
                  pl.BlockSpec((1, LANES), const),
                  pl.BlockSpec((1, ng, gw, gw), lambda i: (layer, 0, 0, 0), pipeline_mode=once),
                  pl.BlockSpec((1, pool_w), const),
                  pl.BlockSpec((1, dn_w, D), lambda i: (layer, 0, 0), pipeline_mode=once),
                  pl.BlockSpec((1, pool_w, D), lambda i: (layer, 0, 0), pipeline_mode=once)],
        out_specs=pl.BlockSpec((tm, D), lambda i: (i, 0)),
        out_shape=jax.ShapeDtypeStruct((n_rows, D), bf16),
        scratch_shapes=[pltpu.VMEM((tm, dn_w), bf16), pltpu.VMEM((tm, pool_w), bf16)],
        compiler_params=_cparams(("parallel",)),
        name="merge_a",
    )(o_dn, z, z, z, pooled_lat, pooled_ctx, onorm_g, w_pool, pool_scale, w_up_dn, w_up_pool)


def _merge_b_body(m_ref, h_ref, gt_ref, sc_ref, sh_ref, g_ref, wout_ref, wr_ref, h1_ref, v_ref, lg_ref,
                  *, ds):
    tm = m_ref.shape[0]
    m = jnp.dot(m_ref[...], wout_ref[0].astype(bf16), preferred_element_type=f32)
    h1 = h_ref[...] + gt_ref[0] * m
    h1_ref[...] = h1
    y = h1 * lax.rsqrt(jnp.mean(h1 * h1, -1, keepdims=True) + EPS) * g_ref[...]
    v = y * (1.0 + sc_ref[0]) + sh_ref[0]
    E = lg_ref.shape[1]
    vh = v.astype(bf16)
    vl = (v - vh.astype(f32)).astype(bf16)
    th = jnp.dot(vh, wr_ref[...], preferred_element_type=f32)
    tl = jnp.dot(vl, wr_ref[...], preferred_element_type=f32)
    lg_ref[...] = th[:, :E] + (th[:, E:] + tl[:, :E])
    half = ds * LANES
    for j in range(ds):
        v_ref[pl.ds(j, tm, stride=ds), :] = _pack_pair(v[:, j * LANES:(j + 1) * LANES],
                                                       v[:, half + j * LANES:half + (j + 1) * LANES])


def _merge_b(merged, h, mod, gain2, w_out, w_router, *, layer, n_rows, tm, tpb, nlat, nb):
    D = h.shape[1]
    E = w_router.shape[1]
    ds = D // (2 * LANES)
    const = lambda *_: (0, 0)
    wr_hi = w_router.astype(bf16)
    wr_lo = (w_router - wr_hi.astype(f32)).astype(bf16)
    wr_cat = jnp.concatenate([wr_hi, wr_lo], 1)
    return pl.pallas_call(
        functools.partial(_merge_b_body, ds=ds),
        grid=(n_rows // tm,),
        in_specs=[pl.BlockSpec((tm, D), lambda i: (i, 0)),
                  pl.BlockSpec((tm, D), lambda i: (i, 0)),
                  _mod_spec(2, D, tpb, nlat, nb),
                  _mod_spec(4, D, tpb, nlat, nb),
                  _mod_spec(3, D, tpb, nlat, nb),
                  pl.BlockSpec((1, D), const),
                  pl.BlockSpec((1, D, D), lambda i: (layer, 0, 0), pipeline_mode=pl.Buffered(1)),
                  pl.BlockSpec((D, 2 * E), const)],
        out_specs=[pl.BlockSpec((tm, D), lambda i: (i, 0)),
                   pl.BlockSpec((tm * ds, LANES), lambda i: (i, 0)),
                   pl.BlockSpec((tm, E), lambda i: (i, 0))],
        out_shape=[jax.ShapeDtypeStruct((n_rows, D), f32),
                   jax.ShapeDtypeStruct((n_rows * ds, LANES), jnp.uint32),
                   jax.ShapeDtypeStruct((n_rows, E), f32)],
        compiler_params=_cparams(("parallel",)),
        name="merge_b",
    )(merged, h, mod, mod, mod, gain2, w_out, wr_cat)


def _route_body(lg_ref, bias_ref, idx_ref, gate_ref, cnt_ref, carry_scr, *, n_groups):
    i = pl.program_id(0)
    E, tt = lg_ref.shape

    @pl.when(i == 0)
    def _():
        carry_scr[...] = jnp.zeros_like(carry_scr)

    aff = _sigmoid(lg_ref[...])
    sel = aff + bias_ref[...]
    sub = lax.broadcasted_iota(jnp.int32, (GROUP_SIZE, tt), 0)
    neg = -1e30
    best = None
    for g in range(n_groups):
        s = sel[g * GROUP_SIZE:(g + 1) * GROUP_SIZE]
        a = aff[g * GROUP_SIZE:(g + 1) * GROUP_SIZE]
        m1 = jnp.max(s, axis=0, keepdims=True)
        i1 = jnp.min(jnp.where(s == m1, sub, GROUP_SIZE), axis=0, keepdims=True)
        s2 = jnp.where(sub == i1, neg, s)
        m2 = jnp.max(s2, axis=0, keepdims=True)
        i2 = jnp.min(jnp.where(s2 == m2, sub, GROUP_SIZE), axis=0, keepdims=True)
        a1 = jnp.sum(jnp.where(sub == i1, a, 0.0), axis=0, keepdims=True)
        a2 = jnp.sum(jnp.where(sub == i2, a, 0.0), axis=0, keepdims=True)
        cand = (m1 + m2, i1 + g * GROUP_SIZE, i2 + g * GROUP_SIZE, a1, a2)
        if best is None:
            best = cand
        else:
            better = cand[0] > best[0]
            best = tuple(jnp.where(better, cn, bs) for cn, bs in zip(cand, best))
    _, e1, e2, a1, a2 = best
    den = a1 + a2
    gate_ref[0:1, :] = a1 / den
    gate_ref[1:2, :] = a2 / den
    erow = lax.broadcasted_iota(jnp.int32, (E, tt), 0)
    oh1 = erow == e1
    oh2 = erow == e2
    cnt = jnp.where(oh1 | oh2, 1.0, 0.0)
    ti = lax.broadcasted_iota(jnp.int32, (tt, tt), 0)
    tj = lax.broadcasted_iota(jnp.int32, (tt, tt), 1)
    before = jnp.where(ti < tj, 1.0, 0.0).astype(bf16)
    pos = jnp.dot(cnt.astype(bf16), before, preferred_element_type=f32) + carry_scr[...]
    r1 = jnp.sum(jnp.where(oh1, pos, 0.0), axis=0, keepdims=True)
    r2 = jnp.sum(jnp.where(oh2, pos, 0.0), axis=0, keepdims=True)
    idx_ref[0:1, :] = e1
    idx_ref[1:2, :] = e2
    idx_ref[2:3, :] = r1.astype(jnp.int32)
    idx_ref[3:4, :] = r2.astype(jnp.int32)
    carry_scr[...] = carry_scr[...] + jnp.sum(cnt, axis=1, keepdims=True)
    cnt_ref[...] = jnp.broadcast_to(carry_scr[...], cnt_ref.shape)


def _route(logits_t, bias_col, *, tt):
    E, T = logits_t.shape
    return pl.pallas_call(
        functools.partial(_route_body, n_groups=E // GROUP_SIZE),
        grid=(T // tt,),
        in_specs=[pl.BlockSpec((E, tt), lambda i: (0, i)),
                  pl.BlockSpec((E, 1), lambda i: (0, 0))],
        out_specs=[pl.BlockSpec((4, tt), lambda i: (0, i)),
                   pl.BlockSpec((2, tt), lambda i: (0, i)),
                   pl.BlockSpec((E, LANES), lambda i: (0, 0))],
        out_shape=[jax.ShapeDtypeStruct((4, T), jnp.int32),
                   jax.ShapeDtypeStruct((2, T), f32),
                   jax.ShapeDtypeStruct((E, LANES), f32)],
        scratch_shapes=[pltpu.VMEM((E, 1), f32)],
        compiler_params=_cparams(("arbitrary",)),
        name="route",
    )(logits_t, bias_col)


def _row_copy(src, dst, s, t, n, sem):
    def rows(x):
        return pl.ds(x * n if isinstance(x, int) else pl.multiple_of(x * n, 8), n)

    return pltpu.make_async_copy(src.at[rows(s), :], dst.at[rows(t), :], sem)


def _dispatch_body(dest_ref, pad_ref, v_ref, xs_hbm, zero_scr, sem, zsem, *, tt, ds, n_exp, n_blocks, rb):
    i = pl.program_id(0)

    @pl.when(i == 0)
    def _():
        zero_scr[...] = jnp.zeros_like(zero_scr)

        def pad_copy(row, p):
            return pltpu.make_async_copy(zero_scr.at[pl.ds(0, p * ds), :],
                                         xs_hbm.at[pl.ds(pl.multiple_of(row * ds, ds), p * ds), :], zsem)

        def per_expert(wait):
            def body(e, m):
                row = pad_ref[0, e]
                npad = pad_ref[1, e]
                p = rb // 2
                while p >= 1:
                    hit = (npad & p) != 0

                    @pl.when(hit)
                    def _(row=row, p=p):
                        if wait:
                            pad_copy(row, p).wait()
                        else:
                            pad_copy(row, p).start()

                    row = row + jnp.where(hit, p, 0)
                    p //= 2
                return m

            lax.fori_loop(0, n_exp, body, 0)

        per_expert(False)
        per_expert(True)
        n_used = pad_ref[2, 0]

        def blk_start(bk, m):
            _row_copy(zero_scr, xs_hbm, 0, bk, rb * ds, zsem).start()
            return m

        lax.fori_loop(n_used, n_blocks, blk_start, 0)

        def blk_wait(bk, m):
            _row_copy(zero_scr, xs_hbm, 0, 0, rb * ds, zsem).wait()
            return m

        lax.fori_loop(n_used, n_blocks, blk_wait, 0)

    def issue(r, m):
        src = v_ref.at[pl.ds(pl.multiple_of(r * ds, ds), ds), :]
        for kk in range(2):
            dst = xs_hbm.at[pl.ds(pl.multiple_of(dest_ref[0, kk, r] * ds, ds), ds), :]
            pltpu.make_async_copy(src, dst, sem).start(priority=kk)
        return m

    lax.fori_loop(0, tt, issue, 0, unroll=8)
    for kk in range(2):
        pltpu.make_async_copy(v_ref, xs_hbm.at[pl.ds(0, tt * ds), :], sem).wait()


def _dispatch(dest3, pad_info, v_ts, *, n_tok, n_blocks, tt, ds):
    n_exp = pad_info.shape[1]
    rb = EXPERT_BLOCK
    return pl.pallas_call(
        functools.partial(_dispatch_body, tt=tt, ds=ds, n_exp=n_exp, n_blocks=n_blocks, rb=rb),
        grid=(n_tok // tt,),
        in_specs=[pl.BlockSpec((1, 2, tt), lambda i: (i, 0, 0), memory_space=pltpu.SMEM),
                  pl.BlockSpec(memory_space=pltpu.SMEM),
                  pl.BlockSpec((tt * ds, LANES), lambda i: (i, 0))],
        out_specs=pl.BlockSpec(memory_space=pl.ANY),
        out_shape=jax.ShapeDtypeStruct((n_blocks * rb * ds, LANES), v_ts.dtype),
        scratch_shapes=[pltpu.VMEM((rb * ds, LANES), v_ts.dtype), pltpu.SemaphoreType.DMA(()),
                        pltpu.SemaphoreType.DMA(())],
        compiler_params=_cparams(("arbitrary",)),
        name="dispatch",
    )(dest3, pad_info, v_ts)


def _expert_body(blk0_ref, nblk_ref, nu_ref, x_hbm, wg_ref, wu_ref, wd_ref, y_hbm, x_buf, y_buf, x_s, sem_x, sem_y,
                 *, ds, rb, n_blocks):
    e = pl.program_id(0)
    b0 = blk0_ref[e]
    nb = nblk_ref[e]
    n_used = nu_ref[0]
    rows = rb * ds

    def x_copy(g, slot):
        return pltpu.make_async_copy(x_hbm.at[pl.ds(pl.multiple_of(g * rows, rows), rows), :],
                                     x_buf.at[slot], sem_x.at[slot])

    def y_copy(g, slot):
        return pltpu.make_async_copy(y_buf.at[slot],
                                     y_hbm.at[pl.ds(pl.multiple_of(g * rows, rows), rows), :], sem_y.at[slot])

    @pl.when(jnp.logical_and(e == 0, n_used > 0))
    def _():
        x_copy(0, 0).start()

    def block(i, carry):
        g = b0 + i
        slot = g % 2
        x_copy(g, slot).wait()

        @pl.when(g + 1 < n_used)
        def _():
            x_copy(g + 1, 1 - slot).start()

        @pl.when(g >= 2)
        def _():
            y_copy(g - 2, slot).wait()

        half = ds * LANES
        for j in range(ds):
            lo, hi = _unpack_pair(x_buf[slot, pl.ds(j, rb, stride=ds), :])
            x_s[:, j * LANES:(j + 1) * LANES] = lo.astype(bf16)
            x_s[:, half + j * LANES:half + (j + 1) * LANES] = hi.astype(bf16)
        x = x_s[...]
        hg = jnp.dot(x, wg_ref[0, 0].astype(bf16), preferred_element_type=f32)
        hu = jnp.dot(x, wu_ref[0, 0].astype(bf16), preferred_element_type=f32)
        a = (hg * _sigmoid(hg) * hu).astype(bf16)
        y = jnp.dot(a, wd_ref[0, 0].astype(bf16), preferred_element_type=f32)
        for j in range(ds):
            y_buf[slot, pl.ds(j, rb, stride=ds), :] = _pack_pair(y[:, j * LANES:(j + 1) * LANES],
                                                                 y[:, half + j * LANES:half + (j + 1) * LANES])
        y_copy(g, slot).start()
        return carry

    lax.fori_loop(0, nb, block, 0)

    @pl.when(e == pl.num_programs(0) - 1)
    def _():
        @pl.when(n_used >= 2)
        def _():
            y_copy(n_used - 2, n_used % 2).wait()

        @pl.when(n_used >= 1)
        def _():
            y_copy(n_used - 1, (n_used - 1) % 2).wait()

        y_buf[0] = jnp.zeros(y_buf.shape[1:], y_buf.dtype)

        def start(bk, m):
            _row_copy(y_buf.at[0], y_hbm, 0, bk, rows, sem_y.at[0]).start()
            return m

        lax.fori_loop(n_used, n_blocks, start, 0)

        def wait(bk, m):
            _row_copy(y_buf.at[0], y_hbm, 0, 0, rows, sem_y.at[0]).wait()
            return m

        lax.fori_loop(n_used, n_blocks, wait, 0)


def _experts(blk0, nblk, n_used, xs_ts, w_g, w_u, w_d, *, layer, ds):
    _, E, D, De = w_g.shape
    rb = EXPERT_BLOCK
    n_blocks = xs_ts.shape[0] // (rb * ds)
    grid_spec = pltpu.PrefetchScalarGridSpec(
        num_scalar_prefetch=3,
        grid=(E,),
        in_specs=[pl.BlockSpec(memory_space=pl.ANY),
                  pl.BlockSpec((1, 1, D, De), lambda e, *_: (layer, e, 0, 0)),
                  pl.BlockSpec((1, 1, D, De), lambda e, *_: (layer, e, 0, 0)),
                  pl.BlockSpec((1, 1, De, D), lambda e, *_: (layer, e, 0, 0))],
        out_specs=pl.BlockSpec(memory_space=pl.ANY),
        scratch_shapes=[pltpu.VMEM((2, rb * ds, LANES), xs_ts.dtype), pltpu.VMEM((2, rb * ds, LANES), xs_ts.dtype),
                        pltpu.VMEM((rb, D), bf16), pltpu.SemaphoreType.DMA((2,)), pltpu.SemaphoreType.DMA((2,))],
    )
    return pl.pallas_call(
        functools.partial(_expert_body, ds=ds, rb=rb, n_blocks=n_blocks),
        grid_spec=grid_spec,
        out_shape=jax.ShapeDtypeStruct(xs_ts.shape, xs_ts.dtype),
        compiler_params=_cparams(("arbitrary",)),
        name="experts",
    )(blk0, nblk, n_used, xs_ts, w_g, w_u, w_d)


def _combine_body(dest_ref, dnext_ref, gate_ref, h_ref, gt_ref, fg_ref, y_hbm, o_ref, y_scr, sem, *, tt, ds, final):
    i = pl.program_id(0)
    slot = i % 2

    def gather(d_ref, to_slot):
        def body(r, m):
            for kk in range(2):
                _row_copy(y_hbm, y_scr.at[to_slot, kk], d_ref[0, kk, r], r, ds, sem.at[to_slot]).start(priority=kk)
            return m

        lax.fori_loop(0, tt, body, 0, unroll=8)

    @pl.when(i == 0)
    def _():
        gather(dest_ref, 0)

    @pl.when(i + 1 < pl.num_programs(0))
    def _():
        gather(dnext_ref, 1 - slot)

    for kk in range(2):
        pltpu.make_async_copy(y_hbm.at[pl.ds(0, tt * ds), :], y_scr.at[slot, kk], sem.at[slot]).wait()
    gt = gt_ref[0]
    g0 = gate_ref[:, 0:1]
    g1 = gate_ref[:, 1:2]
    half = ds * LANES
    for j in range(ds):
        lo0, hi0 = _unpack_pair(y_scr[slot, 0, pl.ds(j, tt, stride=ds), :])
        lo1, hi1 = _unpack_pair(y_scr[slot, 1, pl.ds(j, tt, stride=ds), :])
        for base, y0, y1 in ((0, lo0, lo1), (half, hi0, hi1)):
            sl = slice(base + j * LANES, base + (j + 1) * LANES)
            o_ref[:, sl] = h_ref[:, sl] + gt[:, sl] * (g0 * y0 + g1 * y1)
    if final:
        x = o_ref[...]
        o_ref[...] = x * lax.rsqrt(jnp.mean(x * x, -1, keepdims=True) + EPS) * fg_ref[...]


def _combine(dest3, gates_t, h1, mod, final_g, y_ts, *, n_tok, tt, ds, tpb, nlat, nb, final):
    D = h1.shape[1]
    nt = n_tok // tt
    return pl.pallas_call(
        functools.partial(_combine_body, tt=tt, ds=ds, final=final),
        grid=(nt,),
        in_specs=[pl.BlockSpec((1, 2, tt), lambda i: (i, 0, 0), memory_space=pltpu.SMEM),
                  pl.BlockSpec((1, 2, tt), lambda i: (jnp.minimum(i + 1, nt - 1), 0, 0), memory_space=pltpu.SMEM),
                  pl.BlockSpec((tt, 2), lambda i: (i, 0)),
                  pl.BlockSpec((tt, D), lambda i: (i, 0)),
                  _mod_spec(5, D, tpb, nlat, nb),
                  pl.BlockSpec((1, D), lambda i: (0, 0)),
                  pl.BlockSpec(memory_space=pl.ANY)],
        out_specs=pl.BlockSpec((tt, D), lambda i: (i, 0)),
        out_shape=jax.ShapeDtypeStruct((n_tok, D), f32),
        scratch_shapes=[pltpu.VMEM((2, 2, tt * ds, LANES), y_ts.dtype), pltpu.SemaphoreType.DMA((2,))],
        compiler_params=_cparams(("arbitrary",)),
        name="combine",
    )(dest3, dest3, gates_t, h1, mod, final_g, y_ts)


def _moe(v_ts, logits, h1, mod, w_router_bias, w_g, w_u, w_d, final_g, *, layer, n_tok, tt, ds, tpb, nlat, nb,
         final):
    E = w_g.shape[1]
    route_tt = next(t for t in (512, 256, LANES) if n_tok % t == 0)
    idx, gates, counts = _route(logits[:n_tok].T, w_router_bias.reshape(E, 1), tt=route_tt)
    counts = counts[:, 0].astype(jnp.int32)
    padded = (counts + EXPERT_BLOCK - 1) // EXPERT_BLOCK * EXPERT_BLOCK
    pad_end = jnp.cumsum(padded)
    pad_start = pad_end - padded
    n_blocks = (n_tok * 2 + EXPERT_BLOCK - 1) // EXPERT_BLOCK + E
    n_used = (pad_end[-1] // EXPERT_BLOCK).astype(jnp.int32).reshape(1)
    e_sel = idx[0:2][..., None] == jnp.arange(E, dtype=jnp.int32)
    dest = jnp.sum(jnp.where(e_sel, pad_start.astype(jnp.int32), 0), -1) + idx[2:4]
    dest3 = dest.reshape(2, n_tok // tt, tt).transpose(1, 0, 2)
    pad_info = jnp.stack([pad_start + counts, padded - counts, jnp.broadcast_to(n_used, (E,))]).astype(jnp.int32)
    xs_ts = _dispatch(dest3, pad_info, v_ts, n_tok=n_tok, n_blocks=n_blocks, tt=tt, ds=ds)
    y_ts = _experts((pad_start // EXPERT_BLOCK).astype(jnp.int32), (padded // EXPERT_BLOCK).astype(jnp.int32), n_used,
                    xs_ts, w_g, w_u, w_d, layer=layer, ds=ds)
    return _combine(dest3, gates.T, h1, mod, final_g, y_ts, n_tok=n_tok, tt=tt, ds=ds, tpb=tpb, nlat=nlat,
                    nb=nb, final=final)


def kernel(x, c, ctx, c_ctx, w_mod, b_mod, norm1_g, norm2_g, w_in, conv_w, a_log, dt_bias, onorm_g, w_pool, pool_scale, w_up_dn, w_up_pool, w_out, w_router, router_bias, w_gate_e, w_up_e, w_down_e, final_g):
    B, S, D = x.shape
    C = ctx.shape[1]
    L = w_mod.shape[0]
    H = a_log.shape[-1]
    dk = onorm_g.shape[-1]
    assert dk == LANES
    dn_w = H * dk
    _, ng, gw, _ = w_pool.shape
    pool_w = ng * gw
    ds = D // (2 * LANES)
    n_lat = B * S
    T = n_lat + B * C
    tm = 256
    assert S % tm == 0 and (B * C) % tm == 0 and C % CHUNK == 0 and S % (4 * GRID_W) == 0
    tpb = S // tm
    nlat_tiles = n_lat // tm
    tm_in = next(t for t in (1024, 512, 256) if S % t == 0 and (B * C) % t == 0)
    tm_mg = next(t for t in (512, 256) if S % t == 0 and (B * C) % t == 0)
    tr = min(256, C)
    G = min(H, 16)

    c8 = jnp.zeros((8, D), f32).at[:B].set(c).at[B].set(c_ctx)
    mod = _mod_table(c8, w_mod, b_mod)

    qkv_c = 3 * dn_w
    scan_c = qkv_c + 4 * H
    og_off, pool_off = scan_c, scan_c + dn_w
    mg_off = pool_off + pool_w
    assert (4 * dn_w) % D == 0 and (4 * dn_w + 2 * D) % pool_w == 0
    pool_off_blocks = (4 * dn_w + 2 * D) // pool_w

    h = jnp.concatenate([x.reshape(n_lat, D), ctx.reshape(B * C, D)], 0)
    for l in range(L):
        last = l == L - 1
        mod_l = mod[l].reshape(8, 1, mod.shape[-1])
        wl = w_in[l]
        w_z = jnp.concatenate([wl[:, :qkv_c], wl[:, og_off:pool_off], wl[:, mg_off:], wl[:, pool_off:mg_off]],
                              1).astype(bf16)
        w_ba = jnp.pad(wl[:, qkv_c:scan_c], ((0, 0), (0, LANES - 4 * H))).astype(bf16)
        tn = 1024 if w_z.shape[1] % 1024 == 0 else w_z.shape[1]
        z, ba = _in_proj(h, mod_l, 0, norm1_g[l].reshape(1, D), w_z, w_ba, tm=tm_in, tn=tn, tpb=S // tm_in,
                         nlat=n_lat // tm_in, nb=B)
        qkv = _conv_prep(z, conv_w[l], tr=tr, tc=min(1024, dn_w), S=S, C=C, nlat_rows=n_lat, dn_w=dn_w, dk=dk)
        o_dn = _deltanet(qkv, ba, a_log[l], dt_bias[l], B=B, S=S, C=C, H=H, G=G)
        pooled_lat, pooled_ctx = _pool(z, B=B, S=S, C=C, pool_off_blocks=pool_off_blocks, pool_w=pool_w, gw=gw)
        n_rows = n_lat if last else T
        merged = _merge_a(o_dn, z, pooled_lat, pooled_ctx, onorm_g[l].reshape(1, dk), w_pool,
                          pool_scale[l].reshape(1, pool_w), w_up_dn, w_up_pool,
                          layer=l, n_rows=n_rows, tm=tm, H=H, D=D)
        h1, v_ts, logits = _merge_b(merged, h, mod_l, norm2_g[l].reshape(1, D), w_out, w_router,
                                    layer=l, n_rows=n_rows, tm=tm_mg, tpb=S // tm_mg, nlat=n_lat // tm_mg, nb=B)
        h = _moe(v_ts, logits, h1, mod_l, router_bias, w_gate_e, w_up_e, w_down_e,
                 final_g.reshape(1, D), layer=l, n_tok=n_rows, tt=tm, ds=ds, tpb=tpb, nlat=nlat_tiles, nb=B, final=last)
    return h.reshape(B, S, D)
```

```python
import functools

import jax
import jax.numpy as jnp
from jax import lax
from jax.experimental import pallas as pl
from jax.experimental.pallas import tpu as pltpu

f32 = jnp.float32
bf16 = jnp.bfloat16
HIGHEST = lax.Precision.HIGHEST

EPS = 1e-6
LANES = 128
GRID_W = 64
CHUNK = 64
POOL_WINDOWS = (2, 4, 8, 16)
GROUP_SIZE = 8
EXPERT_BLOCK = 256
VMEM_LIMIT = 56 * 1024 * 1024


def _cparams(sem, vmem=VMEM_LIMIT):
    return pltpu.CompilerParams(dimension_semantics=sem, vmem_limit_bytes=vmem)


def _sigmoid(x):
    return 1.0 / (1.0 + jnp.exp(-x))


def _softplus(x):
    return jnp.maximum(x, 0.0) + jnp.log(1.0 + jnp.exp(-jnp.abs(x)))


def _pack_pair(lo, hi):
    def rounded(x):
        u = lax.bitcast_convert_type(x, jnp.uint32)
        return u + jnp.uint32(0x7FFF) + ((u >> 16) & jnp.uint32(1))

    return (rounded(lo) >> 16) | (rounded(hi) & jnp.uint32(0xFFFF0000))


def _unpack_pair(w):
    lo = lax.bitcast_convert_type(w << 16, f32)
    hi = lax.bitcast_convert_type(w & jnp.uint32(0xFFFF0000), f32)
    return lo, hi


def _mod_spec(sec, D, tiles_per_batch, n_lat_tiles, n_batch):
    def index(i, *_):
        return (jnp.where(i < n_lat_tiles, i // tiles_per_batch, n_batch), 0, sec)

    return pl.BlockSpec((1, 1, D), index)


def _mod_body(c_ref, w_ref, b_ref, o_ref):
    c = c_ref[...]
    s = c * _sigmoid(c)
    sh = s.astype(bf16)
    sl = (s - sh.astype(f32)).astype(bf16)
    w = w_ref[0]
    wh = w.astype(bf16)
    wl = (w - wh.astype(f32)).astype(bf16)
    t = jnp.dot(jnp.concatenate([sh, sl], axis=0), wh, preferred_element_type=f32)
    n = s.shape[0]
    o_ref[0] = t[:n] + (t[n:] + jnp.dot(sh, wl, preferred_element_type=f32)) + b_ref[0]


def _mod_table(c8, w_mod, b_mod):
    L, D, N = w_mod.shape
    tn = next(t for t in (1024, 512, 256, LANES) if N % t == 0)
    return pl.pallas_call(
        _mod_body,
        grid=(L, N // tn),
        in_specs=[pl.BlockSpec((8, D), lambda l, j: (0, 0)),
                  pl.BlockSpec((1, D, tn), lambda l, j: (l, 0, j)),
                  pl.BlockSpec((1, 1, tn), lambda l, j: (l, 0, j))],
        out_specs=pl.BlockSpec((1, 8, tn), lambda l, j: (l, 0, j)),
        out_shape=jax.ShapeDtypeStruct((L, 8, N), f32),
        compiler_params=_cparams(("parallel", "parallel")),
        name="mod_table",
    )(c8, w_mod, b_mod.reshape(L, 1, N))


def _inproj_body(h_ref, sc_ref, sh_ref, g_ref, w_ref, wba_ref, z_ref, ba_ref, u_scr):
    j = pl.program_id(1)

    @pl.when(j == 0)
    def _():
        x = h_ref[...]
        y = x * lax.rsqrt(jnp.mean(x * x, -1, keepdims=True) + EPS) * g_ref[...]
        u = (y * (1.0 + sc_ref[0]) + sh_ref[0]).astype(bf16)
        u_scr[...] = u
        ba_ref[...] = jnp.dot(u, wba_ref[...], preferred_element_type=f32)

    z_ref[...] = jnp.dot(u_scr[...], w_ref[...], preferred_element_type=f32).astype(z_ref.dtype)


def _in_proj(h, mod, sec, gain, w_z, w_ba, *, tm, tn, tpb, nlat, nb):
    T, D = h.shape
    Nz = w_z.shape[1]
    return pl.pallas_call(
        _inproj_body,
        grid=(T // tm, Nz // tn),
        in_specs=[pl.BlockSpec((tm, D), lambda i, j: (i, 0)),
                  _mod_spec(sec + 1, D, tpb, nlat, nb),
                  _mod_spec(sec, D, tpb, nlat, nb),
                  pl.BlockSpec((1, D), lambda i, j: (0, 0)),
                  pl.BlockSpec((D, tn), lambda i, j: (0, j)),
                  pl.BlockSpec((D, LANES), lambda i, j: (0, 0))],
        out_specs=[pl.BlockSpec((tm, tn), lambda i, j: (i, j)),
                   pl.BlockSpec((tm, LANES), lambda i, j: (i, 0))],
        out_shape=[jax.ShapeDtypeStruct((T, Nz), bf16), jax.ShapeDtypeStruct((T, LANES), f32)],
        scratch_shapes=[pltpu.VMEM((tm, D), bf16)],
        compiler_params=_cparams(("parallel", "arbitrary")),
        name="in_proj",
    )(h, mod, mod, gain, w_z, w_ba)


def _conv_body(x_ref, xp_ref, xn_ref, w_ref, sh_ref, o_ref, *, tr, tps_lat, tps_ctx, nlat, qscale, tiles_per_sec,
               halo):
    i = pl.program_id(0)
    il =jnp.where(i < nlat, i % tps_lat, (i - nlat) % tps_ctx)
    nt = jnp.where(i < nlat, tps_lat, tps_ctx)
    first = il == 0
    last = il == nt - 1
    xb = x_ref[...]
    w = w_ref[...]
    tc_ = xb.shape[1]
    ys = jnp.dot(sh_ref[...], xb, preferred_element_type=f32).reshape(tr // 8, 32, tc_)
    acc = (ys[:, 0:8] * w[0:1] + ys[:, 8:16] * w[1:2] + ys[:, 16:24] * w[3:4] + ys[:, 24:32] * w[4:5])
    acc = acc.reshape(tr, tc_) + xb.astype(f32) * w[2:3]
    p = jnp.where(first, 0.0, xp_ref[...].astype(f32))[halo - 8:]
    n = jnp.where(last, 0.0, xn_ref[...].astype(f32))[:8]
    r8 = lax.broadcasted_iota(jnp.int32, (8, 1), 0)
    top = jnp.where(r8 == 0, p[6:7] * w[0:1] + p[7:8] * w[1:2], jnp.where(r8 == 1, p[7:8] * w[0:1], 0.0))
    bot = jnp.where(r8 == 7, n[0:1] * w[3:4] + n[1:2] * w[4:5], jnp.where(r8 == 6, n[0:1] * w[4:5], 0.0))
    acc = jnp.concatenate([acc[:8] + top, acc[8:tr - 8], acc[tr - 8:] + bot], axis=0)
    y = acc * _sigmoid(acc)
    sec = pl.program_id(1) // tiles_per_sec

    @pl.when(sec == 2)
    def _():
        o_ref[...] = y.astype(o_ref.dtype)

    @pl.when(sec != 2)
    def _():
        qk_scale = jnp.where(sec == 0, qscale, 1.0)
        for s in range(acc.shape[1] // LANES):
            ys = y[:, s * LANES:(s + 1) * LANES]
            rs = lax.rsqrt(jnp.sum(ys * ys, -1, keepdims=True) + EPS) * qk_scale
            o_ref[:, s * LANES:(s + 1) * LANES] = (ys * rs).astype(o_ref.dtype)


def _conv_prep(z, conv_w, *, tr, tc, S, C, nlat_rows, dn_w, dk):
    T = z.shape[0]
    halo = 16
    hb = tr // halo
    nh = T // halo
    body = functools.partial(_conv_body, tr=tr, tps_lat=S // tr, tps_ctx=C // tr, nlat=nlat_rows // tr,
                             qscale=float(dk) ** -0.5, tiles_per_sec=dn_w // tc, halo=halo)
    assert conv_w.shape[0] == 5
    ri = jnp.arange(tr)[:, None]
    ci = jnp.arange(tr)[None, :]
    shifts = jnp.stack([(ci == ri + off).astype(bf16) for off in (-2, -1, 1, 2)], 0)
    shifts = shifts.reshape(4, tr // 8, 8, tr).transpose(1, 0, 2, 3).reshape(4 * tr, tr)
    return pl.pallas_call(
        body,
        grid=(T // tr, 3 * dn_w // tc),
        in_specs=[pl.BlockSpec((tr, tc), lambda i, j: (i, j)),
                  pl.BlockSpec((halo, tc), lambda i, j: (jnp.maximum(i * hb - 1, 0), j)),
                  pl.BlockSpec((halo, tc), lambda i, j: (jnp.minimum((i + 1) * hb, nh - 1), j)),
                  pl.BlockSpec((conv_w.shape[0], tc), lambda i, j: (0, j)),
                  pl.BlockSpec((4 * tr, tr), lambda i, j: (0, 0))],
        out_specs=pl.BlockSpec((tr, tc), lambda i, j: (i, j)),
        out_shape=jax.ShapeDtypeStruct((T, 3 * dn_w), bf16),
        compiler_params=_cparams(("parallel", "parallel")),
        name="conv_prep",
    )(z, z, z, conv_w, shifts)


def _mm(x, y):
    return jnp.dot(x.astype(bf16), y.astype(bf16), preferred_element_type=f32)


def _unit_tri_inverse(a_list, ri, ci, interleave=None):
    b16 = (ri >> 4) == (ci >> 4)
    b32 = (ri >> 5) == (ci >> 5)
    eye = jnp.where(ri == ci, 1.0, 0.0)
    n = a_list[0].shape[0]

    def mmb(x, y):
        return jnp.dot(x, y, preferred_element_type=f32)

    a16 = [jnp.where(b16, a, 0.0) for a in a_list]
    t = [eye - x for x in a16]
    pb = [x.astype(bf16) for x in a16]
    pb = [mmb(x, x).astype(bf16) for x in pb]
    for step in range(2):
        tp = [mmb(jnp.concatenate([ti.astype(bf16), pi], axis=0), pi) for ti, pi in zip(t, pb)]
        t = [ti + x[:n] for ti, x in zip(t, tp)]
        pb = [x[n:].astype(bf16) for x in tp]
        if step == 0 and interleave is not None:
            interleave()
    t = [ti + mmb(ti.astype(bf16), pi) for ti, pi in zip(t, pb)]
    for off in ([jnp.where(b32, a, 0.0) - x for a, x in zip(a_list, a16)],
                [jnp.where(b32, 0.0, a) for a in a_list]):
        tb = [ti.astype(bf16) for ti in t]
        ta = [mmb(ti, ai.astype(bf16)).astype(bf16) for ti, ai in zip(tb, off)]
        t = [ti - mmb(tai, tbi) for ti, tai, tbi in zip(t, ta, tb)]
    return t


def _dn_body(q_ref, k_ref, v_ref, ba_ref, al_ref, dt_ref, o_ref, s_scr, u_scr, wq_scr, qk_scr, eg_scr, *, G):
    d = pl.program_id(1)
    c = pl.program_id(3)
    dk = LANES
    hs = range(G)

    @pl.when(c == 0)
    def _():
        for ref in (s_scr, u_scr, wq_scr, qk_scr, eg_scr):
            ref[...] = jnp.zeros_like(ref)

    ba = ba_ref[0]
    beta_t = _sigmoid(ba)
    g_t = -jnp.exp(al_ref[0]) * _softplus(ba + dt_ref[0])
    ri = lax.broadcasted_iota(jnp.int32, (CHUNK, CHUNK), 0)
    ci = lax.broadcasted_iota(jnp.int32, (CHUNK, CHUNK), 1)
    fwd = d == 0
    later = jnp.where(fwd, ri, ci)
    earlier = jnp.where(fwd, ci, ri)
    incl = later >= earlier
    strict = later > earlier
    tri = jnp.where(incl, 1.0, 0.0)
    gc = jnp.dot(tri, g_t, preferred_element_type=f32, precision=HIGHEST)
    gct = gc.T
    gtot = jnp.sum(g_t, axis=0, keepdims=True)

    def pick(x, col):
        return jnp.where(fwd, x[:, col:col + 1], x[:, G + col:G + col + 1])

    sl = [slice(h * dk, (h + 1) * dk) for h in hs]
    beta_c = [pick(beta_t, h) for h in hs]
    gc_c = [pick(gc, 2 * G + h) for h in hs]
    gc_r = [jnp.where(fwd, gct[2 * G + h:2 * G + h + 1], gct[3 * G + h:3 * G + h + 1]) for h in hs]
    gt = [pick(gtot, 2 * G + h) for h in hs]
    q = [q_ref[:, sl[h]].astype(f32) for h in hs]
    k = [k_ref[:, sl[h]].astype(f32) for h in hs]
    kb = [k[h] * beta_c[h] for h in hs]
    eg = [jnp.exp(gc_c[h]) for h in hs]
    dec = [jnp.where(incl, jnp.exp(jnp.where(incl, gc_c[h] - gc_r[h], 0.0)), 0.0) for h in hs]
    kq = [lax.dot_general(jnp.concatenate([kb[h], q[h]], axis=0).astype(bf16), k_ref[:, sl[h]],
                          (((1,), (1,)), ((), ())), preferred_element_type=f32) for h in hs]
    s = [s_scr[h] for h in hs]
    r = [jnp.dot(wq_scr[h], s[h].astype(bf16), preferred_element_type=f32) for h in hs]
    a = [jnp.where(strict, kq[h][:CHUNK] * dec[h], 0.0) for h in hs]
    qk = [(kq[h][CHUNK:] * dec[h]).astype(bf16) for h in hs]

    def recurrence_tail():
        vnb = [(u_scr[h] - r[h][:CHUNK]).astype(bf16) for h in hs]
        ou = [jnp.dot(qk_scr[h], vnb[h], preferred_element_type=f32) for h in hs]
        for h in hs:
            s_scr[h] = s[h] * eg_scr[h] + ou[h][CHUNK:]
            o_ref[0, :, sl[h]] = (r[h][CHUNK:] + ou[h][:CHUNK]).astype(o_ref.dtype)

    t = _unit_tri_inverse(a, ri, ci, interleave=recurrence_tail)
    rhs = [jnp.concatenate([v_ref[:, sl[h]].astype(f32) * beta_c[h], kb[h] * eg[h]], axis=1) for h in hs]
    sol = [_mm(t[h], rhs[h]) for h in hs]
    for h in hs:
        u_scr[h] = sol[h][:, :dk]
        wq_scr[h] = jnp.concatenate([sol[h][:, dk:], q[h] * eg[h]], axis=0).astype(bf16)
        kd_t = (k[h] * jnp.exp(gt[h] - gc_c[h])).T.astype(bf16)
        qk_scr[h] = jnp.concatenate([qk[h], kd_t], axis=0)
        eg_scr[h] = jnp.broadcast_to(jnp.exp(gt[h]), (1, dk))


def _deltanet(qkv, ba, a_log, dt_bias, *, B, S, C, H, G):
    T = qkv.shape[0]
    dn_w = H * LANES
    ncl = S // CHUNK
    ncc = C // CHUNK
    nc = ncl + ncc
    hgs = H // G

    def rowblk(b, d, c):
        cc = jnp.where(d == 0, c, ncc - 1 - c)
        cl = jnp.where(d == 0, c - ncc, nc - 1 - c)
        return jnp.where(c < ncc, B * ncl + b * ncc + cc, b * ncl + cl)

    def grouped(t):
        r = t.shape[0]
        t = t.reshape(r, 4, hgs, G).transpose(2, 0, 1, 3).reshape(hgs, r, 4 * G)
        return jnp.pad(t, ((0, 0), (0, 0), (0, LANES - 4 * G)))

    zeros = jnp.zeros((1, 2 * H), f32)
    ba_g = grouped(ba[:, :4 * H])
    al_g = grouped(jnp.concatenate([zeros, a_log.reshape(1, 2 * H)], 1))
    dt_g = grouped(jnp.concatenate([zeros, dt_bias.reshape(1, 2 * H)], 1))
    body = functools.partial(_dn_body, G=G)

    def blk_in(b, d, c):
        return rowblk(b, d, jnp.minimum(c, nc - 1))

    def blk_out(b, d, c):
        return rowblk(b, d, jnp.maximum(c - 1, 0))

    return pl.pallas_call(
        body,
        grid=(B, 2, hgs, nc + 1),
        in_specs=[pl.BlockSpec((CHUNK, G * LANES), lambda b, d, g, c: (blk_in(b, d, c), g)),
                  pl.BlockSpec((CHUNK, G * LANES), lambda b, d, g, c: (blk_in(b, d, c), hgs + g)),
                  pl.BlockSpec((CHUNK, G * LANES), lambda b, d, g, c: (blk_in(b, d, c), 2 * hgs + g)),
                  pl.BlockSpec((1, CHUNK, LANES), lambda b, d, g, c: (g, blk_in(b, d, c), 0)),
                  pl.BlockSpec((1, 1, LANES), lambda b, d, g, c: (g, 0, 0)),
                  pl.BlockSpec((1, 1, LANES), lambda b, d, g, c: (g, 0, 0))],
        out_specs=pl.BlockSpec((1, CHUNK, G * LANES), lambda b, d, g, c: (d, blk_out(b, d, c), g)),
        out_shape=jax.ShapeDtypeStruct((2, T, dn_w), bf16),
        scratch_shapes=[pltpu.VMEM((G, LANES, LANES), f32),
                        pltpu.VMEM((G, CHUNK, LANES), f32),
                        pltpu.VMEM((G, 2 * CHUNK, LANES), bf16),
                        pltpu.VMEM((G, CHUNK + LANES, CHUNK), bf16),
                        pltpu.VMEM((G, 1, LANES), f32)],
        compiler_params=_cparams(("parallel", "parallel", "parallel", "arbitrary")),
        name="deltanet",
    )(qkv, qkv, qkv, ba_g, al_g, dt_g)


def _pool_ctx_body(x_ref, o_ref, *, gw):
    C = x_ref.shape[0]
    ri = lax.broadcasted_iota(jnp.int32, (C, C), 0)
    ci = lax.broadcasted_iota(jnp.int32, (C, C), 1)
    t = lax.broadcasted_iota(jnp.int32, (C, 1), 0)
    for gi, w in enumerate(POOL_WINDOWS):
        sl = slice(gi * gw, (gi + 1) * gw)
        band = jnp.where((ci >= ri - w // 2) & (ci < ri + w - w // 2), 1.0, 0.0).astype(bf16)
        cnt = (jnp.minimum(t + w - w // 2, C) - jnp.maximum(t - w // 2, 0)).astype(f32)
        x = x_ref[:, sl]
        box = jnp.dot(band, x, preferred_element_type=f32)
        o_ref[:, sl] = (box / cnt - x.astype(f32)).astype(o_ref.dtype)


def _pool_lat_body(x_ref, o_ref, y_scr):
    g = pl.program_id(1)
    S = x_ref.shape[0]
    rows = S // GRID_W
    sb = 4 * GRID_W
    ri = lax.broadcasted_iota(jnp.int32, (sb, sb), 0)
    ci = lax.broadcasted_iota(jnp.int32, (sb, sb), 1)
    cr = ri & (GRID_W - 1)
    cc = ci & (GRID_W - 1)
    same = (ri >> 6) == (ci >> 6)
    col = lax.broadcasted_iota(jnp.int32, (GRID_W, 1), 0)
    for gi, w in enumerate(POOL_WINDOWS):
        @pl.when(g == gi)
        def _(w=w):
            band = jnp.where(same & (cc >= cr - w // 2) & (cc < cr + w - w // 2), 1.0, 0.0).astype(bf16)
            for s in range(S // sb):
                y_scr[s * sb:(s + 1) * sb, :] = jnp.dot(band, x_ref[s * sb:(s + 1) * sb, :],
                                                        preferred_element_type=f32)
            ncol = (jnp.minimum(col + w - w // 2, GRID_W) - jnp.maximum(col - w // 2, 0)).astype(f32)
            for r in range(rows):
                lo = max(r - w // 2, 0)
                hi = min(r + w - w // 2, rows)
                acc = y_scr[lo * GRID_W:(lo + 1) * GRID_W, :]
                for rr in range(lo + 1, hi):
                    acc = acc + y_scr[rr * GRID_W:(rr + 1) * GRID_W, :]
                cnt = ncol * float(hi - lo)
                xs = x_ref[r * GRID_W:(r + 1) * GRID_W, :].astype(f32)
                o_ref[r * GRID_W:(r + 1) * GRID_W, :] = (acc / cnt - xs).astype(o_ref.dtype)


def _pool(z, *, B, S, C, pool_off_blocks, pool_w, gw):
    nlat_c = B * S // C
    ng = pool_w // gw
    assert ng == len(POOL_WINDOWS)
    pooled_ctx = pl.pallas_call(
        functools.partial(_pool_ctx_body, gw=gw),
        grid=(B,),
        in_specs=[pl.BlockSpec((C, pool_w), lambda b: (nlat_c + b, pool_off_blocks))],
        out_specs=pl.BlockSpec((C, pool_w), lambda b: (b, 0)),
        out_shape=jax.ShapeDtypeStruct((B * C, pool_w), bf16),
        compiler_params=_cparams(("parallel",)),
        name="pool_ctx",
    )(z)
    off = pool_off_blocks * ng
    pooled_lat = pl.pallas_call(
        _pool_lat_body,
        grid=(B, ng),
        in_specs=[pl.BlockSpec((S, gw), lambda b, g: (b, off + g))],
        out_specs=pl.BlockSpec((S, gw), lambda b, g: (b, g)),
        out_shape=jax.ShapeDtypeStruct((B * S, pool_w), bf16),
        scratch_shapes=[pltpu.VMEM((S, gw), f32)],
        compiler_params=_cparams(("parallel", "parallel")),
        name="pool_lat",
    )(z)
    return pooled_lat, pooled_ctx


def _merge_a_body(o_ref, og_ref, m1_ref, m2_ref, pl_ref, pc_ref, ong_ref, wpool_ref, ps_ref, wud_ref, wup_ref,
                  out_ref, a_scr, p_scr, *, H, gw, nlat):
    is_lat = pl.program_id(0) < nlat
    for h in range(H):
        sl = slice(h * LANES, (h + 1) * LANES)
        o = o_ref[0, :, sl].astype(f32) + o_ref[1, :, sl].astype(f32)
        og = og_ref[:, sl].astype(f32)
        y = o * lax.rsqrt(jnp.mean(o * o, -1, keepdims=True) + EPS) * ong_ref[...] * (og * _sigmoid(og))
        a_scr[:, sl] = y.astype(bf16)
    y_dn = jnp.dot(a_scr[...], wud_ref[0].astype(bf16), preferred_element_type=f32)
    for g in range(wpool_ref.shape[1]):
        sl = slice(g * gw, (g + 1) * gw)
        pooled = jnp.where(is_lat, pl_ref[:, sl], pc_ref[:, sl])
        yp = jnp.dot(pooled, wpool_ref[0, g].astype(bf16), preferred_element_type=f32) * ps_ref[:, sl]
        p_scr[:, sl] = yp.astype(bf16)
    y_pool = jnp.dot(p_scr[...], wup_ref[0].astype(bf16), preferred_element_type=f32)
    g1 = _sigmoid(m1_ref[...].astype(f32))
    g2 = _sigmoid(m2_ref[...].astype(f32))
    out_ref[...] = (g1 * y_dn + g2 * y_pool).astype(out_ref.dtype)


def _merge_a(o_dn, z, pooled_lat, pooled_ctx, onorm_g, w_pool, pool_scale, w_up_dn, w_up_pool, *, layer, n_rows, tm,
             H, D):
    dn_w = H * LANES
    pool_w = pooled_lat.shape[1]
    nlat = pooled_lat.shape[0] // tm
    _, ng, gw, _ = w_pool.shape
    once = pl.Buffered(1)
    og_blk = 3
    m_blk = 4 * dn_w // D
    const = lambda *_: (0, 0)
    return pl.pallas_call(
        functools.partial(_merge_a_body, H=H, gw=gw, nlat=nlat),
        grid=(n_rows // tm,),
        in_specs=[pl.BlockSpec((2, tm, dn_w), lambda i: (0, i, 0)),
                  pl.BlockSpec((tm, dn_w), lambda i: (i, og_blk)),
                  pl.BlockSpec((tm, D), lambda i: (i, m_blk)),
                  pl.BlockSpec((tm, D), lambda i: (i, m_blk + 1)),
                  pl.BlockSpec((tm, pool_w), lambda i: (jnp.minimum(i, nlat - 1), 0)),
                  pl.BlockSpec((tm, pool_w), lambda i: (jnp.maximum(i - nlat, 0), 0)),
                  pl.BlockSpec((1, LANES), const),
                  pl.BlockSpec((1, ng, gw, gw), lambda i: (layer, 0, 0, 0), pipeline_mode=once),
                  pl.BlockSpec((1, pool_w), const),
                  pl.BlockSpec((1, dn_w, D), lambda i: (layer, 0, 0), pipeline_mode=once),
                  pl.BlockSpec((1, pool_w, D), lambda i: (layer, 0, 0), pipeline_mode=once)],
        out_specs=pl.BlockSpec((tm, D), lambda i: (i, 0)),
        out_shape=jax.ShapeDtypeStruct((n_rows, D), bf16),
        scratch_shapes=[pltpu.VMEM((tm, dn_w), bf16), pltpu.VMEM((tm, pool_w), bf16)],
        compiler_params=_cparams(("parallel",)),
        name="merge_a",
    )(o_dn, z, z, z, pooled_lat, pooled_ctx, onorm_g, w_pool, pool_scale, w_up_dn, w_up_pool)


def _merge_b_body(m_ref, h_ref, gt_ref, sc_ref, sh_ref, g_ref, wout_ref, wr_ref, h1_ref, v_ref, lg_ref,
                  *, ds):
    tm = m_ref.shape[0]
    m = jnp.dot(m_ref[...], wout_ref[0].astype(bf16), preferred_element_type=f32)
    h1 = h_ref[...] + gt_ref[0] * m
    h1_ref[...] = h1
    y = h1 * lax.rsqrt(jnp.mean(h1 * h1, -1, keepdims=True) + EPS) * g_ref[...]
    v = y * (1.0 + sc_ref[0]) + sh_ref[0]
    E = lg_ref.shape[1]
    vh = v.astype(bf16)
    vl = (v - vh.astype(f32)).astype(bf16)
    th = jnp.dot(vh, wr_ref[...], preferred_element_type=f32)
    tl = jnp.dot(vl, wr_ref[...], preferred_element_type=f32)
    lg_ref[...] = th[:, :E] + (th[:, E:] + tl[:, :E])
    half = ds * LANES
    for j in range(ds):
        v_ref[pl.ds(j, tm, stride=ds), :] = _pack_pair(v[:, j * LANES:(j + 1) * LANES],
                                                       v[:, half + j * LANES:half + (j + 1) * LANES])


def _merge_b(merged, h, mod, gain2, w_out, w_router, *, layer, n_rows, tm, tpb, nlat, nb):
    D = h.shape[1]
    E = w_router.shape[1]
    ds = D // (2 * LANES)
    const = lambda *_: (0, 0)
    wr_hi = w_router.astype(bf16)
    wr_lo = (w_router - wr_hi.astype(f32)).astype(bf16)
    wr_cat = jnp.concatenate([wr_hi, wr_lo], 1)
    return pl.pallas_call(
        functools.partial(_merge_b_body, ds=ds),
        grid=(n_rows // tm,),
        in_specs=[pl.BlockSpec((tm, D), lambda i: (i, 0)),
                  pl.BlockSpec((tm, D), lambda i: (i, 0)),
                  _mod_spec(2, D, tpb, nlat, nb),
                  _mod_spec(4, D, tpb, nlat, nb),
                  _mod_spec(3, D, tpb, nlat, nb),
                  pl.BlockSpec((1, D), const),
                  pl.BlockSpec((1, D, D), lambda i: (layer, 0, 0), pipeline_mode=pl.Buffered(1)),
                  pl.BlockSpec((D, 2 * E), const)],
        out_specs=[pl.BlockSpec((tm, D), lambda i: (i, 0)),
                   pl.BlockSpec((tm * ds, LANES), lambda i: (i, 0)),
                   pl.BlockSpec((tm, E), lambda i: (i, 0))],
        out_shape=[jax.ShapeDtypeStruct((n_rows, D), f32),
                   jax.ShapeDtypeStruct((n_rows * ds, LANES), jnp.uint32),
                   jax.ShapeDtypeStruct((n_rows, E), f32)],
        compiler_params=_cparams(("parallel",)),
        name="merge_b",
    )(merged, h, mod, mod, mod, gain2, w_out, wr_cat)


def _route_body(lg_ref, bias_ref, idx_ref, gate_ref, cnt_ref, carry_scr, *, n_groups):
    i = pl.program_id(0)
    E, tt = lg_ref.shape

    @pl.when(i == 0)
    def _():
        carry_scr[...] = jnp.zeros_like(carry_scr)

    aff = _sigmoid(lg_ref[...])
    sel = aff + bias_ref[...]
    sub = lax.broadcasted_iota(jnp.int32, (GROUP_SIZE, tt), 0)
    neg = -1e30
    best = None
    for g in range(n_groups):
        s = sel[g * GROUP_SIZE:(g + 1) * GROUP_SIZE]
        a = aff[g * GROUP_SIZE:(g + 1) * GROUP_SIZE]
        m1 = jnp.max(s, axis=0, keepdims=True)
        i1 = jnp.min(jnp.where(s == m1, sub, GROUP_SIZE), axis=0, keepdims=True)
        s2 = jnp.where(sub == i1, neg, s)
        m2 = jnp.max(s2, axis=0, keepdims=True)
        i2 = jnp.min(jnp.where(s2 == m2, sub, GROUP_SIZE), axis=0, keepdims=True)
        a1 = jnp.sum(jnp.where(sub == i1, a, 0.0), axis=0, keepdims=True)
        a2 = jnp.sum(jnp.where(sub == i2, a, 0.0), axis=0, keepdims=True)
        cand = (m1 + m2, i1 + g * GROUP_SIZE, i2 + g * GROUP_SIZE, a1, a2)
        if best is None:
            best = cand
        else:
            better = cand[0] > best[0]
            best = tuple(jnp.where(better, cn, bs) for cn, bs in zip(cand, best))
    _, e1, e2, a1, a2 = best
    den = a1 + a2
    gate_ref[0:1, :] = a1 / den
    gate_ref[1:2, :] = a2 / den
    erow = lax.broadcasted_iota(jnp.int32, (E, tt), 0)
    oh1 = erow == e1
    oh2 = erow == e2
    cnt = jnp.where(oh1 | oh2, 1.0, 0.0)
    ti = lax.broadcasted_iota(jnp.int32, (tt, tt), 0)
    tj = lax.broadcasted_iota(jnp.int32, (tt, tt), 1)
    before = jnp.where(ti < tj, 1.0, 0.0).astype(bf16)
    pos = jnp.dot(cnt.astype(bf16), before, preferred_element_type=f32) + carry_scr[...]
    r1 = jnp.sum(jnp.where(oh1, pos, 0.0), axis=0, keepdims=True)
    r2 = jnp.sum(jnp.where(oh2, pos, 0.0), axis=0, keepdims=True)
    idx_ref[0:1, :] = e1
    idx_ref[1:2, :] = e2
    idx_ref[2:3, :] = r1.astype(jnp.int32)
    idx_ref[3:4, :] = r2.astype(jnp.int32)
    carry_scr[...] = carry_scr[...] + jnp.sum(cnt, axis=1, keepdims=True)
    cnt_ref[...] = jnp.broadcast_to(carry_scr[...], cnt_ref.shape)


def _route(logits_t, bias_col, *, tt):
    E, T = logits_t.shape
    return pl.pallas_call(
        functools.partial(_route_body, n_groups=E // GROUP_SIZE),
        grid=(T // tt,),
        in_specs=[pl.BlockSpec((E, tt), lambda i: (0, i)),
                  pl.BlockSpec((E, 1), lambda i: (0, 0))],
        out_specs=[pl.BlockSpec((4, tt), lambda i: (0, i)),
                   pl.BlockSpec((2, tt), lambda i: (0, i)),
                   pl.BlockSpec((E, LANES), lambda i: (0, 0))],
        out_shape=[jax.ShapeDtypeStruct((4, T), jnp.int32),
                   jax.ShapeDtypeStruct((2, T), f32),
                   jax.ShapeDtypeStruct((E, LANES), f32)],
        scratch_shapes=[pltpu.VMEM((E, 1), f32)],
        compiler_params=_cparams(("arbitrary",)),
        name="route",
    )(logits_t, bias_col)


def _row_copy(src, dst, s, t, n, sem):
    def rows(x):
        return pl.ds(x * n if isinstance(x, int) else pl.multiple_of(x * n, 8), n)

    return pltpu.make_async_copy(src.at[rows(s), :], dst.at[rows(t), :], sem)


def _dispatch_body(dest_ref, pad_ref, v_ref, xs_hbm, zero_scr, sem, zsem, *, tt, ds, n_exp, n_blocks, rb):
    i = pl.program_id(0)

    @pl.when(i == 0)
    def _():
        zero_scr[...] = jnp.zeros_like(zero_scr)

        def pad_copy(row, p):
            return pltpu.make_async_copy(zero_scr.at[pl.ds(0, p * ds), :],
                                         xs_hbm.at[pl.ds(pl.multiple_of(row * ds, ds), p * ds), :], zsem)

        def per_expert(wait):
            def body(e, m):
                row = pad_ref[0, e]
                npad = pad_ref[1, e]
                p = rb // 2
                while p >= 1:
                    hit = (npad & p) != 0

                    @pl.when(hit)
                    def _(row=row, p=p):
                        if wait:
                            pad_copy(row, p).wait()
                        else:
                            pad_copy(row, p).start()

                    row = row + jnp.where(hit, p, 0)
                    p //= 2
                return m

            lax.fori_loop(0, n_exp, body, 0)

        per_expert(False)
        per_expert(True)
        n_used = pad_ref[2, 0]

        def blk_start(bk, m):
            _row_copy(zero_scr, xs_hbm, 0, bk, rb * ds, zsem).start()
            return m

        lax.fori_loop(n_used, n_blocks, blk_start, 0)

        def blk_wait(bk, m):
            _row_copy(zero_scr, xs_hbm, 0, 0, rb * ds, zsem).wait()
            return m

        lax.fori_loop(n_used, n_blocks, blk_wait, 0)

    def issue(r, m):
        src = v_ref.at[pl.ds(pl.multiple_of(r * ds, ds), ds), :]
        for kk in range(2):
            dst = xs_hbm.at[pl.ds(pl.multiple_of(dest_ref[0, kk, r] * ds, ds), ds), :]
            pltpu.make_async_copy(src, dst, sem).start(priority=kk)
        return m

    lax.fori_loop(0, tt, issue, 0, unroll=8)
    for kk in range(2):
        pltpu.make_async_copy(v_ref, xs_hbm.at[pl.ds(0, tt * ds), :], sem).wait()


def _dispatch(dest3, pad_info, v_ts, *, n_tok, n_blocks, tt, ds):
    n_exp = pad_info.shape[1]
    rb = EXPERT_BLOCK
    return pl.pallas_call(
        functools.partial(_dispatch_body, tt=tt, ds=ds, n_exp=n_exp, n_blocks=n_blocks, rb=rb),
        grid=(n_tok // tt,),
        in_specs=[pl.BlockSpec((1, 2, tt), lambda i: (i, 0, 0), memory_space=pltpu.SMEM),
                  pl.BlockSpec(memory_space=pltpu.SMEM),
                  pl.BlockSpec((tt * ds, LANES), lambda i: (i, 0))],
        out_specs=pl.BlockSpec(memory_space=pl.ANY),
        out_shape=jax.ShapeDtypeStruct((n_blocks * rb * ds, LANES), v_ts.dtype),
        scratch_shapes=[pltpu.VMEM((rb * ds, LANES), v_ts.dtype), pltpu.SemaphoreType.DMA(()),
                        pltpu.SemaphoreType.DMA(())],
        compiler_params=_cparams(("arbitrary",)),
        name="dispatch",
    )(dest3, pad_info, v_ts)


def _expert_body(blk0_ref, nblk_ref, nu_ref, x_hbm, wg_ref, wu_ref, wd_ref, y_hbm, x_buf, y_buf, x_s, sem_x, sem_y,
                 *, ds, rb, n_blocks):
    e = pl.program_id(0)
    b0 = blk0_ref[e]
    nb = nblk_ref[e]
    n_used = nu_ref[0]
    rows = rb * ds

    def x_copy(g, slot):
        return pltpu.make_async_copy(x_hbm.at[pl.ds(pl.multiple_of(g * rows, rows), rows), :],
                                     x_buf.at[slot], sem_x.at[slot])

    def y_copy(g, slot):
        return pltpu.make_async_copy(y_buf.at[slot],
                                     y_hbm.at[pl.ds(pl.multiple_of(g * rows, rows), rows), :], sem_y.at[slot])

    @pl.when(jnp.logical_and(e == 0, n_used > 0))
    def _():
        x_copy(0, 0).start()

    def block(i, carry):
        g = b0 + i
        slot = g % 2
        x_copy(g, slot).wait()

        @pl.when(g + 1 < n_used)
        def _():
            x_copy(g + 1, 1 - slot).start()

        @pl.when(g >= 2)
        def _():
            y_copy(g - 2, slot).wait()

        half = ds * LANES
        for j in range(ds):
            lo, hi = _unpack_pair(x_buf[slot, pl.ds(j, rb, stride=ds), :])
            x_s[:, j * LANES:(j + 1) * LANES] = lo.astype(bf16)
            x_s[:, half + j * LANES:half + (j + 1) * LANES] = hi.astype(bf16)
        x = x_s[...]
        hg = jnp.dot(x, wg_ref[0, 0].astype(bf16), preferred_element_type=f32)
        hu = jnp.dot(x, wu_ref[0, 0].astype(bf16), preferred_element_type=f32)
        a = (hg * _sigmoid(hg) * hu).astype(bf16)
        y = jnp.dot(a, wd_ref[0, 0].astype(bf16), preferred_element_type=f32)
        for j in range(ds):
            y_buf[slot, pl.ds(j, rb, stride=ds), :] = _pack_pair(y[:, j * LANES:(j + 1) * LANES],
                                                                 y[:, half + j * LANES:half + (j + 1) * LANES])
        y_copy(g, slot).start()
        return carry

    lax.fori_loop(0, nb, block, 0)

    @pl.when(e == pl.num_programs(0) - 1)
    def _():
        @pl.when(n_used >= 2)
        def _():
            y_copy(n_used - 2, n_used % 2).wait()

        @pl.when(n_used >= 1)
        def _():
            y_copy(n_used - 1, (n_used - 1) % 2).wait()

        y_buf[0] = jnp.zeros(y_buf.shape[1:], y_buf.dtype)

        def start(bk, m):
            _row_copy(y_buf.at[0], y_hbm, 0, bk, rows, sem_y.at[0]).start()
            return m

        lax.fori_loop(n_used, n_blocks, start, 0)

        def wait(bk, m):
            _row_copy(y_buf.at[0], y_hbm, 0, 0, rows, sem_y.at[0]).wait()
            return m

        lax.fori_loop(n_used, n_blocks, wait, 0)


def _experts(blk0, nblk, n_used, xs_ts, w_g, w_u, w_d, *, layer, ds):
    _, E, D, De = w_g.shape
    rb = EXPERT_BLOCK
    n_blocks = xs_ts.shape[0] // (rb * ds)
    grid_spec = pltpu.PrefetchScalarGridSpec(
        num_scalar_prefetch=3,
        grid=(E,),
        in_specs=[pl.BlockSpec(memory_space=pl.ANY),
                  pl.BlockSpec((1, 1, D, De), lambda e, *_: (layer, e, 0, 0)),
                  pl.BlockSpec((1, 1, D, De), lambda e, *_: (layer, e, 0, 0)),
                  pl.BlockSpec((1, 1, De, D), lambda e, *_: (layer, e, 0, 0))],
        out_specs=pl.BlockSpec(memory_space=pl.ANY),
        scratch_shapes=[pltpu.VMEM((2, rb * ds, LANES), xs_ts.dtype), pltpu.VMEM((2, rb * ds, LANES), xs_ts.dtype),
                        pltpu.VMEM((rb, D), bf16), pltpu.SemaphoreType.DMA((2,)), pltpu.SemaphoreType.DMA((2,))],
    )
    return pl.pallas_call(
        functools.partial(_expert_body, ds=ds, rb=rb, n_blocks=n_blocks),
        grid_spec=grid_spec,
        out_shape=jax.ShapeDtypeStruct(xs_ts.shape, xs_ts.dtype),
        compiler_params=_cparams(("arbitrary",)),
        name="experts",
    )(blk0, nblk, n_used, xs_ts, w_g, w_u, w_d)


def _combine_body(dest_ref, dnext_ref, gate_ref, h_ref, gt_ref, fg_ref, y_hbm, o_ref, y_scr, sem, *, tt, ds, final):
    i = pl.program_id(0)
    slot = i % 2

    def gather(d_ref, to_slot):
        def body(r, m):
            for kk in range(2):
                _row_copy(y_hbm, y_scr.at[to_slot, kk], d_ref[0, kk, r], r, ds, sem.at[to_slot]).start(priority=kk)
            return m

        lax.fori_loop(0, tt, body, 0, unroll=8)

    @pl.when(i == 0)
    def _():
        gather(dest_ref, 0)

    @pl.when(i + 1 < pl.num_programs(0))
    def _():
        gather(dnext_ref, 1 - slot)

    for kk in range(2):
        pltpu.make_async_copy(y_hbm.at[pl.ds(0, tt * ds), :], y_scr.at[slot, kk], sem.at[slot]).wait()
    gt = gt_ref[0]
    g0 = gate_ref[:, 0:1]
    g1 = gate_ref[:, 1:2]
    half = ds * LANES
    for j in range(ds):
        lo0, hi0 = _unpack_pair(y_scr[slot, 0, pl.ds(j, tt, stride=ds), :])
        lo1, hi1 = _unpack_pair(y_scr[slot, 1, pl.ds(j, tt, stride=ds), :])
        for base, y0, y1 in ((0, lo0, lo1), (half, hi0, hi1)):
            sl = slice(base + j * LANES, base + (j + 1) * LANES)
            o_ref[:, sl] = h_ref[:, sl] + gt[:, sl] * (g0 * y0 + g1 * y1)
    if final:
        x = o_ref[...]
        o_ref[...] = x * lax.rsqrt(jnp.mean(x * x, -1, keepdims=True) + EPS) * fg_ref[...]


def _combine(dest3, gates_t, h1, mod, final_g, y_ts, *, n_tok, tt, ds, tpb, nlat, nb, final):
    D = h1.shape[1]
    nt = n_tok // tt
    return pl.pallas_call(
        functools.partial(_combine_body, tt=tt, ds=ds, final=final),
        grid=(nt,),
        in_specs=[pl.BlockSpec((1, 2, tt), lambda i: (i, 0, 0), memory_space=pltpu.SMEM),
                  pl.BlockSpec((1, 2, tt), lambda i: (jnp.minimum(i + 1, nt - 1), 0, 0), memory_space=pltpu.SMEM),
                  pl.BlockSpec((tt, 2), lambda i: (i, 0)),
                  pl.BlockSpec((tt, D), lambda i: (i, 0)),
                  _mod_spec(5, D, tpb, nlat, nb),
                  pl.BlockSpec((1, D), lambda i: (0, 0)),
                  pl.BlockSpec(memory_space=pl.ANY)],
        out_specs=pl.BlockSpec((tt, D), lambda i: (i, 0)),
        out_shape=jax.ShapeDtypeStruct((n_tok, D), f32),
        scratch_shapes=[pltpu.VMEM((2, 2, tt * ds, LANES), y_ts.dtype), pltpu.SemaphoreType.DMA((2,))],
        compiler_params=_cparams(("arbitrary",)),
        name="combine",
    )(dest3, dest3, gates_t, h1, mod, final_g, y_ts)


def _moe(v_ts, logits, h1, mod, w_router_bias, w_g, w_u, w_d, final_g, *, layer, n_tok, tt, ds, tpb, nlat, nb,
         final):
    E = w_g.shape[1]
    route_tt = next(t for t in (512, 256, LANES) if n_tok % t == 0)
    idx, gates, counts = _route(logits[:n_tok].T, w_router_bias.reshape(E, 1), tt=route_tt)
    counts = counts[:, 0].astype(jnp.int32)
    padded = (counts + EXPERT_BLOCK - 1) // EXPERT_BLOCK * EXPERT_BLOCK
    pad_end = jnp.cumsum(padded)
    pad_start = pad_end - padded
    n_blocks = (n_tok * 2 + EXPERT_BLOCK - 1) // EXPERT_BLOCK + E
    n_used = (pad_end[-1] // EXPERT_BLOCK).astype(jnp.int32).reshape(1)
    e_sel = idx[0:2][..., None] == jnp.arange(E, dtype=jnp.int32)
    dest = jnp.sum(jnp.where(e_sel, pad_start.astype(jnp.int32), 0), -1) + idx[2:4]
    dest3 = dest.reshape(2, n_tok // tt, tt).transpose(1, 0, 2)
    pad_info = jnp.stack([pad_start + counts, padded - counts, jnp.broadcast_to(n_used, (E,))]).astype(jnp.int32)
    xs_ts = _dispatch(dest3, pad_info, v_ts, n_tok=n_tok, n_blocks=n_blocks, tt=tt, ds=ds)
    y_ts = _experts((pad_start // EXPERT_BLOCK).astype(jnp.int32), (padded // EXPERT_BLOCK).astype(jnp.int32), n_used,
                    xs_ts, w_g, w_u, w_d, layer=layer, ds=ds)
    return _combine(dest3, gates.T, h1, mod, final_g, y_ts, n_tok=n_tok, tt=tt, ds=ds, tpb=tpb, nlat=nlat,
                    nb=nb, final=final)


def kernel(x, c, ctx, c_ctx, w_mod, b_mod, norm1_g, norm2_g, w_in, conv_w, a_log, dt_bias, onorm_g, w_pool, pool_scale, w_up_dn, w_up_pool, w_out, w_router, router_bias, w_gate_e, w_up_e, w_down_e, final_g):
    B, S, D = x.shape
    C = ctx.shape[1]
    L = w_mod.shape[0]
    H = a_log.shape[-1]
    dk = onorm_g.shape[-1]
    assert dk == LANES
    dn_w = H * dk
    _, ng, gw, _ = w_pool.shape
    pool_w = ng * gw
    ds = D // (2 * LANES)
    n_lat = B * S
    T = n_lat + B * C
    tm = 256
    assert S % tm == 0 and (B * C) % tm == 0 and C % CHUNK == 0 and S % (4 * GRID_W) == 0
    tpb = S // tm
    nlat_tiles = n_lat // tm
    tm_in = next(t for t in (1024, 512, 256) if S % t == 0 and (B * C) % t == 0)
    tm_mg = next(t for t in (512, 256) if S % t == 0 and (B * C) % t == 0)
    tr = min(256, C)
    G = min(H, 16)

    c8 = jnp.zeros((8, D), f32).at[:B].set(c).at[B].set(c_ctx)
    mod = _mod_table(c8, w_mod, b_mod)

    qkv_c = 3 * dn_w
    scan_c = qkv_c + 4 * H
    og_off, pool_off = scan_c, scan_c + dn_w
    mg_off = pool_off + pool_w
    assert (4 * dn_w) % D == 0 and (4 * dn_w + 2 * D) % pool_w == 0
    pool_off_blocks = (4 * dn_w + 2 * D) // pool_w

    h = jnp.concatenate([x.reshape(n_lat, D), ctx.reshape(B * C, D)], 0)
    for l in range(L):
        last = l == L - 1
        mod_l = mod[l].reshape(8, 1, mod.shape[-1])
        wl = w_in[l]
        w_z = jnp.concatenate([wl[:, :qkv_c], wl[:, og_off:pool_off], wl[:, mg_off:], wl[:, pool_off:mg_off]],
                              1).astype(bf16)
        w_ba = jnp.pad(wl[:, qkv_c:scan_c], ((0, 0), (0, LANES - 4 * H))).astype(bf16)
        tn = 1024 if w_z.shape[1] % 1024 == 0 else w_z.shape[1]
        z, ba = _in_proj(h, mod_l, 0, norm1_g[l].reshape(1, D), w_z, w_ba, tm=tm_in, tn=tn, tpb=S // tm_in,
                         nlat=n_lat // tm_in, nb=B)
        qkv = _conv_prep(z, conv_w[l], tr=tr, tc=min(2048, dn_w), S=S, C=C, nlat_rows=n_lat, dn_w=dn_w, dk=dk)
        o_dn = _deltanet(qkv, ba, a_log[l], dt_bias[l], B=B, S=S, C=C, H=H, G=G)
        pooled_lat, pooled_ctx = _pool(z, B=B, S=S, C=C, pool_off_blocks=pool_off_blocks, pool_w=pool_w, gw=gw)
        n_rows = n_lat if last else T
        merged = _merge_a(o_dn, z, pooled_lat, pooled_ctx, onorm_g[l].reshape(1, dk), w_pool,
                          pool_scale[l].reshape(1, pool_w), w_up_dn, w_up_pool,
                          layer=l, n_rows=n_rows, tm=tm, H=H, D=D)
        h1, v_ts, logits = _merge_b(merged, h, mod_l, norm2_g[l].reshape(1, D), w_out, w_router,
                                    layer=l, n_rows=n_rows, tm=tm_mg, tpb=S // tm_mg, nlat=n_lat // tm_mg, nb=B)
        h = _moe(v_ts, logits, h1, mod_l, router_bias, w_gate_e, w_up_e, w_down_e,
                 final_g.reshape(1, D), layer=l, n_tok=n_rows, tt=tm, ds=ds, tpb=tpb, nlat=nlat_tiles, nb=B, final=last)
    return h.reshape(B, S, D)
```

```python
import functools

import jax
import jax.numpy as jnp
from jax import lax
from jax.experimental import pallas as pl
from jax.experimental.pallas import tpu as pltpu

f32 = jnp.float32
bf16 = jnp.bfloat16
HIGHEST = lax.Precision.HIGHEST

EPS = 1e-6
LANES = 128
GRID_W = 64
CHUNK = 64
POOL_WINDOWS = (2, 4, 8, 16)
GROUP_SIZE = 8
EXPERT_BLOCK = 256
VMEM_LIMIT = 56 * 1024 * 1024


def _cparams(sem, vmem=VMEM_LIMIT):
    return pltpu.CompilerParams(dimension_semantics=sem, vmem_limit_bytes=vmem)


def _sigmoid(x):
    return 1.0 / (1.0 + jnp.exp(-x))


def _softplus(x):
    return jnp.maximum(x, 0.0) + jnp.log(1.0 + jnp.exp(-jnp.abs(x)))


def _pack_pair(lo, hi):
    def rounded(x):
        u = lax.bitcast_convert_type(x, jnp.uint32)
        return u + jnp.uint32(0x7FFF) + ((u >> 16) & jnp.uint32(1))

    return (rounded(lo) >> 16) | (rounded(hi) & jnp.uint32(0xFFFF0000))


def _unpack_pair(w):
    lo = lax.bitcast_convert_type(w << 16, f32)
    hi = lax.bitcast_convert_type(w & jnp.uint32(0xFFFF0000), f32)
    return lo, hi


def _mod_spec(sec, D, tiles_per_batch, n_lat_tiles, n_batch):
    def index(i, *_):
        return (jnp.where(i < n_lat_tiles, i // tiles_per_batch, n_batch), 0, sec)

    return pl.BlockSpec((1, 1, D), index)


def _mod_body(c_ref, w_ref, b_ref, o_ref):
    c = c_ref[...]
    s = c * _sigmoid(c)
    sh = s.astype(bf16)
    sl = (s - sh.astype(f32)).astype(bf16)
    w = w_ref[0]
    wh = w.astype(bf16)
    wl = (w - wh.astype(f32)).astype(bf16)
    t = jnp.dot(jnp.concatenate([sh, sl], axis=0), wh, preferred_element_type=f32)
    n = s.shape[0]
    o_ref[0] = t[:n] + (t[n:] + jnp.dot(sh, wl, preferred_element_type=f32)) + b_ref[0]


def _mod_table(c8, w_mod, b_mod):
    L, D, N = w_mod.shape
    tn = next(t for t in (1024, 512, 256, LANES) if N % t == 0)
    return pl.pallas_call(
        _mod_body,
        grid=(L, N // tn),
        in_specs=[pl.BlockSpec((8, D), lambda l, j: (0, 0)),
                  pl.BlockSpec((1, D, tn), lambda l, j: (l, 0, j)),
                  pl.BlockSpec((1, 1, tn), lambda l, j: (l, 0, j))],
        out_specs=pl.BlockSpec((1, 8, tn), lambda l, j: (l, 0, j)),
        out_shape=jax.ShapeDtypeStruct((L, 8, N), f32),
        compiler_params=_cparams(("parallel", "parallel")),
        name="mod_table",
    )(c8, w_mod, b_mod.reshape(L, 1, N))


def _inproj_body(h_ref, sc_ref, sh_ref, g_ref, w_ref, wba_ref, z_ref, ba_ref, u_scr):
    j = pl.program_id(1)

    @pl.when(j == 0)
    def _():
        x = h_ref[...]
        y = x * lax.rsqrt(jnp.mean(x * x, -1, keepdims=True) + EPS) * g_ref[...]
        u = (y * (1.0 + sc_ref[0]) + sh_ref[0]).astype(bf16)
        u_scr[...] = u
        ba_ref[...] = jnp.dot(u, wba_ref[...], preferred_element_type=f32)

    z_ref[...] = jnp.dot(u_scr[...], w_ref[...], preferred_element_type=f32).astype(z_ref.dtype)


def _in_proj(h, mod, sec, gain, w_z, w_ba, *, tm, tn, tpb, nlat, nb):
    T, D = h.shape
    Nz = w_z.shape[1]
    return pl.pallas_call(
        _inproj_body,
        grid=(T // tm, Nz // tn),
        in_specs=[pl.BlockSpec((tm, D), lambda i, j: (i, 0)),
                  _mod_spec(sec + 1, D, tpb, nlat, nb),
                  _mod_spec(sec, D, tpb, nlat, nb),
                  pl.BlockSpec((1, D), lambda i, j: (0, 0)),
                  pl.BlockSpec((D, tn), lambda i, j: (0, j)),
                  pl.BlockSpec((D, LANES), lambda i, j: (0, 0))],
        out_specs=[pl.BlockSpec((tm, tn), lambda i, j: (i, j)),
                   pl.BlockSpec((tm, LANES), lambda i, j: (i, 0))],
        out_shape=[jax.ShapeDtypeStruct((T, Nz), bf16), jax.ShapeDtypeStruct((T, LANES), f32)],
        scratch_shapes=[pltpu.VMEM((tm, D), bf16)],
        compiler_params=_cparams(("parallel", "arbitrary")),
        name="in_proj",
    )(h, mod, mod, gain, w_z, w_ba)


def _conv_body(x_ref, xp_ref, xn_ref, w_ref, sh_ref, o_ref, *, tr, tps_lat, tps_ctx, nlat, qscale, tiles_per_sec,
               halo):
    i = pl.program_id(0)
    il =jnp.where(i < nlat, i % tps_lat, (i - nlat) % tps_ctx)
    nt = jnp.where(i < nlat, tps_lat, tps_ctx)
    first = il == 0
    last = il == nt - 1
    xb = x_ref[...]
    w = w_ref[...]
    tc_ = xb.shape[1]
    ys = jnp.dot(sh_ref[...], xb, preferred_element_type=f32).reshape(tr // 8, 32, tc_)
    acc = (ys[:, 0:8] * w[0:1] + ys[:, 8:16] * w[1:2] + ys[:, 16:24] * w[3:4] + ys[:, 24:32] * w[4:5])
    acc = acc.reshape(tr, tc_) + xb.astype(f32) * w[2:3]
    p = jnp.where(first, 0.0, xp_ref[...].astype(f32))[halo - 8:]
    n = jnp.where(last, 0.0, xn_ref[...].astype(f32))[:8]
    r8 = lax.broadcasted_iota(jnp.int32, (8, 1), 0)
    top = jnp.where(r8 == 0, p[6:7] * w[0:1] + p[7:8] * w[1:2], jnp.where(r8 == 1, p[7:8] * w[0:1], 0.0))
    bot = jnp.where(r8 == 7, n[0:1] * w[3:4] + n[1:2] * w[4:5], jnp.where(r8 == 6, n[0:1] * w[4:5], 0.0))
    acc = jnp.concatenate([acc[:8] + top, acc[8:tr - 8], acc[tr - 8:] + bot], axis=0)
    y = acc * _sigmoid(acc)
    sec = pl.program_id(1) // tiles_per_sec

    @pl.when(sec == 2)
    def _():
        o_ref[...] = y.astype(o_ref.dtype)

    @pl.when(sec != 2)
    def _():
        qk_scale = jnp.where(sec == 0, qscale, 1.0)
        for s in range(acc.shape[1] // LANES):
            ys = y[:, s * LANES:(s + 1) * LANES]
            rs = lax.rsqrt(jnp.sum(ys * ys, -1, keepdims=True) + EPS) * qk_scale
            o_ref[:, s * LANES:(s + 1) * LANES] = (ys * rs).astype(o_ref.dtype)


def _conv_prep(z, conv_w, *, tr, tc, S, C, nlat_rows, dn_w, dk):
    T = z.shape[0]
    halo = 16
    hb = tr // halo
    nh = T // halo
    body = functools.partial(_conv_body, tr=tr, tps_lat=S // tr, tps_ctx=C // tr, nlat=nlat_rows // tr,
                             qscale=float(dk) ** -0.5, tiles_per_sec=dn_w // tc, halo=halo)
    assert conv_w.shape[0] == 5
    ri = jnp.arange(tr)[:, None]
    ci = jnp.arange(tr)[None, :]
    shifts = jnp.stack([(ci == ri + off).astype(bf16) for off in (-2, -1, 1, 2)], 0)
    shifts = shifts.reshape(4, tr // 8, 8, tr).transpose(1, 0, 2, 3).reshape(4 * tr, tr)
    return pl.pallas_call(
        body,
        grid=(T // tr, 3 * dn_w // tc),
        in_specs=[pl.BlockSpec((tr, tc), lambda i, j: (i, j)),
                  pl.BlockSpec((halo, tc), lambda i, j: (jnp.maximum(i * hb - 1, 0), j)),
                  pl.BlockSpec((halo, tc), lambda i, j: (jnp.minimum((i + 1) * hb, nh - 1), j)),
                  pl.BlockSpec((conv_w.shape[0], tc), lambda i, j: (0, j)),
                  pl.BlockSpec((4 * tr, tr), lambda i, j: (0, 0))],
        out_specs=pl.BlockSpec((tr, tc), lambda i, j: (i, j)),
        out_shape=jax.ShapeDtypeStruct((T, 3 * dn_w), bf16),
        compiler_params=_cparams(("parallel", "parallel")),
        name="conv_prep",
    )(z, z, z, conv_w, shifts)


def _mm(x, y):
    return jnp.dot(x.astype(bf16), y.astype(bf16), preferred_element_type=f32)


def _unit_tri_inverse(a_list, ri, ci):
    b16 = (ri >> 4) == (ci >> 4)
    b32 = (ri >> 5) == (ci >> 5)
    eye = jnp.where(ri == ci, 1.0, 0.0)
    n = a_list[0].shape[0]

    def mmb(x, y):
        return jnp.dot(x, y, preferred_element_type=f32)

    a16 = [jnp.where(b16, a, 0.0) for a in a_list]
    t = [eye - x for x in a16]
    pb = [x.astype(bf16) for x in a16]
    pb = [mmb(x, x).astype(bf16) for x in pb]
    for step in range(2):
        tp = [mmb(jnp.concatenate([ti.astype(bf16), pi], axis=0), pi) for ti, pi in zip(t, pb)]
        t = [ti + x[:n] for ti, x in zip(t, tp)]
        pb = [x[n:].astype(bf16) for x in tp]
    t = [ti + mmb(ti.astype(bf16), pi) for ti, pi in zip(t, pb)]
    for off in ([jnp.where(b32, a, 0.0) - x for a, x in zip(a_list, a16)],
                [jnp.where(b32, 0.0, a) for a in a_list]):
        tb = [ti.astype(bf16) for ti in t]
        ta = [mmb(ti, ai.astype(bf16)).astype(bf16) for ti, ai in zip(tb, off)]
        t = [ti - mmb(tai, tbi) for ti, tai, tbi in zip(t, ta, tb)]
    return t


def _dn_body(q_ref, k_ref, v_ref, ba_ref, al_ref, dt_ref, o_ref, s_scr, *, G):
    d = pl.program_id(1)
    c = pl.program_id(3)
    dk = LANES
    hs = range(G)

    @pl.when(c == 0)
    def _():
        s_scr[...] = jnp.zeros_like(s_scr)

    ba = ba_ref[0]
    beta_t = _sigmoid(ba)
    g_t = -jnp.exp(al_ref[0]) * _softplus(ba + dt_ref[0])
    ri = lax.broadcasted_iota(jnp.int32, (CHUNK, CHUNK), 0)
    ci = lax.broadcasted_iota(jnp.int32, (CHUNK, CHUNK), 1)
    fwd = d == 0
    later = jnp.where(fwd, ri, ci)
    earlier = jnp.where(fwd, ci, ri)
    incl = later >= earlier
    strict = later > earlier
    tri = jnp.where(incl, 1.0, 0.0)
    gc = jnp.dot(tri, g_t, preferred_element_type=f32, precision=HIGHEST)
    gct = gc.T
    gtot = jnp.sum(g_t, axis=0, keepdims=True)

    def pick(x, col):
        return jnp.where(fwd, x[:, col:col + 1], x[:, G + col:G + col + 1])

    sl = [slice(h * dk, (h + 1) * dk) for h in hs]
    beta_c = [pick(beta_t, h) for h in hs]
    gc_c = [pick(gc, 2 * G + h) for h in hs]
    gc_r = [jnp.where(fwd, gct[2 * G + h:2 * G + h + 1], gct[3 * G + h:3 * G + h + 1]) for h in hs]
    gt = [pick(gtot, 2 * G + h) for h in hs]
    q = [q_ref[:, sl[h]].astype(f32) for h in hs]
    k = [k_ref[:, sl[h]].astype(f32) for h in hs]
    kb = [k[h] * beta_c[h] for h in hs]
    eg = [jnp.exp(gc_c[h]) for h in hs]
    dec = [jnp.where(incl, jnp.exp(jnp.where(incl, gc_c[h] - gc_r[h], 0.0)), 0.0) for h in hs]
    kq = [lax.dot_general(jnp.concatenate([kb[h], q[h]], axis=0).astype(bf16), k_ref[:, sl[h]],
                          (((1,), (1,)), ((), ())), preferred_element_type=f32) for h in hs]
    a = [jnp.where(strict, kq[h][:CHUNK] * dec[h], 0.0) for h in hs]
    t = _unit_tri_inverse(a, ri, ci)
    rhs = [jnp.concatenate([v_ref[:, sl[h]].astype(f32) * beta_c[h], kb[h] * eg[h]], axis=1) for h in hs]
    sol = [_mm(t[h], rhs[h]) for h in hs]
    s = [s_scr[h] for h in hs]
    r = [_mm(jnp.concatenate([sol[h][:, dk:], q[h] * eg[h]], axis=0), s[h]) for h in hs]
    vnb = [(sol[h][:, :dk] - r[h][:CHUNK]).astype(bf16) for h in hs]
    qkd = [jnp.concatenate([(kq[h][CHUNK:] * dec[h]).astype(bf16),
                            (k[h] * jnp.exp(gt[h] - gc_c[h])).T.astype(bf16)], axis=0) for h in hs]
    ou = [jnp.dot(qkd[h], vnb[h], preferred_element_type=f32) for h in hs]
    for h in hs:
        s_scr[h] = s[h] * jnp.exp(gt[h]) + ou[h][CHUNK:]
        o_ref[0, :, sl[h]] = (r[h][CHUNK:] + ou[h][:CHUNK]).astype(o_ref.dtype)


def _deltanet(qkv, ba, a_log, dt_bias, *, B, S, C, H, G):
    T = qkv.shape[0]
    dn_w = H * LANES
    ncl = S // CHUNK
    ncc = C // CHUNK
    nc = ncl + ncc
    hgs = H // G

    def rowblk(b, d, c):
        cc = jnp.where(d == 0, c, ncc - 1 - c)
        cl = jnp.where(d == 0, c - ncc, nc - 1 - c)
        return jnp.where(c < ncc, B * ncl + b * ncc + cc, b * ncl + cl)

    def grouped(t):
        r = t.shape[0]
        t = t.reshape(r, 4, hgs, G).transpose(2, 0, 1, 3).reshape(hgs, r, 4 * G)
        return jnp.pad(t, ((0, 0), (0, 0), (0, LANES - 4 * G)))

    zeros = jnp.zeros((1, 2 * H), f32)
    ba_g = grouped(ba[:, :4 * H])
    al_g = grouped(jnp.concatenate([zeros, a_log.reshape(1, 2 * H)], 1))
    dt_g = grouped(jnp.concatenate([zeros, dt_bias.reshape(1, 2 * H)], 1))
    body = functools.partial(_dn_body, G=G)
    return pl.pallas_call(
        body,
        grid=(B, 2, hgs, nc),
        in_specs=[pl.BlockSpec((CHUNK, G * LANES), lambda b, d, g, c: (rowblk(b, d, c), g)),
                  pl.BlockSpec((CHUNK, G * LANES), lambda b, d, g, c: (rowblk(b, d, c), hgs + g)),
                  pl.BlockSpec((CHUNK, G * LANES), lambda b, d, g, c: (rowblk(b, d, c), 2 * hgs + g)),
                  pl.BlockSpec((1, CHUNK, LANES), lambda b, d, g, c: (g, rowblk(b, d, c), 0)),
                  pl.BlockSpec((1, 1, LANES), lambda b, d, g, c: (g, 0, 0)),
                  pl.BlockSpec((1, 1, LANES), lambda b, d, g, c: (g, 0, 0))],
        out_specs=pl.BlockSpec((1, CHUNK, G * LANES), lambda b, d, g, c: (d, rowblk(b, d, c), g)),
        out_shape=jax.ShapeDtypeStruct((2, T, dn_w), bf16),
        scratch_shapes=[pltpu.VMEM((G, LANES, LANES), f32)],
        compiler_params=_cparams(("parallel", "parallel", "parallel", "arbitrary")),
        name="deltanet",
    )(qkv, qkv, qkv, ba_g, al_g, dt_g)


def _pool_ctx_body(x_ref, o_ref, *, gw):
    C = x_ref.shape[0]
    ri = lax.broadcasted_iota(jnp.int32, (C, C), 0)
    ci = lax.broadcasted_iota(jnp.int32, (C, C), 1)
    t = lax.broadcasted_iota(jnp.int32, (C, 1), 0)
    for gi, w in enumerate(POOL_WINDOWS):
        sl = slice(gi * gw, (gi + 1) * gw)
        band = jnp.where((ci >= ri - w // 2) & (ci < ri + w - w // 2), 1.0, 0.0).astype(bf16)
        cnt = (jnp.minimum(t + w - w // 2, C) - jnp.maximum(t - w // 2, 0)).astype(f32)
        x = x_ref[:, sl]
        box = jnp.dot(band, x, preferred_element_type=f32)
        o_ref[:, sl] = (box / cnt - x.astype(f32)).astype(o_ref.dtype)


def _pool_lat_body(x_ref, o_ref, y_scr):
    g = pl.program_id(1)
    S = x_ref.shape[0]
    rows = S // GRID_W
    sb = 4 * GRID_W
    ri = lax.broadcasted_iota(jnp.int32, (sb, sb), 0)
    ci = lax.broadcasted_iota(jnp.int32, (sb, sb), 1)
    cr = ri & (GRID_W - 1)
    cc = ci & (GRID_W - 1)
    same = (ri >> 6) == (ci >> 6)
    col = lax.broadcasted_iota(jnp.int32, (GRID_W, 1), 0)
    for gi, w in enumerate(POOL_WINDOWS):
        @pl.when(g == gi)
        def _(w=w):
            band = jnp.where(same & (cc >= cr - w // 2) & (cc < cr + w - w // 2), 1.0, 0.0).astype(bf16)
            for s in range(S // sb):
                y_scr[s * sb:(s + 1) * sb, :] = jnp.dot(band, x_ref[s * sb:(s + 1) * sb, :],
                                                        preferred_element_type=f32)
            ncol = (jnp.minimum(col + w - w // 2, GRID_W) - jnp.maximum(col - w // 2, 0)).astype(f32)
            for r in range(rows):
                lo = max(r - w // 2, 0)
                hi = min(r + w - w // 2, rows)
                acc = y_scr[lo * GRID_W:(lo + 1) * GRID_W, :]
                for rr in range(lo + 1, hi):
                    acc = acc + y_scr[rr * GRID_W:(rr + 1) * GRID_W, :]
                cnt = ncol * float(hi - lo)
                xs = x_ref[r * GRID_W:(r + 1) * GRID_W, :].astype(f32)
                o_ref[r * GRID_W:(r + 1) * GRID_W, :] = (acc / cnt - xs).astype(o_ref.dtype)


def _pool(z, *, B, S, C, pool_off_blocks, pool_w, gw):
    nlat_c = B * S // C
    ng = pool_w // gw
    assert ng == len(POOL_WINDOWS)
    pooled_ctx = pl.pallas_call(
        functools.partial(_pool_ctx_body, gw=gw),
        grid=(B,),
        in_specs=[pl.BlockSpec((C, pool_w), lambda b: (nlat_c + b, pool_off_blocks))],
        out_specs=pl.BlockSpec((C, pool_w), lambda b: (b, 0)),
        out_shape=jax.ShapeDtypeStruct((B * C, pool_w), bf16),
        compiler_params=_cparams(("parallel",)),
        name="pool_ctx",
    )(z)
    off = pool_off_blocks * ng
    pooled_lat = pl.pallas_call(
        _pool_lat_body,
        grid=(B, ng),
        in_specs=[pl.BlockSpec((S, gw), lambda b, g: (b, off + g))],
        out_specs=pl.BlockSpec((S, gw), lambda b, g: (b, g)),
        out_shape=jax.ShapeDtypeStruct((B * S, pool_w), bf16),
        scratch_shapes=[pltpu.VMEM((S, gw), f32)],
        compiler_params=_cparams(("parallel", "parallel")),
        name="pool_lat",
    )(z)
    return pooled_lat, pooled_ctx


def _merge_a_body(o_ref, og_ref, m1_ref, m2_ref, pl_ref, pc_ref, ong_ref, wpool_ref, ps_ref, wud_ref, wup_ref,
                  out_ref, a_scr, p_scr, *, H, gw, nlat):
    is_lat = pl.program_id(0) < nlat
    for h in range(H):
        sl = slice(h * LANES, (h + 1) * LANES)
        o = o_ref[0, :, sl].astype(f32) + o_ref[1, :, sl].astype(f32)
        og = og_ref[:, sl].astype(f32)
        y = o * lax.rsqrt(jnp.mean(o * o, -1, keepdims=True) + EPS) * ong_ref[...] * (og * _sigmoid(og))
        a_scr[:, sl] = y.astype(bf16)
    y_dn = jnp.dot(a_scr[...], wud_ref[0].astype(bf16), preferred_element_type=f32)
    for g in range(wpool_ref.shape[1]):
        sl = slice(g * gw, (g + 1) * gw)
        pooled = jnp.where(is_lat, pl_ref[:, sl], pc_ref[:, sl])
        yp = jnp.dot(pooled, wpool_ref[0, g].astype(bf16), preferred_element_type=f32) * ps_ref[:, sl]
        p_scr[:, sl] = yp.astype(bf16)
    y_pool = jnp.dot(p_scr[...], wup_ref[0].astype(bf16), preferred_element_type=f32)
    g1 = _sigmoid(m1_ref[...].astype(f32))
    g2 = _sigmoid(m2_ref[...].astype(f32))
    out_ref[...] = (g1 * y_dn + g2 * y_pool).astype(out_ref.dtype)


def _merge_a(o_dn, z, pooled_lat, pooled_ctx, onorm_g, w_pool, pool_scale, w_up_dn, w_up_pool, *, layer, n_rows, tm,
             H, D):
    dn_w = H * LANES
    pool_w = pooled_lat.shape[1]
    nlat = pooled_lat.shape[0] // tm
    _, ng, gw, _ = w_pool.shape
    once = pl.Buffered(1)
    og_blk = 3
    m_blk = 4 * dn_w // D
    const = lambda *_: (0, 0)
    return pl.pallas_call(
        functools.partial(_merge_a_body, H=H, gw=gw, nlat=nlat),
        grid=(n_rows // tm,),
        in_specs=[pl.BlockSpec((2, tm, dn_w), lambda i: (0, i, 0)),
                  pl.BlockSpec((tm, dn_w), lambda i: (i, og_blk)),
                  pl.BlockSpec((tm, D), lambda i: (i, m_blk)),
                  pl.BlockSpec((tm, D), lambda i: (i, m_blk + 1)),
                  pl.BlockSpec((tm, pool_w), lambda i: (jnp.minimum(i, nlat - 1), 0)),
                  pl.BlockSpec((tm, pool_w), lambda i: (jnp.maximum(i - nlat, 0), 0)),
                  pl.BlockSpec((1, LANES), const),
                  pl.BlockSpec((1, ng, gw, gw), lambda i: (layer, 0, 0, 0), pipeline_mode=once),
                  pl.BlockSpec((1, pool_w), const),
                  pl.BlockSpec((1, dn_w, D), lambda i: (layer, 0, 0), pipeline_mode=once),
                  pl.BlockSpec((1, pool_w, D), lambda i: (layer, 0, 0), pipeline_mode=once)],
        out_specs=pl.BlockSpec((tm, D), lambda i: (i, 0)),
        out_shape=jax.ShapeDtypeStruct((n_rows, D), bf16),
        scratch_shapes=[pltpu.VMEM((tm, dn_w), bf16), pltpu.VMEM((tm, pool_w), bf16)],
        compiler_params=_cparams(("parallel",)),
        name="merge_a",
    )(o_dn, z, z, z, pooled_lat, pooled_ctx, onorm_g, w_pool, pool_scale, w_up_dn, w_up_pool)


def _merge_b_body(m_ref, h_ref, gt_ref, sc_ref, sh_ref, g_ref, wout_ref, wr_ref, h1_ref, v_ref, lg_ref,
                  *, ds):
    tm = m_ref.shape[0]
    m = jnp.dot(m_ref[...], wout_ref[0].astype(bf16), preferred_element_type=f32)
    h1 = h_ref[...] + gt_ref[0] * m
    h1_ref[...] = h1
    y = h1 * lax.rsqrt(jnp.mean(h1 * h1, -1, keepdims=True) + EPS) * g_ref[...]
    v = y * (1.0 + sc_ref[0]) + sh_ref[0]
    E = lg_ref.shape[1]
    vh = v.astype(bf16)
    vl = (v - vh.astype(f32)).astype(bf16)
    th = jnp.dot(vh, wr_ref[...], preferred_element_type=f32)
    tl = jnp.dot(vl, wr_ref[...], preferred_element_type=f32)
    lg_ref[...] = th[:, :E] + (th[:, E:] + tl[:, :E])
    half = ds * LANES
    for j in range(ds):
        v_ref[pl.ds(j, tm, stride=ds), :] = _pack_pair(v[:, j * LANES:(j + 1) * LANES],
                                                       v[:, half + j * LANES:half + (j + 1) * LANES])


def _merge_b(merged, h, mod, gain2, w_out, w_router, *, layer, n_rows, tm, tpb, nlat, nb):
    D = h.shape[1]
    E = w_router.shape[1]
    ds = D // (2 * LANES)
    const = lambda *_: (0, 0)
    wr_hi = w_router.astype(bf16)
    wr_lo = (w_router - wr_hi.astype(f32)).astype(bf16)
    wr_cat = jnp.concatenate([wr_hi, wr_lo], 1)
    return pl.pallas_call(
        functools.partial(_merge_b_body, ds=ds),
        grid=(n_rows // tm,),
        in_specs=[pl.BlockSpec((tm, D), lambda i: (i, 0)),
                  pl.BlockSpec((tm, D), lambda i: (i, 0)),
                  _mod_spec(2, D, tpb, nlat, nb),
                  _mod_spec(4, D, tpb, nlat, nb),
                  _mod_spec(3, D, tpb, nlat, nb),
                  pl.BlockSpec((1, D), const),
                  pl.BlockSpec((1, D, D), lambda i: (layer, 0, 0), pipeline_mode=pl.Buffered(1)),
                  pl.BlockSpec((D, 2 * E), const)],
        out_specs=[pl.BlockSpec((tm, D), lambda i: (i, 0)),
                   pl.BlockSpec((tm * ds, LANES), lambda i: (i, 0)),
                   pl.BlockSpec((tm, E), lambda i: (i, 0))],
        out_shape=[jax.ShapeDtypeStruct((n_rows, D), f32),
                   jax.ShapeDtypeStruct((n_rows * ds, LANES), jnp.uint32),
                   jax.ShapeDtypeStruct((n_rows, E), f32)],
        compiler_params=_cparams(("parallel",)),
        name="merge_b",
    )(merged, h, mod, mod, mod, gain2, w_out, wr_cat)


def _route_body(lg_ref, bias_ref, idx_ref, gate_ref, cnt_ref, carry_scr, *, n_groups):
    i = pl.program_id(0)
    E, tt = lg_ref.shape

    @pl.when(i == 0)
    def _():
        carry_scr[...] = jnp.zeros_like(carry_scr)

    aff = _sigmoid(lg_ref[...])
    sel = aff + bias_ref[...]
    sub = lax.broadcasted_iota(jnp.int32, (GROUP_SIZE, tt), 0)
    neg = -1e30
    best = None
    for g in range(n_groups):
        s = sel[g * GROUP_SIZE:(g + 1) * GROUP_SIZE]
        a = aff[g * GROUP_SIZE:(g + 1) * GROUP_SIZE]
        m1 = jnp.max(s, axis=0, keepdims=True)
        i1 = jnp.min(jnp.where(s == m1, sub, GROUP_SIZE), axis=0, keepdims=True)
        s2 = jnp.where(sub == i1, neg, s)
        m2 = jnp.max(s2, axis=0, keepdims=True)
        i2 = jnp.min(jnp.where(s2 == m2, sub, GROUP_SIZE), axis=0, keepdims=True)
        a1 = jnp.sum(jnp.where(sub == i1, a, 0.0), axis=0, keepdims=True)
        a2 = jnp.sum(jnp.where(sub == i2, a, 0.0), axis=0, keepdims=True)
        cand = (m1 + m2, i1 + g * GROUP_SIZE, i2 + g * GROUP_SIZE, a1, a2)
        if best is None:
            best = cand
        else:
            better = cand[0] > best[0]
            best = tuple(jnp.where(better, cn, bs) for cn, bs in zip(cand, best))
    _, e1, e2, a1, a2 = best
    den = a1 + a2
    gate_ref[0:1, :] = a1 / den
    gate_ref[1:2, :] = a2 / den
    erow = lax.broadcasted_iota(jnp.int32, (E, tt), 0)
    oh1 = erow == e1
    oh2 = erow == e2
    cnt = jnp.where(oh1 | oh2, 1.0, 0.0)
    ti = lax.broadcasted_iota(jnp.int32, (tt, tt), 0)
    tj = lax.broadcasted_iota(jnp.int32, (tt, tt), 1)
    before = jnp.where(ti < tj, 1.0, 0.0).astype(bf16)
    pos = jnp.dot(cnt.astype(bf16), before, preferred_element_type=f32) + carry_scr[...]
    r1 = jnp.sum(jnp.where(oh1, pos, 0.0), axis=0, keepdims=True)
    r2 = jnp.sum(jnp.where(oh2, pos, 0.0), axis=0, keepdims=True)
    idx_ref[0:1, :] = e1
    idx_ref[1:2, :] = e2
    idx_ref[2:3, :] = r1.astype(jnp.int32)
    idx_ref[3:4, :] = r2.astype(jnp.int32)
    carry_scr[...] = carry_scr[...] + jnp.sum(cnt, axis=1, keepdims=True)
    cnt_ref[...] = jnp.broadcast_to(carry_scr[...], cnt_ref.shape)


def _route(logits_t, bias_col, *, tt):
    E, T = logits_t.shape
    return pl.pallas_call(
        functools.partial(_route_body, n_groups=E // GROUP_SIZE),
        grid=(T // tt,),
        in_specs=[pl.BlockSpec((E, tt), lambda i: (0, i)),
                  pl.BlockSpec((E, 1), lambda i: (0, 0))],
        out_specs=[pl.BlockSpec((4, tt), lambda i: (0, i)),
                   pl.BlockSpec((2, tt), lambda i: (0, i)),
                   pl.BlockSpec((E, LANES), lambda i: (0, 0))],
        out_shape=[jax.ShapeDtypeStruct((4, T), jnp.int32),
                   jax.ShapeDtypeStruct((2, T), f32),
                   jax.ShapeDtypeStruct((E, LANES), f32)],
        scratch_shapes=[pltpu.VMEM((E, 1), f32)],
        compiler_params=_cparams(("arbitrary",)),
        name="route",
    )(logits_t, bias_col)


def _row_copy(src, dst, s, t, n, sem):
    def rows(x):
        return pl.ds(x * n if isinstance(x, int) else pl.multiple_of(x * n, 8), n)

    return pltpu.make_async_copy(src.at[rows(s), :], dst.at[rows(t), :], sem)


def _dispatch_body(dest_ref, pad_ref, v_ref, xs_hbm, zero_scr, sem, zsem, *, tt, ds, n_exp, n_blocks, rb):
    i = pl.program_id(0)

    @pl.when(i == 0)
    def _():
        zero_scr[...] = jnp.zeros_like(zero_scr)

        def pad_copy(row, p):
            return pltpu.make_async_copy(zero_scr.at[pl.ds(0, p * ds), :],
                                         xs_hbm.at[pl.ds(pl.multiple_of(row * ds, ds), p * ds), :], zsem)

        def per_expert(wait):
            def body(e, m):
                row = pad_ref[0, e]
                npad = pad_ref[1, e]
                p = rb // 2
                while p >= 1:
                    hit = (npad & p) != 0

                    @pl.when(hit)
                    def _(row=row, p=p):
                        if wait:
                            pad_copy(row, p).wait()
                        else:
                            pad_copy(row, p).start()

                    row = row + jnp.where(hit, p, 0)
                    p //= 2
                return m

            lax.fori_loop(0, n_exp, body, 0)

        per_expert(False)
        per_expert(True)
        n_used = pad_ref[2, 0]

        def blk_start(bk, m):
            _row_copy(zero_scr, xs_hbm, 0, bk, rb * ds, zsem).start()
            return m

        lax.fori_loop(n_used, n_blocks, blk_start, 0)

        def blk_wait(bk, m):
            _row_copy(zero_scr, xs_hbm, 0, 0, rb * ds, zsem).wait()
            return m

        lax.fori_loop(n_used, n_blocks, blk_wait, 0)

    def issue(r, m):
        src = v_ref.at[pl.ds(pl.multiple_of(r * ds, ds), ds), :]
        for kk in range(2):
            dst = xs_hbm.at[pl.ds(pl.multiple_of(dest_ref[0, kk, r] * ds, ds), ds), :]
            pltpu.make_async_copy(src, dst, sem).start(priority=kk)
        return m

    lax.fori_loop(0, tt, issue, 0, unroll=8)
    for kk in range(2):
        pltpu.make_async_copy(v_ref, xs_hbm.at[pl.ds(0, tt * ds), :], sem).wait()


def _dispatch(dest3, pad_info, v_ts, *, n_tok, n_blocks, tt, ds):
    n_exp = pad_info.shape[1]
    rb = EXPERT_BLOCK
    return pl.pallas_call(
        functools.partial(_dispatch_body, tt=tt, ds=ds, n_exp=n_exp, n_blocks=n_blocks, rb=rb),
        grid=(n_tok // tt,),
        in_specs=[pl.BlockSpec((1, 2, tt), lambda i: (i, 0, 0), memory_space=pltpu.SMEM),
                  pl.BlockSpec(memory_space=pltpu.SMEM),
                  pl.BlockSpec((tt * ds, LANES), lambda i: (i, 0))],
        out_specs=pl.BlockSpec(memory_space=pl.ANY),
        out_shape=jax.ShapeDtypeStruct((n_blocks * rb * ds, LANES), v_ts.dtype),
        scratch_shapes=[pltpu.VMEM((rb * ds, LANES), v_ts.dtype), pltpu.SemaphoreType.DMA(()),
                        pltpu.SemaphoreType.DMA(())],
        compiler_params=_cparams(("arbitrary",)),
        name="dispatch",
    )(dest3, pad_info, v_ts)


def _expert_body(blk0_ref, nblk_ref, nu_ref, x_hbm, wg_ref, wu_ref, wd_ref, y_hbm, x_buf, y_buf, x_s, sem_x, sem_y,
                 *, ds, rb, n_blocks):
    e = pl.program_id(0)
    b0 = blk0_ref[e]
    nb = nblk_ref[e]
    n_used = nu_ref[0]
    rows = rb * ds

    def x_copy(g, slot):
        return pltpu.make_async_copy(x_hbm.at[pl.ds(pl.multiple_of(g * rows, rows), rows), :],
                                     x_buf.at[slot], sem_x.at[slot])

    def y_copy(g, slot):
        return pltpu.make_async_copy(y_buf.at[slot],
                                     y_hbm.at[pl.ds(pl.multiple_of(g * rows, rows), rows), :], sem_y.at[slot])

    @pl.when(jnp.logical_and(e == 0, n_used > 0))
    def _():
        x_copy(0, 0).start()

    def block(i, carry):
        g = b0 + i
        slot = g % 2
        x_copy(g, slot).wait()

        @pl.when(g + 1 < n_used)
        def _():
            x_copy(g + 1, 1 - slot).start()

        @pl.when(g >= 2)
        def _():
            y_copy(g - 2, slot).wait()

        half = ds * LANES
        for j in range(ds):
            lo, hi = _unpack_pair(x_buf[slot, pl.ds(j, rb, stride=ds), :])
            x_s[:, j * LANES:(j + 1) * LANES] = lo.astype(bf16)
            x_s[:, half + j * LANES:half + (j + 1) * LANES] = hi.astype(bf16)
        x = x_s[...]
        hg = jnp.dot(x, wg_ref[0, 0].astype(bf16), preferred_element_type=f32)
        hu = jnp.dot(x, wu_ref[0, 0].astype(bf16), preferred_element_type=f32)
        a = (hg * _sigmoid(hg) * hu).astype(bf16)
        y = jnp.dot(a, wd_ref[0, 0].astype(bf16), preferred_element_type=f32)
        for j in range(ds):
            y_buf[slot, pl.ds(j, rb, stride=ds), :] = _pack_pair(y[:, j * LANES:(j + 1) * LANES],
                                                                 y[:, half + j * LANES:half + (j + 1) * LANES])
        y_copy(g, slot).start()
        return carry

    lax.fori_loop(0, nb, block, 0)

    @pl.when(e == pl.num_programs(0) - 1)
    def _():
        @pl.when(n_used >= 2)
        def _():
            y_copy(n_used - 2, n_used % 2).wait()

        @pl.when(n_used >= 1)
        def _():
            y_copy(n_used - 1, (n_used - 1) % 2).wait()

        y_buf[0] = jnp.zeros(y_buf.shape[1:], y_buf.dtype)

        def start(bk, m):
            _row_copy(y_buf.at[0], y_hbm, 0, bk, rows, sem_y.at[0]).start()
            return m

        lax.fori_loop(n_used, n_blocks, start, 0)

        def wait(bk, m):
            _row_copy(y_buf.at[0], y_hbm, 0, 0, rows, sem_y.at[0]).wait()
            return m

        lax.fori_loop(n_used, n_blocks, wait, 0)


def _experts(blk0, nblk, n_used, xs_ts, w_g, w_u, w_d, *, layer, ds):
    _, E, D, De = w_g.shape
    rb = EXPERT_BLOCK
    n_blocks = xs_ts.shape[0] // (rb * ds)
    grid_spec = pltpu.PrefetchScalarGridSpec(
        num_scalar_prefetch=3,
        grid=(E,),
        in_specs=[pl.BlockSpec(memory_space=pl.ANY),
                  pl.BlockSpec((1, 1, D, De), lambda e, *_: (layer, e, 0, 0)),
                  pl.BlockSpec((1, 1, D, De), lambda e, *_: (layer, e, 0, 0)),
                  pl.BlockSpec((1, 1, De, D), lambda e, *_: (layer, e, 0, 0))],
        out_specs=pl.BlockSpec(memory_space=pl.ANY),
        scratch_shapes=[pltpu.VMEM((2, rb * ds, LANES), xs_ts.dtype), pltpu.VMEM((2, rb * ds, LANES), xs_ts.dtype),
                        pltpu.VMEM((rb, D), bf16), pltpu.SemaphoreType.DMA((2,)), pltpu.SemaphoreType.DMA((2,))],
    )
    return pl.pallas_call(
        functools.partial(_expert_body, ds=ds, rb=rb, n_blocks=n_blocks),
        grid_spec=grid_spec,
        out_shape=jax.ShapeDtypeStruct(xs_ts.shape, xs_ts.dtype),
        compiler_params=_cparams(("arbitrary",)),
        name="experts",
    )(blk0, nblk, n_used, xs_ts, w_g, w_u, w_d)


def _combine_body(dest_ref, dnext_ref, gate_ref, h_ref, gt_ref, fg_ref, y_hbm, o_ref, y_scr, sem, *, tt, ds, final):
    i = pl.program_id(0)
    slot = i % 2

    def gather(d_ref, to_slot):
        def body(r, m):
            for kk in range(2):
                _row_copy(y_hbm, y_scr.at[to_slot, kk], d_ref[0, kk, r], r, ds, sem.at[to_slot]).start(priority=kk)
            return m

        lax.fori_loop(0, tt, body, 0, unroll=8)

    @pl.when(i == 0)
    def _():
        gather(dest_ref, 0)

    @pl.when(i + 1 < pl.num_programs(0))
    def _():
        gather(dnext_ref, 1 - slot)

    for kk in range(2):
        pltpu.make_async_copy(y_hbm.at[pl.ds(0, tt * ds), :], y_scr.at[slot, kk], sem.at[slot]).wait()
    gt = gt_ref[0]
    g0 = gate_ref[:, 0:1]
    g1 = gate_ref[:, 1:2]
    half = ds * LANES
    for j in range(ds):
        lo0, hi0 = _unpack_pair(y_scr[slot, 0, pl.ds(j, tt, stride=ds), :])
        lo1, hi1 = _unpack_pair(y_scr[slot, 1, pl.ds(j, tt, stride=ds), :])
        for base, y0, y1 in ((0, lo0, lo1), (half, hi0, hi1)):
            sl = slice(base + j * LANES, base + (j + 1) * LANES)
            o_ref[:, sl] = h_ref[:, sl] + gt[:, sl] * (g0 * y0 + g1 * y1)
    if final:
        x = o_ref[...]
        o_ref[...] = x * lax.rsqrt(jnp.mean(x * x, -1, keepdims=True) + EPS) * fg_ref[...]


def _combine(dest3, gates_t, h1, mod, final_g, y_ts, *, n_tok, tt, ds, tpb, nlat, nb, final):
    D = h1.shape[1]
    nt = n_tok // tt
    return pl.pallas_call(
        functools.partial(_combine_body, tt=tt, ds=ds, final=final),
        grid=(nt,),
        in_specs=[pl.BlockSpec((1, 2, tt), lambda i: (i, 0, 0), memory_space=pltpu.SMEM),
                  pl.BlockSpec((1, 2, tt), lambda i: (jnp.minimum(i + 1, nt - 1), 0, 0), memory_space=pltpu.SMEM),
                  pl.BlockSpec((tt, 2), lambda i: (i, 0)),
                  pl.BlockSpec((tt, D), lambda i: (i, 0)),
                  _mod_spec(5, D, tpb, nlat, nb),
                  pl.BlockSpec((1, D), lambda i: (0, 0)),
                  pl.BlockSpec(memory_space=pl.ANY)],
        out_specs=pl.BlockSpec((tt, D), lambda i: (i, 0)),
        out_shape=jax.ShapeDtypeStruct((n_tok, D), f32),
        scratch_shapes=[pltpu.VMEM((2, 2, tt * ds, LANES), y_ts.dtype), pltpu.SemaphoreType.DMA((2,))],
        compiler_params=_cparams(("arbitrary",)),
        name="combine",
    )(dest3, dest3, gates_t, h1, mod, final_g, y_ts)


def _moe(v_ts, logits, h1, mod, w_router_bias, w_g, w_u, w_d, final_g, *, layer, n_tok, tt, ds, tpb, nlat, nb,
         final):
    E = w_g.shape[1]
    route_tt = next(t for t in (512, 256, LANES) if n_tok % t == 0)
    idx, gates, counts = _route(logits[:n_tok].T, w_router_bias.reshape(E, 1), tt=route_tt)
    counts = counts[:, 0].astype(jnp.int32)
    padded = (counts + EXPERT_BLOCK - 1) // EXPERT_BLOCK * EXPERT_BLOCK
    pad_end = jnp.cumsum(padded)
    pad_start = pad_end - padded
    n_blocks = (n_tok * 2 + EXPERT_BLOCK - 1) // EXPERT_BLOCK + E
    n_used = (pad_end[-1] // EXPERT_BLOCK).astype(jnp.int32).reshape(1)
    e_sel = idx[0:2][..., None] == jnp.arange(E, dtype=jnp.int32)
    dest = jnp.sum(jnp.where(e_sel, pad_start.astype(jnp.int32), 0), -1) + idx[2:4]
    dest3 = dest.reshape(2, n_tok // tt, tt).transpose(1, 0, 2)
    pad_info = jnp.stack([pad_start + counts, padded - counts, jnp.broadcast_to(n_used, (E,))]).astype(jnp.int32)
    xs_ts = _dispatch(dest3, pad_info, v_ts, n_tok=n_tok, n_blocks=n_blocks, tt=tt, ds=ds)
    y_ts = _experts((pad_start // EXPERT_BLOCK).astype(jnp.int32), (padded // EXPERT_BLOCK).astype(jnp.int32), n_used,
                    xs_ts, w_g, w_u, w_d, layer=layer, ds=ds)
    return _combine(dest3, gates.T, h1, mod, final_g, y_ts, n_tok=n_tok, tt=tt, ds=ds, tpb=tpb, nlat=nlat,
                    nb=nb, final=final)


def kernel(x, c, ctx, c_ctx, w_mod, b_mod, norm1_g, norm2_g, w_in, conv_w, a_log, dt_bias, onorm_g, w_pool, pool_scale, w_up_dn, w_up_pool, w_out, w_router, router_bias, w_gate_e, w_up_e, w_down_e, final_g):
    B, S, D = x.shape
    C = ctx.shape[1]
    L = w_mod.shape[0]
    H = a_log.shape[-1]
    dk = onorm_g.shape[-1]
    assert dk == LANES
    dn_w = H * dk
    _, ng, gw, _ = w_pool.shape
    pool_w = ng * gw
    ds = D // (2 * LANES)
    n_lat = B * S
    T = n_lat + B * C
    tm = 256
    assert S % tm == 0 and (B * C) % tm == 0 and C % CHUNK == 0 and S % (4 * GRID_W) == 0
    tpb = S // tm
    nlat_tiles = n_lat // tm
    tm_in = next(t for t in (1024, 512, 256) if S % t == 0 and (B * C) % t == 0)
    tm_mg = next(t for t in (512, 256) if S % t == 0 and (B * C) % t == 0)
    tr = min(256, C)
    G = min(H, 16)

    c8 = jnp.zeros((8, D), f32).at[:B].set(c).at[B].set(c_ctx)
    mod = _mod_table(c8, w_mod, b_mod)

    qkv_c = 3 * dn_w
    scan_c = qkv_c + 4 * H
    og_off, pool_off = scan_c, scan_c + dn_w
    mg_off = pool_off + pool_w
    assert (4 * dn_w) % D == 0 and (4 * dn_w + 2 * D) % pool_w == 0
    pool_off_blocks = (4 * dn_w + 2 * D) // pool_w

    h = jnp.concatenate([x.reshape(n_lat, D), ctx.reshape(B * C, D)], 0)
    for l in range(L):
        last = l == L - 1
        mod_l = mod[l].reshape(8, 1, mod.shape[-1])
        wl = w_in[l]
        w_z = jnp.concatenate([wl[:, :qkv_c], wl[:, og_off:pool_off], wl[:, mg_off:], wl[:, pool_off:mg_off]],
                              1).astype(bf16)
        w_ba = jnp.pad(wl[:, qkv_c:scan_c], ((0, 0), (0, LANES - 4 * H))).astype(bf16)
        tn = 1024 if w_z.shape[1] % 1024 == 0 else w_z.shape[1]
        z, ba = _in_proj(h, mod_l, 0, norm1_g[l].reshape(1, D), w_z, w_ba, tm=tm_in, tn=tn, tpb=S // tm_in,
                         nlat=n_lat // tm_in, nb=B)
        qkv = _conv_prep(z, conv_w[l], tr=tr, tc=min(2048, dn_w), S=S, C=C, nlat_rows=n_lat, dn_w=dn_w, dk=dk)
        o_dn = _deltanet(qkv, ba, a_log[l], dt_bias[l], B=B, S=S, C=C, H=H, G=G)
        pooled_lat, pooled_ctx = _pool(z, B=B, S=S, C=C, pool_off_blocks=pool_off_blocks, pool_w=pool_w, gw=gw)
        n_rows = n_lat if last else T
        merged = _merge_a(o_dn, z, pooled_lat, pooled_ctx, onorm_g[l].reshape(1, dk), w_pool,
                          pool_scale[l].reshape(1, pool_w), w_up_dn, w_up_pool,
                          layer=l, n_rows=n_rows, tm=tm, H=H, D=D)
        h1, v_ts, logits = _merge_b(merged, h, mod_l, norm2_g[l].reshape(1, D), w_out, w_router,
                                    layer=l, n_rows=n_rows, tm=tm_mg, tpb=S // tm_mg, nlat=n_lat // tm_mg, nb=B)
        h = _moe(v_ts, logits, h1, mod_l, router_bias, w_gate_e, w_up_e, w_down_e,
                 final_g.reshape(1, D), layer=l, n_tok=n_rows, tt=tm, ds=ds, tpb=tpb, nlat=nlat_tiles, nb=B, final=last)
    return h.reshape(B, S, D)
```

```python
import functools

import jax
import jax.numpy as jnp
from jax import lax
from jax.experimental import pallas as pl
from jax.experimental.pallas import tpu as pltpu

f32 = jnp.float32
bf16 = jnp.bfloat16
HIGHEST = lax.Precision.HIGHEST

EPS = 1e-6
LANES = 128
GRID_W = 64
CHUNK = 64
POOL_WINDOWS = (2, 4, 8, 16)
GROUP_SIZE = 8
EXPERT_BLOCK = 256
VMEM_LIMIT = 56 * 1024 * 1024


def _cparams(sem, vmem=VMEM_LIMIT):
    return pltpu.CompilerParams(dimension_semantics=sem, vmem_limit_bytes=vmem)


def _sigmoid(x):
    return 1.0 / (1.0 + jnp.exp(-x))


def _softplus(x):
    return jnp.maximum(x, 0.0) + jnp.log(1.0 + jnp.exp(-jnp.abs(x)))


def _pack_pair(lo, hi):
    def rounded(x):
        u = lax.bitcast_convert_type(x, jnp.uint32)
        return u + jnp.uint32(0x7FFF) + ((u >> 16) & jnp.uint32(1))

    return (rounded(lo) >> 16) | (rounded(hi) & jnp.uint32(0xFFFF0000))


def _unpack_pair(w):
    lo = lax.bitcast_convert_type(w << 16, f32)
    hi = lax.bitcast_convert_type(w & jnp.uint32(0xFFFF0000), f32)
    return lo, hi


def _mod_spec(sec, D, tiles_per_batch, n_lat_tiles, n_batch):
    def index(i, *_):
        return (jnp.where(i < n_lat_tiles, i // tiles_per_batch, n_batch), 0, sec)

    return pl.BlockSpec((1, 1, D), index)


def _mod_body(c_ref, w_ref, b_ref, o_ref):
    c = c_ref[...]
    s = c * _sigmoid(c)
    sh = s.astype(bf16)
    sl = (s - sh.astype(f32)).astype(bf16)
    w = w_ref[0]
    wh = w.astype(bf16)
    wl = (w - wh.astype(f32)).astype(bf16)
    t = jnp.dot(jnp.concatenate([sh, sl], axis=0), wh, preferred_element_type=f32)
    n = s.shape[0]
    o_ref[0] = t[:n] + (t[n:] + jnp.dot(sh, wl, preferred_element_type=f32)) + b_ref[0]


def _mod_table(c8, w_mod, b_mod):
    L, D, N = w_mod.shape
    tn = next(t for t in (1024, 512, 256, LANES) if N % t == 0)
    return pl.pallas_call(
        _mod_body,
        grid=(L, N // tn),
        in_specs=[pl.BlockSpec((8, D), lambda l, j: (0, 0)),
                  pl.BlockSpec((1, D, tn), lambda l, j: (l, 0, j)),
                  pl.BlockSpec((1, 1, tn), lambda l, j: (l, 0, j))],
        out_specs=pl.BlockSpec((1, 8, tn), lambda l, j: (l, 0, j)),
        out_shape=jax.ShapeDtypeStruct((L, 8, N), f32),
        compiler_params=_cparams(("parallel", "parallel")),
        name="mod_table",
    )(c8, w_mod, b_mod.reshape(L, 1, N))


def _inproj_body(h_ref, sc_ref, sh_ref, g_ref, w_ref, wba_ref, z_ref, ba_ref, u_scr):
    j = pl.program_id(1)

    @pl.when(j == 0)
    def _():
        x = h_ref[...]
        y = x * lax.rsqrt(jnp.mean(x * x, -1, keepdims=True) + EPS) * g_ref[...]
        u = (y * (1.0 + sc_ref[0]) + sh_ref[0]).astype(bf16)
        u_scr[...] = u
        ba_ref[...] = jnp.dot(u, wba_ref[...], preferred_element_type=f32)

    z_ref[...] = jnp.dot(u_scr[...], w_ref[...], preferred_element_type=f32).astype(z_ref.dtype)


def _in_proj(h, mod, sec, gain, w_z, w_ba, *, tm, tn, tpb, nlat, nb):
    T, D = h.shape
    Nz = w_z.shape[1]
    return pl.pallas_call(
        _inproj_body,
        grid=(T // tm, Nz // tn),
        in_specs=[pl.BlockSpec((tm, D), lambda i, j: (i, 0)),
                  _mod_spec(sec + 1, D, tpb, nlat, nb),
                  _mod_spec(sec, D, tpb, nlat, nb),
                  pl.BlockSpec((1, D), lambda i, j: (0, 0)),
                  pl.BlockSpec((D, tn), lambda i, j: (0, j)),
                  pl.BlockSpec((D, LANES), lambda i, j: (0, 0))],
        out_specs=[pl.BlockSpec((tm, tn), lambda i, j: (i, j)),
                   pl.BlockSpec((tm, LANES), lambda i, j: (i, 0))],
        out_shape=[jax.ShapeDtypeStruct((T, Nz), bf16), jax.ShapeDtypeStruct((T, LANES), f32)],
        scratch_shapes=[pltpu.VMEM((tm, D), bf16)],
        compiler_params=_cparams(("parallel", "arbitrary")),
        name="in_proj",
    )(h, mod, mod, gain, w_z, w_ba)


def _conv_body(x_ref, xp_ref, xn_ref, w_ref, sh_ref, o_ref, *, tr, tps_lat, tps_ctx, nlat, qscale, tiles_per_sec,
               halo):
    i = pl.program_id(0)
    il =jnp.where(i < nlat, i % tps_lat, (i - nlat) % tps_ctx)
    nt = jnp.where(i < nlat, tps_lat, tps_ctx)
    first = il == 0
    last = il == nt - 1
    xb = x_ref[...]
    w = w_ref[...]
    tc_ = xb.shape[1]
    ys = jnp.dot(sh_ref[...], xb, preferred_element_type=f32).reshape(tr // 8, 32, tc_)
    acc = (ys[:, 0:8] * w[0:1] + ys[:, 8:16] * w[1:2] + ys[:, 16:24] * w[3:4] + ys[:, 24:32] * w[4:5])
    acc = acc.reshape(tr, tc_) + xb.astype(f32) * w[2:3]
    p = jnp.where(first, 0.0, xp_ref[...].astype(f32))[halo - 8:]
    n = jnp.where(last, 0.0, xn_ref[...].astype(f32))[:8]
    r8 = lax.broadcasted_iota(jnp.int32, (8, 1), 0)
    top = jnp.where(r8 == 0, p[6:7] * w[0:1] + p[7:8] * w[1:2], jnp.where(r8 == 1, p[7:8] * w[0:1], 0.0))
    bot = jnp.where(r8 == 7, n[0:1] * w[3:4] + n[1:2] * w[4:5], jnp.where(r8 == 6, n[0:1] * w[4:5], 0.0))
    acc = jnp.concatenate([acc[:8] + top, acc[8:tr - 8], acc[tr - 8:] + bot], axis=0)
    y = acc * _sigmoid(acc)
    sec = pl.program_id(1) // tiles_per_sec

    @pl.when(sec == 2)
    def _():
        o_ref[...] = y.astype(o_ref.dtype)

    @pl.when(sec != 2)
    def _():
        qk_scale = jnp.where(sec == 0, qscale, 1.0)
        for s in range(acc.shape[1] // LANES):
            ys = y[:, s * LANES:(s + 1) * LANES]
            rs = lax.rsqrt(jnp.sum(ys * ys, -1, keepdims=True) + EPS) * qk_scale
            o_ref[:, s * LANES:(s + 1) * LANES] = (ys * rs).astype(o_ref.dtype)


def _conv_prep(z, conv_w, *, tr, tc, S, C, nlat_rows, dn_w, dk):
    T = z.shape[0]
    halo = 16
    hb = tr // halo
    nh = T // halo
    body = functools.partial(_conv_body, tr=tr, tps_lat=S // tr, tps_ctx=C // tr, nlat=nlat_rows // tr,
                             qscale=float(dk) ** -0.5, tiles_per_sec=dn_w // tc, halo=halo)
    assert conv_w.shape[0] == 5
    ri = jnp.arange(tr)[:, None]
    ci = jnp.arange(tr)[None, :]
    shifts = jnp.stack([(ci == ri + off).astype(bf16) for off in (-2, -1, 1, 2)], 0)
    shifts = shifts.reshape(4, tr // 8, 8, tr).transpose(1, 0, 2, 3).reshape(4 * tr, tr)
    return pl.pallas_call(
        body,
        grid=(T // tr, 3 * dn_w // tc),
        in_specs=[pl.BlockSpec((tr, tc), lambda i, j: (i, j)),
                  pl.BlockSpec((halo, tc), lambda i, j: (jnp.maximum(i * hb - 1, 0), j)),
                  pl.BlockSpec((halo, tc), lambda i, j: (jnp.minimum((i + 1) * hb, nh - 1), j)),
                  pl.BlockSpec((conv_w.shape[0], tc), lambda i, j: (0, j)),
                  pl.BlockSpec((4 * tr, tr), lambda i, j: (0, 0))],
        out_specs=pl.BlockSpec((tr, tc), lambda i, j: (i, j)),
        out_shape=jax.ShapeDtypeStruct((T, 3 * dn_w), bf16),
        compiler_params=_cparams(("parallel", "parallel")),
        name="conv_prep",
    )(z, z, z, conv_w, shifts)


def _mm(x, y):
    return jnp.dot(x.astype(bf16), y.astype(bf16), preferred_element_type=f32)


def _unit_tri_inverse(a_list, ri, ci):
    b16 = (ri >> 4) == (ci >> 4)
    b32 = (ri >> 5) == (ci >> 5)
    eye = jnp.where(ri == ci, 1.0, 0.0)
    n = a_list[0].shape[0]

    def mmb(x, y):
        return jnp.dot(x, y, preferred_element_type=f32)

    a16 = [jnp.where(b16, a, 0.0) for a in a_list]
    t = [eye - x for x in a16]
    pb = [x.astype(bf16) for x in a16]
    pb = [mmb(x, x).astype(bf16) for x in pb]
    for step in range(2):
        tp = [mmb(jnp.concatenate([ti.astype(bf16), pi], axis=0), pi) for ti, pi in zip(t, pb)]
        t = [ti + x[:n] for ti, x in zip(t, tp)]
        pb = [x[n:].astype(bf16) for x in tp]
    t = [ti + mmb(ti.astype(bf16), pi) for ti, pi in zip(t, pb)]
    for off in ([jnp.where(b32, a, 0.0) - x for a, x in zip(a_list, a16)],
                [jnp.where(b32, 0.0, a) for a in a_list]):
        tb = [ti.astype(bf16) for ti in t]
        ta = [mmb(ti, ai.astype(bf16)).astype(bf16) for ti, ai in zip(tb, off)]
        t = [ti - mmb(tai, tbi) for ti, tai, tbi in zip(t, ta, tb)]
    return t


def _dn_body(q_ref, k_ref, v_ref, ba_ref, al_ref, dt_ref, o_ref, s_scr, *, G):
    d = pl.program_id(1)
    c = pl.program_id(3)
    dk = LANES
    hs = range(G)

    @pl.when(c == 0)
    def _():
        s_scr[...] = jnp.zeros_like(s_scr)

    ba = ba_ref[0]
    beta_t = _sigmoid(ba)
    g_t = -jnp.exp(al_ref[0]) * _softplus(ba + dt_ref[0])
    ri = lax.broadcasted_iota(jnp.int32, (CHUNK, CHUNK), 0)
    ci = lax.broadcasted_iota(jnp.int32, (CHUNK, CHUNK), 1)
    fwd = d == 0
    later = jnp.where(fwd, ri, ci)
    earlier = jnp.where(fwd, ci, ri)
    incl = later >= earlier
    strict = later > earlier
    tri = jnp.where(incl, 1.0, 0.0)
    gc = jnp.dot(tri, g_t, preferred_element_type=f32, precision=HIGHEST)
    gct = gc.T
    gtot = jnp.sum(g_t, axis=0, keepdims=True)

    def pick(x, col):
        return jnp.where(fwd, x[:, col:col + 1], x[:, G + col:G + col + 1])

    sl = [slice(h * dk, (h + 1) * dk) for h in hs]
    beta_c = [pick(beta_t, h) for h in hs]
    gc_c = [pick(gc, 2 * G + h) for h in hs]
    gc_r = [jnp.where(fwd, gct[2 * G + h:2 * G + h + 1], gct[3 * G + h:3 * G + h + 1]) for h in hs]
    gt = [pick(gtot, 2 * G + h) for h in hs]
    q = [q_ref[:, sl[h]].astype(f32) for h in hs]
    k = [k_ref[:, sl[h]].astype(f32) for h in hs]
    kb = [k[h] * beta_c[h] for h in hs]
    eg = [jnp.exp(gc_c[h]) for h in hs]
    dec = [jnp.where(incl, jnp.exp(jnp.where(incl, gc_c[h] - gc_r[h], 0.0)), 0.0) for h in hs]
    kq = [lax.dot_general(jnp.concatenate([kb[h], q[h]], axis=0).astype(bf16), k_ref[:, sl[h]],
                          (((1,), (1,)), ((), ())), preferred_element_type=f32) for h in hs]
    a = [jnp.where(strict, kq[h][:CHUNK] * dec[h], 0.0) for h in hs]
    t = _unit_tri_inverse(a, ri, ci)
    rhs = [jnp.concatenate([v_ref[:, sl[h]].astype(f32) * beta_c[h], kb[h] * eg[h]], axis=1) for h in hs]
    sol = [_mm(t[h], rhs[h]) for h in hs]
    s = [s_scr[h] for h in hs]
    r = [_mm(jnp.concatenate([sol[h][:, dk:], q[h] * eg[h]], axis=0), s[h]) for h in hs]
    vnb = [(sol[h][:, :dk] - r[h][:CHUNK]).astype(bf16) for h in hs]
    qkd = [jnp.concatenate([(kq[h][CHUNK:] * dec[h]).astype(bf16),
                            (k[h] * jnp.exp(gt[h] - gc_c[h])).T.astype(bf16)], axis=0) for h in hs]
    ou = [jnp.dot(qkd[h], vnb[h], preferred_element_type=f32) for h in hs]
    for h in hs:
        s_scr[h] = s[h] * jnp.exp(gt[h]) + ou[h][CHUNK:]
        o_ref[0, :, sl[h]] = (r[h][CHUNK:] + ou[h][:CHUNK]).astype(o_ref.dtype)


def _deltanet(qkv, ba, a_log, dt_bias, *, B, S, C, H, G):
    T = qkv.shape[0]
    dn_w = H * LANES
    ncl = S // CHUNK
    ncc = C // CHUNK
    nc = ncl + ncc
    hgs = H // G

    def rowblk(b, d, c):
        cc = jnp.where(d == 0, c, ncc - 1 - c)
        cl = jnp.where(d == 0, c - ncc, nc - 1 - c)
        return jnp.where(c < ncc, B * ncl + b * ncc + cc, b * ncl + cl)

    def grouped(t):
        r = t.shape[0]
        t = t.reshape(r, 4, hgs, G).transpose(2, 0, 1, 3).reshape(hgs, r, 4 * G)
        return jnp.pad(t, ((0, 0), (0, 0), (0, LANES - 4 * G)))

    zeros = jnp.zeros((1, 2 * H), f32)
    ba_g = grouped(ba[:, :4 * H])
    al_g = grouped(jnp.concatenate([zeros, a_log.reshape(1, 2 * H)], 1))
    dt_g = grouped(jnp.concatenate([zeros, dt_bias.reshape(1, 2 * H)], 1))
    body = functools.partial(_dn_body, G=G)
    return pl.pallas_call(
        body,
        grid=(B, 2, hgs, nc),
        in_specs=[pl.BlockSpec((CHUNK, G * LANES), lambda b, d, g, c: (rowblk(b, d, c), g)),
                  pl.BlockSpec((CHUNK, G * LANES), lambda b, d, g, c: (rowblk(b, d, c), hgs + g)),
                  pl.BlockSpec((CHUNK, G * LANES), lambda b, d, g, c: (rowblk(b, d, c), 2 * hgs + g)),
                  pl.BlockSpec((1, CHUNK, LANES), lambda b, d, g, c: (g, rowblk(b, d, c), 0)),
                  pl.BlockSpec((1, 1, LANES), lambda b, d, g, c: (g, 0, 0)),
                  pl.BlockSpec((1, 1, LANES), lambda b, d, g, c: (g, 0, 0))],
        out_specs=pl.BlockSpec((1, CHUNK, G * LANES), lambda b, d, g, c: (d, rowblk(b, d, c), g)),
        out_shape=jax.ShapeDtypeStruct((2, T, dn_w), bf16),
        scratch_shapes=[pltpu.VMEM((G, LANES, LANES), f32)],
        compiler_params=_cparams(("parallel", "parallel", "parallel", "arbitrary")),
        name="deltanet",
    )(qkv, qkv, qkv, ba_g, al_g, dt_g)


def _pool_ctx_body(x_ref, o_ref, *, gw):
    C = x_ref.shape[0]
    ri = lax.broadcasted_iota(jnp.int32, (C, C), 0)
    ci = lax.broadcasted_iota(jnp.int32, (C, C), 1)
    t = lax.broadcasted_iota(jnp.int32, (C, 1), 0)
    for gi, w in enumerate(POOL_WINDOWS):
        sl = slice(gi * gw, (gi + 1) * gw)
        band = jnp.where((ci >= ri - w // 2) & (ci < ri + w - w // 2), 1.0, 0.0).astype(bf16)
        cnt = (jnp.minimum(t + w - w // 2, C) - jnp.maximum(t - w // 2, 0)).astype(f32)
        x = x_ref[:, sl]
        box = jnp.dot(band, x, preferred_element_type=f32)
        o_ref[:, sl] = (box / cnt - x.astype(f32)).astype(o_ref.dtype)


def _pool_lat_body(x_ref, o_ref, y_scr):
    g = pl.program_id(1)
    S = x_ref.shape[0]
    rows = S // GRID_W
    sb = 4 * GRID_W
    ri = lax.broadcasted_iota(jnp.int32, (sb, sb), 0)
    ci = lax.broadcasted_iota(jnp.int32, (sb, sb), 1)
    cr = ri & (GRID_W - 1)
    cc = ci & (GRID_W - 1)
    same = (ri >> 6) == (ci >> 6)
    col = lax.broadcasted_iota(jnp.int32, (GRID_W, 1), 0)
    for gi, w in enumerate(POOL_WINDOWS):
        @pl.when(g == gi)
        def _(w=w):
            band = jnp.where(same & (cc >= cr - w // 2) & (cc < cr + w - w // 2), 1.0, 0.0).astype(bf16)
            for s in range(S // sb):
                y_scr[s * sb:(s + 1) * sb, :] = jnp.dot(band, x_ref[s * sb:(s + 1) * sb, :],
                                                        preferred_element_type=f32)
            ncol = (jnp.minimum(col + w - w // 2, GRID_W) - jnp.maximum(col - w // 2, 0)).astype(f32)
            for r in range(rows):
                lo = max(r - w // 2, 0)
                hi = min(r + w - w // 2, rows)
                acc = y_scr[lo * GRID_W:(lo + 1) * GRID_W, :]
                for rr in range(lo + 1, hi):
                    acc = acc + y_scr[rr * GRID_W:(rr + 1) * GRID_W, :]
                cnt = ncol * float(hi - lo)
                xs = x_ref[r * GRID_W:(r + 1) * GRID_W, :].astype(f32)
                o_ref[r * GRID_W:(r + 1) * GRID_W, :] = (acc / cnt - xs).astype(o_ref.dtype)


def _pool(z, *, B, S, C, pool_off_blocks, pool_w, gw):
    nlat_c = B * S // C
    ng = pool_w // gw
    assert ng == len(POOL_WINDOWS)
    pooled_ctx = pl.pallas_call(
        functools.partial(_pool_ctx_body, gw=gw),
        grid=(B,),
        in_specs=[pl.BlockSpec((C, pool_w), lambda b: (nlat_c + b, pool_off_blocks))],
        out_specs=pl.BlockSpec((C, pool_w), lambda b: (b, 0)),
        out_shape=jax.ShapeDtypeStruct((B * C, pool_w), bf16),
        compiler_params=_cparams(("parallel",)),
        name="pool_ctx",
    )(z)
    off = pool_off_blocks * ng
    pooled_lat = pl.pallas_call(
        _pool_lat_body,
        grid=(B, ng),
        in_specs=[pl.BlockSpec((S, gw), lambda b, g: (b, off + g))],
        out_specs=pl.BlockSpec((S, gw), lambda b, g: (b, g)),
        out_shape=jax.ShapeDtypeStruct((B * S, pool_w), bf16),
        scratch_shapes=[pltpu.VMEM((S, gw), f32)],
        compiler_params=_cparams(("parallel", "parallel")),
        name="pool_lat",
    )(z)
    return pooled_lat, pooled_ctx


def _merge_a_body(o_ref, og_ref, m1_ref, m2_ref, pl_ref, pc_ref, ong_ref, wpool_ref, ps_ref, wud_ref, wup_ref,
                  out_ref, a_scr, p_scr, *, H, gw, nlat):
    is_lat = pl.program_id(0) < nlat
    for h in range(H):
        sl = slice(h * LANES, (h + 1) * LANES)
        o = o_ref[0, :, sl].astype(f32) + o_ref[1, :, sl].astype(f32)
        og = og_ref[:, sl].astype(f32)
        y = o * lax.rsqrt(jnp.mean(o * o, -1, keepdims=True) + EPS) * ong_ref[...] * (og * _sigmoid(og))
        a_scr[:, sl] = y.astype(bf16)
    y_dn = jnp.dot(a_scr[...], wud_ref[0].astype(bf16), preferred_element_type=f32)
    for g in range(wpool_ref.shape[1]):
        sl = slice(g * gw, (g + 1) * gw)
        pooled = jnp.where(is_lat, pl_ref[:, sl], pc_ref[:, sl])
        yp = jnp.dot(pooled, wpool_ref[0, g].astype(bf16), preferred_element_type=f32) * ps_ref[:, sl]
        p_scr[:, sl] = yp.astype(bf16)
    y_pool = jnp.dot(p_scr[...], wup_ref[0].astype(bf16), preferred_element_type=f32)
    g1 = _sigmoid(m1_ref[...].astype(f32))
    g2 = _sigmoid(m2_ref[...].astype(f32))
    out_ref[...] = (g1 * y_dn + g2 * y_pool).astype(out_ref.dtype)


def _merge_a(o_dn, z, pooled_lat, pooled_ctx, onorm_g, w_pool, pool_scale, w_up_dn, w_up_pool, *, layer, n_rows, tm,
             H, D):
    dn_w = H * LANES
    pool_w = pooled_lat.shape[1]
    nlat = pooled_lat.shape[0] // tm
    _, ng, gw, _ = w_pool.shape
    once = pl.Buffered(1)
    og_blk = 3
    m_blk = 4 * dn_w // D
    const = lambda *_: (0, 0)
    return pl.pallas_call(
        functools.partial(_merge_a_body, H=H, gw=gw, nlat=nlat),
        grid=(n_rows // tm,),
        in_specs=[pl.BlockSpec((2, tm, dn_w), lambda i: (0, i, 0)),
                  pl.BlockSpec((tm, dn_w), lambda i: (i, og_blk)),
                  pl.BlockSpec((tm, D), lambda i: (i, m_blk)),
                  pl.BlockSpec((tm, D), lambda i: (i, m_blk + 1)),
                  pl.BlockSpec((tm, pool_w), lambda i: (jnp.minimum(i, nlat - 1), 0)),
                  pl.BlockSpec((tm, pool_w), lambda i: (jnp.maximum(i - nlat, 0), 0)),
                  pl.BlockSpec((1, LANES), const),
                  pl.BlockSpec((1, ng, gw, gw), lambda i: (layer, 0, 0, 0), pipeline_mode=once),
                  pl.BlockSpec((1, pool_w), const),
                  pl.BlockSpec((1, dn_w, D), lambda i: (layer, 0, 0), pipeline_mode=once),
                  pl.BlockSpec((1, pool_w, D), lambda i: (layer, 0, 0), pipeline_mode=once)],
        out_specs=pl.BlockSpec((tm, D), lambda i: (i, 0)),
        out_shape=jax.ShapeDtypeStruct((n_rows, D), bf16),
        scratch_shapes=[pltpu.VMEM((tm, dn_w), bf16), pltpu.VMEM((tm, pool_w), bf16)],
        compiler_params=_cparams(("parallel",)),
        name="merge_a",
    )(o_dn, z, z, z, pooled_lat, pooled_ctx, onorm_g, w_pool, pool_scale, w_up_dn, w_up_pool)


def _merge_b_body(m_ref, h_ref, gt_ref, sc_ref, sh_ref, g_ref, wout_ref, wr_ref, h1_ref, v_ref, lg_ref,
                  *, ds):
    tm = m_ref.shape[0]
    m = jnp.dot(m_ref[...], wout_ref[0].astype(bf16), preferred_element_type=f32)
    h1 = h_ref[...] + gt_ref[0] * m
    h1_ref[...] = h1
    y = h1 * lax.rsqrt(jnp.mean(h1 * h1, -1, keepdims=True) + EPS) * g_ref[...]
    v = y * (1.0 + sc_ref[0]) + sh_ref[0]
    E = lg_ref.shape[1]
    vh = v.astype(bf16)
    vl = (v - vh.astype(f32)).astype(bf16)
    th = jnp.dot(vh, wr_ref[...], preferred_element_type=f32)
    tl = jnp.dot(vl, wr_ref[...], preferred_element_type=f32)
    lg_ref[...] = th[:, :E] + (th[:, E:] + tl[:, :E])
    half = ds * LANES
    for j in range(ds):
        v_ref[pl.ds(j, tm, stride=ds), :] = _pack_pair(v[:, j * LANES:(j + 1) * LANES],
                                                       v[:, half + j * LANES:half + (j + 1) * LANES])


def _merge_b(merged, h, mod, gain2, w_out, w_router, *, layer, n_rows, tm, tpb, nlat, nb):
    D = h.shape[1]
    E = w_router.shape[1]
    ds = D // (2 * LANES)
    const = lambda *_: (0, 0)
    wr_hi = w_router.astype(bf16)
    wr_lo = (w_router - wr_hi.astype(f32)).astype(bf16)
    wr_cat = jnp.concatenate([wr_hi, wr_lo], 1)
    return pl.pallas_call(
        functools.partial(_merge_b_body, ds=ds),
        grid=(n_rows // tm,),
        in_specs=[pl.BlockSpec((tm, D), lambda i: (i, 0)),
                  pl.BlockSpec((tm, D), lambda i: (i, 0)),
                  _mod_spec(2, D, tpb, nlat, nb),
                  _mod_spec(4, D, tpb, nlat, nb),
                  _mod_spec(3, D, tpb, nlat, nb),
                  pl.BlockSpec((1, D), const),
                  pl.BlockSpec((1, D, D), lambda i: (layer, 0, 0), pipeline_mode=pl.Buffered(1)),
                  pl.BlockSpec((D, 2 * E), const)],
        out_specs=[pl.BlockSpec((tm, D), lambda i: (i, 0)),
                   pl.BlockSpec((tm * ds, LANES), lambda i: (i, 0)),
                   pl.BlockSpec((tm, E), lambda i: (i, 0))],
        out_shape=[jax.ShapeDtypeStruct((n_rows, D), f32),
                   jax.ShapeDtypeStruct((n_rows * ds, LANES), jnp.uint32),
                   jax.ShapeDtypeStruct((n_rows, E), f32)],
        compiler_params=_cparams(("parallel",)),
        name="merge_b",
    )(merged, h, mod, mod, mod, gain2, w_out, wr_cat)


def _route_body(lg_ref, bias_ref, idx_ref, gate_ref, cnt_ref, carry_scr, *, n_groups):
    i = pl.program_id(0)
    E, tt = lg_ref.shape

    @pl.when(i == 0)
    def _():
        carry_scr[...] = jnp.zeros_like(carry_scr)

    aff = _sigmoid(lg_ref[...])
    sel = aff + bias_ref[...]
    sub = lax.broadcasted_iota(jnp.int32, (GROUP_SIZE, tt), 0)
    neg = -1e30
    best = None
    for g in range(n_groups):
        s = sel[g * GROUP_SIZE:(g + 1) * GROUP_SIZE]
        a = aff[g * GROUP_SIZE:(g + 1) * GROUP_SIZE]
        m1 = jnp.max(s, axis=0, keepdims=True)
        i1 = jnp.min(jnp.where(s == m1, sub, GROUP_SIZE), axis=0, keepdims=True)
        s2 = jnp.where(sub == i1, neg, s)
        m2 = jnp.max(s2, axis=0, keepdims=True)
        i2 = jnp.min(jnp.where(s2 == m2, sub, GROUP_SIZE), axis=0, keepdims=True)
        a1 = jnp.sum(jnp.where(sub == i1, a, 0.0), axis=0, keepdims=True)
        a2 = jnp.sum(jnp.where(sub == i2, a, 0.0), axis=0, keepdims=True)
        cand = (m1 + m2, i1 + g * GROUP_SIZE, i2 + g * GROUP_SIZE, a1, a2)
        if best is None:
            best = cand
        else:
            better = cand[0] > best[0]
            best = tuple(jnp.where(better, cn, bs) for cn, bs in zip(cand, best))
    _, e1, e2, a1, a2 = best
    den = a1 + a2
    gate_ref[0:1, :] = a1 / den
    gate_ref[1:2, :] = a2 / den
    erow = lax.broadcasted_iota(jnp.int32, (E, tt), 0)
    oh1 = erow == e1
    oh2 = erow == e2
    cnt = jnp.where(oh1 | oh2, 1.0, 0.0)
    ti = lax.broadcasted_iota(jnp.int32, (tt, tt), 0)
    tj = lax.broadcasted_iota(jnp.int32, (tt, tt), 1)
    before = jnp.where(ti < tj, 1.0, 0.0).astype(bf16)
    pos = jnp.dot(cnt.astype(bf16), before, preferred_element_type=f32) + carry_scr[...]
    r1 = jnp.sum(jnp.where(oh1, pos, 0.0), axis=0, keepdims=True)
    r2 = jnp.sum(jnp.where(oh2, pos, 0.0), axis=0, keepdims=True)
    idx_ref[0:1, :] = e1
    idx_ref[1:2, :] = e2
    idx_ref[2:3, :] = r1.astype(jnp.int32)
    idx_ref[3:4, :] = r2.astype(jnp.int32)
    carry_scr[...] = carry_scr[...] + jnp.sum(cnt, axis=1, keepdims=True)
    cnt_ref[...] = jnp.broadcast_to(carry_scr[...], cnt_ref.shape)


def _route(logits_t, bias_col, *, tt):
    E, T = logits_t.shape
    return pl.pallas_call(
        functools.partial(_route_body, n_groups=E // GROUP_SIZE),
        grid=(T // tt,),
        in_specs=[pl.BlockSpec((E, tt), lambda i: (0, i)),
                  pl.BlockSpec((E, 1), lambda i: (0, 0))],
        out_specs=[pl.BlockSpec((4, tt), lambda i: (0, i)),
                   pl.BlockSpec((2, tt), lambda i: (0, i)),
                   pl.BlockSpec((E, LANES), lambda i: (0, 0))],
        out_shape=[jax.ShapeDtypeStruct((4, T), jnp.int32),
                   jax.ShapeDtypeStruct((2, T), f32),
                   jax.ShapeDtypeStruct((E, LANES), f32)],
        scratch_shapes=[pltpu.VMEM((E, 1), f32)],
        compiler_params=_cparams(("arbitrary",)),
        name="route",
    )(logits_t, bias_col)


def _row_copy(src, dst, s, t, n, sem):
    def rows(x):
        return pl.ds(x * n if isinstance(x, int) else pl.multiple_of(x * n, 8), n)

    return pltpu.make_async_copy(src.at[rows(s), :], dst.at[rows(t), :], sem)


def _dispatch_body(dest_ref, pad_ref, v_ref, xs_hbm, zero_scr, sem, zsem, *, tt, ds, n_exp, n_blocks, rb):
    i = pl.program_id(0)

    @pl.when(i == 0)
    def _():
        zero_scr[...] = jnp.zeros_like(zero_scr)

        def pad_copy(row, p):
            return pltpu.make_async_copy(zero_scr.at[pl.ds(0, p * ds), :],
                                         xs_hbm.at[pl.ds(pl.multiple_of(row * ds, ds), p * ds), :], zsem)

        def per_expert(wait):
            def body(e, m):
                row = pad_ref[0, e]
                npad = pad_ref[1, e]
                p = rb // 2
                while p >= 1:
                    hit = (npad & p) != 0

                    @pl.when(hit)
                    def _(row=row, p=p):
                        if wait:
                            pad_copy(row, p).wait()
                        else:
                            pad_copy(row, p).start()

                    row = row + jnp.where(hit, p, 0)
                    p //= 2
                return m

            lax.fori_loop(0, n_exp, body, 0)

        per_expert(False)
        per_expert(True)
        n_used = pad_ref[2, 0]

        def blk_start(bk, m):
            _row_copy(zero_scr, xs_hbm, 0, bk, rb * ds, zsem).start()
            return m

        lax.fori_loop(n_used, n_blocks, blk_start, 0)

        def blk_wait(bk, m):
            _row_copy(zero_scr, xs_hbm, 0, 0, rb * ds, zsem).wait()
            return m

        lax.fori_loop(n_used, n_blocks, blk_wait, 0)

    def issue(r, m):
        src = v_ref.at[pl.ds(pl.multiple_of(r * ds, ds), ds), :]
        for kk in range(2):
            dst = xs_hbm.at[pl.ds(pl.multiple_of(dest_ref[0, kk, r] * ds, ds), ds), :]
            pltpu.make_async_copy(src, dst, sem).start(priority=kk)
        return m

    lax.fori_loop(0, tt, issue, 0, unroll=8)
    for kk in range(2):
        pltpu.make_async_copy(v_ref, xs_hbm.at[pl.ds(0, tt * ds), :], sem).wait()


def _dispatch(dest3, pad_info, v_ts, *, n_tok, n_blocks, tt, ds):
    n_exp = pad_info.shape[1]
    rb = EXPERT_BLOCK
    return pl.pallas_call(
        functools.partial(_dispatch_body, tt=tt, ds=ds, n_exp=n_exp, n_blocks=n_blocks, rb=rb),
        grid=(n_tok // tt,),
        in_specs=[pl.BlockSpec((1, 2, tt), lambda i: (i, 0, 0), memory_space=pltpu.SMEM),
                  pl.BlockSpec(memory_space=pltpu.SMEM),
                  pl.BlockSpec((tt * ds, LANES), lambda i: (i, 0))],
        out_specs=pl.BlockSpec(memory_space=pl.ANY),
        out_shape=jax.ShapeDtypeStruct((n_blocks * rb * ds, LANES), v_ts.dtype),
        scratch_shapes=[pltpu.VMEM((rb * ds, LANES), v_ts.dtype), pltpu.SemaphoreType.DMA(()),
                        pltpu.SemaphoreType.DMA(())],
        compiler_params=_cparams(("arbitrary",)),
        name="dispatch",
    )(dest3, pad_info, v_ts)


def _expert_body(blk0_ref, nblk_ref, nu_ref, x_hbm, wg_ref, wu_ref, wd_ref, y_hbm, x_buf, y_buf, x_s, sem_x, sem_y,
                 *, ds, rb, n_blocks):
    e = pl.program_id(0)
    b0 = blk0_ref[e]
    nb = nblk_ref[e]
    n_used = nu_ref[0]
    rows = rb * ds

    def x_copy(g, slot):
        return pltpu.make_async_copy(x_hbm.at[pl.ds(pl.multiple_of(g * rows, rows), rows), :],
                                     x_buf.at[slot], sem_x.at[slot])

    def y_copy(g, slot):
        return pltpu.make_async_copy(y_buf.at[slot],
                                     y_hbm.at[pl.ds(pl.multiple_of(g * rows, rows), rows), :], sem_y.at[slot])

    @pl.when(jnp.logical_and(e == 0, n_used > 0))
    def _():
        x_copy(0, 0).start()

    def block(i, carry):
        g = b0 + i
        slot = g % 2
        x_copy(g, slot).wait()

        @pl.when(g + 1 < n_used)
        def _():
            x_copy(g + 1, 1 - slot).start()

        @pl.when(g >= 2)
        def _():
            y_copy(g - 2, slot).wait()

        half = ds * LANES
        for j in range(ds):
            lo, hi = _unpack_pair(x_buf[slot, pl.ds(j, rb, stride=ds), :])
            x_s[:, j * LANES:(j + 1) * LANES] = lo.astype(bf16)
            x_s[:, half + j * LANES:half + (j + 1) * LANES] = hi.astype(bf16)
        x = x_s[...]
        hg = jnp.dot(x, wg_ref[0, 0].astype(bf16), preferred_element_type=f32)
        hu = jnp.dot(x, wu_ref[0, 0].astype(bf16), preferred_element_type=f32)
        a = (hg * _sigmoid(hg) * hu).astype(bf16)
        y = jnp.dot(a, wd_ref[0, 0].astype(bf16), preferred_element_type=f32)
        for j in range(ds):
            y_buf[slot, pl.ds(j, rb, stride=ds), :] = _pack_pair(y[:, j * LANES:(j + 1) * LANES],
                                                                 y[:, half + j * LANES:half + (j + 1) * LANES])
        y_copy(g, slot).start()
        return carry

    lax.fori_loop(0, nb, block, 0)

    @pl.when(e == pl.num_programs(0) - 1)
    def _():
        @pl.when(n_used >= 2)
        def _():
            y_copy(n_used - 2, n_used % 2).wait()

        @pl.when(n_used >= 1)
        def _():
            y_copy(n_used - 1, (n_used - 1) % 2).wait()

        y_buf[0] = jnp.zeros(y_buf.shape[1:], y_buf.dtype)

        def start(bk, m):
            _row_copy(y_buf.at[0], y_hbm, 0, bk, rows, sem_y.at[0]).start()
            return m

        lax.fori_loop(n_used, n_blocks, start, 0)

        def wait(bk, m):
            _row_copy(y_buf.at[0], y_hbm, 0, 0, rows, sem_y.at[0]).wait()
            return m

        lax.fori_loop(n_used, n_blocks, wait, 0)


def _experts(blk0, nblk, n_used, xs_ts, w_g, w_u, w_d, *, layer, ds):
    _, E, D, De = w_g.shape
    rb = EXPERT_BLOCK
    n_blocks = xs_ts.shape[0] // (rb * ds)
    grid_spec = pltpu.PrefetchScalarGridSpec(
        num_scalar_prefetch=3,
        grid=(E,),
        in_specs=[pl.BlockSpec(memory_space=pl.ANY),
                  pl.BlockSpec((1, 1, D, De), lambda e, *_: (layer, e, 0, 0)),
                  pl.BlockSpec((1, 1, D, De), lambda e, *_: (layer, e, 0, 0)),
                  pl.BlockSpec((1, 1, De, D), lambda e, *_: (layer, e, 0, 0))],
        out_specs=pl.BlockSpec(memory_space=pl.ANY),
        scratch_shapes=[pltpu.VMEM((2, rb * ds, LANES), xs_ts.dtype), pltpu.VMEM((2, rb * ds, LANES), xs_ts.dtype),
                        pltpu.VMEM((rb, D), bf16), pltpu.SemaphoreType.DMA((2,)), pltpu.SemaphoreType.DMA((2,))],
    )
    return pl.pallas_call(
        functools.partial(_expert_body, ds=ds, rb=rb, n_blocks=n_blocks),
        grid_spec=grid_spec,
        out_shape=jax.ShapeDtypeStruct(xs_ts.shape, xs_ts.dtype),
        compiler_params=_cparams(("arbitrary",)),
        name="experts",
    )(blk0, nblk, n_used, xs_ts, w_g, w_u, w_d)


def _combine_body(dest_ref, dnext_ref, gate_ref, h_ref, gt_ref, fg_ref, y_hbm, o_ref, y_scr, sem, *, tt, ds, final):
    i = pl.program_id(0)
    slot = i % 2

    def gather(d_ref, to_slot):
        def body(r, m):
            for kk in range(2):
                _row_copy(y_hbm, y_scr.at[to_slot, kk], d_ref[0, kk, r], r, ds, sem.at[to_slot]).start(priority=kk)
            return m

        lax.fori_loop(0, tt, body, 0, unroll=8)

    @pl.when(i == 0)
    def _():
        gather(dest_ref, 0)

    @pl.when(i + 1 < pl.num_programs(0))
    def _():
        gather(dnext_ref, 1 - slot)

    for kk in range(2):
        pltpu.make_async_copy(y_hbm.at[pl.ds(0, tt * ds), :], y_scr.at[slot, kk], sem.at[slot]).wait()
    gt = gt_ref[0]
    g0 = gate_ref[:, 0:1]
    g1 = gate_ref[:, 1:2]
    half = ds * LANES
    for j in range(ds):
        lo0, hi0 = _unpack_pair(y_scr[slot, 0, pl.ds(j, tt, stride=ds), :])
        lo1, hi1 = _unpack_pair(y_scr[slot, 1, pl.ds(j, tt, stride=ds), :])
        for base, y0, y1 in ((0, lo0, lo1), (half, hi0, hi1)):
            sl = slice(base + j * LANES, base + (j + 1) * LANES)
            o_ref[:, sl] = h_ref[:, sl] + gt[:, sl] * (g0 * y0 + g1 * y1)
    if final:
        x = o_ref[...]
        o_ref[...] = x * lax.rsqrt(jnp.mean(x * x, -1, keepdims=True) + EPS) * fg_ref[...]


def _combine(dest3, gates_t, h1, mod, final_g, y_ts, *, n_tok, tt, ds, tpb, nlat, nb, final):
    D = h1.shape[1]
    nt = n_tok // tt
    return pl.pallas_call(
        functools.partial(_combine_body, tt=tt, ds=ds, final=final),
        grid=(nt,),
        in_specs=[pl.BlockSpec((1, 2, tt), lambda i: (i, 0, 0), memory_space=pltpu.SMEM),
                  pl.BlockSpec((1, 2, tt), lambda i: (jnp.minimum(i + 1, nt - 1), 0, 0), memory_space=pltpu.SMEM),
                  pl.BlockSpec((tt, 2), lambda i: (i, 0)),
                  pl.BlockSpec((tt, D), lambda i: (i, 0)),
                  _mod_spec(5, D, tpb, nlat, nb),
                  pl.BlockSpec((1, D), lambda i: (0, 0)),
                  pl.BlockSpec(memory_space=pl.ANY)],
        out_specs=pl.BlockSpec((tt, D), lambda i: (i, 0)),
        out_shape=jax.ShapeDtypeStruct((n_tok, D), f32),
        scratch_shapes=[pltpu.VMEM((2, 2, tt * ds, LANES), y_ts.dtype), pltpu.SemaphoreType.DMA((2,))],
        compiler_params=_cparams(("arbitrary",)),
        name="combine",
    )(dest3, dest3, gates_t, h1, mod, final_g, y_ts)


def _moe(v_ts, logits, h1, mod, w_router_bias, w_g, w_u, w_d, final_g, *, layer, n_tok, tt, ds, tpb, nlat, nb,
         final):
    E = w_g.shape[1]
    route_tt = next(t for t in (512, 256, LANES) if n_tok % t == 0)
    idx, gates, counts = _route(logits[:n_tok].T, w_router_bias.reshape(E, 1), tt=route_tt)
    counts = counts[:, 0].astype(jnp.int32)
    padded = (counts + EXPERT_BLOCK - 1) // EXPERT_BLOCK * EXPERT_BLOCK
    pad_end = jnp.cumsum(padded)
    pad_start = pad_end - padded
    n_blocks = (n_tok * 2 + EXPERT_BLOCK - 1) // EXPERT_BLOCK + E
    n_used = (pad_end[-1] // EXPERT_BLOCK).astype(jnp.int32).reshape(1)
    e_sel = idx[0:2][..., None] == jnp.arange(E, dtype=jnp.int32)
    dest = jnp.sum(jnp.where(e_sel, pad_start.astype(jnp.int32), 0), -1) + idx[2:4]
    dest3 = dest.reshape(2, n_tok // tt, tt).transpose(1, 0, 2)
    pad_info = jnp.stack([pad_start + counts, padded - counts, jnp.broadcast_to(n_used, (E,))]).astype(jnp.int32)
    xs_ts = _dispatch(dest3, pad_info, v_ts, n_tok=n_tok, n_blocks=n_blocks, tt=tt, ds=ds)
    y_ts = _experts((pad_start // EXPERT_BLOCK).astype(jnp.int32), (padded // EXPERT_BLOCK).astype(jnp.int32), n_used,
                    xs_ts, w_g, w_u, w_d, layer=layer, ds=ds)
    return _combine(dest3, gates.T, h1, mod, final_g, y_ts, n_tok=n_tok, tt=tt, ds=ds, tpb=tpb, nlat=nlat,
                    nb=nb, final=final)


def kernel(x, c, ctx, c_ctx, w_mod, b_mod, norm1_g, norm2_g, w_in, conv_w, a_log, dt_bias, onorm_g, w_pool, pool_scale, w_up_dn, w_up_pool, w_out, w_router, router_bias, w_gate_e, w_up_e, w_down_e, final_g):
    B, S, D = x.shape
    C = ctx.shape[1]
    L = w_mod.shape[0]
    H = a_log.shape[-1]
    dk = onorm_g.shape[-1]
    assert dk == LANES
    dn_w = H * dk
    _, ng, gw, _ = w_pool.shape
    pool_w = ng * gw
    ds = D // (2 * LANES)
    n_lat = B * S
    T = n_lat + B * C
    tm = 256
    assert S % tm == 0 and (B * C) % tm == 0 and C % CHUNK == 0 and S % (4 * GRID_W) == 0
    tpb = S // tm
    nlat_tiles = n_lat // tm
    tm_in = next(t for t in (1024, 512, 256) if S % t == 0 and (B * C) % t == 0)
    tm_mg = next(t for t in (512, 256) if S % t == 0 and (B * C) % t == 0)
    tr = min(256, C)
    G = min(H, 16)

    c8 = jnp.zeros((8, D), f32).at[:B].set(c).at[B].set(c_ctx)
    mod = _mod_table(c8, w_mod, b_mod)

    qkv_c = 3 * dn_w
    scan_c = qkv_c + 4 * H
    og_off, pool_off = scan_c, scan_c + dn_w
    mg_off = pool_off + pool_w
    assert (4 * dn_w) % D == 0 and (4 * dn_w + 2 * D) % pool_w == 0
    pool_off_blocks = (4 * dn_w + 2 * D) // pool_w

    h = jnp.concatenate([x.reshape(n_lat, D), ctx.reshape(B * C, D)], 0)
    for l in range(L):
        last = l == L - 1
        mod_l = mod[l].reshape(8, 1, mod.shape[-1])
        wl = w_in[l]
        w_z = jnp.concatenate([wl[:, :qkv_c], wl[:, og_off:pool_off], wl[:, mg_off:], wl[:, pool_off:mg_off]],
                              1).astype(bf16)
        w_ba = jnp.pad(wl[:, qkv_c:scan_c], ((0, 0), (0, LANES - 4 * H))).astype(bf16)
        tn = 1024 if w_z.shape[1] % 1024 == 0 else w_z.shape[1]
        z, ba = _in_proj(h, mod_l, 0, norm1_g[l].reshape(1, D), w_z, w_ba, tm=tm_in, tn=tn, tpb=S // tm_in,
                         nlat=n_lat // tm_in, nb=B)
        qkv = _conv_prep(z, conv_w[l], tr=tr, tc=min(2048, dn_w), S=S, C=C, nlat_rows=n_lat, dn_w=dn_w, dk=dk)
        o_dn = _deltanet(qkv, ba, a_log[l], dt_bias[l], B=B, S=S, C=C, H=H, G=G)
        pooled_lat, pooled_ctx = _pool(z, B=B, S=S, C=C, pool_off_blocks=pool_off_blocks, pool_w=pool_w, gw=gw)
        n_rows = n_lat if last else T
        merged = _merge_a(o_dn, z, pooled_lat, pooled_ctx, onorm_g[l].reshape(1, dk), w_pool,
                          pool_scale[l].reshape(1, pool_w), w_up_dn, w_up_pool,
                          layer=l, n_rows=n_rows, tm=tm, H=H, D=D)
        h1, v_ts, logits = _merge_b(merged, h, mod_l, norm2_g[l].reshape(1, D), w_out, w_router,
                                    layer=l, n_rows=n_rows, tm=tm_mg, tpb=S // tm_mg, nlat=n_lat // tm_mg, nb=B)
        h = _moe(v_ts, logits, h1, mod_l, router_bias, w_gate_e, w_up_e, w_down_e,
                 final_g.reshape(1, D), layer=l, n_tok=n_rows, tt=tm_mg, ds=ds, tpb=S // tm_mg, nlat=n_lat // tm_mg, nb=B,
                 final=last)
    return h.reshape(B, S, D)
```

```python
import functools

import jax
import jax.numpy as jnp
from jax import lax
from jax.experimental import pallas as pl
from jax.experimental.pallas import tpu as pltpu

f32 = jnp.float32
bf16 = jnp.bfloat16
HIGHEST = lax.Precision.HIGHEST

EPS = 1e-6
LANES = 128
GRID_W = 64
CHUNK = 64
POOL_WINDOWS = (2, 4, 8, 16)
GROUP_SIZE = 8
EXPERT_BLOCK = 256
VMEM_LIMIT = 56 * 1024 * 1024


def _cparams(sem, vmem=VMEM_LIMIT):
    return pltpu.CompilerParams(dimension_semantics=sem, vmem_limit_bytes=vmem)


def _sigmoid(x):
    return 1.0 / (1.0 + jnp.exp(-x))


def _softplus(x):
    return jnp.maximum(x, 0.0) + jnp.log(1.0 + jnp.exp(-jnp.abs(x)))


def _pack_pair(lo, hi):
    def rounded(x):
        u = lax.bitcast_convert_type(x, jnp.uint32)
        return u + jnp.uint32(0x7FFF) + ((u >> 16) & jnp.uint32(1))

    return (rounded(lo) >> 16) | (rounded(hi) & jnp.uint32(0xFFFF0000))


def _unpack_pair(w):
    lo = lax.bitcast_convert_type(w << 16, f32)
    hi = lax.bitcast_convert_type(w & jnp.uint32(0xFFFF0000), f32)
    return lo, hi


def _mod_spec(sec, D, tiles_per_batch, n_lat_tiles, n_batch):
    def index(i, *_):
        return (jnp.where(i < n_lat_tiles, i // tiles_per_batch, n_batch), 0, sec)

    return pl.BlockSpec((1, 1, D), index)


def _mod_body(c_ref, w_ref, b_ref, o_ref):
    c = c_ref[...]
    s = c * _sigmoid(c)
    sh = s.astype(bf16)
    sl = (s - sh.astype(f32)).astype(bf16)
    w = w_ref[0]
    wh = w.astype(bf16)
    wl = (w - wh.astype(f32)).astype(bf16)
    t = jnp.dot(jnp.concatenate([sh, sl], axis=0), wh, preferred_element_type=f32)
    n = s.shape[0]
    o_ref[0] = t[:n] + (t[n:] + jnp.dot(sh, wl, preferred_element_type=f32)) + b_ref[0]


def _mod_table(c8, w_mod, b_mod):
    L, D, N = w_mod.shape
    tn = next(t for t in (1024, 512, 256, LANES) if N % t == 0)
    return pl.pallas_call(
        _mod_body,
        grid=(L, N // tn),
        in_specs=[pl.BlockSpec((8, D), lambda l, j: (0, 0)),
                  pl.BlockSpec((1, D, tn), lambda l, j: (l, 0, j)),
                  pl.BlockSpec((1, 1, tn), lambda l, j: (l, 0, j))],
        out_specs=pl.BlockSpec((1, 8, tn), lambda l, j: (l, 0, j)),
        out_shape=jax.ShapeDtypeStruct((L, 8, N), f32),
        compiler_params=_cparams(("parallel", "parallel")),
        name="mod_table",
    )(c8, w_mod, b_mod.reshape(L, 1, N))


def _inproj_body(h_ref, sc_ref, sh_ref, g_ref, w_ref, wba_ref, z_ref, ba_ref, u_scr):
    j = pl.program_id(1)

    @pl.when(j == 0)
    def _():
        x = h_ref[...]
        y = x * lax.rsqrt(jnp.mean(x * x, -1, keepdims=True) + EPS) * g_ref[...]
        u = (y * (1.0 + sc_ref[0]) + sh_ref[0]).astype(bf16)
        u_scr[...] = u
        ba_ref[...] = jnp.dot(u, wba_ref[...], preferred_element_type=f32)

    z_ref[...] = jnp.dot(u_scr[...], w_ref[...], preferred_element_type=f32).astype(z_ref.dtype)


def _in_proj(h, mod, sec, gain, w_z, w_ba, *, tm, tn, tpb, nlat, nb):
    T, D = h.shape
    Nz = w_z.shape[1]
    return pl.pallas_call(
        _inproj_body,
        grid=(T // tm, Nz // tn),
        in_specs=[pl.BlockSpec((tm, D), lambda i, j: (i, 0)),
                  _mod_spec(sec + 1, D, tpb, nlat, nb),
                  _mod_spec(sec, D, tpb, nlat, nb),
                  pl.BlockSpec((1, D), lambda i, j: (0, 0)),
                  pl.BlockSpec((D, tn), lambda i, j: (0, j)),
                  pl.BlockSpec((D, LANES), lambda i, j: (0, 0))],
        out_specs=[pl.BlockSpec((tm, tn), lambda i, j: (i, j)),
                   pl.BlockSpec((tm, LANES), lambda i, j: (i, 0))],
        out_shape=[jax.ShapeDtypeStruct((T, Nz), bf16), jax.ShapeDtypeStruct((T, LANES), f32)],
        scratch_shapes=[pltpu.VMEM((tm, D), bf16)],
        compiler_params=_cparams(("parallel", "arbitrary")),
        name="in_proj",
    )(h, mod, mod, gain, w_z, w_ba)


def _conv_body(x_ref, xp_ref, xn_ref, w_ref, sh_ref, o_ref, *, tr, tps_lat, tps_ctx, nlat, qscale, tiles_per_sec,
               halo):
    i = pl.program_id(0)
    il =jnp.where(i < nlat, i % tps_lat, (i - nlat) % tps_ctx)
    nt = jnp.where(i < nlat, tps_lat, tps_ctx)
    first = il == 0
    last = il == nt - 1
    xb = x_ref[...]
    w = w_ref[...]
    tc_ = xb.shape[1]
    ys = jnp.dot(sh_ref[...], xb, preferred_element_type=f32).reshape(tr // 8, 32, tc_)
    acc = (ys[:, 0:8] * w[0:1] + ys[:, 8:16] * w[1:2] + ys[:, 16:24] * w[3:4] + ys[:, 24:32] * w[4:5])
    acc = acc.reshape(tr, tc_) + xb.astype(f32) * w[2:3]
    p = jnp.where(first, 0.0, xp_ref[...].astype(f32))[halo - 8:]
    n = jnp.where(last, 0.0, xn_ref[...].astype(f32))[:8]
    r8 = lax.broadcasted_iota(jnp.int32, (8, 1), 0)
    top = jnp.where(r8 == 0, p[6:7] * w[0:1] + p[7:8] * w[1:2], jnp.where(r8 == 1, p[7:8] * w[0:1], 0.0))
    bot = jnp.where(r8 == 7, n[0:1] * w[3:4] + n[1:2] * w[4:5], jnp.where(r8 == 6, n[0:1] * w[4:5], 0.0))
    acc = jnp.concatenate([acc[:8] + top, acc[8:tr - 8], acc[tr - 8:] + bot], axis=0)
    y = acc * _sigmoid(acc)
    sec = pl.program_id(1) // tiles_per_sec

    @pl.when(sec == 2)
    def _():
        o_ref[...] = y.astype(o_ref.dtype)

    @pl.when(sec != 2)
    def _():
        qk_scale = jnp.where(sec == 0, qscale, 1.0)
        for s in range(acc.shape[1] // LANES):
            ys = y[:, s * LANES:(s + 1) * LANES]
            rs = lax.rsqrt(jnp.sum(ys * ys, -1, keepdims=True) + EPS) * qk_scale
            o_ref[:, s * LANES:(s + 1) * LANES] = (ys * rs).astype(o_ref.dtype)


def _conv_prep(z, conv_w, *, tr, tc, S, C, nlat_rows, dn_w, dk):
    T = z.shape[0]
    halo = 16
    hb = tr // halo
    nh = T // halo
    body = functools.partial(_conv_body, tr=tr, tps_lat=S // tr, tps_ctx=C // tr, nlat=nlat_rows // tr,
                             qscale=float(dk) ** -0.5, tiles_per_sec=dn_w // tc, halo=halo)
    assert conv_w.shape[0] == 5
    ri = jnp.arange(tr)[:, None]
    ci = jnp.arange(tr)[None, :]
    shifts = jnp.stack([(ci == ri + off).astype(bf16) for off in (-2, -1, 1, 2)], 0)
    shifts = shifts.reshape(4, tr // 8, 8, tr).transpose(1, 0, 2, 3).reshape(4 * tr, tr)
    return pl.pallas_call(
        body,
        grid=(T // tr, 3 * dn_w // tc),
        in_specs=[pl.BlockSpec((tr, tc), lambda i, j: (i, j)),
                  pl.BlockSpec((halo, tc), lambda i, j: (jnp.maximum(i * hb - 1, 0), j)),
                  pl.BlockSpec((halo, tc), lambda i, j: (jnp.minimum((i + 1) * hb, nh - 1), j)),
                  pl.BlockSpec((conv_w.shape[0], tc), lambda i, j: (0, j)),
                  pl.BlockSpec((4 * tr, tr), lambda i, j: (0, 0))],
        out_specs=pl.BlockSpec((tr, tc), lambda i, j: (i, j)),
        out_shape=jax.ShapeDtypeStruct((T, 3 * dn_w), bf16),
        compiler_params=_cparams(("parallel", "parallel")),
        name="conv_prep",
    )(z, z, z, conv_w, shifts)


def _mm(x, y):
    return jnp.dot(x.astype(bf16), y.astype(bf16), preferred_element_type=f32)


def _unit_tri_inverse(a_list, ri, ci):
    b16 = (ri >> 4) == (ci >> 4)
    b32 = (ri >> 5) == (ci >> 5)
    eye = jnp.where(ri == ci, 1.0, 0.0)
    n = a_list[0].shape[0]

    def mmb(x, y):
        return jnp.dot(x, y, preferred_element_type=f32)

    a16 = [jnp.where(b16, a, 0.0) for a in a_list]
    t = [eye - x for x in a16]
    pb = [x.astype(bf16) for x in a16]
    pb = [mmb(x, x).astype(bf16) for x in pb]
    for step in range(2):
        tp = [mmb(jnp.concatenate([ti.astype(bf16), pi], axis=0), pi) for ti, pi in zip(t, pb)]
        t = [ti + x[:n] for ti, x in zip(t, tp)]
        pb = [x[n:].astype(bf16) for x in tp]
    t = [ti + mmb(ti.astype(bf16), pi) for ti, pi in zip(t, pb)]
    for off in ([jnp.where(b32, a, 0.0) - x for a, x in zip(a_list, a16)],
                [jnp.where(b32, 0.0, a) for a in a_list]):
        tb = [ti.astype(bf16) for ti in t]
        ta = [mmb(ti, ai.astype(bf16)).astype(bf16) for ti, ai in zip(tb, off)]
        t = [ti - mmb(tai, tbi) for ti, tai, tbi in zip(t, ta, tb)]
    return t


def _dn_body(q_ref, k_ref, v_ref, ba_ref, al_ref, dt_ref, o_ref, s_scr, *, G):
    d = pl.program_id(1)
    c = pl.program_id(3)
    dk = LANES
    hs = range(G)

    @pl.when(c == 0)
    def _():
        s_scr[...] = jnp.zeros_like(s_scr)

    ba = ba_ref[0]
    beta_t = _sigmoid(ba)
    g_t = -jnp.exp(al_ref[0]) * _softplus(ba + dt_ref[0])
    ri = lax.broadcasted_iota(jnp.int32, (CHUNK, CHUNK), 0)
    ci = lax.broadcasted_iota(jnp.int32, (CHUNK, CHUNK), 1)
    fwd = d == 0
    later = jnp.where(fwd, ri, ci)
    earlier = jnp.where(fwd, ci, ri)
    incl = later >= earlier
    strict = later > earlier
    tri = jnp.where(incl, 1.0, 0.0)
    gc = jnp.dot(tri, g_t, preferred_element_type=f32, precision=HIGHEST)
    gct = gc.T
    gtot = jnp.sum(g_t, axis=0, keepdims=True)

    def pick(x, col):
        return jnp.where(fwd, x[:, col:col + 1], x[:, G + col:G + col + 1])

    sl = [slice(h * dk, (h + 1) * dk) for h in hs]
    beta_c = [pick(beta_t, h) for h in hs]
    gc_c = [pick(gc, 2 * G + h) for h in hs]
    gc_r = [jnp.where(fwd, gct[2 * G + h:2 * G + h + 1], gct[3 * G + h:3 * G + h + 1]) for h in hs]
    gt = [pick(gtot, 2 * G + h) for h in hs]
    q = [q_ref[:, sl[h]].astype(f32) for h in hs]
    k = [k_ref[:, sl[h]].astype(f32) for h in hs]
    kb = [k[h] * beta_c[h] for h in hs]
    eg = [jnp.exp(gc_c[h]) for h in hs]
    dec = [jnp.where(incl, jnp.exp(jnp.where(incl, gc_c[h] - gc_r[h], 0.0)), 0.0) for h in hs]
    kq = [lax.dot_general(jnp.concatenate([kb[h], q[h]], axis=0).astype(bf16), k_ref[:, sl[h]],
                          (((1,), (1,)), ((), ())), preferred_element_type=f32) for h in hs]
    a = [jnp.where(strict, kq[h][:CHUNK] * dec[h], 0.0) for h in hs]
    t = _unit_tri_inverse(a, ri, ci)
    rhs = [jnp.concatenate([v_ref[:, sl[h]].astype(f32) * beta_c[h], kb[h] * eg[h]], axis=1) for h in hs]
    sol = [_mm(t[h], rhs[h]) for h in hs]
    s = [s_scr[h] for h in hs]
    r = [_mm(jnp.concatenate([sol[h][:, dk:], q[h] * eg[h]], axis=0), s[h]) for h in hs]
    vnb = [(sol[h][:, :dk] - r[h][:CHUNK]).astype(bf16) for h in hs]
    qkd = [jnp.concatenate([(kq[h][CHUNK:] * dec[h]).astype(bf16),
                            (k[h] * jnp.exp(gt[h] - gc_c[h])).T.astype(bf16)], axis=0) for h in hs]
    ou = [jnp.dot(qkd[h], vnb[h], preferred_element_type=f32) for h in hs]
    for h in hs:
        s_scr[h] = s[h] * jnp.exp(gt[h]) + ou[h][CHUNK:]
        o_ref[0, :, sl[h]] = (r[h][CHUNK:] + ou[h][:CHUNK]).astype(o_ref.dtype)


def _deltanet(qkv, ba, a_log, dt_bias, *, B, S, C, H, G):
    T = qkv.shape[0]
    dn_w = H * LANES
    ncl = S // CHUNK
    ncc = C // CHUNK
    nc = ncl + ncc
    hgs = H // G

    def rowblk(b, d, c):
        cc = jnp.where(d == 0, c, ncc - 1 - c)
        cl = jnp.where(d == 0, c - ncc, nc - 1 - c)
        return jnp.where(c < ncc, B * ncl + b * ncc + cc, b * ncl + cl)

    def grouped(t):
        r = t.shape[0]
        t = t.reshape(r, 4, hgs, G).transpose(2, 0, 1, 3).reshape(hgs, r, 4 * G)
        return jnp.pad(t, ((0, 0), (0, 0), (0, LANES - 4 * G)))

    zeros = jnp.zeros((1, 2 * H), f32)
    ba_g = grouped(ba[:, :4 * H])
    al_g = grouped(jnp.concatenate([zeros, a_log.reshape(1, 2 * H)], 1))
    dt_g = grouped(jnp.concatenate([zeros, dt_bias.reshape(1, 2 * H)], 1))
    body = functools.partial(_dn_body, G=G)
    return pl.pallas_call(
        body,
        grid=(B, 2, hgs, nc),
        in_specs=[pl.BlockSpec((CHUNK, G * LANES), lambda b, d, g, c: (rowblk(b, d, c), g)),
                  pl.BlockSpec((CHUNK, G * LANES), lambda b, d, g, c: (rowblk(b, d, c), hgs + g)),
                  pl.BlockSpec((CHUNK, G * LANES), lambda b, d, g, c: (rowblk(b, d, c), 2 * hgs + g)),
                  pl.BlockSpec((1, CHUNK, LANES), lambda b, d, g, c: (g, rowblk(b, d, c), 0)),
                  pl.BlockSpec((1, 1, LANES), lambda b, d, g, c: (g, 0, 0)),
                  pl.BlockSpec((1, 1, LANES), lambda b, d, g, c: (g, 0, 0))],
        out_specs=pl.BlockSpec((1, CHUNK, G * LANES), lambda b, d, g, c: (d, rowblk(b, d, c), g)),
        out_shape=jax.ShapeDtypeStruct((2, T, dn_w), bf16),
        scratch_shapes=[pltpu.VMEM((G, LANES, LANES), f32)],
        compiler_params=_cparams(("parallel", "parallel", "parallel", "arbitrary")),
        name="deltanet",
    )(qkv, qkv, qkv, ba_g, al_g, dt_g)


def _pool_ctx_body(x_ref, o_ref, *, gw):
    C = x_ref.shape[0]
    ri = lax.broadcasted_iota(jnp.int32, (C, C), 0)
    ci = lax.broadcasted_iota(jnp.int32, (C, C), 1)
    t = lax.broadcasted_iota(jnp.int32, (C, 1), 0)
    for gi, w in enumerate(POOL_WINDOWS):
        sl = slice(gi * gw, (gi + 1) * gw)
        band = jnp.where((ci >= ri - w // 2) & (ci < ri + w - w // 2), 1.0, 0.0).astype(bf16)
        cnt = (jnp.minimum(t + w - w // 2, C) - jnp.maximum(t - w // 2, 0)).astype(f32)
        x = x_ref[:, sl]
        box = jnp.dot(band, x, preferred_element_type=f32)
        o_ref[:, sl] = (box / cnt - x.astype(f32)).astype(o_ref.dtype)


def _pool_lat_body(x_ref, o_ref, y_scr):
    g = pl.program_id(1)
    S = x_ref.shape[0]
    rows = S // GRID_W
    sb = 4 * GRID_W
    ri = lax.broadcasted_iota(jnp.int32, (sb, sb), 0)
    ci = lax.broadcasted_iota(jnp.int32, (sb, sb), 1)
    cr = ri & (GRID_W - 1)
    cc = ci & (GRID_W - 1)
    same = (ri >> 6) == (ci >> 6)
    col = lax.broadcasted_iota(jnp.int32, (GRID_W, 1), 0)
    for gi, w in enumerate(POOL_WINDOWS):
        @pl.when(g == gi)
        def _(w=w):
            band = jnp.where(same & (cc >= cr - w // 2) & (cc < cr + w - w // 2), 1.0, 0.0).astype(bf16)
            for s in range(S // sb):
                y_scr[s * sb:(s + 1) * sb, :] = jnp.dot(band, x_ref[s * sb:(s + 1) * sb, :],
                                                        preferred_element_type=f32)
            ncol = (jnp.minimum(col + w - w // 2, GRID_W) - jnp.maximum(col - w // 2, 0)).astype(f32)
            for r in range(rows):
                lo = max(r - w // 2, 0)
                hi = min(r + w - w // 2, rows)
                acc = y_scr[lo * GRID_W:(lo + 1) * GRID_W, :]
                for rr in range(lo + 1, hi):
                    acc = acc + y_scr[rr * GRID_W:(rr + 1) * GRID_W, :]
                cnt = ncol * float(hi - lo)
                xs = x_ref[r * GRID_W:(r + 1) * GRID_W, :].astype(f32)
                o_ref[r * GRID_W:(r + 1) * GRID_W, :] = (acc / cnt - xs).astype(o_ref.dtype)


def _pool(z, *, B, S, C, pool_off_blocks, pool_w, gw):
    nlat_c = B * S // C
    ng = pool_w // gw
    assert ng == len(POOL_WINDOWS)
    pooled_ctx = pl.pallas_call(
        functools.partial(_pool_ctx_body, gw=gw),
        grid=(B,),
        in_specs=[pl.BlockSpec((C, pool_w), lambda b: (nlat_c + b, pool_off_blocks))],
        out_specs=pl.BlockSpec((C, pool_w), lambda b: (b, 0)),
        out_shape=jax.ShapeDtypeStruct((B * C, pool_w), bf16),
        compiler_params=_cparams(("parallel",)),
        name="pool_ctx",
    )(z)
    off = pool_off_blocks * ng
    pooled_lat = pl.pallas_call(
        _pool_lat_body,
        grid=(B, ng),
        in_specs=[pl.BlockSpec((S, gw), lambda b, g: (b, off + g))],
        out_specs=pl.BlockSpec((S, gw), lambda b, g: (b, g)),
        out_shape=jax.ShapeDtypeStruct((B * S, pool_w), bf16),
        scratch_shapes=[pltpu.VMEM((S, gw), f32)],
        compiler_params=_cparams(("parallel", "parallel")),
        name="pool_lat",
    )(z)
    return pooled_lat, pooled_ctx


def _merge_a_body(o_ref, og_ref, m1_ref, m2_ref, pl_ref, pc_ref, ong_ref, wpool_ref, ps_ref, wud_ref, wup_ref,
                  out_ref, a_scr, p_scr, *, H, gw, nlat):
    is_lat = pl.program_id(0) < nlat
    for h in range(H):
        sl = slice(h * LANES, (h + 1) * LANES)
        o = o_ref[0, :, sl].astype(f32) + o_ref[1, :, sl].astype(f32)
        og = og_ref[:, sl].astype(f32)
        y = o * lax.rsqrt(jnp.mean(o * o, -1, keepdims=True) + EPS) * ong_ref[...] * (og * _sigmoid(og))
        a_scr[:, sl] = y.astype(bf16)
    y_dn = jnp.dot(a_scr[...], wud_ref[0].astype(bf16), preferred_element_type=f32)
    for g in range(wpool_ref.shape[1]):
        sl = slice(g * gw, (g + 1) * gw)
        pooled = jnp.where(is_lat, pl_ref[:, sl], pc_ref[:, sl])
        yp = jnp.dot(pooled, wpool_ref[0, g].astype(bf16), preferred_element_type=f32) * ps_ref[:, sl]
        p_scr[:, sl] = yp.astype(bf16)
    y_pool = jnp.dot(p_scr[...], wup_ref[0].astype(bf16), preferred_element_type=f32)
    g1 = _sigmoid(m1_ref[...].astype(f32))
    g2 = _sigmoid(m2_ref[...].astype(f32))
    out_ref[...] = (g1 * y_dn + g2 * y_pool).astype(out_ref.dtype)


def _merge_a(o_dn, z, pooled_lat, pooled_ctx, onorm_g, w_pool, pool_scale, w_up_dn, w_up_pool, *, layer, n_rows, tm,
             H, D):
    dn_w = H * LANES
    pool_w = pooled_lat.shape[1]
    nlat = pooled_lat.shape[0] // tm
    _, ng, gw, _ = w_pool.shape
    once = pl.Buffered(1)
    og_blk = 3
    m_blk = 4 * dn_w // D
    const = lambda *_: (0, 0)
    return pl.pallas_call(
        functools.partial(_merge_a_body, H=H, gw=gw, nlat=nlat),
        grid=(n_rows // tm,),
        in_specs=[pl.BlockSpec((2, tm, dn_w), lambda i: (0, i, 0)),
                  pl.BlockSpec((tm, dn_w), lambda i: (i, og_blk)),
                  pl.BlockSpec((tm, D), lambda i: (i, m_blk)),
                  pl.BlockSpec((tm, D), lambda i: (i, m_blk + 1)),
                  pl.BlockSpec((tm, pool_w), lambda i: (jnp.minimum(i, nlat - 1), 0)),
                  pl.BlockSpec((tm, pool_w), lambda i: (jnp.maximum(i - nlat, 0), 0)),
                  pl.BlockSpec((1, LANES), const),
                  pl.BlockSpec((1, ng, gw, gw), lambda i: (layer, 0, 0, 0), pipeline_mode=once),
                  pl.BlockSpec((1, pool_w), const),
                  pl.BlockSpec((1, dn_w, D), lambda i: (layer, 0, 0), pipeline_mode=once),
                  pl.BlockSpec((1, pool_w, D), lambda i: (layer, 0, 0), pipeline_mode=once)],
        out_specs=pl.BlockSpec((tm, D), lambda i: (i, 0)),
        out_shape=jax.ShapeDtypeStruct((n_rows, D), bf16),
        scratch_shapes=[pltpu.VMEM((tm, dn_w), bf16), pltpu.VMEM((tm, pool_w), bf16)],
        compiler_params=_cparams(("parallel",)),
        name="merge_a",
    )(o_dn, z, z, z, pooled_lat, pooled_ctx, onorm_g, w_pool, pool_scale, w_up_dn, w_up_pool)


def _merge_b_body(m_ref, h_ref, gt_ref, sc_ref, sh_ref, g_ref, wout_ref, wr_ref, h1_ref, v_ref, lg_ref,
                  *, ds):
    tm = m_ref.shape[0]
    m = jnp.dot(m_ref[...], wout_ref[0].astype(bf16), preferred_element_type=f32)
    h1 = h_ref[...] + gt_ref[0] * m
    h1_ref[...] = h1
    y = h1 * lax.rsqrt(jnp.mean(h1 * h1, -1, keepdims=True) + EPS) * g_ref[...]
    v = y * (1.0 + sc_ref[0]) + sh_ref[0]
    E = lg_ref.shape[1]
    vh = v.astype(bf16)
    vl = (v - vh.astype(f32)).astype(bf16)
    th = jnp.dot(vh, wr_ref[...], preferred_element_type=f32)
    tl = jnp.dot(vl, wr_ref[...], preferred_element_type=f32)
    lg_ref[...] = th[:, :E] + (th[:, E:] + tl[:, :E])
    half = ds * LANES
    for j in range(ds):
        v_ref[pl.ds(j, tm, stride=ds), :] = _pack_pair(v[:, j * LANES:(j + 1) * LANES],
                                                       v[:, half + j * LANES:half + (j + 1) * LANES])


def _merge_b(merged, h, mod, gain2, w_out, w_router, *, layer, n_rows, tm, tpb, nlat, nb):
    D = h.shape[1]
    E = w_router.shape[1]
    ds = D // (2 * LANES)
    const = lambda *_: (0, 0)
    wr_hi = w_router.astype(bf16)
    wr_lo = (w_router - wr_hi.astype(f32)).astype(bf16)
    wr_cat = jnp.concatenate([wr_hi, wr_lo], 1)
    return pl.pallas_call(
        functools.partial(_merge_b_body, ds=ds),
        grid=(n_rows // tm,),
        in_specs=[pl.BlockSpec((tm, D), lambda i: (i, 0)),
                  pl.BlockSpec((tm, D), lambda i: (i, 0)),
                  _mod_spec(2, D, tpb, nlat, nb),
                  _mod_spec(4, D, tpb, nlat, nb),
                  _mod_spec(3, D, tpb, nlat, nb),
                  pl.BlockSpec((1, D), const),
                  pl.BlockSpec((1, D, D), lambda i: (layer, 0, 0), pipeline_mode=pl.Buffered(1)),
                  pl.BlockSpec((D, 2 * E), const)],
        out_specs=[pl.BlockSpec((tm, D), lambda i: (i, 0)),
                   pl.BlockSpec((tm * ds, LANES), lambda i: (i, 0)),
                   pl.BlockSpec((tm, E), lambda i: (i, 0))],
        out_shape=[jax.ShapeDtypeStruct((n_rows, D), f32),
                   jax.ShapeDtypeStruct((n_rows * ds, LANES), jnp.uint32),
                   jax.ShapeDtypeStruct((n_rows, E), f32)],
        compiler_params=_cparams(("parallel",)),
        name="merge_b",
    )(merged, h, mod, mod, mod, gain2, w_out, wr_cat)


def _route_body(lg_ref, bias_ref, idx_ref, gate_ref, cnt_ref, carry_scr, *, n_groups):
    i = pl.program_id(0)
    E, tt = lg_ref.shape

    @pl.when(i == 0)
    def _():
        carry_scr[...] = jnp.zeros_like(carry_scr)

    aff = _sigmoid(lg_ref[...])
    sel = aff + bias_ref[...]
    sub = lax.broadcasted_iota(jnp.int32, (GROUP_SIZE, tt), 0)
    neg = -1e30
    best = None
    for g in range(n_groups):
        s = sel[g * GROUP_SIZE:(g + 1) * GROUP_SIZE]
        a = aff[g * GROUP_SIZE:(g + 1) * GROUP_SIZE]
        m1 = jnp.max(s, axis=0, keepdims=True)
        i1 = jnp.min(jnp.where(s == m1, sub, GROUP_SIZE), axis=0, keepdims=True)
        s2 = jnp.where(sub == i1, neg, s)
        m2 = jnp.max(s2, axis=0, keepdims=True)
        i2 = jnp.min(jnp.where(s2 == m2, sub, GROUP_SIZE), axis=0, keepdims=True)
        a1 = jnp.sum(jnp.where(sub == i1, a, 0.0), axis=0, keepdims=True)
        a2 = jnp.sum(jnp.where(sub == i2, a, 0.0), axis=0, keepdims=True)
        cand = (m1 + m2, i1 + g * GROUP_SIZE, i2 + g * GROUP_SIZE, a1, a2)
        if best is None:
            best = cand
        else:
            better = cand[0] > best[0]
            best = tuple(jnp.where(better, cn, bs) for cn, bs in zip(cand, best))
    _, e1, e2, a1, a2 = best
    den = a1 + a2
    gate_ref[0:1, :] = a1 / den
    gate_ref[1:2, :] = a2 / den
    erow = lax.broadcasted_iota(jnp.int32, (E, tt), 0)
    oh1 = erow == e1
    oh2 = erow == e2
    cnt = jnp.where(oh1 | oh2, 1.0, 0.0)
    ti = lax.broadcasted_iota(jnp.int32, (tt, tt), 0)
    tj = lax.broadcasted_iota(jnp.int32, (tt, tt), 1)
    before = jnp.where(ti < tj, 1.0, 0.0).astype(bf16)
    pos = jnp.dot(cnt.astype(bf16), before, preferred_element_type=f32) + carry_scr[...]
    r1 = jnp.sum(jnp.where(oh1, pos, 0.0), axis=0, keepdims=True)
    r2 = jnp.sum(jnp.where(oh2, pos, 0.0), axis=0, keepdims=True)
    idx_ref[0:1, :] = e1
    idx_ref[1:2, :] = e2
    idx_ref[2:3, :] = r1.astype(jnp.int32)
    idx_ref[3:4, :] = r2.astype(jnp.int32)
    carry_scr[...] = carry_scr[...] + jnp.sum(cnt, axis=1, keepdims=True)
    cnt_ref[...] = jnp.broadcast_to(carry_scr[...], cnt_ref.shape)


def _route(logits_t, bias_col, *, tt):
    E, T = logits_t.shape
    return pl.pallas_call(
        functools.partial(_route_body, n_groups=E // GROUP_SIZE),
        grid=(T // tt,),
        in_specs=[pl.BlockSpec((E, tt), lambda i: (0, i)),
                  pl.BlockSpec((E, 1), lambda i: (0, 0))],
        out_specs=[pl.BlockSpec((4, tt), lambda i: (0, i)),
                   pl.BlockSpec((2, tt), lambda i: (0, i)),
                   pl.BlockSpec((E, LANES), lambda i: (0, 0))],
        out_shape=[jax.ShapeDtypeStruct((4, T), jnp.int32),
                   jax.ShapeDtypeStruct((2, T), f32),
                   jax.ShapeDtypeStruct((E, LANES), f32)],
        scratch_shapes=[pltpu.VMEM((E, 1), f32)],
        compiler_params=_cparams(("arbitrary",)),
        name="route",
    )(logits_t, bias_col)


def _row_copy(src, dst, s, t, n, sem):
    def rows(x):
        return pl.ds(x * n if isinstance(x, int) else pl.multiple_of(x * n, 8), n)

    return pltpu.make_async_copy(src.at[rows(s), :], dst.at[rows(t), :], sem)


def _dispatch_body(dest_ref, pad_ref, v_ref, xs_hbm, zero_scr, sem, zsem, *, tt, ds, n_exp, n_blocks, rb):
    i = pl.program_id(0)

    @pl.when(i == 0)
    def _():
        zero_scr[...] = jnp.zeros_like(zero_scr)

        def pad_copy(row, p):
            return pltpu.make_async_copy(zero_scr.at[pl.ds(0, p * ds), :],
                                         xs_hbm.at[pl.ds(pl.multiple_of(row * ds, ds), p * ds), :], zsem)

        def per_expert(wait):
            def body(e, m):
                row = pad_ref[0, e]
                npad = pad_ref[1, e]
                p = rb // 2
                while p >= 1:
                    hit = (npad & p) != 0

                    @pl.when(hit)
                    def _(row=row, p=p):
                        if wait:
                            pad_copy(row, p).wait()
                        else:
                            pad_copy(row, p).start()

                    row = row + jnp.where(hit, p, 0)
                    p //= 2
                return m

            lax.fori_loop(0, n_exp, body, 0)

        per_expert(False)
        per_expert(True)
        n_used = pad_ref[2, 0]

        def blk_start(bk, m):
            _row_copy(zero_scr, xs_hbm, 0, bk, rb * ds, zsem).start()
            return m

        lax.fori_loop(n_used, n_blocks, blk_start, 0)

        def blk_wait(bk, m):
            _row_copy(zero_scr, xs_hbm, 0, 0, rb * ds, zsem).wait()
            return m

        lax.fori_loop(n_used, n_blocks, blk_wait, 0)

    def issue(r, m):
        src = v_ref.at[pl.ds(pl.multiple_of(r * ds, ds), ds), :]
        for kk in range(2):
            dst = xs_hbm.at[pl.ds(pl.multiple_of(dest_ref[0, kk, r] * ds, ds), ds), :]
            pltpu.make_async_copy(src, dst, sem).start(priority=kk)
        return m

    lax.fori_loop(0, tt, issue, 0, unroll=8)
    for kk in range(2):
        pltpu.make_async_copy(v_ref, xs_hbm.at[pl.ds(0, tt * ds), :], sem).wait()


def _dispatch(dest3, pad_info, v_ts, *, n_tok, n_blocks, tt, ds):
    n_exp = pad_info.shape[1]
    rb = EXPERT_BLOCK
    return pl.pallas_call(
        functools.partial(_dispatch_body, tt=tt, ds=ds, n_exp=n_exp, n_blocks=n_blocks, rb=rb),
        grid=(n_tok // tt,),
        in_specs=[pl.BlockSpec((1, 2, tt), lambda i: (i, 0, 0), memory_space=pltpu.SMEM),
                  pl.BlockSpec(memory_space=pltpu.SMEM),
                  pl.BlockSpec((tt * ds, LANES), lambda i: (i, 0))],
        out_specs=pl.BlockSpec(memory_space=pl.ANY),
        out_shape=jax.ShapeDtypeStruct((n_blocks * rb * ds, LANES), v_ts.dtype),
        scratch_shapes=[pltpu.VMEM((rb * ds, LANES), v_ts.dtype), pltpu.SemaphoreType.DMA(()),
                        pltpu.SemaphoreType.DMA(())],
        compiler_params=_cparams(("arbitrary",)),
        name="dispatch",
    )(dest3, pad_info, v_ts)


def _expert_body(blk0_ref, nblk_ref, nu_ref, x_hbm, wg_ref, wu_ref, wd_ref, y_hbm, x_buf, y_buf, x_s, sem_x, sem_y,
                 *, ds, rb, n_blocks):
    e = pl.program_id(0)
    b0 = blk0_ref[e]
    nb = nblk_ref[e]
    n_used = nu_ref[0]
    rows = rb * ds

    def x_copy(g, slot):
        return pltpu.make_async_copy(x_hbm.at[pl.ds(pl.multiple_of(g * rows, rows), rows), :],
                                     x_buf.at[slot], sem_x.at[slot])

    def y_copy(g, slot):
        return pltpu.make_async_copy(y_buf.at[slot],
                                     y_hbm.at[pl.ds(pl.multiple_of(g * rows, rows), rows), :], sem_y.at[slot])

    @pl.when(jnp.logical_and(e == 0, n_used > 0))
    def _():
        x_copy(0, 0).start()

    def block(i, carry):
        g = b0 + i
        slot = g % 2
        x_copy(g, slot).wait()

        @pl.when(g + 1 < n_used)
        def _():
            x_copy(g + 1, 1 - slot).start()

        @pl.when(g >= 2)
        def _():
            y_copy(g - 2, slot).wait()

        half = ds * LANES
        for j in range(ds):
            lo, hi = _unpack_pair(x_buf[slot, pl.ds(j, rb, stride=ds), :])
            x_s[:, j * LANES:(j + 1) * LANES] = lo.astype(bf16)
            x_s[:, half + j * LANES:half + (j + 1) * LANES] = hi.astype(bf16)
        x = x_s[...]
        hg = jnp.dot(x, wg_ref[0, 0].astype(bf16), preferred_element_type=f32)
        hu = jnp.dot(x, wu_ref[0, 0].astype(bf16), preferred_element_type=f32)
        a = (hg * _sigmoid(hg) * hu).astype(bf16)
        y = jnp.dot(a, wd_ref[0, 0].astype(bf16), preferred_element_type=f32)
        for j in range(ds):
            y_buf[slot, pl.ds(j, rb, stride=ds), :] = _pack_pair(y[:, j * LANES:(j + 1) * LANES],
                                                                 y[:, half + j * LANES:half + (j + 1) * LANES])
        y_copy(g, slot).start()
        return carry

    lax.fori_loop(0, nb, block, 0)

    @pl.when(e == pl.num_programs(0) - 1)
    def _():
        @pl.when(n_used >= 2)
        def _():
            y_copy(n_used - 2, n_used % 2).wait()

        @pl.when(n_used >= 1)
        def _():
            y_copy(n_used - 1, (n_used - 1) % 2).wait()

        y_buf[0] = jnp.zeros(y_buf.shape[1:], y_buf.dtype)

        def start(bk, m):
            _row_copy(y_buf.at[0], y_hbm, 0, bk, rows, sem_y.at[0]).start()
            return m

        lax.fori_loop(n_used, n_blocks, start, 0)

        def wait(bk, m):
            _row_copy(y_buf.at[0], y_hbm, 0, 0, rows, sem_y.at[0]).wait()
            return m

        lax.fori_loop(n_used, n_blocks, wait, 0)


def _experts(blk0, nblk, n_used, xs_ts, w_g, w_u, w_d, *, layer, ds):
    _, E, D, De = w_g.shape
    rb = EXPERT_BLOCK
    n_blocks = xs_ts.shape[0] // (rb * ds)
    grid_spec = pltpu.PrefetchScalarGridSpec(
        num_scalar_prefetch=3,
        grid=(E,),
        in_specs=[pl.BlockSpec(memory_space=pl.ANY),
                  pl.BlockSpec((1, 1, D, De), lambda e, *_: (layer, e, 0, 0)),
                  pl.BlockSpec((1, 1, D, De), lambda e, *_: (layer, e, 0, 0)),
                  pl.BlockSpec((1, 1, De, D), lambda e, *_: (layer, e, 0, 0))],
        out_specs=pl.BlockSpec(memory_space=pl.ANY),
        scratch_shapes=[pltpu.VMEM((2, rb * ds, LANES), xs_ts.dtype), pltpu.VMEM((2, rb * ds, LANES), xs_ts.dtype),
                        pltpu.VMEM((rb, D), bf16), pltpu.SemaphoreType.DMA((2,)), pltpu.SemaphoreType.DMA((2,))],
    )
    return pl.pallas_call(
        functools.partial(_expert_body, ds=ds, rb=rb, n_blocks=n_blocks),
        grid_spec=grid_spec,
        out_shape=jax.ShapeDtypeStruct(xs_ts.shape, xs_ts.dtype),
        compiler_params=_cparams(("arbitrary",)),
        name="experts",
    )(blk0, nblk, n_used, xs_ts, w_g, w_u, w_d)


def _combine_body(dest_ref, dnext_ref, gate_ref, h_ref, gt_ref, fg_ref, y_hbm, o_ref, y_scr, sem, *, tt, ds, final):
    i = pl.program_id(0)
    slot = i % 2

    def gather(d_ref, to_slot):
        def body(r, m):
            for kk in range(2):
                _row_copy(y_hbm, y_scr.at[to_slot, kk], d_ref[0, kk, r], r, ds, sem.at[to_slot]).start(priority=kk)
            return m

        lax.fori_loop(0, tt, body, 0, unroll=8)

    @pl.when(i == 0)
    def _():
        gather(dest_ref, 0)

    @pl.when(i + 1 < pl.num_programs(0))
    def _():
        gather(dnext_ref, 1 - slot)

    for kk in range(2):
        pltpu.make_async_copy(y_hbm.at[pl.ds(0, tt * ds), :], y_scr.at[slot, kk], sem.at[slot]).wait()
    gt = gt_ref[0]
    g0 = gate_ref[:, 0:1]
    g1 = gate_ref[:, 1:2]
    half = ds * LANES
    for j in range(ds):
        lo0, hi0 = _unpack_pair(y_scr[slot, 0, pl.ds(j, tt, stride=ds), :])
        lo1, hi1 = _unpack_pair(y_scr[slot, 1, pl.ds(j, tt, stride=ds), :])
        for base, y0, y1 in ((0, lo0, lo1), (half, hi0, hi1)):
            sl = slice(base + j * LANES, base + (j + 1) * LANES)
            o_ref[:, sl] = h_ref[:, sl] + gt[:, sl] * (g0 * y0 + g1 * y1)
    if final:
        x = o_ref[...]
        o_ref[...] = x * lax.rsqrt(jnp.mean(x * x, -1, keepdims=True) + EPS) * fg_ref[...]


def _combine(dest3, gates_t, h1, mod, final_g, y_ts, *, n_tok, tt, ds, tpb, nlat, nb, final):
    D = h1.shape[1]
    nt = n_tok // tt
    return pl.pallas_call(
        functools.partial(_combine_body, tt=tt, ds=ds, final=final),
        grid=(nt,),
        in_specs=[pl.BlockSpec((1, 2, tt), lambda i: (i, 0, 0), memory_space=pltpu.SMEM),
                  pl.BlockSpec((1, 2, tt), lambda i: (jnp.minimum(i + 1, nt - 1), 0, 0), memory_space=pltpu.SMEM),
                  pl.BlockSpec((tt, 2), lambda i: (i, 0)),
                  pl.BlockSpec((tt, D), lambda i: (i, 0)),
                  _mod_spec(5, D, tpb, nlat, nb),
                  pl.BlockSpec((1, D), lambda i: (0, 0)),
                  pl.BlockSpec(memory_space=pl.ANY)],
        out_specs=pl.BlockSpec((tt, D), lambda i: (i, 0)),
        out_shape=jax.ShapeDtypeStruct((n_tok, D), f32),
        scratch_shapes=[pltpu.VMEM((2, 2, tt * ds, LANES), y_ts.dtype), pltpu.SemaphoreType.DMA((2,))],
        compiler_params=_cparams(("arbitrary",)),
        name="combine",
    )(dest3, dest3, gates_t, h1, mod, final_g, y_ts)


def _moe(v_ts, logits, h1, mod, w_router_bias, w_g, w_u, w_d, final_g, *, layer, n_tok, tt_dispatch, tt, ds, tpb,
         nlat, nb, final):
    E = w_g.shape[1]
    route_tt = next(t for t in (512, 256, LANES) if n_tok % t == 0)
    idx, gates, counts = _route(logits[:n_tok].T, w_router_bias.reshape(E, 1), tt=route_tt)
    counts = counts[:, 0].astype(jnp.int32)
    padded = (counts + EXPERT_BLOCK - 1) // EXPERT_BLOCK * EXPERT_BLOCK
    pad_end = jnp.cumsum(padded)
    pad_start = pad_end - padded
    n_blocks = (n_tok * 2 + EXPERT_BLOCK - 1) // EXPERT_BLOCK + E
    n_used = (pad_end[-1] // EXPERT_BLOCK).astype(jnp.int32).reshape(1)
    e_sel = idx[0:2][..., None] == jnp.arange(E, dtype=jnp.int32)
    dest = jnp.sum(jnp.where(e_sel, pad_start.astype(jnp.int32), 0), -1) + idx[2:4]
    dest3 = dest.reshape(2, n_tok // tt, tt).transpose(1, 0, 2)
    dest3_d = dest.reshape(2, n_tok // tt_dispatch, tt_dispatch).transpose(1, 0, 2)
    pad_info = jnp.stack([pad_start + counts, padded - counts, jnp.broadcast_to(n_used, (E,))]).astype(jnp.int32)
    xs_ts = _dispatch(dest3_d, pad_info, v_ts, n_tok=n_tok, n_blocks=n_blocks, tt=tt_dispatch, ds=ds)
    y_ts = _experts((pad_start // EXPERT_BLOCK).astype(jnp.int32), (padded // EXPERT_BLOCK).astype(jnp.int32), n_used,
                    xs_ts, w_g, w_u, w_d, layer=layer, ds=ds)
    return _combine(dest3, gates.T, h1, mod, final_g, y_ts, n_tok=n_tok, tt=tt, ds=ds, tpb=tpb, nlat=nlat,
                    nb=nb, final=final)


def kernel(x, c, ctx, c_ctx, w_mod, b_mod, norm1_g, norm2_g, w_in, conv_w, a_log, dt_bias, onorm_g, w_pool, pool_scale, w_up_dn, w_up_pool, w_out, w_router, router_bias, w_gate_e, w_up_e, w_down_e, final_g):
    B, S, D = x.shape
    C = ctx.shape[1]
    L = w_mod.shape[0]
    H = a_log.shape[-1]
    dk = onorm_g.shape[-1]
    assert dk == LANES
    dn_w = H * dk
    _, ng, gw, _ = w_pool.shape
    pool_w = ng * gw
    ds = D // (2 * LANES)
    n_lat = B * S
    T = n_lat + B * C
    tm = 256
    assert S % tm == 0 and (B * C) % tm == 0 and C % CHUNK == 0 and S % (4 * GRID_W) == 0
    tpb = S // tm
    nlat_tiles = n_lat // tm
    tm_in = next(t for t in (1024, 512, 256) if S % t == 0 and (B * C) % t == 0)
    tm_mg = next(t for t in (512, 256) if S % t == 0 and (B * C) % t == 0)
    tr = min(256, C)
    G = min(H, 16)

    c8 = jnp.zeros((8, D), f32).at[:B].set(c).at[B].set(c_ctx)
    mod = _mod_table(c8, w_mod, b_mod)

    qkv_c = 3 * dn_w
    scan_c = qkv_c + 4 * H
    og_off, pool_off = scan_c, scan_c + dn_w
    mg_off = pool_off + pool_w
    assert (4 * dn_w) % D == 0 and (4 * dn_w + 2 * D) % pool_w == 0
    pool_off_blocks = (4 * dn_w + 2 * D) // pool_w

    h = jnp.concatenate([x.reshape(n_lat, D), ctx.reshape(B * C, D)], 0)
    for l in range(L):
        last = l == L - 1
        mod_l = mod[l].reshape(8, 1, mod.shape[-1])
        wl = w_in[l]
        w_z = jnp.concatenate([wl[:, :qkv_c], wl[:, og_off:pool_off], wl[:, mg_off:], wl[:, pool_off:mg_off]],
                              1).astype(bf16)
        w_ba = jnp.pad(wl[:, qkv_c:scan_c], ((0, 0), (0, LANES - 4 * H))).astype(bf16)
        tn = 1024 if w_z.shape[1] % 1024 == 0 else w_z.shape[1]
        z, ba = _in_proj(h, mod_l, 0, norm1_g[l].reshape(1, D), w_z, w_ba, tm=tm_in, tn=tn, tpb=S // tm_in,
                         nlat=n_lat // tm_in, nb=B)
        qkv = _conv_prep(z, conv_w[l], tr=tr, tc=min(2048, dn_w), S=S, C=C, nlat_rows=n_lat, dn_w=dn_w, dk=dk)
        o_dn = _deltanet(qkv, ba, a_log[l], dt_bias[l], B=B, S=S, C=C, H=H, G=G)
        pooled_lat, pooled_ctx = _pool(z, B=B, S=S, C=C, pool_off_blocks=pool_off_blocks, pool_w=pool_w, gw=gw)
        n_rows = n_lat if last else T
        merged = _merge_a(o_dn, z, pooled_lat, pooled_ctx, onorm_g[l].reshape(1, dk), w_pool,
                          pool_scale[l].reshape(1, pool_w), w_up_dn, w_up_pool,
                          layer=l, n_rows=n_rows, tm=tm, H=H, D=D)
        h1, v_ts, logits = _merge_b(merged, h, mod_l, norm2_g[l].reshape(1, D), w_out, w_router,
                                    layer=l, n_rows=n_rows, tm=tm_mg, tpb=S // tm_mg, nlat=n_lat // tm_mg, nb=B)
        h = _moe(v_ts, logits, h1, mod_l, router_bias, w_gate_e, w_up_e, w_down_e,
                 final_g.reshape(1, D), layer=l, n_tok=n_rows, tt_dispatch=tm_mg, tt=tm, ds=ds, tpb=tpb, nlat=nlat_tiles, nb=B,
                 final=last)
    return h.reshape(B, S, D)
```

```python
import functools

import jax
import jax.numpy as jnp
from jax import lax
from jax.experimental import pallas as pl
from jax.experimental.pallas import tpu as pltpu

f32 = jnp.float32
bf16 = jnp.bfloat16
HIGHEST = lax.Precision.HIGHEST

EPS = 1e-6
LANES = 128
GRID_W = 64
CHUNK = 64
POOL_WINDOWS = (2, 4, 8, 16)
GROUP_SIZE = 8
EXPERT_BLOCK = 256
VMEM_LIMIT = 56 * 1024 * 1024


def _cparams(sem, vmem=VMEM_LIMIT):
    return pltpu.CompilerParams(dimension_semantics=sem, vmem_limit_bytes=vmem)


def _sigmoid(x):
    return 1.0 / (1.0 + jnp.exp(-x))


def _softplus(x):
    return jnp.maximum(x, 0.0) + jnp.log(1.0 + jnp.exp(-jnp.abs(x)))


def _pack_pair(lo, hi):
    def rounded(x):
        u = lax.bitcast_convert_type(x, jnp.uint32)
        return u + jnp.uint32(0x7FFF) + ((u >> 16) & jnp.uint32(1))

    return (rounded(lo) >> 16) | (rounded(hi) & jnp.uint32(0xFFFF0000))


def _unpack_pair(w):
    lo = lax.bitcast_convert_type(w << 16, f32)
    hi = lax.bitcast_convert_type(w & jnp.uint32(0xFFFF0000), f32)
    return lo, hi


def _mod_spec(sec, D, tiles_per_batch, n_lat_tiles, n_batch):
    def index(i, *_):
        return (jnp.where(i < n_lat_tiles, i // tiles_per_batch, n_batch), 0, sec)

    return pl.BlockSpec((1, 1, D), index)


def _mod_body(c_ref, w_ref, b_ref, o_ref):
    c = c_ref[...]
    s = c * _sigmoid(c)
    sh = s.astype(bf16)
    sl = (s - sh.astype(f32)).astype(bf16)
    w = w_ref[0]
    wh = w.astype(bf16)
    wl = (w - wh.astype(f32)).astype(bf16)
    t = jnp.dot(jnp.concatenate([sh, sl], axis=0), wh, preferred_element_type=f32)
    n = s.shape[0]
    o_ref[0] = t[:n] + (t[n:] + jnp.dot(sh, wl, preferred_element_type=f32)) + b_ref[0]


def _mod_table(c8, w_mod, b_mod):
    L, D, N = w_mod.shape
    tn = next(t for t in (1024, 512, 256, LANES) if N % t == 0)
    return pl.pallas_call(
        _mod_body,
        grid=(L, N // tn),
        in_specs=[pl.BlockSpec((8, D), lambda l, j: (0, 0)),
                  pl.BlockSpec((1, D, tn), lambda l, j: (l, 0, j)),
                  pl.BlockSpec((1, 1, tn), lambda l, j: (l, 0, j))],
        out_specs=pl.BlockSpec((1, 8, tn), lambda l, j: (l, 0, j)),
        out_shape=jax.ShapeDtypeStruct((L, 8, N), f32),
        compiler_params=_cparams(("parallel", "parallel")),
        name="mod_table",
    )(c8, w_mod, b_mod.reshape(L, 1, N))


def _inproj_body(h_ref, sc_ref, sh_ref, g_ref, w_ref, wba_ref, z_ref, ba_ref, u_scr):
    j = pl.program_id(1)

    @pl.when(j == 0)
    def _():
        x = h_ref[...]
        y = x * lax.rsqrt(jnp.mean(x * x, -1, keepdims=True) + EPS) * g_ref[...]
        u = (y * (1.0 + sc_ref[0]) + sh_ref[0]).astype(bf16)
        u_scr[...] = u
        ba_ref[...] = jnp.dot(u, wba_ref[...], preferred_element_type=f32)

    z_ref[...] = jnp.dot(u_scr[...], w_ref[...], preferred_element_type=f32).astype(z_ref.dtype)


def _in_proj(h, mod, sec, gain, w_z, w_ba, *, tm, tn, tpb, nlat, nb):
    T, D = h.shape
    Nz = w_z.shape[1]
    return pl.pallas_call(
        _inproj_body,
        grid=(T // tm, Nz // tn),
        in_specs=[pl.BlockSpec((tm, D), lambda i, j: (i, 0)),
                  _mod_spec(sec + 1, D, tpb, nlat, nb),
                  _mod_spec(sec, D, tpb, nlat, nb),
                  pl.BlockSpec((1, D), lambda i, j: (0, 0)),
                  pl.BlockSpec((D, tn), lambda i, j: (0, j)),
                  pl.BlockSpec((D, LANES), lambda i, j: (0, 0))],
        out_specs=[pl.BlockSpec((tm, tn), lambda i, j: (i, j)),
                   pl.BlockSpec((tm, LANES), lambda i, j: (i, 0))],
        out_shape=[jax.ShapeDtypeStruct((T, Nz), bf16), jax.ShapeDtypeStruct((T, LANES), f32)],
        scratch_shapes=[pltpu.VMEM((tm, D), bf16)],
        compiler_params=_cparams(("parallel", "arbitrary")),
        name="in_proj",
    )(h, mod, mod, gain, w_z, w_ba)


def _conv_body(x_ref, xp_ref, xn_ref, w_ref, sh_ref, o_ref, *, tr, tps_lat, tps_ctx, nlat, qscale, tiles_per_sec,
               halo):
    i = pl.program_id(0)
    il =jnp.where(i < nlat, i % tps_lat, (i - nlat) % tps_ctx)
    nt = jnp.where(i < nlat, tps_lat, tps_ctx)
    first = il == 0
    last = il == nt - 1
    xb = x_ref[...]
    w = w_ref[...]
    tc_ = xb.shape[1]
    ys = jnp.dot(sh_ref[...], xb, preferred_element_type=f32).reshape(tr // 8, 32, tc_)
    acc = (ys[:, 0:8] * w[0:1] + ys[:, 8:16] * w[1:2] + ys[:, 16:24] * w[3:4] + ys[:, 24:32] * w[4:5])
    acc = acc.reshape(tr, tc_) + xb.astype(f32) * w[2:3]
    p = jnp.where(first, 0.0, xp_ref[...].astype(f32))[halo - 8:]
    n = jnp.where(last, 0.0, xn_ref[...].astype(f32))[:8]
    r8 = lax.broadcasted_iota(jnp.int32, (8, 1), 0)
    top = jnp.where(r8 == 0, p[6:7] * w[0:1] + p[7:8] * w[1:2], jnp.where(r8 == 1, p[7:8] * w[0:1], 0.0))
    bot = jnp.where(r8 == 7, n[0:1] * w[3:4] + n[1:2] * w[4:5], jnp.where(r8 == 6, n[0:1] * w[4:5], 0.0))
    acc = jnp.concatenate([acc[:8] + top, acc[8:tr - 8], acc[tr - 8:] + bot], axis=0)
    y = acc * _sigmoid(acc)
    sec = pl.program_id(1) // tiles_per_sec

    @pl.when(sec == 2)
    def _():
        o_ref[...] = y.astype(o_ref.dtype)

    @pl.when(sec != 2)
    def _():
        qk_scale = jnp.where(sec == 0, qscale, 1.0)
        for s in range(acc.shape[1] // LANES):
            ys = y[:, s * LANES:(s + 1) * LANES]
            rs = lax.rsqrt(jnp.sum(ys * ys, -1, keepdims=True) + EPS) * qk_scale
            o_ref[:, s * LANES:(s + 1) * LANES] = (ys * rs).astype(o_ref.dtype)


def _conv_prep(z, conv_w, *, tr, tc, S, C, nlat_rows, dn_w, dk):
    T = z.shape[0]
    halo = 16
    hb = tr // halo
    nh = T // halo
    body = functools.partial(_conv_body, tr=tr, tps_lat=S // tr, tps_ctx=C // tr, nlat=nlat_rows // tr,
                             qscale=float(dk) ** -0.5, tiles_per_sec=dn_w // tc, halo=halo)
    assert conv_w.shape[0] == 5
    ri = jnp.arange(tr)[:, None]
    ci = jnp.arange(tr)[None, :]
    shifts = jnp.stack([(ci == ri + off).astype(bf16) for off in (-2, -1, 1, 2)], 0)
    shifts = shifts.reshape(4, tr // 8, 8, tr).transpose(1, 0, 2, 3).reshape(4 * tr, tr)
    return pl.pallas_call(
        body,
        grid=(T // tr, 3 * dn_w // tc),
        in_specs=[pl.BlockSpec((tr, tc), lambda i, j: (i, j)),
                  pl.BlockSpec((halo, tc), lambda i, j: (jnp.maximum(i * hb - 1, 0), j)),
                  pl.BlockSpec((halo, tc), lambda i, j: (jnp.minimum((i + 1) * hb, nh - 1), j)),
                  pl.BlockSpec((conv_w.shape[0], tc), lambda i, j: (0, j)),
                  pl.BlockSpec((4 * tr, tr), lambda i, j: (0, 0))],
        out_specs=pl.BlockSpec((tr, tc), lambda i, j: (i, j)),
        out_shape=jax.ShapeDtypeStruct((T, 3 * dn_w), bf16),
        compiler_params=_cparams(("parallel", "parallel")),
        name="conv_prep",
    )(z, z, z, conv_w, shifts)


def _mm(x, y):
    return jnp.dot(x.astype(bf16), y.astype(bf16), preferred_element_type=f32)


def _unit_tri_inverse(a_list, ri, ci):
    b16 = (ri >> 4) == (ci >> 4)
    b32 = (ri >> 5) == (ci >> 5)
    eye = jnp.where(ri == ci, 1.0, 0.0)
    n = a_list[0].shape[0]

    def mmb(x, y):
        return jnp.dot(x, y, preferred_element_type=f32)

    a16 = [jnp.where(b16, a, 0.0) for a in a_list]
    t = [eye - x for x in a16]
    pb = [x.astype(bf16) for x in a16]
    pb = [mmb(x, x).astype(bf16) for x in pb]
    for step in range(2):
        tp = [mmb(jnp.concatenate([ti.astype(bf16), pi], axis=0), pi) for ti, pi in zip(t, pb)]
        t = [ti + x[:n] for ti, x in zip(t, tp)]
        pb = [x[n:].astype(bf16) for x in tp]
    t = [ti + mmb(ti.astype(bf16), pi) for ti, pi in zip(t, pb)]
    for off in ([jnp.where(b32, a, 0.0) - x for a, x in zip(a_list, a16)],
                [jnp.where(b32, 0.0, a) for a in a_list]):
        tb = [ti.astype(bf16) for ti in t]
        ta = [mmb(ti, ai.astype(bf16)).astype(bf16) for ti, ai in zip(tb, off)]
        t = [ti - mmb(tai, tbi) for ti, tai, tbi in zip(t, ta, tb)]
    return t


def _dn_body(q_ref, k_ref, v_ref, ba_ref, al_ref, dt_ref, o_ref, s_scr, *, G):
    d = pl.program_id(1)
    c = pl.program_id(3)
    dk = LANES
    hs = range(G)

    @pl.when(c == 0)
    def _():
        s_scr[...] = jnp.zeros_like(s_scr)

    ba = ba_ref[0]
    beta_t = _sigmoid(ba)
    g_t = -jnp.exp(al_ref[0]) * _softplus(ba + dt_ref[0])
    ri = lax.broadcasted_iota(jnp.int32, (CHUNK, CHUNK), 0)
    ci = lax.broadcasted_iota(jnp.int32, (CHUNK, CHUNK), 1)
    fwd = d == 0
    later = jnp.where(fwd, ri, ci)
    earlier = jnp.where(fwd, ci, ri)
    incl = later >= earlier
    strict = later > earlier
    tri = jnp.where(incl, 1.0, 0.0)
    gc = jnp.dot(tri, g_t, preferred_element_type=f32, precision=HIGHEST)
    gct = gc.T
    gtot = jnp.sum(g_t, axis=0, keepdims=True)

    def pick(x, col):
        return jnp.where(fwd, x[:, col:col + 1], x[:, G + col:G + col + 1])

    sl = [slice(h * dk, (h + 1) * dk) for h in hs]
    beta_c = [pick(beta_t, h) for h in hs]
    gc_c = [pick(gc, 2 * G + h) for h in hs]
    gc_r = [jnp.where(fwd, gct[2 * G + h:2 * G + h + 1], gct[3 * G + h:3 * G + h + 1]) for h in hs]
    gt = [pick(gtot, 2 * G + h) for h in hs]
    q = [q_ref[:, sl[h]].astype(f32) for h in hs]
    k = [k_ref[:, sl[h]].astype(f32) for h in hs]
    kb = [k[h] * beta_c[h] for h in hs]
    eg = [jnp.exp(gc_c[h]) for h in hs]
    dec = [jnp.where(incl, jnp.exp(jnp.where(incl, gc_c[h] - gc_r[h], 0.0)), 0.0) for h in hs]
    kq = [lax.dot_general(jnp.concatenate([kb[h], q[h]], axis=0).astype(bf16), k_ref[:, sl[h]],
                          (((1,), (1,)), ((), ())), preferred_element_type=f32) for h in hs]
    a = [jnp.where(strict, kq[h][:CHUNK] * dec[h], 0.0) for h in hs]
    t = _unit_tri_inverse(a, ri, ci)
    rhs = [jnp.concatenate([v_ref[:, sl[h]].astype(f32) * beta_c[h], kb[h] * eg[h]], axis=1) for h in hs]
    sol = [_mm(t[h], rhs[h]) for h in hs]
    s = [s_scr[h] for h in hs]
    r = [_mm(jnp.concatenate([sol[h][:, dk:], q[h] * eg[h]], axis=0), s[h]) for h in hs]
    vnb = [(sol[h][:, :dk] - r[h][:CHUNK]).astype(bf16) for h in hs]
    qkd = [jnp.concatenate([(kq[h][CHUNK:] * dec[h]).astype(bf16),
                            (k[h] * jnp.exp(gt[h] - gc_c[h])).T.astype(bf16)], axis=0) for h in hs]
    ou = [jnp.dot(qkd[h], vnb[h], preferred_element_type=f32) for h in hs]
    for h in hs:
        s_scr[h] = s[h] * jnp.exp(gt[h]) + ou[h][CHUNK:]
        o_ref[0, :, sl[h]] = (r[h][CHUNK:] + ou[h][:CHUNK]).astype(o_ref.dtype)


def _deltanet(qkv, ba, a_log, dt_bias, *, B, S, C, H, G):
    T = qkv.shape[0]
    dn_w = H * LANES
    ncl = S // CHUNK
    ncc = C // CHUNK
    nc = ncl + ncc
    hgs = H // G

    def rowblk(b, d, c):
        cc = jnp.where(d == 0, c, ncc - 1 - c)
        cl = jnp.where(d == 0, c - ncc, nc - 1 - c)
        return jnp.where(c < ncc, B * ncl + b * ncc + cc, b * ncl + cl)

    def grouped(t):
        r = t.shape[0]
        t = t.reshape(r, 4, hgs, G).transpose(2, 0, 1, 3).reshape(hgs, r, 4 * G)
        return jnp.pad(t, ((0, 0), (0, 0), (0, LANES - 4 * G)))

    zeros = jnp.zeros((1, 2 * H), f32)
    ba_g = grouped(ba[:, :4 * H])
    al_g = grouped(jnp.concatenate([zeros, a_log.reshape(1, 2 * H)], 1))
    dt_g = grouped(jnp.concatenate([zeros, dt_bias.reshape(1, 2 * H)], 1))
    body = functools.partial(_dn_body, G=G)
    return pl.pallas_call(
        body,
        grid=(B, 2, hgs, nc),
        in_specs=[pl.BlockSpec((CHUNK, G * LANES), lambda b, d, g, c: (rowblk(b, d, c), g)),
                  pl.BlockSpec((CHUNK, G * LANES), lambda b, d, g, c: (rowblk(b, d, c), hgs + g)),
                  pl.BlockSpec((CHUNK, G * LANES), lambda b, d, g, c: (rowblk(b, d, c), 2 * hgs + g)),
                  pl.BlockSpec((1, CHUNK, LANES), lambda b, d, g, c: (g, rowblk(b, d, c), 0)),
                  pl.BlockSpec((1, 1, LANES), lambda b, d, g, c: (g, 0, 0)),
                  pl.BlockSpec((1, 1, LANES), lambda b, d, g, c: (g, 0, 0))],
        out_specs=pl.BlockSpec((1, CHUNK, G * LANES), lambda b, d, g, c: (d, rowblk(b, d, c), g)),
        out_shape=jax.ShapeDtypeStruct((2, T, dn_w), bf16),
        scratch_shapes=[pltpu.VMEM((G, LANES, LANES), f32)],
        compiler_params=_cparams(("parallel", "parallel", "parallel", "arbitrary")),
        name="deltanet",
    )(qkv, qkv, qkv, ba_g, al_g, dt_g)


def _pool_ctx_body(x_ref, o_ref, *, gw):
    C = x_ref.shape[0]
    ri = lax.broadcasted_iota(jnp.int32, (C, C), 0)
    ci = lax.broadcasted_iota(jnp.int32, (C, C), 1)
    t = lax.broadcasted_iota(jnp.int32, (C, 1), 0)
    for gi, w in enumerate(POOL_WINDOWS):
        sl = slice(gi * gw, (gi + 1) * gw)
        band = jnp.where((ci >= ri - w // 2) & (ci < ri + w - w // 2), 1.0, 0.0).astype(bf16)
        cnt = (jnp.minimum(t + w - w // 2, C) - jnp.maximum(t - w // 2, 0)).astype(f32)
        x = x_ref[:, sl]
        box = jnp.dot(band, x, preferred_element_type=f32)
        o_ref[:, sl] = (box / cnt - x.astype(f32)).astype(o_ref.dtype)


def _pool_lat_body(x_ref, o_ref, y_scr):
    g = pl.program_id(1)
    S = x_ref.shape[0]
    rows = S // GRID_W
    sb = 4 * GRID_W
    ri = lax.broadcasted_iota(jnp.int32, (sb, sb), 0)
    ci = lax.broadcasted_iota(jnp.int32, (sb, sb), 1)
    cr = ri & (GRID_W - 1)
    cc = ci & (GRID_W - 1)
    same = (ri >> 6) == (ci >> 6)
    col = lax.broadcasted_iota(jnp.int32, (GRID_W, 1), 0)
    for gi, w in enumerate(POOL_WINDOWS):
        @pl.when(g == gi)
        def _(w=w):
            band = jnp.where(same & (cc >= cr - w // 2) & (cc < cr + w - w // 2), 1.0, 0.0).astype(bf16)
            for s in range(S // sb):
                y_scr[s * sb:(s + 1) * sb, :] = jnp.dot(band, x_ref[s * sb:(s + 1) * sb, :],
                                                        preferred_element_type=f32)
            ncol = (jnp.minimum(col + w - w // 2, GRID_W) - jnp.maximum(col - w // 2, 0)).astype(f32)
            for r in range(rows):
                lo = max(r - w // 2, 0)
                hi = min(r + w - w // 2, rows)
                acc = y_scr[lo * GRID_W:(lo + 1) * GRID_W, :]
                for rr in range(lo + 1, hi):
                    acc = acc + y_scr[rr * GRID_W:(rr + 1) * GRID_W, :]
                cnt = ncol * float(hi - lo)
                xs = x_ref[r * GRID_W:(r + 1) * GRID_W, :].astype(f32)
                o_ref[r * GRID_W:(r + 1) * GRID_W, :] = (acc / cnt - xs).astype(o_ref.dtype)


def _pool(z, *, B, S, C, pool_off_blocks, pool_w, gw):
    nlat_c = B * S // C
    ng = pool_w // gw
    assert ng == len(POOL_WINDOWS)
    pooled_ctx = pl.pallas_call(
        functools.partial(_pool_ctx_body, gw=gw),
        grid=(B,),
        in_specs=[pl.BlockSpec((C, pool_w), lambda b: (nlat_c + b, pool_off_blocks))],
        out_specs=pl.BlockSpec((C, pool_w), lambda b: (b, 0)),
        out_shape=jax.ShapeDtypeStruct((B * C, pool_w), bf16),
        compiler_params=_cparams(("parallel",)),
        name="pool_ctx",
    )(z)
    off = pool_off_blocks * ng
    pooled_lat = pl.pallas_call(
        _pool_lat_body,
        grid=(B, ng),
        in_specs=[pl.BlockSpec((S, gw), lambda b, g: (b, off + g))],
        out_specs=pl.BlockSpec((S, gw), lambda b, g: (b, g)),
        out_shape=jax.ShapeDtypeStruct((B * S, pool_w), bf16),
        scratch_shapes=[pltpu.VMEM((S, gw), f32)],
        compiler_params=_cparams(("parallel", "parallel")),
        name="pool_lat",
    )(z)
    return pooled_lat, pooled_ctx


def _merge_a_body(o_ref, og_ref, m1_ref, m2_ref, pl_ref, pc_ref, ong_ref, wpool_ref, ps_ref, wud_ref, wup_ref,
                  out_ref, a_scr, p_scr, *, H, gw, nlat):
    is_lat = pl.program_id(0) < nlat
    for h in range(H):
        sl = slice(h * LANES, (h + 1) * LANES)
        o = o_ref[0, :, sl].astype(f32) + o_ref[1, :, sl].astype(f32)
        og = og_ref[:, sl].astype(f32)
        y = o * lax.rsqrt(jnp.mean(o * o, -1, keepdims=True) + EPS) * ong_ref[...] * (og * _sigmoid(og))
        a_scr[:, sl] = y.astype(bf16)
    y_dn = jnp.dot(a_scr[...], wud_ref[0].astype(bf16), preferred_element_type=f32)
    for g in range(wpool_ref.shape[1]):
        sl = slice(g * gw, (g + 1) * gw)
        pooled = jnp.where(is_lat, pl_ref[:, sl], pc_ref[:, sl])
        yp = jnp.dot(pooled, wpool_ref[0, g].astype(bf16), preferred_element_type=f32) * ps_ref[:, sl]
        p_scr[:, sl] = yp.astype(bf16)
    y_pool = jnp.dot(p_scr[...], wup_ref[0].astype(bf16), preferred_element_type=f32)
    g1 = _sigmoid(m1_ref[...].astype(f32))
    g2 = _sigmoid(m2_ref[...].astype(f32))
    out_ref[...] = (g1 * y_dn + g2 * y_pool).astype(out_ref.dtype)


def _merge_a(o_dn, z, pooled_lat, pooled_ctx, onorm_g, w_pool, pool_scale, w_up_dn, w_up_pool, *, layer, n_rows, tm,
             H, D):
    dn_w = H * LANES
    pool_w = pooled_lat.shape[1]
    nlat = pooled_lat.shape[0] // tm
    _, ng, gw, _ = w_pool.shape
    once = pl.Buffered(1)
    og_blk = 3
    m_blk = 4 * dn_w // D
    const = lambda *_: (0, 0)
    return pl.pallas_call(
        functools.partial(_merge_a_body, H=H, gw=gw, nlat=nlat),
        grid=(n_rows // tm,),
        in_specs=[pl.BlockSpec((2, tm, dn_w), lambda i: (0, i, 0)),
                  pl.BlockSpec((tm, dn_w), lambda i: (i, og_blk)),
                  pl.BlockSpec((tm, D), lambda i: (i, m_blk)),
                  pl.BlockSpec((tm, D), lambda i: (i, m_blk + 1)),
                  pl.BlockSpec((tm, pool_w), lambda i: (jnp.minimum(i, nlat - 1), 0)),
                  pl.BlockSpec((tm, pool_w), lambda i: (jnp.maximum(i - nlat, 0), 0)),
                  pl.BlockSpec((1, LANES), const),
                  pl.BlockSpec((1, ng, gw, gw), lambda i: (layer, 0, 0, 0), pipeline_mode=once),
                  pl.BlockSpec((1, pool_w), const),
                  pl.BlockSpec((1, dn_w, D), lambda i: (layer, 0, 0), pipeline_mode=once),
                  pl.BlockSpec((1, pool_w, D), lambda i: (layer, 0, 0), pipeline_mode=once)],
        out_specs=pl.BlockSpec((tm, D), lambda i: (i, 0)),
        out_shape=jax.ShapeDtypeStruct((n_rows, D), bf16),
        scratch_shapes=[pltpu.VMEM((tm, dn_w), bf16), pltpu.VMEM((tm, pool_w), bf16)],
        compiler_params=_cparams(("parallel",)),
        name="merge_a",
    )(o_dn, z, z, z, pooled_lat, pooled_ctx, onorm_g, w_pool, pool_scale, w_up_dn, w_up_pool)


def _merge_b_body(m_ref, h_ref, gt_ref, sc_ref, sh_ref, g_ref, wout_ref, wr_ref, h1_ref, v_ref, lg_ref,
                  *, ds):
    tm = m_ref.shape[0]
    m = jnp.dot(m_ref[...], wout_ref[0].astype(bf16), preferred_element_type=f32)
    h1 = h_ref[...] + gt_ref[0] * m
    h1_ref[...] = h1
    y = h1 * lax.rsqrt(jnp.mean(h1 * h1, -1, keepdims=True) + EPS) * g_ref[...]
    v = y * (1.0 + sc_ref[0]) + sh_ref[0]
    E = lg_ref.shape[1]
    vh = v.astype(bf16)
    vl = (v - vh.astype(f32)).astype(bf16)
    th = jnp.dot(vh, wr_ref[...], preferred_element_type=f32)
    tl = jnp.dot(vl, wr_ref[...], preferred_element_type=f32)
    lg_ref[...] = th[:, :E] + (th[:, E:] + tl[:, :E])
    half = ds * LANES
    for j in range(ds):
        v_ref[pl.ds(j, tm, stride=ds), :] = _pack_pair(v[:, j * LANES:(j + 1) * LANES],
                                                       v[:, half + j * LANES:half + (j + 1) * LANES])


def _merge_b(merged, h, mod, gain2, w_out, w_router, *, layer, n_rows, tm, tpb, nlat, nb):
    D = h.shape[1]
    E = w_router.shape[1]
    ds = D // (2 * LANES)
    const = lambda *_: (0, 0)
    wr_hi = w_router.astype(bf16)
    wr_lo = (w_router - wr_hi.astype(f32)).astype(bf16)
    wr_cat = jnp.concatenate([wr_hi, wr_lo], 1)
    return pl.pallas_call(
        functools.partial(_merge_b_body, ds=ds),
        grid=(n_rows // tm,),
        in_specs=[pl.BlockSpec((tm, D), lambda i: (i, 0)),
                  pl.BlockSpec((tm, D), lambda i: (i, 0)),
                  _mod_spec(2, D, tpb, nlat, nb),
                  _mod_spec(4, D, tpb, nlat, nb),
                  _mod_spec(3, D, tpb, nlat, nb),
                  pl.BlockSpec((1, D), const),
                  pl.BlockSpec((1, D, D), lambda i: (layer, 0, 0), pipeline_mode=pl.Buffered(1)),
                  pl.BlockSpec((D, 2 * E), const)],
        out_specs=[pl.BlockSpec((tm, D), lambda i: (i, 0)),
                   pl.BlockSpec((tm * ds, LANES), lambda i: (i, 0)),
                   pl.BlockSpec((tm, E), lambda i: (i, 0))],
        out_shape=[jax.ShapeDtypeStruct((n_rows, D), f32),
                   jax.ShapeDtypeStruct((n_rows * ds, LANES), jnp.uint32),
                   jax.ShapeDtypeStruct((n_rows, E), f32)],
        compiler_params=_cparams(("parallel",)),
        name="merge_b",
    )(merged, h, mod, mod, mod, gain2, w_out, wr_cat)


def _route_body(lg_ref, bias_ref, idx_ref, gate_ref, cnt_ref, carry_scr, *, n_groups):
    i = pl.program_id(0)
    E, tt = lg_ref.shape

    @pl.when(i == 0)
    def _():
        carry_scr[...] = jnp.zeros_like(carry_scr)

    aff = _sigmoid(lg_ref[...])
    sel = aff + bias_ref[...]
    sub = lax.broadcasted_iota(jnp.int32, (GROUP_SIZE, tt), 0)
    neg = -1e30
    best = None
    for g in range(n_groups):
        s = sel[g * GROUP_SIZE:(g + 1) * GROUP_SIZE]
        a = aff[g * GROUP_SIZE:(g + 1) * GROUP_SIZE]
        m1 = jnp.max(s, axis=0, keepdims=True)
        i1 = jnp.min(jnp.where(s == m1, sub, GROUP_SIZE), axis=0, keepdims=True)
        s2 = jnp.where(sub == i1, neg, s)
        m2 = jnp.max(s2, axis=0, keepdims=True)
        i2 = jnp.min(jnp.where(s2 == m2, sub, GROUP_SIZE), axis=0, keepdims=True)
        a1 = jnp.sum(jnp.where(sub == i1, a, 0.0), axis=0, keepdims=True)
        a2 = jnp.sum(jnp.where(sub == i2, a, 0.0), axis=0, keepdims=True)
        cand = (m1 + m2, i1 + g * GROUP_SIZE, i2 + g * GROUP_SIZE, a1, a2)
        if best is None:
            best = cand
        else:
            better = cand[0] > best[0]
            best = tuple(jnp.where(better, cn, bs) for cn, bs in zip(cand, best))
    _, e1, e2, a1, a2 = best
    den = a1 + a2
    gate_ref[0:1, :] = a1 / den
    gate_ref[1:2, :] = a2 / den
    erow = lax.broadcasted_iota(jnp.int32, (E, tt), 0)
    oh1 = erow == e1
    oh2 = erow == e2
    cnt = jnp.where(oh1 | oh2, 1.0, 0.0)
    ti = lax.broadcasted_iota(jnp.int32, (tt, tt), 0)
    tj = lax.broadcasted_iota(jnp.int32, (tt, tt), 1)
    before = jnp.where(ti < tj, 1.0, 0.0).astype(bf16)
    pos = jnp.dot(cnt.astype(bf16), before, preferred_element_type=f32) + carry_scr[...]
    r1 = jnp.sum(jnp.where(oh1, pos, 0.0), axis=0, keepdims=True)
    r2 = jnp.sum(jnp.where(oh2, pos, 0.0), axis=0, keepdims=True)
    idx_ref[0:1, :] = e1
    idx_ref[1:2, :] = e2
    idx_ref[2:3, :] = r1.astype(jnp.int32)
    idx_ref[3:4, :] = r2.astype(jnp.int32)
    carry_scr[...] = carry_scr[...] + jnp.sum(cnt, axis=1, keepdims=True)
    cnt_ref[...] = jnp.broadcast_to(carry_scr[...], cnt_ref.shape)


def _route(logits_t, bias_col, *, tt):
    E, T = logits_t.shape
    return pl.pallas_call(
        functools.partial(_route_body, n_groups=E // GROUP_SIZE),
        grid=(T // tt,),
        in_specs=[pl.BlockSpec((E, tt), lambda i: (0, i)),
                  pl.BlockSpec((E, 1), lambda i: (0, 0))],
        out_specs=[pl.BlockSpec((4, tt), lambda i: (0, i)),
                   pl.BlockSpec((2, tt), lambda i: (0, i)),
                   pl.BlockSpec((E, LANES), lambda i: (0, 0))],
        out_shape=[jax.ShapeDtypeStruct((4, T), jnp.int32),
                   jax.ShapeDtypeStruct((2, T), f32),
                   jax.ShapeDtypeStruct((E, LANES), f32)],
        scratch_shapes=[pltpu.VMEM((E, 1), f32)],
        compiler_params=_cparams(("arbitrary",)),
        name="route",
    )(logits_t, bias_col)


def _row_copy(src, dst, s, t, n, sem):
    def rows(x):
        return pl.ds(x * n if isinstance(x, int) else pl.multiple_of(x * n, 8), n)

    return pltpu.make_async_copy(src.at[rows(s), :], dst.at[rows(t), :], sem)


def _dispatch_body(dest_ref, pad_ref, v_ref, xs_hbm, zero_scr, sem, zsem, *, tt, ds, n_exp, n_blocks, rb):
    i = pl.program_id(0)

    @pl.when(i == 0)
    def _():
        zero_scr[...] = jnp.zeros_like(zero_scr)

        def pad_copy(row, p):
            return pltpu.make_async_copy(zero_scr.at[pl.ds(0, p * ds), :],
                                         xs_hbm.at[pl.ds(pl.multiple_of(row * ds, ds), p * ds), :], zsem)

        def per_expert(wait):
            def body(e, m):
                row = pad_ref[0, e]
                npad = pad_ref[1, e]
                p = rb // 2
                while p >= 1:
                    hit = (npad & p) != 0

                    @pl.when(hit)
                    def _(row=row, p=p):
                        if wait:
                            pad_copy(row, p).wait()
                        else:
                            pad_copy(row, p).start()

                    row = row + jnp.where(hit, p, 0)
                    p //= 2
                return m

            lax.fori_loop(0, n_exp, body, 0)

        per_expert(False)
        per_expert(True)
        n_used = pad_ref[2, 0]

        def blk_start(bk, m):
            _row_copy(zero_scr, xs_hbm, 0, bk, rb * ds, zsem).start()
            return m

        lax.fori_loop(n_used, n_blocks, blk_start, 0)

        def blk_wait(bk, m):
            _row_copy(zero_scr, xs_hbm, 0, 0, rb * ds, zsem).wait()
            return m

        lax.fori_loop(n_used, n_blocks, blk_wait, 0)

    def issue(r, m):
        src = v_ref.at[pl.ds(pl.multiple_of(r * ds, ds), ds), :]
        for kk in range(2):
            dst = xs_hbm.at[pl.ds(pl.multiple_of(dest_ref[0, kk, r] * ds, ds), ds), :]
            pltpu.make_async_copy(src, dst, sem).start(priority=kk)
        return m

    lax.fori_loop(0, tt, issue, 0, unroll=8)
    for kk in range(2):
        pltpu.make_async_copy(v_ref, xs_hbm.at[pl.ds(0, tt * ds), :], sem).wait()


def _dispatch(dest3, pad_info, v_ts, *, n_tok, n_blocks, tt, ds):
    n_exp = pad_info.shape[1]
    rb = EXPERT_BLOCK
    return pl.pallas_call(
        functools.partial(_dispatch_body, tt=tt, ds=ds, n_exp=n_exp, n_blocks=n_blocks, rb=rb),
        grid=(n_tok // tt,),
        in_specs=[pl.BlockSpec((1, 2, tt), lambda i: (i, 0, 0), memory_space=pltpu.SMEM),
                  pl.BlockSpec(memory_space=pltpu.SMEM),
                  pl.BlockSpec((tt * ds, LANES), lambda i: (i, 0))],
        out_specs=pl.BlockSpec(memory_space=pl.ANY),
        out_shape=jax.ShapeDtypeStruct((n_blocks * rb * ds, LANES), v_ts.dtype),
        scratch_shapes=[pltpu.VMEM((rb * ds, LANES), v_ts.dtype), pltpu.SemaphoreType.DMA(()),
                        pltpu.SemaphoreType.DMA(())],
        compiler_params=_cparams(("arbitrary",)),
        name="dispatch",
    )(dest3, pad_info, v_ts)


def _expert_body(blk0_ref, nblk_ref, nu_ref, x_hbm, wg_ref, wu_ref, wd_ref, y_hbm, x_buf, y_buf, x_s, sem_x, sem_y,
                 *, ds, rb, n_blocks):
    e = pl.program_id(0)
    b0 = blk0_ref[e]
    nb = nblk_ref[e]
    n_used = nu_ref[0]
    rows = rb * ds

    def x_copy(g, slot):
        return pltpu.make_async_copy(x_hbm.at[pl.ds(pl.multiple_of(g * rows, rows), rows), :],
                                     x_buf.at[slot], sem_x.at[slot])

    def y_copy(g, slot):
        return pltpu.make_async_copy(y_buf.at[slot],
                                     y_hbm.at[pl.ds(pl.multiple_of(g * rows, rows), rows), :], sem_y.at[slot])

    @pl.when(jnp.logical_and(e == 0, n_used > 0))
    def _():
        x_copy(0, 0).start()

    def block(i, carry):
        g = b0 + i
        slot = g % 2
        x_copy(g, slot).wait()

        @pl.when(g + 1 < n_used)
        def _():
            x_copy(g + 1, 1 - slot).start()

        @pl.when(g >= 2)
        def _():
            y_copy(g - 2, slot).wait()

        half = ds * LANES
        for j in range(ds):
            lo, hi = _unpack_pair(x_buf[slot, pl.ds(j, rb, stride=ds), :])
            x_s[:, j * LANES:(j + 1) * LANES] = lo.astype(bf16)
            x_s[:, half + j * LANES:half + (j + 1) * LANES] = hi.astype(bf16)
        x = x_s[...]
        hg = jnp.dot(x, wg_ref[0, 0].astype(bf16), preferred_element_type=f32)
        hu = jnp.dot(x, wu_ref[0, 0].astype(bf16), preferred_element_type=f32)
        a = (hg * _sigmoid(hg) * hu).astype(bf16)
        y = jnp.dot(a, wd_ref[0, 0].astype(bf16), preferred_element_type=f32)
        for j in range(ds):
            y_buf[slot, pl.ds(j, rb, stride=ds), :] = _pack_pair(y[:, j * LANES:(j + 1) * LANES],
                                                                 y[:, half + j * LANES:half + (j + 1) * LANES])
        y_copy(g, slot).start()
        return carry

    lax.fori_loop(0, nb, block, 0)

    @pl.when(e == pl.num_programs(0) - 1)
    def _():
        @pl.when(n_used >= 2)
        def _():
            y_copy(n_used - 2, n_used % 2).wait()

        @pl.when(n_used >= 1)
        def _():
            y_copy(n_used - 1, (n_used - 1) % 2).wait()

        y_buf[0] = jnp.zeros(y_buf.shape[1:], y_buf.dtype)

        def start(bk, m):
            _row_copy(y_buf.at[0], y_hbm, 0, bk, rows, sem_y.at[0]).start()
            return m

        lax.fori_loop(n_used, n_blocks, start, 0)

        def wait(bk, m):
            _row_copy(y_buf.at[0], y_hbm, 0, 0, rows, sem_y.at[0]).wait()
            return m

        lax.fori_loop(n_used, n_blocks, wait, 0)


def _experts(blk0, nblk, n_used, xs_ts, w_g, w_u, w_d, *, layer, ds):
    _, E, D, De = w_g.shape
    rb = EXPERT_BLOCK
    n_blocks = xs_ts.shape[0] // (rb * ds)
    grid_spec = pltpu.PrefetchScalarGridSpec(
        num_scalar_prefetch=3,
        grid=(E,),
        in_specs=[pl.BlockSpec(memory_space=pl.ANY),
                  pl.BlockSpec((1, 1, D, De), lambda e, *_: (layer, e, 0, 0)),
                  pl.BlockSpec((1, 1, D, De), lambda e, *_: (layer, e, 0, 0)),
                  pl.BlockSpec((1, 1, De, D), lambda e, *_: (layer, e, 0, 0))],
        out_specs=pl.BlockSpec(memory_space=pl.ANY),
        scratch_shapes=[pltpu.VMEM((2, rb * ds, LANES), xs_ts.dtype), pltpu.VMEM((2, rb * ds, LANES), xs_ts.dtype),
                        pltpu.VMEM((rb, D), bf16), pltpu.SemaphoreType.DMA((2,)), pltpu.SemaphoreType.DMA((2,))],
    )
    return pl.pallas_call(
        functools.partial(_expert_body, ds=ds, rb=rb, n_blocks=n_blocks),
        grid_spec=grid_spec,
        out_shape=jax.ShapeDtypeStruct(xs_ts.shape, xs_ts.dtype),
        compiler_params=_cparams(("arbitrary",)),
        name="experts",
    )(blk0, nblk, n_used, xs_ts, w_g, w_u, w_d)


def _combine_body(dest_ref, dnext_ref, gate_ref, h_ref, gt_ref, fg_ref, y_hbm, o_ref, y_scr, sem, *, tt, ds, final):
    i = pl.program_id(0)
    slot = i % 2

    def gather(d_ref, to_slot):
        def body(r, m):
            for kk in range(2):
                _row_copy(y_hbm, y_scr.at[to_slot, kk], d_ref[0, kk, r], r, ds, sem.at[to_slot]).start(priority=kk)
            return m

        lax.fori_loop(0, tt, body, 0, unroll=8)

    @pl.when(i == 0)
    def _():
        gather(dest_ref, 0)

    @pl.when(i + 1 < pl.num_programs(0))
    def _():
        gather(dnext_ref, 1 - slot)

    for kk in range(2):
        pltpu.make_async_copy(y_hbm.at[pl.ds(0, tt * ds), :], y_scr.at[slot, kk], sem.at[slot]).wait()
    gt = gt_ref[0]
    g0 = gate_ref[:, 0:1]
    g1 = gate_ref[:, 1:2]
    half = ds * LANES
    for j in range(ds):
        lo0, hi0 = _unpack_pair(y_scr[slot, 0, pl.ds(j, tt, stride=ds), :])
        lo1, hi1 = _unpack_pair(y_scr[slot, 1, pl.ds(j, tt, stride=ds), :])
        for base, y0, y1 in ((0, lo0, lo1), (half, hi0, hi1)):
            sl = slice(base + j * LANES, base + (j + 1) * LANES)
            o_ref[:, sl] = h_ref[:, sl] + gt[:, sl] * (g0 * y0 + g1 * y1)
    if final:
        x = o_ref[...]
        o_ref[...] = x * lax.rsqrt(jnp.mean(x * x, -1, keepdims=True) + EPS) * fg_ref[...]


def _combine(dest3, gates_t, h1, mod, final_g, y_ts, *, n_tok, tt, ds, tpb, nlat, nb, final):
    D = h1.shape[1]
    nt = n_tok // tt
    return pl.pallas_call(
        functools.partial(_combine_body, tt=tt, ds=ds, final=final),
        grid=(nt,),
        in_specs=[pl.BlockSpec((1, 2, tt), lambda i: (i, 0, 0), memory_space=pltpu.SMEM),
                  pl.BlockSpec((1, 2, tt), lambda i: (jnp.minimum(i + 1, nt - 1), 0, 0), memory_space=pltpu.SMEM),
                  pl.BlockSpec((tt, 2), lambda i: (i, 0)),
                  pl.BlockSpec((tt, D), lambda i: (i, 0)),
                  _mod_spec(5, D, tpb, nlat, nb),
                  pl.BlockSpec((1, D), lambda i: (0, 0)),
                  pl.BlockSpec(memory_space=pl.ANY)],
        out_specs=pl.BlockSpec((tt, D), lambda i: (i, 0)),
        out_shape=jax.ShapeDtypeStruct((n_tok, D), f32),
        scratch_shapes=[pltpu.VMEM((2, 2, tt * ds, LANES), y_ts.dtype), pltpu.SemaphoreType.DMA((2,))],
        compiler_params=_cparams(("arbitrary",)),
        name="combine",
    )(dest3, dest3, gates_t, h1, mod, final_g, y_ts)


def _moe(v_ts, logits, h1, mod, w_router_bias, w_g, w_u, w_d, final_g, *, layer, n_tok, tt_dispatch, tt, ds, tpb,
         nlat, nb, final):
    E = w_g.shape[1]
    route_tt = next(t for t in (512, 256, LANES) if n_tok % t == 0)
    idx, gates, counts = _route(logits[:n_tok].T, w_router_bias.reshape(E, 1), tt=route_tt)
    counts = counts[:, 0].astype(jnp.int32)
    padded = (counts + EXPERT_BLOCK - 1) // EXPERT_BLOCK * EXPERT_BLOCK
    pad_end = jnp.cumsum(padded)
    pad_start = pad_end - padded
    n_blocks = (n_tok * 2 + EXPERT_BLOCK - 1) // EXPERT_BLOCK + E
    n_used = (pad_end[-1] // EXPERT_BLOCK).astype(jnp.int32).reshape(1)
    e_sel = idx[0:2][..., None] == jnp.arange(E, dtype=jnp.int32)
    dest = jnp.sum(jnp.where(e_sel, pad_start.astype(jnp.int32), 0), -1) + idx[2:4]
    dest3 = dest.reshape(2, n_tok // tt, tt).transpose(1, 0, 2)
    dest3_d = dest.reshape(2, n_tok // tt_dispatch, tt_dispatch).transpose(1, 0, 2)
    pad_info = jnp.stack([pad_start + counts, padded - counts, jnp.broadcast_to(n_used, (E,))]).astype(jnp.int32)
    xs_ts = _dispatch(dest3_d, pad_info, v_ts, n_tok=n_tok, n_blocks=n_blocks, tt=tt_dispatch, ds=ds)
    y_ts = _experts((pad_start // EXPERT_BLOCK).astype(jnp.int32), (padded // EXPERT_BLOCK).astype(jnp.int32), n_used,
                    xs_ts, w_g, w_u, w_d, layer=layer, ds=ds)
    return _combine(dest3, gates.T, h1, mod, final_g, y_ts, n_tok=n_tok, tt=tt, ds=ds, tpb=tpb, nlat=nlat,
                    nb=nb, final=final)


def kernel(x, c, ctx, c_ctx, w_mod, b_mod, norm1_g, norm2_g, w_in, conv_w, a_log, dt_bias, onorm_g, w_pool, pool_scale, w_up_dn, w_up_pool, w_out, w_router, router_bias, w_gate_e, w_up_e, w_down_e, final_g):
    B, S, D = x.shape
    C = ctx.shape[1]
    L = w_mod.shape[0]
    H = a_log.shape[-1]
    dk = onorm_g.shape[-1]
    assert dk == LANES
    dn_w = H * dk
    _, ng, gw, _ = w_pool.shape
    pool_w = ng * gw
    ds = D // (2 * LANES)
    n_lat = B * S
    T = n_lat + B * C
    tm = 256
    assert S % tm == 0 and (B * C) % tm == 0 and C % CHUNK == 0 and S % (4 * GRID_W) == 0
    tpb = S // tm
    nlat_tiles = n_lat // tm
    tm_in = next(t for t in (1024, 512, 256) if S % t == 0 and (B * C) % t == 0)
    tm_mg = next(t for t in (512, 256) if S % t == 0 and (B * C) % t == 0)
    tr = min(256, C)
    G = min(H, 16)

    c8 = jnp.zeros((8, D), f32).at[:B].set(c).at[B].set(c_ctx)
    mod = _mod_table(c8, w_mod, b_mod)

    qkv_c = 3 * dn_w
    scan_c = qkv_c + 4 * H
    og_off, pool_off = scan_c, scan_c + dn_w
    mg_off = pool_off + pool_w
    assert (4 * dn_w) % D == 0 and (4 * dn_w + 2 * D) % pool_w == 0
    pool_off_blocks = (4 * dn_w + 2 * D) // pool_w

    h = jnp.concatenate([x.reshape(n_lat, D), ctx.reshape(B * C, D)], 0)
    for l in range(L):
        last = l == L - 1
        mod_l = mod[l].reshape(8, 1, mod.shape[-1])
        wl = w_in[l]
        w_z = jnp.concatenate([wl[:, :qkv_c], wl[:, og_off:pool_off], wl[:, mg_off:], wl[:, pool_off:mg_off]],
                              1).astype(bf16)
        w_ba = jnp.pad(wl[:, qkv_c:scan_c], ((0, 0), (0, LANES - 4 * H))).astype(bf16)
        tn = 1024 if w_z.shape[1] % 1024 == 0 else w_z.shape[1]
        z, ba = _in_proj(h, mod_l, 0, norm1_g[l].reshape(1, D), w_z, w_ba, tm=tm_in, tn=tn, tpb=S // tm_in,
                         nlat=n_lat // tm_in, nb=B)
        qkv = _conv_prep(z, conv_w[l], tr=tr, tc=min(2048, dn_w), S=S, C=C, nlat_rows=n_lat, dn_w=dn_w, dk=dk)
        o_dn = _deltanet(qkv, ba, a_log[l], dt_bias[l], B=B, S=S, C=C, H=H, G=G)
        pooled_lat, pooled_ctx = _pool(z, B=B, S=S, C=C, pool_off_blocks=pool_off_blocks, pool_w=pool_w, gw=gw)
        n_rows = n_lat if last else T
        merged = _merge_a(o_dn, z, pooled_lat, pooled_ctx, onorm_g[l].reshape(1, dk), w_pool,
                          pool_scale[l].reshape(1, pool_w), w_up_dn, w_up_pool,
                          layer=l, n_rows=n_rows, tm=tm, H=H, D=D)
        h1, v_ts, logits = _merge_b(merged, h, mod_l, norm2_g[l].reshape(1, D), w_out, w_router,
                                    layer=l, n_rows=n_rows, tm=tm_mg, tpb=S // tm_mg, nlat=n_lat // tm_mg, nb=B)
        h = _moe(v_ts, logits, h1, mod_l, router_bias, w_gate_e, w_up_e, w_down_e,
                 final_g.reshape(1, D), layer=l, n_tok=n_rows, tt_dispatch=tm_in, tt=tm, ds=ds, tpb=tpb, nlat=nlat_tiles, nb=B,
                 final=last)
    return h.reshape(B, S, D)
```
